```python
import jax, jax.numpy as jnp
from jax import lax
import numpy as np

D_MODEL = 1024
BATCH = 32
SEQ = 256
DEPTH = 2
DEC_BATCH = 8
DEC_SEQ = 2048
PAST_LEN = 256

GRID_W = 64
N_MIXERS = 2
N_RET = (DEPTH + 1) // 2
N_ATTN = DEPTH // 2
RET_HEADS = 4
RET_DK = 256
RET_DV = 512
RET_CHUNK = 128
ATTN_HEADS = 16
ATTN_KV_HEADS = 4
ATTN_GROUPS = ATTN_HEADS // ATTN_KV_HEADS
HEAD_DIM = 64
WINDOW = 128
ATTN_BLOCK = 128
D_FF = 2816
CONV_WIDTH = 3
ROPE_BASE = 10000.0
EPS = 1e-6
N_MOD = 6
NEG_INF = -1e30

kernel_name = "hybrid_retention_swa_prefix_dit_step"


def rmsnorm(x, g):
    xf = x.astype(jnp.float32)
    y = xf * lax.rsqrt(jnp.mean(xf * xf, axis=-1, keepdims=True) + EPS)
    return y.astype(x.dtype) * g


def head_rms(x):
    xf = x.astype(jnp.float32)
    return (xf * lax.rsqrt(jnp.mean(xf * xf, axis=-1, keepdims=True) + EPS)).astype(x.dtype)


def adaln(cvec, w, b):
    m = jax.nn.silu(cvec) @ w + b
    return jnp.split(m, N_MOD, axis=-1)


def modulate(x, g, shift, scale):
    return rmsnorm(x, g) * (1.0 + scale) + shift


def rope_half(x, pos):
    n_freq = x.shape[-1] // 2
    inv = ROPE_BASE ** (-jnp.arange(n_freq, dtype=jnp.float32) / n_freq)
    ang = pos.astype(jnp.float32)[:, None] * inv[None, :]
    cos = jnp.cos(ang)[None, :, None, :].astype(x.dtype)
    sin = jnp.sin(ang)[None, :, None, :].astype(x.dtype)
    x1, x2 = x[..., :n_freq], x[..., n_freq:]
    return jnp.concatenate([x1 * cos - x2 * sin, x1 * sin + x2 * cos], axis=-1)


def axial_rope(x):
    L = x.shape[1]
    n_rows = L // GRID_W
    rows = jnp.repeat(jnp.arange(n_rows), GRID_W)
    cols = jnp.tile(jnp.arange(GRID_W), n_rows)
    half = x.shape[-1] // 2
    return jnp.concatenate([rope_half(x[..., :half], rows), rope_half(x[..., half:], cols)], axis=-1)


def retention_scan(q, k, v, log_gamma, s0):
    B, L, H, DK = q.shape
    DV = v.shape[-1]
    C = RET_CHUNK
    n = L // C
    idx = jnp.arange(C, dtype=jnp.float32)
    lg = log_gamma[:, None]
    diff = idx[:, None] - idx[None, :]
    dmat = jnp.where(diff >= 0, jnp.exp(lg[:, :, None] * jnp.maximum(diff, 0.0)), 0.0).astype(q.dtype)
    xi = jnp.exp(lg * (idx + 1.0)).T.astype(q.dtype)
    zeta = jnp.exp(lg * (C - 1.0 - idx)).T.astype(q.dtype)
    g_chunk = jnp.exp(log_gamma * C).astype(q.dtype)

    def to_chunks(t):
        return t.reshape(B, n, C, H, t.shape[-1]).transpose(1, 0, 2, 3, 4)

    def step(S, inp):
        qc, kc, vc = inp
        s = jnp.einsum('bihd,bjhd->bhij', qc, kc) * dmat[None]
        inner = jnp.einsum('bhij,bjhv->bihv', s, vc)
        cross = jnp.einsum('bihd,bhdv->bihv', qc, S) * xi[None, :, :, None]
        S_new = S * g_chunk[None, :, None, None] + jnp.einsum('bjhd,bjhv->bhdv', kc * zeta[None, :, :, None], vc)
        return S_new, inner + cross

    s_fin, out = lax.scan(step, s0.astype(q.dtype), (to_chunks(q), to_chunks(k), to_chunks(v)))
    return out.transpose(1, 0, 2, 3, 4).reshape(B, L, H, DV), s_fin


def retention_direction(q, k, v, log_gamma, s0, reverse):
    if reverse:
        o, s = retention_scan(q[:, ::-1], k[:, ::-1], v[:, ::-1], log_gamma, s0)
        return o[:, ::-1], s
    return retention_scan(q, k, v, log_gamma, s0)


def retention_mixer(h, wq, wk, wv, wg_f, wg_b, wo, raw_f, raw_b, s0_f, s0_b, rope):
    B, L, _ = h.shape
    q = (h @ wq).reshape(B, L, RET_HEADS, RET_DK)
    k = (h @ wk).reshape(B, L, RET_HEADS, RET_DK) * (RET_DK ** -0.5)
    v = (h @ wv).reshape(B, L, RET_HEADS, RET_DV)
    if rope:
        q, k = axial_rope(q), axial_rope(k)
    lg_f = jax.nn.log_sigmoid(raw_f.astype(jnp.float32))
    lg_b = jax.nn.log_sigmoid(raw_b.astype(jnp.float32))
    o_f, s_f = retention_direction(q, k, v, lg_f, s0_f, False)
    o_b, s_b = retention_direction(q, k, v, lg_b, s0_b, True)
    y = (head_rms(o_f).reshape(B, L, RET_HEADS * RET_DV) * jax.nn.silu(h @ wg_f)
         + head_rms(o_b).reshape(B, L, RET_HEADS * RET_DV) * jax.nn.silu(h @ wg_b))
    return y @ wo, s_f, s_b


def attn_project(h, wq, wk, wv):
    B, L, _ = h.shape
    q = (h @ wq).reshape(B, L, ATTN_HEADS, HEAD_DIM)
    k = (h @ wk).reshape(B, L, ATTN_KV_HEADS, HEAD_DIM)
    v = (h @ wv).reshape(B, L, ATTN_KV_HEADS, HEAD_DIM)
    return q, k, v


def softmax_with_sink(logits, sink):
    sink_col = jnp.broadcast_to(sink.astype(jnp.float32).reshape(ATTN_KV_HEADS, ATTN_GROUPS, 1, 1),
                                logits.shape[:-1] + (1,))
    p = jax.nn.softmax(jnp.concatenate([logits, sink_col], axis=-1), axis=-1)
    return p[..., :-1]


def context_attention(q, k, v, sink):
    B, L, H, D = q.shape
    n = L // ATTN_BLOCK
    scale = D ** -0.5
    qb = q.reshape(B, n, ATTN_BLOCK, ATTN_KV_HEADS, ATTN_GROUPS, D).transpose(1, 0, 2, 3, 4, 5)

    def block(qi):
        s = jnp.einsum('bqkgd,bskd->bkgqs', qi, k).astype(jnp.float32) * scale
        p = softmax_with_sink(s, sink).astype(v.dtype)
        return jnp.einsum('bkgqs,bskd->bqkgd', p, v)

    out = lax.map(block, qb)
    return out.transpose(1, 0, 2, 3, 4, 5).reshape(B, L, H * D)


def latent_attention(q, k, v, k_ctx, v_ctx, sink):
    B, L, H, D = q.shape
    blk = ATTN_BLOCK
    n = L // blk
    scale = D ** -0.5
    qb = q.reshape(B, n, blk, ATTN_KV_HEADS, ATTN_GROUPS, D)

    def band(t):
        z = jnp.zeros((B, blk) + t.shape[2:], t.dtype)
        tp = jnp.concatenate([z, t, z], axis=1).reshape(B, n + 2, blk, ATTN_KV_HEADS, D)
        return jnp.concatenate([tp[:, :-2], tp[:, 1:-1], tp[:, 2:]], axis=2)

    kb, vb = band(k), band(v)
    qpos = jnp.arange(L).reshape(n, blk)
    kpos = (jnp.arange(n)[:, None] - 1) * blk + jnp.arange(3 * blk)[None, :]
    rel = qpos[:, :, None] - kpos[:, None, :]
    valid = (jnp.abs(rel) <= WINDOW) & (kpos[:, None, :] >= 0) & (kpos[:, None, :] < L)
    s_band = jnp.einsum('bnqkgd,bnskd->bnkgqs', qb, kb).astype(jnp.float32) * scale
    s_band = jnp.where(valid[None, :, None, None], s_band, NEG_INF)
    s_ctx = jnp.einsum('bnqkgd,bskd->bnkgqs', qb, k_ctx).astype(jnp.float32) * scale
    p = softmax_with_sink(jnp.concatenate([s_band, s_ctx], axis=-1), sink).astype(v.dtype)
    nb = 3 * blk
    out = (jnp.einsum('bnkgqs,bnskd->bnqkgd', p[..., :nb], vb)
           + jnp.einsum('bnkgqs,bskd->bnqkgd', p[..., nb:], v_ctx))
    return out.reshape(B, L, H * D)


def conv_ffn(h, w_in, conv_w, conv_b, w_out):
    a = h @ w_in
    L = a.shape[1]
    ap = jnp.pad(a, ((0, 0), (1, 1), (0, 0)))
    a = ap[:, :L] * conv_w[0] + ap[:, 1:L + 1] * conv_w[1] + ap[:, 2:] * conv_w[2] + conv_b
    u, g = jnp.split(a, 2, axis=-1)
    return (jax.nn.silu(g) * u) @ w_out


def setup_inputs(seed: int = 0) -> dict:
    key = jax.random.key(seed)
    ks = jax.random.split(key, 40)

    def nrm(k, shape, scale=1.0):
        return jax.random.normal(k, shape, jnp.float32) * scale

    D = D_MODEL
    gamma0 = 1.0 - jnp.exp(jnp.linspace(jnp.log(1.0 / 32.0), jnp.log(1.0 / 512.0), RET_HEADS))
    raw0 = jnp.log(gamma0 / (1.0 - gamma0)).astype(jnp.float32)
    return {
        "x_prompt": nrm(ks[0], (BATCH, SEQ, D)),
        "x_sample": nrm(ks[1], (DEC_BATCH, DEC_SEQ, D)),
        "state_ret_fwd": nrm(ks[2], (DEC_BATCH, N_RET, RET_HEADS, RET_DK, RET_DV), 0.5),
        "state_ret_bwd": nrm(ks[3], (DEC_BATCH, N_RET, RET_HEADS, RET_DK, RET_DV), 0.5),
        "cache_attn_k": nrm(ks[4], (DEC_BATCH, N_ATTN, PAST_LEN, ATTN_KV_HEADS, HEAD_DIM)),
        "cache_attn_v": nrm(ks[5], (DEC_BATCH, N_ATTN, PAST_LEN, ATTN_KV_HEADS, HEAD_DIM)),
        "c": nrm(ks[6], (DEC_BATCH, D)),
        "c_ctx": nrm(ks[7], (D,)),
        "ada_w": nrm(ks[8], (DEPTH, D, N_MOD * D), D ** -0.5),
        "ada_b": nrm(ks[9], (DEPTH, N_MOD * D), 0.02),
        "norm_mix": 1.0 + nrm(ks[10], (DEPTH, D), 0.02),
        "norm_ffn": 1.0 + nrm(ks[11], (DEPTH, D), 0.02),
        "ret_wq": nrm(ks[12], (N_RET, D, RET_HEADS * RET_DK), D ** -0.5),
        "ret_wk": nrm(ks[13], (N_RET, D, RET_HEADS * RET_DK), D ** -0.5),
        "ret_wv": nrm(ks[14], (N_RET, D, RET_HEADS * RET_DV), D ** -0.5),
        "ret_wg_fwd": nrm(ks[15], (N_RET, D, RET_HEADS * RET_DV), D ** -0.5),
        "ret_wg_bwd": nrm(ks[16], (N_RET, D, RET_HEADS * RET_DV), D ** -0.5),
        "ret_wo": nrm(ks[17], (N_RET, RET_HEADS * RET_DV, D), (RET_HEADS * RET_DV) ** -0.5),
        "ret_decay_fwd": raw0[None, :] + nrm(ks[18], (N_RET, RET_HEADS), 0.01),
        "ret_decay_bwd": raw0[None, :] + nrm(ks[19], (N_RET, RET_HEADS), 0.01),
        "attn_wq": nrm(ks[20], (N_ATTN, D, ATTN_HEADS * HEAD_DIM), D ** -0.5),
        "attn_wk": nrm(ks[21], (N_ATTN, D, ATTN_KV_HEADS * HEAD_DIM), D ** -0.5),
        "attn_wv": nrm(ks[22], (N_ATTN, D, ATTN_KV_HEADS * HEAD_DIM), D ** -0.5),
        "attn_wo": nrm(ks[23], (N_ATTN, ATTN_HEADS * HEAD_DIM, D), (ATTN_HEADS * HEAD_DIM) ** -0.5),
        "attn_sink": nrm(ks[24], (N_ATTN, ATTN_HEADS), 0.5),
        "ffn_w_in": nrm(ks[25], (DEPTH, D, 2 * D_FF), D ** -0.5),
        "ffn_conv_w": nrm(ks[26], (DEPTH, CONV_WIDTH, 2 * D_FF), CONV_WIDTH ** -0.5),
        "ffn_conv_b": nrm(ks[27], (DEPTH, 2 * D_FF), 0.02),
        "ffn_w_out": nrm(ks[28], (DEPTH, D_FF, D), D_FF ** -0.5),
        "norm_final": 1.0 + nrm(ks[29], (D,), 0.02),
    }


def reference(x_prompt, x_sample, state_ret_fwd, state_ret_bwd, cache_attn_k, cache_attn_v, c, c_ctx,
              ada_w, ada_b, norm_mix, norm_ffn,
              ret_wq, ret_wk, ret_wv, ret_wg_fwd, ret_wg_bwd, ret_wo, ret_decay_fwd, ret_decay_bwd,
              attn_wq, attn_wk, attn_wv, attn_wo, attn_sink,
              ffn_w_in, ffn_conv_w, ffn_conv_b, ffn_w_out, norm_final):
    xc, xl = x_prompt, x_sample
    b_ctx = xc.shape[0]
    new_sf, new_sb, new_k, new_v = [], [], [], []
    for l in range(DEPTH):
        mc = adaln(c_ctx, ada_w[l], ada_b[l])
        ml = [t[:, None, :] for t in adaln(c, ada_w[l], ada_b[l])]
        hc = modulate(xc, norm_mix[l], mc[0], mc[1])
        hl = modulate(xl, norm_mix[l], ml[0], ml[1])
        if l % N_MIXERS == 0:
            r = l // N_MIXERS
            w = (ret_wq[r], ret_wk[r], ret_wv[r], ret_wg_fwd[r], ret_wg_bwd[r], ret_wo[r],
                 ret_decay_fwd[r], ret_decay_bwd[r])
            zero = jnp.zeros((b_ctx, RET_HEADS, RET_DK, RET_DV), xc.dtype)
            yc, s_f, s_b = retention_mixer(hc, *w, zero, zero, rope=False)
            yl, _, _ = retention_mixer(hl, *w, state_ret_fwd[:, r], state_ret_bwd[:, r], rope=True)
            new_sf.append(s_f)
            new_sb.append(s_b)
        else:
            a = l // N_MIXERS
            qc, kc, vc = attn_project(hc, attn_wq[a], attn_wk[a], attn_wv[a])
            yc = context_attention(qc, kc, vc, attn_sink[a]) @ attn_wo[a]
            ql, kl, vl = attn_project(hl, attn_wq[a], attn_wk[a], attn_wv[a])
            yl = latent_attention(axial_rope(ql), axial_rope(kl), vl,
                                  cache_attn_k[:, a], cache_attn_v[:, a], attn_sink[a]) @ attn_wo[a]
            new_k.append(kc)
            new_v.append(vc)
        xc = xc + mc[2] * yc
        xl = xl + ml[2] * yl
        hc = modulate(xc, norm_ffn[l], mc[3], mc[4])
        hl = modulate(xl, norm_ffn[l], ml[3], ml[4])
        xc = xc + mc[5] * conv_ffn(hc, ffn_w_in[l], ffn_conv_w[l], ffn_conv_b[l], ffn_w_out[l])
        xl = xl + ml[5] * conv_ffn(hl, ffn_w_in[l], ffn_conv_w[l], ffn_conv_b[l], ffn_w_out[l])
    y_prompt = rmsnorm(xc, norm_final)
    y_sample = rmsnorm(xl, norm_final)
    return (y_prompt, y_sample, jnp.stack(new_sf, axis=1), jnp.stack(new_sb, axis=1),
            jnp.stack(new_k, axis=1), jnp.stack(new_v, axis=1))
```

```python
import functools

import jax
import jax.numpy as jnp
from jax import lax
from jax.experimental import pallas as pl
from jax.experimental.pallas import tpu as pltpu

D = 1024
CTX_B, CTX_L = 32, 256
LAT_B, LAT_L = 8, 2048
T_CTX = CTX_B * CTX_L
T_LAT = LAT_B * LAT_L
T_ALL = T_CTX + T_LAT
GRID_W = 64
N_MOD = 6
MOD_ROWS = 16
CTX_MOD_ROW = LAT_B
RH, RDK, RDV, RC = 4, 256, 512, 128
AH, AKV, AG, AHD = 16, 4, 4, 64
WINDOW = 128
ABLK = 128
D_FF = 2816
ROPE_BASE = 10000.0
EPS = 1e-6
NEG_INF = -1e30

BF16 = jnp.bfloat16
F32 = jnp.float32

VMEM_LIMIT_BYTES = 56 * 1024 * 1024
NORM_ROWS = 32


def _cparams(n_axes):
    return pltpu.CompilerParams(dimension_semantics=("arbitrary",) * n_axes,
                                vmem_limit_bytes=VMEM_LIMIT_BYTES)


def _silu(x):
    return x / (1.0 + jnp.exp(-x))


def _mod_row(i, tm):
    r0 = i * tm
    return jnp.where(r0 < T_CTX, CTX_MOD_ROW, (r0 - T_CTX) // LAT_L)


def _modulate_into(x_ref, g_ref, mod_ref, shift_k, scale_k, h_ref, n_rows, h_row0):
    g = g_ref[...]
    one_plus = 1.0 + mod_ref[0, scale_k:scale_k + 1, :]
    shift = mod_ref[0, shift_k:shift_k + 1, :]

    def body(c, carry):
        r = pl.multiple_of(c * NORM_ROWS, NORM_ROWS)
        x = x_ref[pl.ds(r, NORM_ROWS), :]
        ms = jnp.mean(x * x, axis=-1, keepdims=True)
        y = x * lax.rsqrt(ms + EPS) * g
        h_ref[pl.ds(h_row0 + r, NORM_ROWS), :] = (y * one_plus + shift).astype(BF16)
        return carry

    lax.fori_loop(0, n_rows // NORM_ROWS, body, 0)


ADA_TN = 512


def _adaln_kernel(cv_ref, w_ref, b_ref, o_ref):
    s = _silu(cv_ref[...]).astype(BF16)
    w = w_ref[0].astype(BF16)
    o_ref[0] = jnp.dot(s, w, preferred_element_type=F32) + b_ref[0]


def _adaln(cv, ada_w, ada_b):
    depth = ada_w.shape[0]
    n = N_MOD * D
    return pl.pallas_call(
        _adaln_kernel,
        grid=(depth, n // ADA_TN),
        in_specs=[
            pl.BlockSpec((MOD_ROWS, D), lambda l, j: (0, 0)),
            pl.BlockSpec((1, D, ADA_TN), lambda l, j: (l, 0, j)),
            pl.BlockSpec((1, 1, ADA_TN), lambda l, j: (l, 0, j)),
        ],
        out_specs=pl.BlockSpec((1, MOD_ROWS, ADA_TN), lambda l, j: (l, 0, j)),
        out_shape=jax.ShapeDtypeStruct((depth, MOD_ROWS, n), F32),
        compiler_params=_cparams(2),
        name="adaln",
    )(cv, ada_w, ada_b.reshape(depth, 1, n))


def _rope_tables(head_dim):
    half = head_dim // 2
    n_freq = half // 2
    inv = ROPE_BASE ** (-jnp.arange(n_freq, dtype=F32) / n_freq)
    t = jnp.arange(LAT_L)
    rows = (t // GRID_W).astype(F32)[:, None] * inv[None, :]
    cols = (t % GRID_W).astype(F32)[:, None] * inv[None, :]
    zero = jnp.zeros_like(rows)
    cos = jnp.concatenate([jnp.cos(rows)] * 2 + [jnp.cos(cols)] * 2, axis=-1)
    sin_lo = jnp.concatenate([-jnp.sin(rows), zero, -jnp.sin(cols), zero], axis=-1)
    sin_hi = jnp.concatenate([zero, jnp.sin(rows), zero, jnp.sin(cols)], axis=-1)
    return cos, sin_lo, sin_hi


def _rope_slab(x, cos, sin_lo, sin_hi, n_freq):
    up = pltpu.roll(x, 128 - n_freq, 1)
    dn = pltpu.roll(x, n_freq, 1)
    return x * cos + up * sin_lo + dn * sin_hi


PROJ_TM = 1024
PROJ_MC = 256
RET_TN = 1024
RET_N = 2 * RH * RDK + 3 * RH * RDV


def _ret_proj_kernel(x_ref, g_ref, mod_ref, w_ref, cos_ref, slo_ref, shi_ref, o_ref, h_ref):
    i = pl.program_id(0)
    j = pl.program_id(1)
    is_lat = i * PROJ_TM >= T_CTX

    @pl.when(j == 0)
    def _():
        _modulate_into(x_ref, g_ref, mod_ref, 0, 1, h_ref, PROJ_TM, 0)

    def chunks(epilogue):
        for m in range(PROJ_TM // PROJ_MC):
            rows = slice(m * PROJ_MC, (m + 1) * PROJ_MC)
            acc = jnp.dot(h_ref[rows, :], w_ref[...], preferred_element_type=F32)
            epilogue(acc, rows)

    def plain(acc, rows):
        o_ref[rows, :] = acc.astype(BF16)

    def gate(acc, rows):
        o_ref[rows, :] = _silu(acc).astype(BF16)

    def rope(scale):
        def ep(acc, rows):
            for s in range(RET_TN // 128):
                cols = slice(s * 128, (s + 1) * 128)
                tcols = slice((s % 2) * 128, (s % 2 + 1) * 128)
                x = acc[:, cols]
                if scale != 1.0:
                    x = x * scale
                y = _rope_slab(x, cos_ref[rows, tcols], slo_ref[rows, tcols], shi_ref[rows, tcols], RDK // 4)
                o_ref[rows, cols] = y.astype(BF16)
        return ep

    def scaled(scale):
        def ep(acc, rows):
            o_ref[rows, :] = (acc * scale).astype(BF16)
        return ep

    k_scale = RDK ** -0.5

    @pl.when(jnp.logical_and(j == 0, is_lat))
    def _():
        chunks(rope(1.0))

    @pl.when(jnp.logical_and(j == 0, jnp.logical_not(is_lat)))
    def _():
        chunks(plain)

    @pl.when(jnp.logical_and(j == 1, is_lat))
    def _():
        chunks(rope(k_scale))

    @pl.when(jnp.logical_and(j == 1, jnp.logical_not(is_lat)))
    def _():
        chunks(scaled(k_scale))

    @pl.when(jnp.logical_and(j >= 2, j < 4))
    def _():
        chunks(plain)

    @pl.when(j >= 4)
    def _():
        chunks(gate)


def _ret_proj(x, gain, mods_l, w_all, tabs):
    cos, slo, shi = tabs
    n_pos = LAT_L // PROJ_TM
    tab_spec = pl.BlockSpec((PROJ_TM, RDK), lambda i, j: (i % n_pos, 0))
    return pl.pallas_call(
        _ret_proj_kernel,
        grid=(T_ALL // PROJ_TM, RET_N // RET_TN),
        in_specs=[
            pl.BlockSpec((PROJ_TM, D), lambda i, j: (i, 0)),
            pl.BlockSpec((1, D), lambda i, j: (0, 0)),
            pl.BlockSpec((1, N_MOD, D), lambda i, j: (_mod_row(i, PROJ_TM), 0, 0)),
            pl.BlockSpec((D, RET_TN), lambda i, j: (0, j)),
            tab_spec, tab_spec, tab_spec,
        ],
        out_specs=pl.BlockSpec((PROJ_TM, RET_TN), lambda i, j: (i, j)),
        out_shape=jax.ShapeDtypeStruct((T_ALL, RET_N), BF16),
        scratch_shapes=[pltpu.VMEM((PROJ_TM, D), BF16)],
        compiler_params=_cparams(2),
        name="ret_proj",
    )(x, gain, mods_l, w_all, cos, slo, shi)


RET_BLOCK = 2048


def _ret_scan_kernel(n_seq, seq_len, has_init, emit_state, lg_ref, q_ref, k_ref, v_ref, gf_ref, gb_ref, *rest):
    rest = list(rest)
    s0f_ref = rest.pop(0) if has_init else None
    s0b_ref = rest.pop(0) if has_init else None
    y_ref = rest.pop(0)
    sf_ref = rest.pop(0) if emit_state else None
    sb_ref = rest.pop(0) if emit_state else None
    ybuf_ref, st_ref = rest

    hd = pl.program_id(1)
    lg_f = lg_ref[0, hd]
    lg_b = lg_ref[1, hd]
    n_chunks = seq_len // RC

    ii = lax.broadcasted_iota(jnp.int32, (RC, RC), 0)
    jj = lax.broadcasted_iota(jnp.int32, (RC, RC), 1)
    diff = (ii - jj).astype(F32)
    dmat_f = jnp.where(diff >= 0, jnp.exp(lg_f * jnp.maximum(diff, 0.0)), 0.0)
    dmat_b = jnp.where(diff <= 0, jnp.exp(lg_b * jnp.maximum(-diff, 0.0)), 0.0)
    idx = lax.broadcasted_iota(jnp.int32, (RC, 1), 0).astype(F32)
    xi_f = jnp.exp(lg_f * (idx + 1.0))
    zeta_f = jnp.exp(lg_f * (RC - 1.0 - idx))
    xi_b = jnp.exp(lg_b * (RC - idx))
    zeta_b = jnp.exp(lg_b * idx)
    ones_row = jnp.ones((1, RDV), F32)
    gch_f = jnp.exp(ones_row * (lg_f * RC))
    gch_b = jnp.exp(ones_row * (lg_b * RC))

    def chunk_out(r0, dmat, xi, zeta, gch):
        q = q_ref[pl.ds(r0, RC), :]
        k = k_ref[pl.ds(r0, RC), :]
        v = v_ref[pl.ds(r0, RC), :]
        s = lax.dot_general(q, k, (((1,), (1,)), ((), ())), preferred_element_type=F32)
        inner = jnp.dot((s * dmat).astype(BF16), v, preferred_element_type=F32)
        st = st_ref[...]
        cross = jnp.dot(q, st.astype(BF16), preferred_element_type=F32) * xi
        o = inner + cross
        kz = (k.astype(F32) * zeta).astype(BF16)
        upd = lax.dot_general(kz, v, (((0,), (0,)), ((), ())), preferred_element_type=F32)
        st_ref[...] = st * gch + upd
        return o * lax.rsqrt(jnp.mean(o * o, axis=-1, keepdims=True) + EPS)

    for sq in range(n_seq):
        base = sq * seq_len

        if has_init:
            st_ref[...] = s0f_ref[0, 0, 0]
        else:
            st_ref[...] = jnp.zeros((RDK, RDV), F32)

        def fwd(c, carry):
            r0 = pl.multiple_of(base + c * RC, RC)
            o = chunk_out(r0, dmat_f, xi_f, zeta_f, gch_f)
            ybuf_ref[pl.ds(r0, RC), :] = o * gf_ref[pl.ds(r0, RC), :].astype(F32)
            return carry

        lax.fori_loop(0, n_chunks, fwd, 0)
        if emit_state:
            sf_ref[sq, 0, 0] = st_ref[...]

        if has_init:
            st_ref[...] = s0b_ref[0, 0, 0]
        else:
            st_ref[...] = jnp.zeros((RDK, RDV), F32)

        def bwd(c, carry):
            r0 = pl.multiple_of(base + (n_chunks - 1 - c) * RC, RC)
            o = chunk_out(r0, dmat_b, xi_b, zeta_b, gch_b)
            y = ybuf_ref[pl.ds(r0, RC), :] + o * gb_ref[pl.ds(r0, RC), :].astype(F32)
            y_ref[pl.ds(r0, RC), :] = y.astype(BF16)
            return carry

        lax.fori_loop(0, n_chunks, bwd, 0)
        if emit_state:
            sb_ref[sq, 0, 0] = st_ref[...]


def _ret_scan(qkvg, lg, y_prev, *, n_blocks, n_seq, seq_len, row_block0, s0f=None, s0b=None, emit_state=False):
    has_init = s0f is not None
    kq = RDK
    n_qk = RH * RDK // kq
    in_specs = [
        pl.BlockSpec(memory_space=pltpu.SMEM),
        pl.BlockSpec((RET_BLOCK, RDK), lambda b, h: (row_block0 + b, h)),
        pl.BlockSpec((RET_BLOCK, RDK), lambda b, h: (row_block0 + b, n_qk + h)),
        pl.BlockSpec((RET_BLOCK, RDV), lambda b, h: (row_block0 + b, n_qk + h)),
        pl.BlockSpec((RET_BLOCK, RDV), lambda b, h: (row_block0 + b, n_qk + RH + h)),
        pl.BlockSpec((RET_BLOCK, RDV), lambda b, h: (row_block0 + b, n_qk + 2 * RH + h)),
    ]
    args = [lg, qkvg, qkvg, qkvg, qkvg, qkvg]
    if has_init:
        st_spec = pl.BlockSpec((1, 1, 1, RDK, RDV), lambda b, h: (b, 0, h, 0, 0))
        in_specs += [st_spec, st_spec]
        args += [s0f, s0b]
    in_specs.append(pl.BlockSpec(memory_space=pl.ANY))
    args.append(y_prev)
    y_shape = jax.ShapeDtypeStruct((T_ALL, RH * RDV), BF16)
    out_specs = [pl.BlockSpec((RET_BLOCK, RDV), lambda b, h: (row_block0 + b, h))]
    out_shape = [y_shape]
    if emit_state:
        so_spec = pl.BlockSpec((n_seq, 1, 1, RDK, RDV), lambda b, h: (b, 0, h, 0, 0))
        so_shape = jax.ShapeDtypeStruct((n_blocks * n_seq, 1, RH, RDK, RDV), F32)
        out_specs += [so_spec, so_spec]
        out_shape += [so_shape, so_shape]

    def body(*refs):
        refs = list(refs)
        n_in = len(in_specs)
        ins = refs[:n_in - 1]
        _ret_scan_kernel(n_seq, seq_len, has_init, emit_state, *ins, *refs[n_in:])

    return pl.pallas_call(
        body,
        grid=(n_blocks, RH),
        in_specs=in_specs,
        out_specs=out_specs,
        out_shape=out_shape,
        scratch_shapes=[pltpu.VMEM((RET_BLOCK, RDV), F32), pltpu.VMEM((RDK, RDV), F32)],
        input_output_aliases={len(args) - 1: 0},
        compiler_params=_cparams(2),
        name="ret_scan_ctx" if emit_state else "ret_scan_lat",
    )(*args)


RES_TM = 512
RES_MC = 256


def _res_kernel(gate_k, final_norm, a_ref, w_ref, x_ref, mod_ref, *rest):
    if final_norm:
        gn_ref, o_ref = rest
    else:
        (o_ref,) = rest
    gate = mod_ref[0, gate_k:gate_k + 1, :]
    for m in range(RES_TM // RES_MC):
        rows = slice(m * RES_MC, (m + 1) * RES_MC)
        acc = jnp.dot(a_ref[rows, :], w_ref[...], preferred_element_type=F32)
        xn = x_ref[rows, :] + gate * acc
        if final_norm:
            ms = jnp.mean(xn * xn, axis=-1, keepdims=True)
            xn = xn * lax.rsqrt(ms + EPS) * gn_ref[...]
        o_ref[rows, :] = xn


def _res_matmul(a, w, x, mods_l, gate_k, final_gain=None):
    kdim = a.shape[1]
    final_norm = final_gain is not None
    in_specs = [
        pl.BlockSpec((RES_TM, kdim), lambda i: (i, 0)),
        pl.BlockSpec((kdim, D), lambda i: (0, 0)),
        pl.BlockSpec((RES_TM, D), lambda i: (i, 0)),
        pl.BlockSpec((1, N_MOD, D), lambda i: (_mod_row(i, RES_TM), 0, 0)),
    ]
    args = [a, w, x, mods_l]
    if final_norm:
        in_specs.append(pl.BlockSpec((1, D), lambda i: (0, 0)))
        args.append(final_gain)
    return pl.pallas_call(
        functools.partial(_res_kernel, gate_k, final_norm),
        grid=(T_ALL // RES_TM,),
        in_specs=in_specs,
        out_specs=pl.BlockSpec((RES_TM, D), lambda i: (i, 0)),
        out_shape=jax.ShapeDtypeStruct((T_ALL, D), F32),
        compiler_params=_cparams(1),
        name="res_matmul_k%d%s" % (kdim, "_final" if final_norm else ""),
    )(*args)


FFN_TM = 2048
FFN_MC = 512
FFN_TN = 256
FFN_HALO = 16


def _ffn_up_kernel(x_ref, g_ref, mod_ref, wu_ref, wg_ref, cwu_ref, cwg_ref, cbu_ref, cbg_ref, o_ref, h_ref):
    i = pl.program_id(0)
    j = pl.program_id(1)
    seg = jnp.where(i * FFN_TM < T_CTX, CTX_L, LAT_L)

    @pl.when(j == 0)
    def _():
        zeros = jnp.zeros((FFN_HALO, D), BF16)
        h_ref[0:FFN_HALO, :] = zeros
        h_ref[FFN_HALO + FFN_TM:FFN_TM + 2 * FFN_HALO, :] = zeros
        _modulate_into(x_ref, g_ref, mod_ref, 3, 4, h_ref, FFN_TM, FFN_HALO)

    n_ext = FFN_MC + 2 * FFN_HALO
    mid = slice(FFN_HALO, FFN_HALO + FFN_MC)

    def conv(a, cw_ref, cb_ref, first, last):
        prev = pltpu.roll(a, 1, 0)[mid]
        nxt = pltpu.roll(a, n_ext - 1, 0)[mid]
        prev = jnp.where(first, 0.0, prev)
        nxt = jnp.where(last, 0.0, nxt)
        return prev * cw_ref[0:1, :] + a[mid] * cw_ref[1:2, :] + nxt * cw_ref[2:3, :] + cb_ref[...]

    for m in range(FFN_TM // FFN_MC):
        hm = h_ref[m * FFN_MC:m * FFN_MC + n_ext, :]
        au = jnp.dot(hm, wu_ref[...], preferred_element_type=F32)
        ag = jnp.dot(hm, wg_ref[...], preferred_element_type=F32)
        pos = (m * FFN_MC + lax.broadcasted_iota(jnp.int32, (FFN_MC, 1), 0)) & (seg - 1)
        first = pos == 0
        last = pos == seg - 1
        u = conv(au, cwu_ref, cbu_ref, first, last)
        g = conv(ag, cwg_ref, cbg_ref, first, last)
        o_ref[m * FFN_MC:(m + 1) * FFN_MC, :] = (_silu(g) * u).astype(BF16)


def _ffn_up(x, gain, mods_l, w_in, conv_w, conv_b):
    n_tiles = D_FF // FFN_TN
    return pl.pallas_call(
        _ffn_up_kernel,
        grid=(T_ALL // FFN_TM, n_tiles),
        in_specs=[
            pl.BlockSpec((FFN_TM, D), lambda i, j: (i, 0)),
            pl.BlockSpec((1, D), lambda i, j: (0, 0)),
            pl.BlockSpec((1, N_MOD, D), lambda i, j: (_mod_row(i, FFN_TM), 0, 0)),
            pl.BlockSpec((D, FFN_TN), lambda i, j: (0, j)),
            pl.BlockSpec((D, FFN_TN), lambda i, j: (0, n_tiles + j)),
            pl.BlockSpec((3, FFN_TN), lambda i, j: (0, j)),
            pl.BlockSpec((3, FFN_TN), lambda i, j: (0, n_tiles + j)),
            pl.BlockSpec((1, FFN_TN), lambda i, j: (0, j)),
            pl.BlockSpec((1, FFN_TN), lambda i, j: (0, n_tiles + j)),
        ],
        out_specs=pl.BlockSpec((FFN_TM, FFN_TN), lambda i, j: (i, j)),
        out_shape=jax.ShapeDtypeStruct((T_ALL, D_FF), BF16),
        scratch_shapes=[pltpu.VMEM((FFN_TM + 2 * FFN_HALO, D), BF16)],
        compiler_params=_cparams(2),
        name="ffn_up",
    )(x, gain, mods_l, w_in, w_in, conv_w, conv_w, conv_b, conv_b)


ATT_TN = 512
ATT_NQ = AH * AHD
ATT_NKV = AKV * AHD
ATT_N = ATT_NQ + 2 * ATT_NKV


def _attn_proj_kernel(x_ref, g_ref, mod_ref, w_ref, cos_ref, slo_ref, shi_ref, q_ref, kv_ref, kv32_ref, h_ref):
    i = pl.program_id(0)
    j = pl.program_id(1)
    is_lat = i * PROJ_TM >= T_CTX
    n_q_tiles = ATT_NQ // ATT_TN

    @pl.when(j == 0)
    def _():
        _modulate_into(x_ref, g_ref, mod_ref, 0, 1, h_ref, PROJ_TM, 0)

    def roped(acc, rows, n_cols):
        out = []
        for s in range(n_cols // 128):
            x = acc[:, s * 128:(s + 1) * 128]
            out.append(_rope_slab(x, cos_ref[rows, :], slo_ref[rows, :], shi_ref[rows, :], AHD // 4))
        return out

    for m in range(PROJ_TM // PROJ_MC):
        rows = slice(m * PROJ_MC, (m + 1) * PROJ_MC)

        @pl.when(j < n_q_tiles)
        def _():
            acc = jnp.dot(h_ref[rows, :], w_ref[...], preferred_element_type=F32)

            @pl.when(is_lat)
            def _():
                for s, y in enumerate(roped(acc, rows, ATT_TN)):
                    q_ref[rows, s * 128:(s + 1) * 128] = y.astype(BF16)

            @pl.when(jnp.logical_not(is_lat))
            def _():
                q_ref[rows, :] = acc.astype(BF16)

        @pl.when(j == n_q_tiles)
        def _():
            acc = jnp.dot(h_ref[rows, :], w_ref[...], preferred_element_type=F32)
            kv32_ref[rows, :] = acc
            kv_ref[rows, ATT_NKV:] = acc[:, ATT_NKV:].astype(BF16)

            @pl.when(is_lat)
            def _():
                for s, y in enumerate(roped(acc, rows, ATT_NKV)):
                    kv_ref[rows, s * 128:(s + 1) * 128] = y.astype(BF16)

            @pl.when(jnp.logical_not(is_lat))
            def _():
                kv_ref[rows, :ATT_NKV] = acc[:, :ATT_NKV].astype(BF16)


def _attn_proj(x, gain, mods_l, w_all, tabs):
    cos, slo, shi = tabs
    n_pos = LAT_L // PROJ_TM
    n_q_tiles = ATT_NQ // ATT_TN
    tab_spec = pl.BlockSpec((PROJ_TM, 128), lambda i, j: (i % n_pos, 0))
    return pl.pallas_call(
        _attn_proj_kernel,
        grid=(T_ALL // PROJ_TM, ATT_N // ATT_TN),
        in_specs=[
            pl.BlockSpec((PROJ_TM, D), lambda i, j: (i, 0)),
            pl.BlockSpec((1, D), lambda i, j: (0, 0)),
            pl.BlockSpec((1, N_MOD, D), lambda i, j: (_mod_row(i, PROJ_TM), 0, 0)),
            pl.BlockSpec((D, ATT_TN), lambda i, j: (0, j)),
            tab_spec, tab_spec, tab_spec,
        ],
        out_specs=[
            pl.BlockSpec((PROJ_TM, ATT_TN), lambda i, j: (i, jnp.minimum(j, n_q_tiles - 1))),
            pl.BlockSpec((PROJ_TM, 2 * ATT_NKV), lambda i, j: (i, 0)),
            pl.BlockSpec((PROJ_TM, 2 * ATT_NKV), lambda i, j: (i, 0)),
        ],
        out_shape=[
            jax.ShapeDtypeStruct((T_ALL, ATT_NQ), BF16),
            jax.ShapeDtypeStruct((T_ALL, 2 * ATT_NKV), BF16),
            jax.ShapeDtypeStruct((T_ALL, 2 * ATT_NKV), F32),
        ],
        scratch_shapes=[pltpu.VMEM((PROJ_TM, D), BF16)],
        compiler_params=_cparams(2),
        name="attn_proj",
    )(x, gain, mods_l, w_all, cos, slo, shi)


ATT_SCALE = AHD ** -0.5


def _softmax_pv(s_list, v_list, sink):
    m = sink
    for s in s_list:
        m = jnp.maximum(m, jnp.max(s, axis=-1, keepdims=True))
    denom = jnp.exp(sink - m)
    acc = None
    for s, v in zip(s_list, v_list):
        p = jnp.exp(s - m)
        denom = denom + jnp.sum(p, axis=-1, keepdims=True)
        pv = jnp.dot(p.astype(BF16), v, preferred_element_type=F32)
        acc = pv if acc is None else acc + pv
    return acc / denom


def _sink_col(sink_ref, kh, rows_per_head):
    return jnp.concatenate(
        [jnp.full((rows_per_head, 1), sink_ref[kh * AG + g], F32) for g in range(AG)], axis=0)


def _ctx_attn_kernel(sink_ref, q_ref, kv_ref, o_ref):
    for kh in range(AKV):
        k = kv_ref[:, kh * AHD:(kh + 1) * AHD]
        v = kv_ref[:, ATT_NKV + kh * AHD:ATT_NKV + (kh + 1) * AHD]
        q4 = jnp.concatenate(
            [q_ref[:, (kh * AG + g) * AHD:(kh * AG + g + 1) * AHD] for g in range(AG)], axis=0)
        s = lax.dot_general(q4, k, (((1,), (1,)), ((), ())), preferred_element_type=F32) * ATT_SCALE
        o = _softmax_pv([s], [v], _sink_col(sink_ref, kh, CTX_L))
        for g in range(AG):
            hh = kh * AG + g
            o_ref[:, hh * AHD:(hh + 1) * AHD] = o[g * CTX_L:(g + 1) * CTX_L, :].astype(BF16)


def _ctx_attn(sink, q, kv, o_prev):
    return pl.pallas_call(
        lambda sink_ref, q_ref, kv_ref, prev_ref, o_ref: _ctx_attn_kernel(sink_ref, q_ref, kv_ref, o_ref),
        grid=(CTX_B,),
        in_specs=[
            pl.BlockSpec(memory_space=pltpu.SMEM),
            pl.BlockSpec((CTX_L, ATT_NQ), lambda b: (b, 0)),
            pl.BlockSpec((CTX_L, 2 * ATT_NKV), lambda b: (b, 0)),
            pl.BlockSpec(memory_space=pl.ANY),
        ],
        out_specs=pl.BlockSpec((CTX_L, ATT_NQ), lambda b: (b, 0)),
        out_shape=jax.ShapeDtypeStruct((T_ALL, ATT_NQ), BF16),
        input_output_aliases={3: 0},
        compiler_params=_cparams(1),
        name="ctx_attn",
    )(sink, q, kv, o_prev)


BAND = 3 * ABLK


def _lat_attn_kernel(sink_ref, q_ref, kv_ref, ck_ref, cv_ref, o_ref):
    n = pl.program_id(1)
    start = pl.multiple_of(jnp.clip((n - 1) * ABLK, 0, LAT_L - BAND), ABLK)
    qpos = n * ABLK + (lax.broadcasted_iota(jnp.int32, (AG * ABLK, BAND), 0) & (ABLK - 1))
    kpos = start + lax.broadcasted_iota(jnp.int32, (AG * ABLK, BAND), 1)
    valid = jnp.abs(qpos - kpos) <= WINDOW
    for kh in range(AKV):
        kcols = slice(kh * AHD, (kh + 1) * AHD)
        vcols = slice(ATT_NKV + kh * AHD, ATT_NKV + (kh + 1) * AHD)
        kb = kv_ref[pl.ds(start, BAND), kcols]
        vb = kv_ref[pl.ds(start, BAND), vcols]
        kc = ck_ref[0, :, kcols].astype(BF16)
        vc = cv_ref[0, :, kcols].astype(BF16)
        q4 = jnp.concatenate(
            [q_ref[:, (kh * AG + g) * AHD:(kh * AG + g + 1) * AHD] for g in range(AG)], axis=0)
        nt = (((1,), (1,)), ((), ()))
        s_band = lax.dot_general(q4, kb, nt, preferred_element_type=F32) * ATT_SCALE
        s_band = jnp.where(valid, s_band, NEG_INF)
        s_ctx = lax.dot_general(q4, kc, nt, preferred_element_type=F32) * ATT_SCALE
        o = _softmax_pv([s_band, s_ctx], [vb, vc], _sink_col(sink_ref, kh, ABLK))
        for g in range(AG):
            hh = kh * AG + g
            o_ref[:, hh * AHD:(hh + 1) * AHD] = o[g * ABLK:(g + 1) * ABLK, :].astype(BF16)


def _lat_attn(sink, q, kv, cache_k, cache_v, o_prev):
    n_q = LAT_L // ABLK
    rb0 = T_CTX // ABLK
    sb0 = T_CTX // LAT_L
    return pl.pallas_call(
        lambda sink_ref, q_ref, kv_ref, ck_ref, cv_ref, prev_ref, o_ref: _lat_attn_kernel(
            sink_ref, q_ref, kv_ref, ck_ref, cv_ref, o_ref),
        grid=(LAT_B, n_q),
        in_specs=[
            pl.BlockSpec(memory_space=pltpu.SMEM),
            pl.BlockSpec((ABLK, ATT_NQ), lambda b, n: (rb0 + b * n_q + n, 0)),
            pl.BlockSpec((LAT_L, 2 * ATT_NKV), lambda b, n: (sb0 + b, 0)),
            pl.BlockSpec((1, CTX_L, ATT_NKV), lambda b, n: (b, 0, 0)),
            pl.BlockSpec((1, CTX_L, ATT_NKV), lambda b, n: (b, 0, 0)),
            pl.BlockSpec(memory_space=pl.ANY),
        ],
        out_specs=pl.BlockSpec((ABLK, ATT_NQ), lambda b, n: (rb0 + b * n_q + n, 0)),
        out_shape=jax.ShapeDtypeStruct((T_ALL, ATT_NQ), BF16),
        input_output_aliases={5: 0},
        compiler_params=_cparams(2),
        name="lat_attn",
    )(sink, q, kv, cache_k, cache_v, o_prev)


def _log_sigmoid(x):
    return -(jnp.maximum(-x, 0.0) + jnp.log1p(jnp.exp(-jnp.abs(x))))


def kernel(x_prompt, x_sample, state_ret_fwd, state_ret_bwd, cache_attn_k, cache_attn_v, c, c_ctx, ada_w, ada_b, norm_mix, norm_ffn, ret_wq, ret_wk, ret_wv, ret_wg_fwd, ret_wg_bwd, ret_wo, ret_decay_fwd, ret_decay_bwd, attn_wq, attn_wk, attn_wv, attn_wo, attn_sink, ffn_w_in, ffn_conv_w, ffn_conv_b, ffn_w_out, norm_final):
    x = jnp.concatenate([x_prompt.reshape(T_CTX, D), x_sample.reshape(T_LAT, D)], axis=0)

    cv = jnp.concatenate([c, c_ctx[None, :], jnp.zeros((MOD_ROWS - LAT_B - 1, D), F32)], axis=0)
    mods = _adaln(cv, ada_w, ada_b).reshape(ada_w.shape[0], MOD_ROWS, N_MOD, D)

    w_ret = jnp.concatenate([ret_wq[0], ret_wk[0], ret_wv[0], ret_wg_fwd[0], ret_wg_bwd[0]], axis=1).astype(BF16)
    qkvg = _ret_proj(x, norm_mix[0:1], mods[0], w_ret, _rope_tables(RDK))
    lg = jnp.stack([_log_sigmoid(ret_decay_fwd[0].astype(F32)), _log_sigmoid(ret_decay_bwd[0].astype(F32))])
    y0 = jnp.zeros((T_ALL, RH * RDV), BF16)
    y0, s_f, s_b = _ret_scan(qkvg, lg, y0, n_blocks=T_CTX // RET_BLOCK, n_seq=RET_BLOCK // CTX_L,
                             seq_len=CTX_L, row_block0=0, emit_state=True)
    (y0,) = _ret_scan(qkvg, lg, y0, n_blocks=LAT_B, n_seq=1, seq_len=LAT_L, row_block0=T_CTX // RET_BLOCK,
                      s0f=state_ret_fwd, s0b=state_ret_bwd)
    x = _res_matmul(y0, ret_wo[0].astype(BF16), x, mods[0], 2)
    act = _ffn_up(x, norm_ffn[0:1], mods[0], ffn_w_in[0].astype(BF16), ffn_conv_w[0], ffn_conv_b[0:1])
    x = _res_matmul(act, ffn_w_out[0].astype(BF16), x, mods[0], 5)

    w_att = jnp.concatenate([attn_wq[0], attn_wk[0], attn_wv[0]], axis=1).astype(BF16)
    cos, slo, shi = _rope_tables(AHD)
    tabs = tuple(jnp.tile(t, (1, 128 // AHD)) for t in (cos, slo, shi))
    q, kv, kv32 = _attn_proj(x, norm_mix[1:2], mods[1], w_att, tabs)
    sink = attn_sink[0].astype(F32)
    a = _lat_attn(sink, q, kv, cache_attn_k[:, 0].reshape(LAT_B, CTX_L, ATT_NKV),
                  cache_attn_v[:, 0].reshape(LAT_B, CTX_L, ATT_NKV), jnp.zeros((T_ALL, ATT_NQ), BF16))
    a = _ctx_attn(sink, q, kv, a)
    x = _res_matmul(a, attn_wo[0].astype(BF16), x, mods[1], 2)
    act = _ffn_up(x, norm_ffn[1:2], mods[1], ffn_w_in[1].astype(BF16), ffn_conv_w[1], ffn_conv_b[1:2])
    y = _res_matmul(act, ffn_w_out[1].astype(BF16), x, mods[1], 5, final_gain=norm_final[None, :])

    y_prompt = y[:T_CTX].reshape(CTX_B, CTX_L, D)
    y_sample = y[T_CTX:].reshape(LAT_B, LAT_L, D)
    new_k = kv32[:T_CTX, :ATT_NKV].reshape(CTX_B, 1, CTX_L, AKV, AHD)
    new_v = kv32[:T_CTX, ATT_NKV:].reshape(CTX_B, 1, CTX_L, AKV, AHD)
    return y_prompt, y_sample, s_f, s_b, new_k, new_v
```

```python
import functools

import jax
import jax.numpy as jnp
from jax import lax
from jax.experimental import pallas as pl
from jax.experimental.pallas import tpu as pltpu

D = 1024
CTX_B, CTX_L = 32, 256
LAT_B, LAT_L = 8, 2048
T_CTX = CTX_B * CTX_L
T_LAT = LAT_B * LAT_L
T_ALL = T_CTX + T_LAT
GRID_W = 64
N_MOD = 6
MOD_ROWS = 16
CTX_MOD_ROW = LAT_B
RH, RDK, RDV, RC = 4, 256, 512, 128
AH, AKV, AG, AHD = 16, 4, 4, 64
WINDOW = 128
ABLK = 128
D_FF = 2816
ROPE_BASE = 10000.0
EPS = 1e-6
NEG_INF = -1e30

BF16 = jnp.bfloat16
F32 = jnp.float32

VMEM_LIMIT_BYTES = 56 * 1024 * 1024
NORM_ROWS = 32


def _cparams(n_axes):
    return pltpu.CompilerParams(dimension_semantics=("arbitrary",) * n_axes,
                                vmem_limit_bytes=VMEM_LIMIT_BYTES)


def _silu(x):
    return x / (1.0 + jnp.exp(-x))


def _mod_row(i, tm):
    r0 = i * tm
    return jnp.where(r0 < T_CTX, CTX_MOD_ROW, (r0 - T_CTX) // LAT_L)


def _modulate(x, g_ref, mod_ref, shift_k, scale_k):
    ms = jnp.mean(x * x, axis=-1, keepdims=True)
    y = x * lax.rsqrt(ms + EPS) * g_ref[...]
    return y * (1.0 + mod_ref[0, scale_k:scale_k + 1, :]) + mod_ref[0, shift_k:shift_k + 1, :]


MOD_TM = 512
N_CTX_BLK = T_CTX // MOD_TM


def _ctx_blk(i):
    return jnp.minimum(i, N_CTX_BLK - 1)


def _lat_blk(i):
    return jnp.maximum(i - N_CTX_BLK, 0)


def _first_mod_kernel(xc_ref, xl_ref, g_ref, mod_ref, h_ref):
    i = pl.program_id(0)

    def run(x_ref):
        for r in range(0, MOD_TM, NORM_ROWS):
            rows = slice(r, r + NORM_ROWS)
            h_ref[rows, :] = _modulate(x_ref[rows, :], g_ref, mod_ref, 0, 1).astype(BF16)

    @pl.when(i < N_CTX_BLK)
    def _():
        run(xc_ref)

    @pl.when(i >= N_CTX_BLK)
    def _():
        run(xl_ref)


def _first_mod(x_ctx, x_lat, gain, mods_l):
    return pl.pallas_call(
        _first_mod_kernel,
        grid=(T_ALL // MOD_TM,),
        in_specs=[
            pl.BlockSpec((MOD_TM, D), lambda i: (_ctx_blk(i), 0)),
            pl.BlockSpec((MOD_TM, D), lambda i: (_lat_blk(i), 0)),
            pl.BlockSpec((1, D), lambda i: (0, 0)),
            pl.BlockSpec((1, N_MOD, D), lambda i: (_mod_row(i, MOD_TM), 0, 0)),
        ],
        out_specs=pl.BlockSpec((MOD_TM, D), lambda i: (i, 0)),
        out_shape=jax.ShapeDtypeStruct((T_ALL, D), BF16),
        compiler_params=_cparams(1),
        name="first_mod",
    )(x_ctx, x_lat, gain, mods_l)


ADA_TN = 512


def _adaln_kernel(cv_ref, w_ref, b_ref, o_ref):
    s = _silu(cv_ref[...]).astype(BF16)
    w = w_ref[0].astype(BF16)
    o_ref[0] = jnp.dot(s, w, preferred_element_type=F32) + b_ref[0]


def _adaln(cv, ada_w, ada_b):
    depth = ada_w.shape[0]
    n = N_MOD * D
    return pl.pallas_call(
        _adaln_kernel,
        grid=(depth, n // ADA_TN),
        in_specs=[
            pl.BlockSpec((MOD_ROWS, D), lambda l, j: (0, 0)),
            pl.BlockSpec((1, D, ADA_TN), lambda l, j: (l, 0, j)),
            pl.BlockSpec((1, 1, ADA_TN), lambda l, j: (l, 0, j)),
        ],
        out_specs=pl.BlockSpec((1, MOD_ROWS, ADA_TN), lambda l, j: (l, 0, j)),
        out_shape=jax.ShapeDtypeStruct((depth, MOD_ROWS, n), F32),
        compiler_params=_cparams(2),
        name="adaln",
    )(cv, ada_w, ada_b.reshape(depth, 1, n))


def _rope_tables(head_dim):
    half = head_dim // 2
    n_freq = half // 2
    inv = ROPE_BASE ** (-jnp.arange(n_freq, dtype=F32) / n_freq)
    t = jnp.arange(LAT_L)
    rows = (t // GRID_W).astype(F32)[:, None] * inv[None, :]
    cols = (t % GRID_W).astype(F32)[:, None] * inv[None, :]
    zero = jnp.zeros_like(rows)
    cos = jnp.concatenate([jnp.cos(rows)] * 2 + [jnp.cos(cols)] * 2, axis=-1)
    sin_lo = jnp.concatenate([-jnp.sin(rows), zero, -jnp.sin(cols), zero], axis=-1)
    sin_hi = jnp.concatenate([zero, jnp.sin(rows), zero, jnp.sin(cols)], axis=-1)
    return cos, sin_lo, sin_hi


def _rope_slab(x, cos, sin_lo, sin_hi, n_freq):
    up = pltpu.roll(x, 128 - n_freq, 1)
    dn = pltpu.roll(x, n_freq, 1)
    return x * cos + up * sin_lo + dn * sin_hi


PROJ_TM = 1024
PROJ_MC = 256
RET_TN = 1024
RET_N = 2 * RH * RDK + 3 * RH * RDV


def _ret_proj_kernel(h_ref, w_ref, cos_ref, slo_ref, shi_ref, o_ref):
    i = pl.program_id(0)
    j = pl.program_id(1)
    is_lat = i * PROJ_TM >= T_CTX

    def chunks(epilogue):
        for m in range(PROJ_TM // PROJ_MC):
            rows = slice(m * PROJ_MC, (m + 1) * PROJ_MC)
            acc = jnp.dot(h_ref[rows, :], w_ref[...], preferred_element_type=F32)
            epilogue(acc, rows)

    def plain(acc, rows):
        o_ref[rows, :] = acc.astype(BF16)

    def gate(acc, rows):
        o_ref[rows, :] = _silu(acc).astype(BF16)

    def rope(scale):
        def ep(acc, rows):
            for s in range(RET_TN // 128):
                cols = slice(s * 128, (s + 1) * 128)
                tcols = slice((s % 2) * 128, (s % 2 + 1) * 128)
                x = acc[:, cols]
                if scale != 1.0:
                    x = x * scale
                y = _rope_slab(x, cos_ref[rows, tcols], slo_ref[rows, tcols], shi_ref[rows, tcols], RDK // 4)
                o_ref[rows, cols] = y.astype(BF16)
        return ep

    def scaled(scale):
        def ep(acc, rows):
            o_ref[rows, :] = (acc * scale).astype(BF16)
        return ep

    k_scale = RDK ** -0.5

    @pl.when(jnp.logical_and(j == 0, is_lat))
    def _():
        chunks(rope(1.0))

    @pl.when(jnp.logical_and(j == 0, jnp.logical_not(is_lat)))
    def _():
        chunks(plain)

    @pl.when(jnp.logical_and(j == 1, is_lat))
    def _():
        chunks(rope(k_scale))

    @pl.when(jnp.logical_and(j == 1, jnp.logical_not(is_lat)))
    def _():
        chunks(scaled(k_scale))

    @pl.when(jnp.logical_and(j >= 2, j < 4))
    def _():
        chunks(plain)

    @pl.when(j >= 4)
    def _():
        chunks(gate)


def _ret_proj(h, w_all, tabs):
    cos, slo, shi = tabs
    n_pos = LAT_L // PROJ_TM
    tab_spec = pl.BlockSpec((PROJ_TM, RDK), lambda i, j: (i % n_pos, 0))
    return pl.pallas_call(
        _ret_proj_kernel,
        grid=(T_ALL // PROJ_TM, RET_N // RET_TN),
        in_specs=[
            pl.BlockSpec((PROJ_TM, D), lambda i, j: (i, 0)),
            pl.BlockSpec((D, RET_TN), lambda i, j: (0, j)),
            tab_spec, tab_spec, tab_spec,
        ],
        out_specs=pl.BlockSpec((PROJ_TM, RET_TN), lambda i, j: (i, j)),
        out_shape=jax.ShapeDtypeStruct((T_ALL, RET_N), BF16),
        compiler_params=_cparams(2),
        name="ret_proj",
    )(h, w_all, cos, slo, shi)


RET_BLOCK = 2048


def _ret_scan_kernel(n_seq, seq_len, has_init, emit_state, lg_ref, q_ref, k_ref, v_ref, gf_ref, gb_ref, *rest):
    rest = list(rest)
    s0f_ref = rest.pop(0) if has_init else None
    s0b_ref = rest.pop(0) if has_init else None
    y_ref = rest.pop(0)
    sf_ref = rest.pop(0) if emit_state else None
    sb_ref = rest.pop(0) if emit_state else None
    ybuf_ref, st_ref = rest

    hd = pl.program_id(1)
    lg_f = lg_ref[0, hd]
    lg_b = lg_ref[1, hd]
    n_chunks = seq_len // RC

    ii = lax.broadcasted_iota(jnp.int32, (RC, RC), 0)
    jj = lax.broadcasted_iota(jnp.int32, (RC, RC), 1)
    diff = (ii - jj).astype(F32)
    dmat_f = jnp.where(diff >= 0, jnp.exp(lg_f * jnp.maximum(diff, 0.0)), 0.0)
    dmat_b = jnp.where(diff <= 0, jnp.exp(lg_b * jnp.maximum(-diff, 0.0)), 0.0)
    idx = lax.broadcasted_iota(jnp.int32, (RC, 1), 0).astype(F32)
    xi_f = jnp.exp(lg_f * (idx + 1.0))
    zeta_f = jnp.exp(lg_f * (RC - 1.0 - idx))
    xi_b = jnp.exp(lg_b * (RC - idx))
    zeta_b = jnp.exp(lg_b * idx)
    ones_row = jnp.ones((1, RDV), F32)
    gch_f = jnp.exp(ones_row * (lg_f * RC))
    gch_b = jnp.exp(ones_row * (lg_b * RC))

    def chunk_out(r0, dmat, xi, zeta, gch):
        q = q_ref[pl.ds(r0, RC), :]
        k = k_ref[pl.ds(r0, RC), :]
        v = v_ref[pl.ds(r0, RC), :]
        s = lax.dot_general(q, k, (((1,), (1,)), ((), ())), preferred_element_type=F32)
        inner = jnp.dot((s * dmat).astype(BF16), v, preferred_element_type=F32)
        st = st_ref[...]
        cross = jnp.dot(q, st.astype(BF16), preferred_element_type=F32) * xi
        o = inner + cross
        kz = (k.astype(F32) * zeta).astype(BF16)
        upd = lax.dot_general(kz, v, (((0,), (0,)), ((), ())), preferred_element_type=F32)
        st_ref[...] = st * gch + upd
        return o * lax.rsqrt(jnp.mean(o * o, axis=-1, keepdims=True) + EPS)

    for sq in range(n_seq):
        base = sq * seq_len

        if has_init:
            st_ref[...] = s0f_ref[0, 0, 0]
        else:
            st_ref[...] = jnp.zeros((RDK, RDV), F32)

        def fwd(c, carry):
            r0 = pl.multiple_of(base + c * RC, RC)
            o = chunk_out(r0, dmat_f, xi_f, zeta_f, gch_f)
            ybuf_ref[pl.ds(r0, RC), :] = o * gf_ref[pl.ds(r0, RC), :].astype(F32)
            return carry

        lax.fori_loop(0, n_chunks, fwd, 0)
        if emit_state:
            sf_ref[sq, 0, 0] = st_ref[...]

        if has_init:
            st_ref[...] = s0b_ref[0, 0, 0]
        else:
            st_ref[...] = jnp.zeros((RDK, RDV), F32)

        def bwd(c, carry):
            r0 = pl.multiple_of(base + (n_chunks - 1 - c) * RC, RC)
            o = chunk_out(r0, dmat_b, xi_b, zeta_b, gch_b)
            y = ybuf_ref[pl.ds(r0, RC), :] + o * gb_ref[pl.ds(r0, RC), :].astype(F32)
            y_ref[pl.ds(r0, RC), :] = y.astype(BF16)
            return carry

        lax.fori_loop(0, n_chunks, bwd, 0)
        if emit_state:
            sb_ref[sq, 0, 0] = st_ref[...]


def _ret_scan(qkvg, lg, y_prev, *, n_blocks, n_seq, seq_len, row_block0, s0f=None, s0b=None, emit_state=False):
    has_init = s0f is not None
    kq = RDK
    n_qk = RH * RDK // kq
    in_specs = [
        pl.BlockSpec(memory_space=pltpu.SMEM),
        pl.BlockSpec((RET_BLOCK, RDK), lambda b, h: (row_block0 + b, h)),
        pl.BlockSpec((RET_BLOCK, RDK), lambda b, h: (row_block0 + b, n_qk + h)),
        pl.BlockSpec((RET_BLOCK, RDV), lambda b, h: (row_block0 + b, n_qk + h)),
        pl.BlockSpec((RET_BLOCK, RDV), lambda b, h: (row_block0 + b, n_qk + RH + h)),
        pl.BlockSpec((RET_BLOCK, RDV), lambda b, h: (row_block0 + b, n_qk + 2 * RH + h)),
    ]
    args = [lg, qkvg, qkvg, qkvg, qkvg, qkvg]
    if has_init:
        st_spec = pl.BlockSpec((1, 1, 1, RDK, RDV), lambda b, h: (b, 0, h, 0, 0))
        in_specs += [st_spec, st_spec]
        args += [s0f, s0b]
    in_specs.append(pl.BlockSpec(memory_space=pl.ANY))
    args.append(y_prev)
    y_shape = jax.ShapeDtypeStruct((T_ALL, RH * RDV), BF16)
    out_specs = [pl.BlockSpec((RET_BLOCK, RDV), lambda b, h: (row_block0 + b, h))]
    out_shape = [y_shape]
    if emit_state:
        so_spec = pl.BlockSpec((n_seq, 1, 1, RDK, RDV), lambda b, h: (b, 0, h, 0, 0))
        so_shape = jax.ShapeDtypeStruct((n_blocks * n_seq, 1, RH, RDK, RDV), F32)
        out_specs += [so_spec, so_spec]
        out_shape += [so_shape, so_shape]

    def body(*refs):
        refs = list(refs)
        n_in = len(in_specs)
        ins = refs[:n_in - 1]
        _ret_scan_kernel(n_seq, seq_len, has_init, emit_state, *ins, *refs[n_in:])

    return pl.pallas_call(
        body,
        grid=(n_blocks, RH),
        in_specs=in_specs,
        out_specs=out_specs,
        out_shape=out_shape,
        scratch_shapes=[pltpu.VMEM((RET_BLOCK, RDV), F32), pltpu.VMEM((RDK, RDV), F32)],
        input_output_aliases={len(args) - 1: 0},
        compiler_params=_cparams(2),
        name="ret_scan_ctx" if emit_state else "ret_scan_lat",
    )(*args)


RES_TM = MOD_TM
RES_MC = 256


def _res_kernel(gate_k, split_x, next_mod, a_ref, w_ref, *rest):
    rest = list(rest)
    x_refs = [rest.pop(0), rest.pop(0)] if split_x else [rest.pop(0)]
    mod_ref = rest.pop(0)
    gain_ref = rest.pop(0)
    if next_mod is not None:
        nmod_ref = rest.pop(0)
        xo_ref, h_ref = rest
    else:
        yc_ref, yl_ref = rest
    i = pl.program_id(0)
    is_ctx = i < N_CTX_BLK
    gate = mod_ref[0, gate_k:gate_k + 1, :]

    def run(x_ref, y_ref):
        for m in range(RES_TM // RES_MC):
            rows = slice(m * RES_MC, (m + 1) * RES_MC)
            acc = jnp.dot(a_ref[rows, :], w_ref[...], preferred_element_type=F32)
            xn = x_ref[rows, :] + gate * acc
            if next_mod is not None:
                xo_ref[rows, :] = xn
                h_ref[rows, :] = _modulate(xn, gain_ref, nmod_ref, *next_mod).astype(BF16)
            else:
                ms = jnp.mean(xn * xn, axis=-1, keepdims=True)
                y_ref[rows, :] = xn * lax.rsqrt(ms + EPS) * gain_ref[...]

    if not split_x and next_mod is not None:
        run(x_refs[0], None)
        return

    @pl.when(is_ctx)
    def _():
        run(x_refs[0], None if next_mod is not None else yc_ref)

    @pl.when(jnp.logical_not(is_ctx))
    def _():
        run(x_refs[-1], None if next_mod is not None else yl_ref)


def _res_matmul(a, w, x, mods_l, gate_k, gain, next_mods=None, next_mod=None):
    kdim = a.shape[1]
    split_x = isinstance(x, tuple)
    row_spec = pl.BlockSpec((RES_TM, D), lambda i: (i, 0))
    ctx_spec = pl.BlockSpec((RES_TM, D), lambda i: (_ctx_blk(i), 0))
    lat_spec = pl.BlockSpec((RES_TM, D), lambda i: (_lat_blk(i), 0))
    mod_spec = pl.BlockSpec((1, N_MOD, D), lambda i: (_mod_row(i, RES_TM), 0, 0))
    in_specs = [pl.BlockSpec((RES_TM, kdim), lambda i: (i, 0)), pl.BlockSpec((kdim, D), lambda i: (0, 0))]
    args = [a, w]
    if split_x:
        in_specs += [ctx_spec, lat_spec]
        args += list(x)
    else:
        in_specs.append(row_spec)
        args.append(x)
    in_specs += [mod_spec, pl.BlockSpec((1, D), lambda i: (0, 0))]
    args += [mods_l, gain]
    if next_mod is not None:
        in_specs.append(mod_spec)
        args.append(next_mods)
        out_specs = [row_spec, row_spec]
        out_shape = [jax.ShapeDtypeStruct((T_ALL, D), F32), jax.ShapeDtypeStruct((T_ALL, D), BF16)]
    else:
        out_specs = [ctx_spec, lat_spec]
        out_shape = [jax.ShapeDtypeStruct((T_CTX, D), F32), jax.ShapeDtypeStruct((T_LAT, D), F32)]
    return pl.pallas_call(
        functools.partial(_res_kernel, gate_k, split_x, next_mod),
        grid=(T_ALL // RES_TM,),
        in_specs=in_specs,
        out_specs=out_specs,
        out_shape=out_shape,
        compiler_params=_cparams(1),
        name="res_matmul_k%d%s" % (kdim, "_final" if next_mod is None else ""),
    )(*args)


FFN_TM = 2048
FFN_MC = 512
FFN_TN = 256
FFN_HALO = 16


def _ffn_up_kernel(h_ref, wu_ref, wg_ref, cwu_ref, cwg_ref, cbu_ref, cbg_ref, o_ref, a_scr, t_scr):
    is_ctx = pl.program_id(0) * FFN_TM < T_CTX
    zeros = jnp.zeros((FFN_HALO, D), BF16)
    n_slab = FFN_TN // 128

    def chunk(c, mc, isolated):
        slot = c % 2
        r0 = c * mc
        lo = zeros if (isolated or r0 == 0) else h_ref[r0 - FFN_HALO:r0, :]
        hi = zeros if (isolated or r0 + mc == FFN_TM) else h_ref[r0 + mc:r0 + mc + FFN_HALO, :]
        hm = jnp.concatenate([lo, h_ref[r0:r0 + mc, :], hi], axis=0)
        n_ext = mc + 2 * FFN_HALO
        half = mc // 2
        for k, w_ref in enumerate((wu_ref, wg_ref)):
            a = jnp.dot(hm, w_ref[...], preferred_element_type=F32)
            for s in range(n_slab):
                a_scr[slot, k * n_slab + s, 0:n_ext, :] = a[:, s * 128:(s + 1) * 128]

        def conv(slab, cw_ref, cb_ref, lanes):
            def rows(off):
                return a_scr[slot, slab, pl.ds(FFN_HALO + off, half, stride=2), :]
            w0, w1, w2 = cw_ref[0:1, lanes], cw_ref[1:2, lanes], cw_ref[2:3, lanes]
            b = cb_ref[:, lanes]
            om1, ev, od, ep1 = rows(-1), rows(0), rows(1), rows(2)
            return om1 * w0 + ev * w1 + od * w2 + b, ev * w0 + od * w1 + ep1 * w2 + b

        for s in range(n_slab):
            lanes = slice(s * 128, (s + 1) * 128)
            u_ev, u_od = conv(s, cwu_ref, cbu_ref, lanes)
            g_ev, g_od = conv(n_slab + s, cwg_ref, cbg_ref, lanes)
            t_scr[slot, s, pl.ds(0, half, stride=2), :] = _silu(g_ev) * u_ev
            t_scr[slot, s, pl.ds(1, half, stride=2), :] = _silu(g_od) * u_od
            o_ref[r0:r0 + mc, lanes] = t_scr[slot, s, 0:mc, :].astype(BF16)

    @pl.when(is_ctx)
    def _():
        for c in range(FFN_TM // CTX_L):
            chunk(c, CTX_L, True)

    @pl.when(jnp.logical_not(is_ctx))
    def _():
        for c in range(FFN_TM // FFN_MC):
            chunk(c, FFN_MC, False)


def _ffn_up(h, w_in, conv_w, conv_b):
    n_tiles = D_FF // FFN_TN
    n_slab = FFN_TN // 128
    return pl.pallas_call(
        _ffn_up_kernel,
        grid=(T_ALL // FFN_TM, n_tiles),
        in_specs=[
            pl.BlockSpec((FFN_TM, D), lambda i, j: (i, 0)),
            pl.BlockSpec((D, FFN_TN), lambda i, j: (0, j)),
            pl.BlockSpec((D, FFN_TN), lambda i, j: (0, n_tiles + j)),
            pl.BlockSpec((3, FFN_TN), lambda i, j: (0, j)),
            pl.BlockSpec((3, FFN_TN), lambda i, j: (0, n_tiles + j)),
            pl.BlockSpec((1, FFN_TN), lambda i, j: (0, j)),
            pl.BlockSpec((1, FFN_TN), lambda i, j: (0, n_tiles + j)),
        ],
        out_specs=pl.BlockSpec((FFN_TM, FFN_TN), lambda i, j: (i, j)),
        out_shape=jax.ShapeDtypeStruct((T_ALL, D_FF), BF16),
        scratch_shapes=[
            pltpu.VMEM((2, 2 * n_slab, FFN_MC + 2 * FFN_HALO, 128), F32),
            pltpu.VMEM((2, n_slab, FFN_MC, 128), F32),
        ],
        compiler_params=_cparams(2),
        name="ffn_up",
    )(h, w_in, w_in, conv_w, conv_w, conv_b, conv_b)


ATT_TN = 512
ATT_NQ = AH * AHD
ATT_NKV = AKV * AHD
ATT_N = ATT_NQ + 2 * ATT_NKV


def _attn_proj_kernel(h_ref, w_ref, cos_ref, slo_ref, shi_ref, q_ref, kv_ref, kv32_ref):
    i = pl.program_id(0)
    j = pl.program_id(1)
    is_lat = i * PROJ_TM >= T_CTX
    is_q = j < ATT_NQ // ATT_TN

    def chunks(epilogue):
        for m in range(PROJ_TM // PROJ_MC):
            rows = slice(m * PROJ_MC, (m + 1) * PROJ_MC)
            epilogue(jnp.dot(h_ref[rows, :], w_ref[...], preferred_element_type=F32), rows)

    def rope_into(o_ref, acc, rows, n_cols):
        for s in range(n_cols // 128):
            cols = slice(s * 128, (s + 1) * 128)
            y = _rope_slab(acc[:, cols], cos_ref[rows, :], slo_ref[rows, :], shi_ref[rows, :], AHD // 4)
            o_ref[rows, cols] = y.astype(BF16)

    def q_lat(acc, rows):
        rope_into(q_ref, acc, rows, ATT_TN)

    def q_ctx(acc, rows):
        q_ref[rows, :] = acc.astype(BF16)

    def kv_lat(acc, rows):
        rope_into(kv_ref, acc, rows, ATT_NKV)
        kv_ref[rows, ATT_NKV:] = acc[:, ATT_NKV:].astype(BF16)

    def kv_ctx(acc, rows):
        kv32_ref[rows, :] = acc
        kv_ref[rows, :] = acc.astype(BF16)

    for on_q, on_lat, epilogue in ((True, True, q_lat), (True, False, q_ctx),
                                   (False, True, kv_lat), (False, False, kv_ctx)):
        @pl.when(jnp.logical_and(is_q == on_q, is_lat == on_lat))
        def _():
            chunks(epilogue)


def _attn_proj(h, w_all, tabs):
    cos, slo, shi = tabs
    n_pos = LAT_L // PROJ_TM
    n_q_tiles = ATT_NQ // ATT_TN
    n_ctx = T_CTX // PROJ_TM
    tab_spec = pl.BlockSpec((PROJ_TM, 128), lambda i, j: (i % n_pos, 0))
    return pl.pallas_call(
        _attn_proj_kernel,
        grid=(T_ALL // PROJ_TM, ATT_N // ATT_TN),
        in_specs=[
            pl.BlockSpec((PROJ_TM, D), lambda i, j: (i, 0)),
            pl.BlockSpec((D, ATT_TN), lambda i, j: (0, j)),
            tab_spec, tab_spec, tab_spec,
        ],
        out_specs=[
            pl.BlockSpec((PROJ_TM, ATT_TN), lambda i, j: (i, jnp.minimum(j, n_q_tiles - 1))),
            pl.BlockSpec((PROJ_TM, 2 * ATT_NKV), lambda i, j: (i, 0)),
            pl.BlockSpec((PROJ_TM, 2 * ATT_NKV), lambda i, j: (jnp.minimum(i, n_ctx - 1), 0)),
        ],
        out_shape=[
            jax.ShapeDtypeStruct((T_ALL, ATT_NQ), BF16),
            jax.ShapeDtypeStruct((T_ALL, 2 * ATT_NKV), BF16),
            jax.ShapeDtypeStruct((T_CTX, 2 * ATT_NKV), F32),
        ],
        compiler_params=_cparams(2),
        name="attn_proj",
    )(h, w_all, cos, slo, shi)


ATT_SCALE = AHD ** -0.5


def _softmax_pv(s_list, v_list, sink):
    m = sink
    for s in s_list:
        m = jnp.maximum(m, jnp.max(s, axis=-1, keepdims=True))
    denom = jnp.exp(sink - m)
    acc = None
    for s, v in zip(s_list, v_list):
        p = jnp.exp(s - m)
        denom = denom + jnp.sum(p, axis=-1, keepdims=True)
        pv = jnp.dot(p.astype(BF16), v, preferred_element_type=F32)
        acc = pv if acc is None else acc + pv
    return acc / denom


def _sink_col(sink_ref, kh, rows_per_head):
    return jnp.concatenate(
        [jnp.full((rows_per_head, 1), sink_ref[kh * AG + g], F32) for g in range(AG)], axis=0)


def _ctx_attn_kernel(sink_ref, q_ref, kv_ref, o_ref):
    for kh in range(AKV):
        k = kv_ref[:, kh * AHD:(kh + 1) * AHD]
        v = kv_ref[:, ATT_NKV + kh * AHD:ATT_NKV + (kh + 1) * AHD]
        q4 = jnp.concatenate(
            [q_ref[:, (kh * AG + g) * AHD:(kh * AG + g + 1) * AHD] for g in range(AG)], axis=0)
        s = lax.dot_general(q4, k, (((1,), (1,)), ((), ())), preferred_element_type=F32) * ATT_SCALE
        o = _softmax_pv([s], [v], _sink_col(sink_ref, kh, CTX_L))
        for g in range(AG):
            hh = kh * AG + g
            o_ref[:, hh * AHD:(hh + 1) * AHD] = o[g * CTX_L:(g + 1) * CTX_L, :].astype(BF16)


def _ctx_attn(sink, q, kv, o_prev):
    return pl.pallas_call(
        lambda sink_ref, q_ref, kv_ref, prev_ref, o_ref: _ctx_attn_kernel(sink_ref, q_ref, kv_ref, o_ref),
        grid=(CTX_B,),
        in_specs=[
            pl.BlockSpec(memory_space=pltpu.SMEM),
            pl.BlockSpec((CTX_L, ATT_NQ), lambda b: (b, 0)),
            pl.BlockSpec((CTX_L, 2 * ATT_NKV), lambda b: (b, 0)),
            pl.BlockSpec(memory_space=pl.ANY),
        ],
        out_specs=pl.BlockSpec((CTX_L, ATT_NQ), lambda b: (b, 0)),
        out_shape=jax.ShapeDtypeStruct((T_ALL, ATT_NQ), BF16),
        input_output_aliases={3: 0},
        compiler_params=_cparams(1),
        name="ctx_attn",
    )(sink, q, kv, o_prev)


BAND = 3 * ABLK


def _lat_attn_kernel(sink_ref, q_ref, kv_ref, ck_ref, cv_ref, o_ref):
    n = pl.program_id(1)
    start = pl.multiple_of(jnp.clip((n - 1) * ABLK, 0, LAT_L - BAND), ABLK)
    qpos = n * ABLK + (lax.broadcasted_iota(jnp.int32, (AG * ABLK, BAND), 0) & (ABLK - 1))
    kpos = start + lax.broadcasted_iota(jnp.int32, (AG * ABLK, BAND), 1)
    valid = jnp.abs(qpos - kpos) <= WINDOW
    for kh in range(AKV):
        kcols = slice(kh * AHD, (kh + 1) * AHD)
        vcols = slice(ATT_NKV + kh * AHD, ATT_NKV + (kh + 1) * AHD)
        kb = kv_ref[pl.ds(start, BAND), kcols]
        vb = kv_ref[pl.ds(start, BAND), vcols]
        kc = ck_ref[0, :, kcols].astype(BF16)
        vc = cv_ref[0, :, kcols].astype(BF16)
        q4 = jnp.concatenate(
            [q_ref[:, (kh * AG + g) * AHD:(kh * AG + g + 1) * AHD] for g in range(AG)], axis=0)
        nt = (((1,), (1,)), ((), ()))
        s_band = lax.dot_general(q4, kb, nt, preferred_element_type=F32) * ATT_SCALE
        s_band = jnp.where(valid, s_band, NEG_INF)
        s_ctx = lax.dot_general(q4, kc, nt, preferred_element_type=F32) * ATT_SCALE
        o = _softmax_pv([s_band, s_ctx], [vb, vc], _sink_col(sink_ref, kh, ABLK))
        for g in range(AG):
            hh = kh * AG + g
            o_ref[:, hh * AHD:(hh + 1) * AHD] = o[g * ABLK:(g + 1) * ABLK, :].astype(BF16)


def _lat_attn(sink, q, kv, cache_k, cache_v, o_prev):
    n_q = LAT_L // ABLK
    rb0 = T_CTX // ABLK
    sb0 = T_CTX // LAT_L
    return pl.pallas_call(
        lambda sink_ref, q_ref, kv_ref, ck_ref, cv_ref, prev_ref, o_ref: _lat_attn_kernel(
            sink_ref, q_ref, kv_ref, ck_ref, cv_ref, o_ref),
        grid=(LAT_B, n_q),
        in_specs=[
            pl.BlockSpec(memory_space=pltpu.SMEM),
            pl.BlockSpec((ABLK, ATT_NQ), lambda b, n: (rb0 + b * n_q + n, 0)),
            pl.BlockSpec((LAT_L, 2 * ATT_NKV), lambda b, n: (sb0 + b, 0)),
            pl.BlockSpec((1, CTX_L, ATT_NKV), lambda b, n: (b, 0, 0)),
            pl.BlockSpec((1, CTX_L, ATT_NKV), lambda b, n: (b, 0, 0)),
            pl.BlockSpec(memory_space=pl.ANY),
        ],
        out_specs=pl.BlockSpec((ABLK, ATT_NQ), lambda b, n: (rb0 + b * n_q + n, 0)),
        out_shape=jax.ShapeDtypeStruct((T_ALL, ATT_NQ), BF16),
        input_output_aliases={5: 0},
        compiler_params=_cparams(2),
        name="lat_attn",
    )(sink, q, kv, cache_k, cache_v, o_prev)


def _log_sigmoid(x):
    return -(jnp.maximum(-x, 0.0) + jnp.log1p(jnp.exp(-jnp.abs(x))))


def kernel(x_prompt, x_sample, state_ret_fwd, state_ret_bwd, cache_attn_k, cache_attn_v, c, c_ctx, ada_w, ada_b, norm_mix, norm_ffn, ret_wq, ret_wk, ret_wv, ret_wg_fwd, ret_wg_bwd, ret_wo, ret_decay_fwd, ret_decay_bwd, attn_wq, attn_wk, attn_wv, attn_wo, attn_sink, ffn_w_in, ffn_conv_w, ffn_conv_b, ffn_w_out, norm_final):
    x_ctx = x_prompt.reshape(T_CTX, D)
    x_lat = x_sample.reshape(T_LAT, D)

    cv = jnp.concatenate([c, c_ctx[None, :], jnp.zeros((MOD_ROWS - LAT_B - 1, D), F32)], axis=0)
    mods = _adaln(cv, ada_w, ada_b).reshape(ada_w.shape[0], MOD_ROWS, N_MOD, D)

    h = _first_mod(x_ctx, x_lat, norm_mix[0:1], mods[0])
    w_ret = jnp.concatenate([ret_wq[0], ret_wk[0], ret_wv[0], ret_wg_fwd[0], ret_wg_bwd[0]], axis=1).astype(BF16)
    qkvg = _ret_proj(h, w_ret, _rope_tables(RDK))
    lg = jnp.stack([_log_sigmoid(ret_decay_fwd[0].astype(F32)), _log_sigmoid(ret_decay_bwd[0].astype(F32))])
    y0 = jnp.zeros((T_ALL, RH * RDV), BF16)
    y0, s_f, s_b = _ret_scan(qkvg, lg, y0, n_blocks=T_CTX // RET_BLOCK, n_seq=RET_BLOCK // CTX_L,
                             seq_len=CTX_L, row_block0=0, emit_state=True)
    (y0,) = _ret_scan(qkvg, lg, y0, n_blocks=LAT_B, n_seq=1, seq_len=LAT_L, row_block0=T_CTX // RET_BLOCK,
                      s0f=state_ret_fwd, s0b=state_ret_bwd)
    x, h = _res_matmul(y0, ret_wo[0].astype(BF16), (x_ctx, x_lat), mods[0], 2,
                       norm_ffn[0:1], mods[0], (3, 4))
    act = _ffn_up(h, ffn_w_in[0].astype(BF16), ffn_conv_w[0], ffn_conv_b[0:1])
    x, h = _res_matmul(act, ffn_w_out[0].astype(BF16), x, mods[0], 5, norm_mix[1:2], mods[1], (0, 1))

    w_att = jnp.concatenate([attn_wq[0], attn_wk[0], attn_wv[0]], axis=1).astype(BF16)
    cos, slo, shi = _rope_tables(AHD)
    tabs = tuple(jnp.tile(t, (1, 128 // AHD)) for t in (cos, slo, shi))
    q, kv, kv32 = _attn_proj(h, w_att, tabs)
    sink = attn_sink[0].astype(F32)
    a = _lat_attn(sink, q, kv, cache_attn_k[:, 0].reshape(LAT_B, CTX_L, ATT_NKV),
                  cache_attn_v[:, 0].reshape(LAT_B, CTX_L, ATT_NKV), jnp.zeros((T_ALL, ATT_NQ), BF16))
    a = _ctx_attn(sink, q, kv, a)
    x, h = _res_matmul(a, attn_wo[0].astype(BF16), x, mods[1], 2, norm_ffn[1:2], mods[1], (3, 4))
    act = _ffn_up(h, ffn_w_in[1].astype(BF16), ffn_conv_w[1], ffn_conv_b[1:2])
    y_ctx, y_lat = _res_matmul(act, ffn_w_out[1].astype(BF16), x, mods[1], 5, norm_final[None, :])

    y_prompt = y_ctx.reshape(CTX_B, CTX_L, D)
    y_sample = y_lat.reshape(LAT_B, LAT_L, D)
    new_k = kv32[:, :ATT_NKV].reshape(CTX_B, 1, CTX_L, AKV, AHD)
    new_v = kv32[:, ATT_NKV:].reshape(CTX_B, 1, CTX_L, AKV, AHD)
    return y_prompt, y_sample, s_f, s_b, new_k, new_v
```

```python
import functools

import jax
import jax.numpy as jnp
from jax import lax
from jax.experimental import pallas as pl
from jax.experimental.pallas import tpu as pltpu

D = 1024
CTX_B, CTX_L = 32, 256
LAT_B, LAT_L = 8, 2048
T_CTX = CTX_B * CTX_L
T_LAT = LAT_B * LAT_L
T_ALL = T_CTX + T_LAT
GRID_W = 64
N_MOD = 6
MOD_ROWS = 16
CTX_MOD_ROW = LAT_B
RH, RDK, RDV, RC = 4, 256, 512, 128
AH, AKV, AG, AHD = 16, 4, 4, 64
WINDOW = 128
ABLK = 128
D_FF = 2816
ROPE_BASE = 10000.0
EPS = 1e-6
NEG_INF = -1e30

BF16 = jnp.bfloat16
F32 = jnp.float32

VMEM_LIMIT_BYTES = 56 * 1024 * 1024
NORM_ROWS = 32


def _cparams(n_axes):
    return pltpu.CompilerParams(dimension_semantics=("arbitrary",) * n_axes,
                                vmem_limit_bytes=VMEM_LIMIT_BYTES)


def _silu(x):
    return x / (1.0 + jnp.exp(-x))


def _mod_row(i, tm):
    r0 = i * tm
    return jnp.where(r0 < T_CTX, CTX_MOD_ROW, (r0 - T_CTX) // LAT_L)


def _modulate(x, g_ref, mod_ref, shift_k, scale_k):
    ms = jnp.mean(x * x, axis=-1, keepdims=True)
    y = x * lax.rsqrt(ms + EPS) * g_ref[...]
    return y * (1.0 + mod_ref[0, scale_k:scale_k + 1, :]) + mod_ref[0, shift_k:shift_k + 1, :]


MOD_TM = 512
N_CTX_BLK = T_CTX // MOD_TM


def _ctx_blk(i):
    return jnp.minimum(i, N_CTX_BLK - 1)


def _lat_blk(i):
    return jnp.maximum(i - N_CTX_BLK, 0)


def _first_mod_kernel(xc_ref, xl_ref, g_ref, mod_ref, h_ref):
    i = pl.program_id(0)

    def run(x_ref):
        for r in range(0, MOD_TM, NORM_ROWS):
            rows = slice(r, r + NORM_ROWS)
            h_ref[rows, :] = _modulate(x_ref[rows, :], g_ref, mod_ref, 0, 1).astype(BF16)

    @pl.when(i < N_CTX_BLK)
    def _():
        run(xc_ref)

    @pl.when(i >= N_CTX_BLK)
    def _():
        run(xl_ref)


def _first_mod(x_ctx, x_lat, gain, mods_l):
    return pl.pallas_call(
        _first_mod_kernel,
        grid=(T_ALL // MOD_TM,),
        in_specs=[
            pl.BlockSpec((MOD_TM, D), lambda i: (_ctx_blk(i), 0)),
            pl.BlockSpec((MOD_TM, D), lambda i: (_lat_blk(i), 0)),
            pl.BlockSpec((1, D), lambda i: (0, 0)),
            pl.BlockSpec((1, N_MOD, D), lambda i: (_mod_row(i, MOD_TM), 0, 0)),
        ],
        out_specs=pl.BlockSpec((MOD_TM, D), lambda i: (i, 0)),
        out_shape=jax.ShapeDtypeStruct((T_ALL, D), BF16),
        compiler_params=_cparams(1),
        name="first_mod",
    )(x_ctx, x_lat, gain, mods_l)


ADA_TN = 512


def _adaln_kernel(cv_ref, w_ref, b_ref, o_ref):
    s = _silu(cv_ref[...]).astype(BF16)
    w = w_ref[0].astype(BF16)
    o_ref[0] = jnp.dot(s, w, preferred_element_type=F32) + b_ref[0]


def _adaln(cv, ada_w, ada_b):
    depth = ada_w.shape[0]
    n = N_MOD * D
    return pl.pallas_call(
        _adaln_kernel,
        grid=(depth, n // ADA_TN),
        in_specs=[
            pl.BlockSpec((MOD_ROWS, D), lambda l, j: (0, 0)),
            pl.BlockSpec((1, D, ADA_TN), lambda l, j: (l, 0, j)),
            pl.BlockSpec((1, 1, ADA_TN), lambda l, j: (l, 0, j)),
        ],
        out_specs=pl.BlockSpec((1, MOD_ROWS, ADA_TN), lambda l, j: (l, 0, j)),
        out_shape=jax.ShapeDtypeStruct((depth, MOD_ROWS, n), F32),
        compiler_params=_cparams(2),
        name="adaln",
    )(cv, ada_w, ada_b.reshape(depth, 1, n))


def _rope_tables(head_dim):
    half = head_dim // 2
    n_freq = half // 2
    inv = ROPE_BASE ** (-jnp.arange(n_freq, dtype=F32) / n_freq)
    t = jnp.arange(LAT_L)
    rows = (t // GRID_W).astype(F32)[:, None] * inv[None, :]
    cols = (t % GRID_W).astype(F32)[:, None] * inv[None, :]
    zero = jnp.zeros_like(rows)
    cos = jnp.concatenate([jnp.cos(rows)] * 2 + [jnp.cos(cols)] * 2, axis=-1)
    sin_lo = jnp.concatenate([-jnp.sin(rows), zero, -jnp.sin(cols), zero], axis=-1)
    sin_hi = jnp.concatenate([zero, jnp.sin(rows), zero, jnp.sin(cols)], axis=-1)
    return cos, sin_lo, sin_hi


def _rope_slab(x, cos, sin_lo, sin_hi, n_freq):
    up = pltpu.roll(x, 128 - n_freq, 1)
    dn = pltpu.roll(x, n_freq, 1)
    return x * cos + up * sin_lo + dn * sin_hi


PROJ_TM = 1024
PROJ_MC = 256
RET_TN = 1024
RET_N = 2 * RH * RDK + 3 * RH * RDV


def _ret_proj_kernel(h_ref, w_ref, cos_ref, slo_ref, shi_ref, o_ref):
    i = pl.program_id(0)
    j = pl.program_id(1)
    is_lat = i * PROJ_TM >= T_CTX

    def chunks(epilogue):
        for m in range(PROJ_TM // PROJ_MC):
            rows = slice(m * PROJ_MC, (m + 1) * PROJ_MC)
            acc = jnp.dot(h_ref[rows, :], w_ref[...], preferred_element_type=F32)
            epilogue(acc, rows)

    def plain(acc, rows):
        o_ref[rows, :] = acc.astype(BF16)

    def gate(acc, rows):
        o_ref[rows, :] = _silu(acc).astype(BF16)

    def rope(scale):
        def ep(acc, rows):
            for s in range(RET_TN // 128):
                cols = slice(s * 128, (s + 1) * 128)
                tcols = slice((s % 2) * 128, (s % 2 + 1) * 128)
                x = acc[:, cols]
                if scale != 1.0:
                    x = x * scale
                y = _rope_slab(x, cos_ref[rows, tcols], slo_ref[rows, tcols], shi_ref[rows, tcols], RDK // 4)
                o_ref[rows, cols] = y.astype(BF16)
        return ep

    def scaled(scale):
        def ep(acc, rows):
            o_ref[rows, :] = (acc * scale).astype(BF16)
        return ep

    k_scale = RDK ** -0.5

    @pl.when(jnp.logical_and(j == 0, is_lat))
    def _():
        chunks(rope(1.0))

    @pl.when(jnp.logical_and(j == 0, jnp.logical_not(is_lat)))
    def _():
        chunks(plain)

    @pl.when(jnp.logical_and(j == 1, is_lat))
    def _():
        chunks(rope(k_scale))

    @pl.when(jnp.logical_and(j == 1, jnp.logical_not(is_lat)))
    def _():
        chunks(scaled(k_scale))

    @pl.when(jnp.logical_and(j >= 2, j < 4))
    def _():
        chunks(plain)

    @pl.when(j >= 4)
    def _():
        chunks(gate)


def _ret_proj(h, w_all, tabs):
    cos, slo, shi = tabs
    n_pos = LAT_L // PROJ_TM
    tab_spec = pl.BlockSpec((PROJ_TM, RDK), lambda i, j: (i % n_pos, 0))
    return pl.pallas_call(
        _ret_proj_kernel,
        grid=(T_ALL // PROJ_TM, RET_N // RET_TN),
        in_specs=[
            pl.BlockSpec((PROJ_TM, D), lambda i, j: (i, 0)),
            pl.BlockSpec((D, RET_TN), lambda i, j: (0, j)),
            tab_spec, tab_spec, tab_spec,
        ],
        out_specs=pl.BlockSpec((PROJ_TM, RET_TN), lambda i, j: (i, j)),
        out_shape=jax.ShapeDtypeStruct((T_ALL, RET_N), BF16),
        compiler_params=_cparams(2),
        name="ret_proj",
    )(h, w_all, cos, slo, shi)


RET_BLOCK = 2048


def _ret_scan_kernel(n_seq, seq_len, has_init, emit_state, lg_ref, q_ref, k_ref, v_ref, gf_ref, gb_ref, *rest):
    rest = list(rest)
    s0f_ref = rest.pop(0) if has_init else None
    s0b_ref = rest.pop(0) if has_init else None
    y_ref = rest.pop(0)
    sf_ref = rest.pop(0) if emit_state else None
    sb_ref = rest.pop(0) if emit_state else None
    ybuf_ref, st_ref = rest

    hd = pl.program_id(1)
    lg_f = lg_ref[0, hd]
    lg_b = lg_ref[1, hd]
    n_chunks = seq_len // RC

    ii = lax.broadcasted_iota(jnp.int32, (RC, RC), 0)
    jj = lax.broadcasted_iota(jnp.int32, (RC, RC), 1)
    diff = (ii - jj).astype(F32)
    dmat_f = jnp.where(diff >= 0, jnp.exp(lg_f * jnp.maximum(diff, 0.0)), 0.0)
    dmat_b = jnp.where(diff <= 0, jnp.exp(lg_b * jnp.maximum(-diff, 0.0)), 0.0)
    idx = lax.broadcasted_iota(jnp.int32, (RC, 1), 0).astype(F32)
    xi_f = jnp.exp(lg_f * (idx + 1.0))
    zeta_f = jnp.exp(lg_f * (RC - 1.0 - idx))
    xi_b = jnp.exp(lg_b * (RC - idx))
    zeta_b = jnp.exp(lg_b * idx)
    ones_row = jnp.ones((1, RDV), F32)
    gch_f = jnp.exp(ones_row * (lg_f * RC))
    gch_b = jnp.exp(ones_row * (lg_b * RC))

    def chunk_out(r0, dmat, xi, zeta, gch):
        q = q_ref[pl.ds(r0, RC), :]
        k = k_ref[pl.ds(r0, RC), :]
        v = v_ref[pl.ds(r0, RC), :]
        s = lax.dot_general(q, k, (((1,), (1,)), ((), ())), preferred_element_type=F32)
        inner = jnp.dot((s * dmat).astype(BF16), v, preferred_element_type=F32)
        st = st_ref[...]
        cross = jnp.dot(q, st.astype(BF16), preferred_element_type=F32) * xi
        o = inner + cross
        kz = (k.astype(F32) * zeta).astype(BF16)
        upd = lax.dot_general(kz, v, (((0,), (0,)), ((), ())), preferred_element_type=F32)
        st_ref[...] = st * gch + upd
        return o * lax.rsqrt(jnp.mean(o * o, axis=-1, keepdims=True) + EPS)

    for sq in range(n_seq):
        base = sq * seq_len

        if has_init:
            st_ref[...] = s0f_ref[0, 0, 0]
        else:
            st_ref[...] = jnp.zeros((RDK, RDV), F32)

        def fwd(c, carry):
            r0 = pl.multiple_of(base + c * RC, RC)
            o = chunk_out(r0, dmat_f, xi_f, zeta_f, gch_f)
            ybuf_ref[pl.ds(r0, RC), :] = o * gf_ref[pl.ds(r0, RC), :].astype(F32)
            return carry

        lax.fori_loop(0, n_chunks, fwd, 0)
        if emit_state:
            sf_ref[sq, 0, 0] = st_ref[...]

        if has_init:
            st_ref[...] = s0b_ref[0, 0, 0]
        else:
            st_ref[...] = jnp.zeros((RDK, RDV), F32)

        def bwd(c, carry):
            r0 = pl.multiple_of(base + (n_chunks - 1 - c) * RC, RC)
            o = chunk_out(r0, dmat_b, xi_b, zeta_b, gch_b)
            y = ybuf_ref[pl.ds(r0, RC), :] + o * gb_ref[pl.ds(r0, RC), :].astype(F32)
            y_ref[pl.ds(r0, RC), :] = y.astype(BF16)
            return carry

        lax.fori_loop(0, n_chunks, bwd, 0)
        if emit_state:
            sb_ref[sq, 0, 0] = st_ref[...]


def _ret_scan(qkvg, lg, y_prev, *, n_blocks, n_seq, seq_len, row_block0, s0f=None, s0b=None, emit_state=False):
    has_init = s0f is not None
    kq = RDK
    n_qk = RH * RDK // kq
    in_specs = [
        pl.BlockSpec(memory_space=pltpu.SMEM),
        pl.BlockSpec((RET_BLOCK, RDK), lambda b, h: (row_block0 + b, h)),
        pl.BlockSpec((RET_BLOCK, RDK), lambda b, h: (row_block0 + b, n_qk + h)),
        pl.BlockSpec((RET_BLOCK, RDV), lambda b, h: (row_block0 + b, n_qk + h)),
        pl.BlockSpec((RET_BLOCK, RDV), lambda b, h: (row_block0 + b, n_qk + RH + h)),
        pl.BlockSpec((RET_BLOCK, RDV), lambda b, h: (row_block0 + b, n_qk + 2 * RH + h)),
    ]
    args = [lg, qkvg, qkvg, qkvg, qkvg, qkvg]
    if has_init:
        st_spec = pl.BlockSpec((1, 1, 1, RDK, RDV), lambda b, h: (b, 0, h, 0, 0))
        in_specs += [st_spec, st_spec]
        args += [s0f, s0b]
    in_specs.append(pl.BlockSpec(memory_space=pl.ANY))
    args.append(y_prev)
    y_shape = jax.ShapeDtypeStruct((T_ALL, RH * RDV), BF16)
    out_specs = [pl.BlockSpec((RET_BLOCK, RDV), lambda b, h: (row_block0 + b, h))]
    out_shape = [y_shape]
    if emit_state:
        so_spec = pl.BlockSpec((n_seq, 1, 1, RDK, RDV), lambda b, h: (b, 0, h, 0, 0))
        so_shape = jax.ShapeDtypeStruct((n_blocks * n_seq, 1, RH, RDK, RDV), F32)
        out_specs += [so_spec, so_spec]
        out_shape += [so_shape, so_shape]

    def body(*refs):
        refs = list(refs)
        n_in = len(in_specs)
        ins = refs[:n_in - 1]
        _ret_scan_kernel(n_seq, seq_len, has_init, emit_state, *ins, *refs[n_in:])

    return pl.pallas_call(
        body,
        grid=(n_blocks, RH),
        in_specs=in_specs,
        out_specs=out_specs,
        out_shape=out_shape,
        scratch_shapes=[pltpu.VMEM((RET_BLOCK, RDV), F32), pltpu.VMEM((RDK, RDV), F32)],
        input_output_aliases={len(args) - 1: 0},
        compiler_params=_cparams(2),
        name="ret_scan_ctx" if emit_state else "ret_scan_lat",
    )(*args)


RES_TM = MOD_TM
RES_MC = 256


def _res_kernel(gate_k, split_x, next_mod, a_ref, w_ref, *rest):
    rest = list(rest)
    x_refs = [rest.pop(0), rest.pop(0)] if split_x else [rest.pop(0)]
    mod_ref = rest.pop(0)
    gain_ref = rest.pop(0)
    if next_mod is not None:
        nmod_ref = rest.pop(0)
        xo_ref, h_ref = rest
    else:
        yc_ref, yl_ref = rest
    i = pl.program_id(0)
    is_ctx = i < N_CTX_BLK
    gate = mod_ref[0, gate_k:gate_k + 1, :]

    def run(x_ref, y_ref):
        for m in range(RES_TM // RES_MC):
            rows = slice(m * RES_MC, (m + 1) * RES_MC)
            acc = jnp.dot(a_ref[rows, :], w_ref[...], preferred_element_type=F32)
            xn = x_ref[rows, :] + gate * acc
            if next_mod is not None:
                xo_ref[rows, :] = xn
                h_ref[rows, :] = _modulate(xn, gain_ref, nmod_ref, *next_mod).astype(BF16)
            else:
                ms = jnp.mean(xn * xn, axis=-1, keepdims=True)
                y_ref[rows, :] = xn * lax.rsqrt(ms + EPS) * gain_ref[...]

    if not split_x and next_mod is not None:
        run(x_refs[0], None)
        return

    @pl.when(is_ctx)
    def _():
        run(x_refs[0], None if next_mod is not None else yc_ref)

    @pl.when(jnp.logical_not(is_ctx))
    def _():
        run(x_refs[-1], None if next_mod is not None else yl_ref)


def _res_matmul(a, w, x, mods_l, gate_k, gain, next_mods=None, next_mod=None):
    kdim = a.shape[1]
    split_x = isinstance(x, tuple)
    row_spec = pl.BlockSpec((RES_TM, D), lambda i: (i, 0))
    ctx_spec = pl.BlockSpec((RES_TM, D), lambda i: (_ctx_blk(i), 0))
    lat_spec = pl.BlockSpec((RES_TM, D), lambda i: (_lat_blk(i), 0))
    mod_spec = pl.BlockSpec((1, N_MOD, D), lambda i: (_mod_row(i, RES_TM), 0, 0))
    in_specs = [pl.BlockSpec((RES_TM, kdim), lambda i: (i, 0)), pl.BlockSpec((kdim, D), lambda i: (0, 0))]
    args = [a, w]
    if split_x:
        in_specs += [ctx_spec, lat_spec]
        args += list(x)
    else:
        in_specs.append(row_spec)
        args.append(x)
    in_specs += [mod_spec, pl.BlockSpec((1, D), lambda i: (0, 0))]
    args += [mods_l, gain]
    if next_mod is not None:
        in_specs.append(mod_spec)
        args.append(next_mods)
        out_specs = [row_spec, row_spec]
        out_shape = [jax.ShapeDtypeStruct((T_ALL, D), F32), jax.ShapeDtypeStruct((T_ALL, D), BF16)]
    else:
        out_specs = [ctx_spec, lat_spec]
        out_shape = [jax.ShapeDtypeStruct((T_CTX, D), F32), jax.ShapeDtypeStruct((T_LAT, D), F32)]
    return pl.pallas_call(
        functools.partial(_res_kernel, gate_k, split_x, next_mod),
        grid=(T_ALL // RES_TM,),
        in_specs=in_specs,
        out_specs=out_specs,
        out_shape=out_shape,
        compiler_params=_cparams(1),
        name="res_matmul_k%d%s" % (kdim, "_final" if next_mod is None else ""),
    )(*args)


FFN_TM = 2048
FFN_MC = 512
FFN_TN = 256
FFN_HALO = 16


def _ffn_up_kernel(h_ref, wu_ref, wg_ref, cwu_ref, cwg_ref, cbu_ref, cbg_ref, o_ref, a_scr, t_scr):
    is_ctx = pl.program_id(0) * FFN_TM < T_CTX
    zeros = jnp.zeros((FFN_HALO, D), BF16)
    n_slab = FFN_TN // 128

    def chunk(c, mc, isolated):
        slot = c % 2
        r0 = c * mc
        lo = zeros if (isolated or r0 == 0) else h_ref[r0 - FFN_HALO:r0, :]
        hi = zeros if (isolated or r0 + mc == FFN_TM) else h_ref[r0 + mc:r0 + mc + FFN_HALO, :]
        hm = jnp.concatenate([lo, h_ref[r0:r0 + mc, :], hi], axis=0)
        n_ext = mc + 2 * FFN_HALO
        half = mc // 2
        for k, w_ref in enumerate((wu_ref, wg_ref)):
            a = jnp.dot(hm, w_ref[...], preferred_element_type=F32)
            for s in range(n_slab):
                a_scr[slot, k * n_slab + s, 0:n_ext, :] = a[:, s * 128:(s + 1) * 128]

        def conv(slab, cw_ref, cb_ref, lanes):
            def rows(off):
                return a_scr[slot, slab, pl.ds(FFN_HALO + off, half, stride=2), :]
            w0, w1, w2 = cw_ref[0:1, lanes], cw_ref[1:2, lanes], cw_ref[2:3, lanes]
            b = cb_ref[:, lanes]
            om1, ev, od, ep1 = rows(-1), rows(0), rows(1), rows(2)
            return om1 * w0 + ev * w1 + od * w2 + b, ev * w0 + od * w1 + ep1 * w2 + b

        for s in range(n_slab):
            lanes = slice(s * 128, (s + 1) * 128)
            u_ev, u_od = conv(s, cwu_ref, cbu_ref, lanes)
            g_ev, g_od = conv(n_slab + s, cwg_ref, cbg_ref, lanes)
            t_scr[slot, s, pl.ds(0, half, stride=2), :] = _silu(g_ev) * u_ev
            t_scr[slot, s, pl.ds(1, half, stride=2), :] = _silu(g_od) * u_od
            o_ref[r0:r0 + mc, lanes] = t_scr[slot, s, 0:mc, :].astype(BF16)

    @pl.when(is_ctx)
    def _():
        for c in range(FFN_TM // CTX_L):
            chunk(c, CTX_L, True)

    @pl.when(jnp.logical_not(is_ctx))
    def _():
        for c in range(FFN_TM // FFN_MC):
            chunk(c, FFN_MC, False)


def _ffn_up(h, w_in, conv_w, conv_b):
    n_tiles = D_FF // FFN_TN
    n_slab = FFN_TN // 128
    return pl.pallas_call(
        _ffn_up_kernel,
        grid=(T_ALL // FFN_TM, n_tiles),
        in_specs=[
            pl.BlockSpec((FFN_TM, D), lambda i, j: (i, 0)),
            pl.BlockSpec((D, FFN_TN), lambda i, j: (0, j)),
            pl.BlockSpec((D, FFN_TN), lambda i, j: (0, n_tiles + j)),
            pl.BlockSpec((3, FFN_TN), lambda i, j: (0, j)),
            pl.BlockSpec((3, FFN_TN), lambda i, j: (0, n_tiles + j)),
            pl.BlockSpec((1, FFN_TN), lambda i, j: (0, j)),
            pl.BlockSpec((1, FFN_TN), lambda i, j: (0, n_tiles + j)),
        ],
        out_specs=pl.BlockSpec((FFN_TM, FFN_TN), lambda i, j: (i, j)),
        out_shape=jax.ShapeDtypeStruct((T_ALL, D_FF), BF16),
        scratch_shapes=[
            pltpu.VMEM((2, 2 * n_slab, FFN_MC + 2 * FFN_HALO, 128), F32),
            pltpu.VMEM((2, n_slab, FFN_MC, 128), F32),
        ],
        compiler_params=_cparams(2),
        name="ffn_up",
    )(h, w_in, w_in, conv_w, conv_w, conv_b, conv_b)


ATT_TN = 512
ATT_NQ = AH * AHD
ATT_NKV = AKV * AHD
ATT_DUP = 2 * ATT_NKV
ATT_N = ATT_NQ + 2 * ATT_DUP
ATT_Q_TILES = ATT_NQ // ATT_TN


def _dup_heads(w):
    lead = w.shape[:-1]
    w = w.reshape(lead + (AKV, 1, AHD))
    return jnp.broadcast_to(w, lead + (AKV, 2, AHD)).reshape(lead + (ATT_DUP,))


def _undup_heads(x):
    return x.reshape(x.shape[:-1] + (AKV, 2, AHD))[..., 0, :].reshape(x.shape[:-1] + (ATT_NKV,))


def _attn_proj_kernel(h_ref, w_ref, cos_ref, slo_ref, shi_ref, q_ref, kd_ref, vd_ref, k32_ref, v32_ref):
    i = pl.program_id(0)
    j = pl.program_id(1)
    is_lat = i * PROJ_TM >= T_CTX

    def chunks(epilogue):
        for m in range(PROJ_TM // PROJ_MC):
            rows = slice(m * PROJ_MC, (m + 1) * PROJ_MC)
            epilogue(jnp.dot(h_ref[rows, :], w_ref[...], preferred_element_type=F32), rows)

    def roped(o_ref):
        def ep(acc, rows):
            for s in range(ATT_TN // 128):
                cols = slice(s * 128, (s + 1) * 128)
                y = _rope_slab(acc[:, cols], cos_ref[rows, :], slo_ref[rows, :], shi_ref[rows, :], AHD // 4)
                o_ref[rows, cols] = y.astype(BF16)
        return ep

    def plain(o_ref, o32_ref=None):
        def ep(acc, rows):
            o_ref[rows, :] = acc.astype(BF16)
            if o32_ref is not None:
                o32_ref[rows, :] = acc
        return ep

    cases = (
        (j < ATT_Q_TILES, True, roped(q_ref)),
        (j < ATT_Q_TILES, False, plain(q_ref)),
        (j == ATT_Q_TILES, True, roped(kd_ref)),
        (j == ATT_Q_TILES, False, plain(kd_ref, k32_ref)),
        (j == ATT_Q_TILES + 1, True, plain(vd_ref)),
        (j == ATT_Q_TILES + 1, False, plain(vd_ref, v32_ref)),
    )
    for on_tile, on_lat, epilogue in cases:
        @pl.when(jnp.logical_and(on_tile, is_lat == on_lat))
        def _():
            chunks(epilogue)


def _attn_proj(h, w_all, tabs):
    cos, slo, shi = tabs
    n_pos = LAT_L // PROJ_TM
    n_ctx = T_CTX // PROJ_TM
    tab_spec = pl.BlockSpec((PROJ_TM, 128), lambda i, j: (i % n_pos, 0))
    row_spec = pl.BlockSpec((PROJ_TM, ATT_DUP), lambda i, j: (i, 0))
    ctx_spec = pl.BlockSpec((PROJ_TM, ATT_DUP), lambda i, j: (jnp.minimum(i, n_ctx - 1), 0))
    return pl.pallas_call(
        _attn_proj_kernel,
        grid=(T_ALL // PROJ_TM, ATT_N // ATT_TN),
        in_specs=[
            pl.BlockSpec((PROJ_TM, D), lambda i, j: (i, 0)),
            pl.BlockSpec((D, ATT_TN), lambda i, j: (0, j)),
            tab_spec, tab_spec, tab_spec,
        ],
        out_specs=[
            pl.BlockSpec((PROJ_TM, ATT_TN), lambda i, j: (i, jnp.minimum(j, ATT_Q_TILES - 1))),
            row_spec, row_spec, ctx_spec, ctx_spec,
        ],
        out_shape=[
            jax.ShapeDtypeStruct((T_ALL, ATT_NQ), BF16),
            jax.ShapeDtypeStruct((T_ALL, ATT_DUP), BF16),
            jax.ShapeDtypeStruct((T_ALL, ATT_DUP), BF16),
            jax.ShapeDtypeStruct((T_CTX, ATT_DUP), F32),
            jax.ShapeDtypeStruct((T_CTX, ATT_DUP), F32),
        ],
        compiler_params=_cparams(2),
        name="attn_proj",
    )(h, w_all, cos, slo, shi)


def _head_scores(q_ref, kh, k_dup):
    nq = q_ref.shape[0]
    nk = k_dup.shape[0]
    low = lax.broadcasted_iota(jnp.int32, (nk, 128), 1) < AHD
    zero = jnp.zeros_like(k_dup)
    k2 = jnp.concatenate([jnp.where(low, k_dup, zero), jnp.where(low, zero, k_dup)], axis=0)
    base = kh * AG * AHD
    lhs = jnp.concatenate([q_ref[:, base:base + 128], q_ref[:, base + 128:base + 256]], axis=0)
    s = lax.dot_general(lhs, k2, (((1,), (1,)), ((), ())), preferred_element_type=F32)
    return [s[0:nq, 0:nk], s[0:nq, nk:], s[nq:, 0:nk], s[nq:, nk:]]


def _row_max(s):
    m = s[:, 0:128]
    for c in range(128, s.shape[1], 128):
        m = jnp.maximum(m, s[:, c:c + 128])
    return jnp.max(m, axis=-1, keepdims=True)


def _softmax_pv_store(scores, v_dup, sink_ref, kh, o_ref):
    nq = scores[0].shape[0]
    nk = v_dup.shape[0]
    v_aug = jnp.concatenate([v_dup, jnp.ones((nk, 128), BF16)], axis=1)
    ps, tails = [], []
    for g, s in enumerate(scores):
        sink = sink_ref[kh * AG + g]
        m = jnp.maximum(_row_max(s), sink)
        ps.append(jnp.exp(s - m).astype(BF16))
        tails.append(jnp.exp(sink - m))
    pv = jnp.dot(jnp.concatenate(ps, axis=0), v_aug, preferred_element_type=F32)
    outs = []
    for g in range(AG):
        rows = slice(g * nq, (g + 1) * nq)
        outs.append(pv[rows, 0:128] / (pv[rows, 128:256] + tails[g]))
    low = lax.broadcasted_iota(jnp.int32, (nq, 128), 1) < AHD
    base = kh * AG * AHD
    o_ref[:, base:base + 128] = jnp.where(low, outs[0], outs[1]).astype(BF16)
    o_ref[:, base + 128:base + 256] = jnp.where(low, outs[2], outs[3]).astype(BF16)


def _attend_all(q_ref, sink_ref, o_ref, kv_of, bias_of):
    def scores_of(kh):
        k_dup, _ = kv_of(kh)
        return bias_of(_head_scores(q_ref, kh, k_dup))

    nxt = scores_of(0)
    for kh in range(AKV):
        cur = nxt
        if kh + 1 < AKV:
            nxt = scores_of(kh + 1)
        _softmax_pv_store(cur, kv_of(kh)[1], sink_ref, kh, o_ref)


def _ctx_attn_kernel(sink_ref, q_ref, kd_ref, vd_ref, o_ref):
    def kv_of(kh):
        cols = slice(kh * 128, (kh + 1) * 128)
        return kd_ref[:, cols], vd_ref[:, cols]

    _attend_all(q_ref, sink_ref, o_ref, kv_of, lambda scores: scores)


def _ctx_attn(sink, q, kd, vd, o_prev):
    return pl.pallas_call(
        lambda sink_ref, q_ref, kd_ref, vd_ref, prev_ref, o_ref: _ctx_attn_kernel(
            sink_ref, q_ref, kd_ref, vd_ref, o_ref),
        grid=(CTX_B,),
        in_specs=[
            pl.BlockSpec(memory_space=pltpu.SMEM),
            pl.BlockSpec((CTX_L, ATT_NQ), lambda b: (b, 0)),
            pl.BlockSpec((CTX_L, ATT_DUP), lambda b: (b, 0)),
            pl.BlockSpec((CTX_L, ATT_DUP), lambda b: (b, 0)),
            pl.BlockSpec(memory_space=pl.ANY),
        ],
        out_specs=pl.BlockSpec((CTX_L, ATT_NQ), lambda b: (b, 0)),
        out_shape=jax.ShapeDtypeStruct((T_ALL, ATT_NQ), BF16),
        input_output_aliases={4: 0},
        compiler_params=_cparams(1),
        name="ctx_attn",
    )(sink, q, kd, vd, o_prev)


BAND = 3 * ABLK


def _lat_attn_kernel(sink_ref, q_ref, kd_ref, vd_ref, ck_ref, cv_ref, o_ref):
    n = pl.program_id(1)
    start = pl.multiple_of(jnp.clip((n - 1) * ABLK, 0, LAT_L - BAND), ABLK)
    qpos = n * ABLK + lax.broadcasted_iota(jnp.int32, (ABLK, BAND), 0)
    kpos = start + lax.broadcasted_iota(jnp.int32, (ABLK, BAND), 1)
    band_bias = jnp.where(jnp.abs(qpos - kpos) <= WINDOW, 0.0, NEG_INF)

    def kv_of(kh):
        cols = slice(kh * 128, (kh + 1) * 128)
        k = jnp.concatenate([kd_ref[pl.ds(start, BAND), cols], ck_ref[0, :, cols]], axis=0)
        v = jnp.concatenate([vd_ref[pl.ds(start, BAND), cols], cv_ref[0, :, cols]], axis=0)
        return k, v

    def bias_of(scores):
        return [jnp.concatenate([s[:, :BAND] + band_bias, s[:, BAND:]], axis=1) for s in scores]

    _attend_all(q_ref, sink_ref, o_ref, kv_of, bias_of)


def _lat_attn(sink, q, kd, vd, cache_kd, cache_vd, o_prev):
    n_q = LAT_L // ABLK
    rb0 = T_CTX // ABLK
    sb0 = T_CTX // LAT_L
    seq_spec = pl.BlockSpec((LAT_L, ATT_DUP), lambda b, n: (sb0 + b, 0))
    cache_spec = pl.BlockSpec((1, CTX_L, ATT_DUP), lambda b, n: (b, 0, 0))
    return pl.pallas_call(
        lambda sink_ref, q_ref, kd_ref, vd_ref, ck_ref, cv_ref, prev_ref, o_ref: _lat_attn_kernel(
            sink_ref, q_ref, kd_ref, vd_ref, ck_ref, cv_ref, o_ref),
        grid=(LAT_B, n_q),
        in_specs=[
            pl.BlockSpec(memory_space=pltpu.SMEM),
            pl.BlockSpec((ABLK, ATT_NQ), lambda b, n: (rb0 + b * n_q + n, 0)),
            seq_spec, seq_spec, cache_spec, cache_spec,
            pl.BlockSpec(memory_space=pl.ANY),
        ],
        out_specs=pl.BlockSpec((ABLK, ATT_NQ), lambda b, n: (rb0 + b * n_q + n, 0)),
        out_shape=jax.ShapeDtypeStruct((T_ALL, ATT_NQ), BF16),
        input_output_aliases={6: 0},
        compiler_params=_cparams(2),
        name="lat_attn",
    )(sink, q, kd, vd, cache_kd, cache_vd, o_prev)


def _log_sigmoid(x):
    return -(jnp.maximum(-x, 0.0) + jnp.log1p(jnp.exp(-jnp.abs(x))))


def kernel(x_prompt, x_sample, state_ret_fwd, state_ret_bwd, cache_attn_k, cache_attn_v, c, c_ctx, ada_w, ada_b, norm_mix, norm_ffn, ret_wq, ret_wk, ret_wv, ret_wg_fwd, ret_wg_bwd, ret_wo, ret_decay_fwd, ret_decay_bwd, attn_wq, attn_wk, attn_wv, attn_wo, attn_sink, ffn_w_in, ffn_conv_w, ffn_conv_b, ffn_w_out, norm_final):
    x_ctx = x_prompt.reshape(T_CTX, D)
    x_lat = x_sample.reshape(T_LAT, D)

    cv = jnp.concatenate([c, c_ctx[None, :], jnp.zeros((MOD_ROWS - LAT_B - 1, D), F32)], axis=0)
    mods = _adaln(cv, ada_w, ada_b).reshape(ada_w.shape[0], MOD_ROWS, N_MOD, D)

    h = _first_mod(x_ctx, x_lat, norm_mix[0:1], mods[0])
    w_ret = jnp.concatenate([ret_wq[0], ret_wk[0], ret_wv[0], ret_wg_fwd[0], ret_wg_bwd[0]], axis=1).astype(BF16)
    qkvg = _ret_proj(h, w_ret, _rope_tables(RDK))
    lg = jnp.stack([_log_sigmoid(ret_decay_fwd[0].astype(F32)), _log_sigmoid(ret_decay_bwd[0].astype(F32))])
    y0 = jnp.zeros((T_ALL, RH * RDV), BF16)
    y0, s_f, s_b = _ret_scan(qkvg, lg, y0, n_blocks=T_CTX // RET_BLOCK, n_seq=RET_BLOCK // CTX_L,
                             seq_len=CTX_L, row_block0=0, emit_state=True)
    (y0,) = _ret_scan(qkvg, lg, y0, n_blocks=LAT_B, n_seq=1, seq_len=LAT_L, row_block0=T_CTX // RET_BLOCK,
                      s0f=state_ret_fwd, s0b=state_ret_bwd)
    x, h = _res_matmul(y0, ret_wo[0].astype(BF16), (x_ctx, x_lat), mods[0], 2,
                       norm_ffn[0:1], mods[0], (3, 4))
    act = _ffn_up(h, ffn_w_in[0].astype(BF16), ffn_conv_w[0], ffn_conv_b[0:1])
    x, h = _res_matmul(act, ffn_w_out[0].astype(BF16), x, mods[0], 5, norm_mix[1:2], mods[1], (0, 1))

    w_att = jnp.concatenate([attn_wq[0] * (AHD ** -0.5), _dup_heads(attn_wk[0]), _dup_heads(attn_wv[0])],
                            axis=1).astype(BF16)
    cos, slo, shi = _rope_tables(AHD)
    tabs = tuple(jnp.tile(t, (1, 128 // AHD)) for t in (cos, slo, shi))
    q, kd, vd, k32, v32 = _attn_proj(h, w_att, tabs)
    sink = attn_sink[0].astype(F32)
    cache_kd = _dup_heads(cache_attn_k[:, 0].reshape(LAT_B, CTX_L, ATT_NKV)).astype(BF16)
    cache_vd = _dup_heads(cache_attn_v[:, 0].reshape(LAT_B, CTX_L, ATT_NKV)).astype(BF16)
    a = _lat_attn(sink, q, kd, vd, cache_kd, cache_vd, jnp.zeros((T_ALL, ATT_NQ), BF16))
    a = _ctx_attn(sink, q, kd, vd, a)
    x, h = _res_matmul(a, attn_wo[0].astype(BF16), x, mods[1], 2, norm_ffn[1:2], mods[1], (3, 4))
    act = _ffn_up(h, ffn_w_in[1].astype(BF16), ffn_conv_w[1], ffn_conv_b[1:2])
    y_ctx, y_lat = _res_matmul(act, ffn_w_out[1].astype(BF16), x, mods[1], 5, norm_final[None, :])

    y_prompt = y_ctx.reshape(CTX_B, CTX_L, D)
    y_sample = y_lat.reshape(LAT_B, LAT_L, D)
    new_k = _undup_heads(k32).reshape(CTX_B, 1, CTX_L, AKV, AHD)
    new_v = _undup_heads(v32).reshape(CTX_B, 1, CTX_L, AKV, AHD)
    return y_prompt, y_sample, s_f, s_b, new_k, new_v
```

```python
import functools

import jax
import jax.numpy as jnp
from jax import lax
from jax.experimental import pallas as pl
from jax.experimental.pallas import tpu as pltpu

D = 1024
CTX_B, CTX_L = 32, 256
LAT_B, LAT_L = 8, 2048
T_CTX = CTX_B * CTX_L
T_LAT = LAT_B * LAT_L
T_ALL = T_CTX + T_LAT
GRID_W = 64
N_MOD = 6
MOD_ROWS = 16
CTX_MOD_ROW = LAT_B
RH, RDK, RDV = 4, 256, 512
RC = 256
AH, AKV, AG, AHD = 16, 4, 4, 64
WINDOW = 128
ABLK = 128
D_FF = 2816
ROPE_BASE = 10000.0
EPS = 1e-6
NEG_INF = -1e30

BF16 = jnp.bfloat16
F32 = jnp.float32

VMEM_LIMIT_BYTES = 56 * 1024 * 1024
NORM_ROWS = 32


def _cparams(n_axes):
    return pltpu.CompilerParams(dimension_semantics=("arbitrary",) * n_axes,
                                vmem_limit_bytes=VMEM_LIMIT_BYTES)


def _silu(x):
    return x / (1.0 + jnp.exp(-x))


def _mod_row(i, tm):
    r0 = i * tm
    return jnp.where(r0 < T_CTX, CTX_MOD_ROW, (r0 - T_CTX) // LAT_L)


def _modulate(x, g_ref, mod_ref, shift_k, scale_k):
    ms = jnp.mean(x * x, axis=-1, keepdims=True)
    y = x * lax.rsqrt(ms + EPS) * g_ref[...]
    return y * (1.0 + mod_ref[0, scale_k:scale_k + 1, :]) + mod_ref[0, shift_k:shift_k + 1, :]


MOD_TM = 512
N_CTX_BLK = T_CTX // MOD_TM


def _ctx_blk(i):
    return jnp.minimum(i, N_CTX_BLK - 1)


def _lat_blk(i):
    return jnp.maximum(i - N_CTX_BLK, 0)


def _first_mod_kernel(xc_ref, xl_ref, g_ref, mod_ref, h_ref):
    i = pl.program_id(0)

    def run(x_ref):
        for r in range(0, MOD_TM, NORM_ROWS):
            rows = slice(r, r + NORM_ROWS)
            h_ref[rows, :] = _modulate(x_ref[rows, :], g_ref, mod_ref, 0, 1).astype(BF16)

    @pl.when(i < N_CTX_BLK)
    def _():
        run(xc_ref)

    @pl.when(i >= N_CTX_BLK)
    def _():
        run(xl_ref)


def _first_mod(x_ctx, x_lat, gain, mods_l):
    return pl.pallas_call(
        _first_mod_kernel,
        grid=(T_ALL // MOD_TM,),
        in_specs=[
            pl.BlockSpec((MOD_TM, D), lambda i: (_ctx_blk(i), 0)),
            pl.BlockSpec((MOD_TM, D), lambda i: (_lat_blk(i), 0)),
            pl.BlockSpec((1, D), lambda i: (0, 0)),
            pl.BlockSpec((1, N_MOD, D), lambda i: (_mod_row(i, MOD_TM), 0, 0)),
        ],
        out_specs=pl.BlockSpec((MOD_TM, D), lambda i: (i, 0)),
        out_shape=jax.ShapeDtypeStruct((T_ALL, D), BF16),
        compiler_params=_cparams(1),
        name="first_mod",
    )(x_ctx, x_lat, gain, mods_l)


ADA_TN = 512


def _adaln_kernel(cv_ref, w_ref, b_ref, o_ref):
    s = _silu(cv_ref[...]).astype(BF16)
    w = w_ref[0].astype(BF16)
    o_ref[0] = jnp.dot(s, w, preferred_element_type=F32) + b_ref[0]


def _adaln(cv, ada_w, ada_b):
    depth = ada_w.shape[0]
    n = N_MOD * D
    return pl.pallas_call(
        _adaln_kernel,
        grid=(depth, n // ADA_TN),
        in_specs=[
            pl.BlockSpec((MOD_ROWS, D), lambda l, j: (0, 0)),
            pl.BlockSpec((1, D, ADA_TN), lambda l, j: (l, 0, j)),
            pl.BlockSpec((1, 1, ADA_TN), lambda l, j: (l, 0, j)),
        ],
        out_specs=pl.BlockSpec((1, MOD_ROWS, ADA_TN), lambda l, j: (l, 0, j)),
        out_shape=jax.ShapeDtypeStruct((depth, MOD_ROWS, n), F32),
        compiler_params=_cparams(2),
        name="adaln",
    )(cv, ada_w, ada_b.reshape(depth, 1, n))


def _rope_tables(head_dim):
    half = head_dim // 2
    n_freq = half // 2
    inv = ROPE_BASE ** (-jnp.arange(n_freq, dtype=F32) / n_freq)
    t = jnp.arange(LAT_L)
    rows = (t // GRID_W).astype(F32)[:, None] * inv[None, :]
    cols = (t % GRID_W).astype(F32)[:, None] * inv[None, :]
    zero = jnp.zeros_like(rows)
    cos = jnp.concatenate([jnp.cos(rows)] * 2 + [jnp.cos(cols)] * 2, axis=-1)
    sin_lo = jnp.concatenate([-jnp.sin(rows), zero, -jnp.sin(cols), zero], axis=-1)
    sin_hi = jnp.concatenate([zero, jnp.sin(rows), zero, jnp.sin(cols)], axis=-1)
    return cos, sin_lo, sin_hi


def _rope_slab(x, cos, sin_lo, sin_hi, n_freq):
    up = pltpu.roll(x, 128 - n_freq, 1)
    dn = pltpu.roll(x, n_freq, 1)
    return x * cos + up * sin_lo + dn * sin_hi


PROJ_TM = 1024
PROJ_MC = 256
RET_TN = 1024
RET_N = 2 * RH * RDK + 3 * RH * RDV


def _ret_proj_kernel(h_ref, w_ref, cos_ref, slo_ref, shi_ref, o_ref):
    i = pl.program_id(0)
    j = pl.program_id(1)
    is_lat = i * PROJ_TM >= T_CTX

    def chunks(epilogue):
        for m in range(PROJ_TM // PROJ_MC):
            rows = slice(m * PROJ_MC, (m + 1) * PROJ_MC)
            acc = jnp.dot(h_ref[rows, :], w_ref[...], preferred_element_type=F32)
            epilogue(acc, rows)

    def plain(acc, rows):
        o_ref[rows, :] = acc.astype(BF16)

    def gate(acc, rows):
        o_ref[rows, :] = _silu(acc).astype(BF16)

    def rope(scale):
        def ep(acc, rows):
            for s in range(RET_TN // 128):
                cols = slice(s * 128, (s + 1) * 128)
                tcols = slice((s % 2) * 128, (s % 2 + 1) * 128)
                x = acc[:, cols]
                if scale != 1.0:
                    x = x * scale
                y = _rope_slab(x, cos_ref[rows, tcols], slo_ref[rows, tcols], shi_ref[rows, tcols], RDK // 4)
                o_ref[rows, cols] = y.astype(BF16)
        return ep

    def scaled(scale):
        def ep(acc, rows):
            o_ref[rows, :] = (acc * scale).astype(BF16)
        return ep

    k_scale = RDK ** -0.5

    @pl.when(jnp.logical_and(j == 0, is_lat))
    def _():
        chunks(rope(1.0))

    @pl.when(jnp.logical_and(j == 0, jnp.logical_not(is_lat)))
    def _():
        chunks(plain)

    @pl.when(jnp.logical_and(j == 1, is_lat))
    def _():
        chunks(rope(k_scale))

    @pl.when(jnp.logical_and(j == 1, jnp.logical_not(is_lat)))
    def _():
        chunks(scaled(k_scale))

    @pl.when(jnp.logical_and(j >= 2, j < 4))
    def _():
        chunks(plain)

    @pl.when(j >= 4)
    def _():
        chunks(gate)


def _ret_proj(h, w_all, tabs):
    cos, slo, shi = tabs
    n_pos = LAT_L // PROJ_TM
    tab_spec = pl.BlockSpec((PROJ_TM, RDK), lambda i, j: (i % n_pos, 0))
    return pl.pallas_call(
        _ret_proj_kernel,
        grid=(T_ALL // PROJ_TM, RET_N // RET_TN),
        in_specs=[
            pl.BlockSpec((PROJ_TM, D), lambda i, j: (i, 0)),
            pl.BlockSpec((D, RET_TN), lambda i, j: (0, j)),
            tab_spec, tab_spec, tab_spec,
        ],
        out_specs=pl.BlockSpec((PROJ_TM, RET_TN), lambda i, j: (i, j)),
        out_shape=jax.ShapeDtypeStruct((T_ALL, RET_N), BF16),
        compiler_params=_cparams(2),
        name="ret_proj",
    )(h, w_all, cos, slo, shi)


RET_BLOCK = 2048


def _ret_scan_kernel(n_seq, seq_len, has_init, emit_state, lg_ref, q_ref, k_ref, v_ref, gf_ref, gb_ref, *rest):
    rest = list(rest)
    s0f_ref = rest.pop(0) if has_init else None
    s0b_ref = rest.pop(0) if has_init else None
    y_ref = rest.pop(0)
    sf_ref = rest.pop(0) if emit_state else None
    sb_ref = rest.pop(0) if emit_state else None
    ybuf_ref, stf_ref, stb_ref = rest

    hd = pl.program_id(1)
    n_chunks = seq_len // RC
    idx_col = lax.broadcasted_iota(jnp.int32, (RC, 1), 0).astype(F32)
    idx_row = lax.broadcasted_iota(jnp.int32, (1, RC), 1).astype(F32)
    diff = idx_col - idx_row

    lg_f = lg_ref[0, hd]
    lg_b = lg_ref[1, hd]
    dirs = {
        "f": dict(dmat=jnp.where(diff >= 0, jnp.exp(lg_f * jnp.maximum(diff, 0.0)), 0.0),
                  xi=jnp.exp(lg_f * (idx_col + 1.0)), zeta=jnp.exp(lg_f * (RC - 1.0 - idx_row)),
                  gch=jnp.exp(jnp.ones((1, RDV), F32) * (lg_f * RC)),
                  st_ref=stf_ref, s0_ref=s0f_ref, gate_ref=gf_ref, out_ref=sf_ref),
        "b": dict(dmat=jnp.where(diff <= 0, jnp.exp(lg_b * jnp.maximum(-diff, 0.0)), 0.0),
                  xi=jnp.exp(lg_b * (RC - idx_col)), zeta=jnp.exp(lg_b * idx_row),
                  gch=jnp.exp(jnp.ones((1, RDV), F32) * (lg_b * RC)),
                  st_ref=stb_ref, s0_ref=s0b_ref, gate_ref=gb_ref, out_ref=sb_ref),
    }

    def shared(rows):
        q = q_ref[rows, :]
        k = k_ref[rows, :]
        s = lax.dot_general(q, k, (((1,), (1,)), ((), ())), preferred_element_type=F32)
        return q, s, k.astype(F32).T, v_ref[rows, :]

    def chunk_dir(d, sh, rows, first, last, sq):
        q, s, kt, v = sh
        sm = (s * d["dmat"]).astype(BF16)
        kzt = (kt * d["zeta"]).astype(BF16)
        both = jnp.dot(jnp.concatenate([sm, kzt], axis=0), v, preferred_element_type=F32)
        o, upd = both[:RC], both[RC:]
        st = None
        if has_init and first:
            st = d["s0_ref"][0, 0, 0]
        elif not first:
            st = d["st_ref"][...]
        if st is not None:
            o = o + jnp.dot(q, st.astype(BF16), preferred_element_type=F32) * d["xi"]
            upd = st * d["gch"] + upd
        if last and emit_state:
            d["out_ref"][sq, 0, 0] = upd
        elif not last:
            d["st_ref"][...] = upd
        o = o * lax.rsqrt(jnp.mean(o * o, axis=-1, keepdims=True) + EPS)
        return o * d["gate_ref"][rows, :].astype(F32)

    for sq in range(n_seq):
        half_done = set()

        def emit(c, val):
            rows = slice(sq * seq_len + c * RC, sq * seq_len + (c + 1) * RC)
            if c in half_done:
                y_ref[rows, :] = (ybuf_ref[rows, :] + val).astype(BF16)
            else:
                ybuf_ref[rows, :] = val
                half_done.add(c)

        for t in range(n_chunks):
            cf, cb = t, n_chunks - 1 - t
            rows_f = slice(sq * seq_len + cf * RC, sq * seq_len + (cf + 1) * RC)
            rows_b = slice(sq * seq_len + cb * RC, sq * seq_len + (cb + 1) * RC)
            sh_f = shared(rows_f)
            sh_b = sh_f if cb == cf else shared(rows_b)
            first, last = t == 0, t == n_chunks - 1
            of = chunk_dir(dirs["f"], sh_f, rows_f, first, last, sq)
            ob = chunk_dir(dirs["b"], sh_b, rows_b, first, last, sq)
            if cf == cb:
                y_ref[rows_f, :] = (of + ob).astype(BF16)
            else:
                emit(cf, of)
                emit(cb, ob)


def _ret_scan(qkvg, lg, y_prev, *, n_blocks, n_seq, seq_len, row_block0, s0f=None, s0b=None, emit_state=False):
    has_init = s0f is not None
    kq = RDK
    n_qk = RH * RDK // kq
    in_specs = [
        pl.BlockSpec(memory_space=pltpu.SMEM),
        pl.BlockSpec((RET_BLOCK, RDK), lambda b, h: (row_block0 + b, h)),
        pl.BlockSpec((RET_BLOCK, RDK), lambda b, h: (row_block0 + b, n_qk + h)),
        pl.BlockSpec((RET_BLOCK, RDV), lambda b, h: (row_block0 + b, n_qk + h)),
        pl.BlockSpec((RET_BLOCK, RDV), lambda b, h: (row_block0 + b, n_qk + RH + h)),
        pl.BlockSpec((RET_BLOCK, RDV), lambda b, h: (row_block0 + b, n_qk + 2 * RH + h)),
    ]
    args = [lg, qkvg, qkvg, qkvg, qkvg, qkvg]
    if has_init:
        st_spec = pl.BlockSpec((1, 1, 1, RDK, RDV), lambda b, h: (b, 0, h, 0, 0))
        in_specs += [st_spec, st_spec]
        args += [s0f, s0b]
    in_specs.append(pl.BlockSpec(memory_space=pl.ANY))
    args.append(y_prev)
    y_shape = jax.ShapeDtypeStruct((T_ALL, RH * RDV), BF16)
    out_specs = [pl.BlockSpec((RET_BLOCK, RDV), lambda b, h: (row_block0 + b, h))]
    out_shape = [y_shape]
    if emit_state:
        so_spec = pl.BlockSpec((n_seq, 1, 1, RDK, RDV), lambda b, h: (b, 0, h, 0, 0))
        so_shape = jax.ShapeDtypeStruct((n_blocks * n_seq, 1, RH, RDK, RDV), F32)
        out_specs += [so_spec, so_spec]
        out_shape += [so_shape, so_shape]

    def body(*refs):
        refs = list(refs)
        n_in = len(in_specs)
        ins = refs[:n_in - 1]
        _ret_scan_kernel(n_seq, seq_len, has_init, emit_state, *ins, *refs[n_in:])

    return pl.pallas_call(
        body,
        grid=(n_blocks, RH),
        in_specs=in_specs,
        out_specs=out_specs,
        out_shape=out_shape,
        scratch_shapes=[pltpu.VMEM((RET_BLOCK, RDV), F32), pltpu.VMEM((RDK, RDV), F32),
                        pltpu.VMEM((RDK, RDV), F32)],
        input_output_aliases={len(args) - 1: 0},
        compiler_params=_cparams(2),
        name="ret_scan_ctx" if emit_state else "ret_scan_lat",
    )(*args)


RES_TM = MOD_TM
RES_MC = 256


def _res_kernel(gate_k, split_x, next_mod, a_ref, w_ref, *rest):
    rest = list(rest)
    x_refs = [rest.pop(0), rest.pop(0)] if split_x else [rest.pop(0)]
    mod_ref = rest.pop(0)
    gain_ref = rest.pop(0)
    if next_mod is not None:
        nmod_ref = rest.pop(0)
        xo_ref, h_ref = rest
    else:
        yc_ref, yl_ref = rest
    i = pl.program_id(0)
    is_ctx = i < N_CTX_BLK
    gate = mod_ref[0, gate_k:gate_k + 1, :]

    def run(x_ref, y_ref):
        for m in range(RES_TM // RES_MC):
            rows = slice(m * RES_MC, (m + 1) * RES_MC)
            acc = jnp.dot(a_ref[rows, :], w_ref[...], preferred_element_type=F32)
            xn = x_ref[rows, :] + gate * acc
            if next_mod is not None:
                xo_ref[rows, :] = xn
                h_ref[rows, :] = _modulate(xn, gain_ref, nmod_ref, *next_mod).astype(BF16)
            else:
                ms = jnp.mean(xn * xn, axis=-1, keepdims=True)
                y_ref[rows, :] = xn * lax.rsqrt(ms + EPS) * gain_ref[...]

    if not split_x and next_mod is not None:
        run(x_refs[0], None)
        return

    @pl.when(is_ctx)
    def _():
        run(x_refs[0], None if next_mod is not None else yc_ref)

    @pl.when(jnp.logical_not(is_ctx))
    def _():
        run(x_refs[-1], None if next_mod is not None else yl_ref)


def _res_matmul(a, w, x, mods_l, gate_k, gain, next_mods=None, next_mod=None):
    kdim = a.shape[1]
    split_x = isinstance(x, tuple)
    row_spec = pl.BlockSpec((RES_TM, D), lambda i: (i, 0))
    ctx_spec = pl.BlockSpec((RES_TM, D), lambda i: (_ctx_blk(i), 0))
    lat_spec = pl.BlockSpec((RES_TM, D), lambda i: (_lat_blk(i), 0))
    mod_spec = pl.BlockSpec((1, N_MOD, D), lambda i: (_mod_row(i, RES_TM), 0, 0))
    in_specs = [pl.BlockSpec((RES_TM, kdim), lambda i: (i, 0)), pl.BlockSpec((kdim, D), lambda i: (0, 0))]
    args = [a, w]
    if split_x:
        in_specs += [ctx_spec, lat_spec]
        args += list(x)
    else:
        in_specs.append(row_spec)
        args.append(x)
    in_specs += [mod_spec, pl.BlockSpec((1, D), lambda i: (0, 0))]
    args += [mods_l, gain]
    if next_mod is not None:
        in_specs.append(mod_spec)
        args.append(next_mods)
        out_specs = [row_spec, row_spec]
        out_shape = [jax.ShapeDtypeStruct((T_ALL, D), F32), jax.ShapeDtypeStruct((T_ALL, D), BF16)]
    else:
        out_specs = [ctx_spec, lat_spec]
        out_shape = [jax.ShapeDtypeStruct((T_CTX, D), F32), jax.ShapeDtypeStruct((T_LAT, D), F32)]
    return pl.pallas_call(
        functools.partial(_res_kernel, gate_k, split_x, next_mod),
        grid=(T_ALL // RES_TM,),
        in_specs=in_specs,
        out_specs=out_specs,
        out_shape=out_shape,
        compiler_params=_cparams(1),
        name="res_matmul_k%d%s" % (kdim, "_final" if next_mod is None else ""),
    )(*args)


FFN_TM = 2048
FFN_MC = 512
FFN_TN = 256
FFN_HALO = 16


def _ffn_up_kernel(h_ref, wu_ref, wg_ref, cwu_ref, cwg_ref, cbu_ref, cbg_ref, o_ref, a_scr, t_scr):
    is_ctx = pl.program_id(0) * FFN_TM < T_CTX
    zeros = jnp.zeros((FFN_HALO, D), BF16)
    n_slab = FFN_TN // 128

    def chunk(c, mc, isolated):
        slot = c % 2
        r0 = c * mc
        lo = zeros if (isolated or r0 == 0) else h_ref[r0 - FFN_HALO:r0, :]
        hi = zeros if (isolated or r0 + mc == FFN_TM) else h_ref[r0 + mc:r0 + mc + FFN_HALO, :]
        hm = jnp.concatenate([lo, h_ref[r0:r0 + mc, :], hi], axis=0)
        n_ext = mc + 2 * FFN_HALO
        half = mc // 2
        for k, w_ref in enumerate((wu_ref, wg_ref)):
            a = jnp.dot(hm, w_ref[...], preferred_element_type=F32)
            for s in range(n_slab):
                a_scr[slot, k * n_slab + s, 0:n_ext, :] = a[:, s * 128:(s + 1) * 128]

        def conv(slab, cw_ref, cb_ref, lanes):
            def rows(off):
                return a_scr[slot, slab, pl.ds(FFN_HALO + off, half, stride=2), :]
            w0, w1, w2 = cw_ref[0:1, lanes], cw_ref[1:2, lanes], cw_ref[2:3, lanes]
            b = cb_ref[:, lanes]
            om1, ev, od, ep1 = rows(-1), rows(0), rows(1), rows(2)
            return om1 * w0 + ev * w1 + od * w2 + b, ev * w0 + od * w1 + ep1 * w2 + b

        for s in range(n_slab):
            lanes = slice(s * 128, (s + 1) * 128)
            u_ev, u_od = conv(s, cwu_ref, cbu_ref, lanes)
            g_ev, g_od = conv(n_slab + s, cwg_ref, cbg_ref, lanes)
            t_scr[slot, s, pl.ds(0, half, stride=2), :] = _silu(g_ev) * u_ev
            t_scr[slot, s, pl.ds(1, half, stride=2), :] = _silu(g_od) * u_od
            o_ref[r0:r0 + mc, lanes] = t_scr[slot, s, 0:mc, :].astype(BF16)

    @pl.when(is_ctx)
    def _():
        for c in range(FFN_TM // CTX_L):
            chunk(c, CTX_L, True)

    @pl.when(jnp.logical_not(is_ctx))
    def _():
        for c in range(FFN_TM // FFN_MC):
            chunk(c, FFN_MC, False)


def _ffn_up(h, w_in, conv_w, conv_b):
    n_tiles = D_FF // FFN_TN
    n_slab = FFN_TN // 128
    return pl.pallas_call(
        _ffn_up_kernel,
        grid=(T_ALL // FFN_TM, n_tiles),
        in_specs=[
            pl.BlockSpec((FFN_TM, D), lambda i, j: (i, 0)),
            pl.BlockSpec((D, FFN_TN), lambda i, j: (0, j)),
            pl.BlockSpec((D, FFN_TN), lambda i, j: (0, n_tiles + j)),
            pl.BlockSpec((3, FFN_TN), lambda i, j: (0, j)),
            pl.BlockSpec((3, FFN_TN), lambda i, j: (0, n_tiles + j)),
            pl.BlockSpec((1, FFN_TN), lambda i, j: (0, j)),
            pl.BlockSpec((1, FFN_TN), lambda i, j: (0, n_tiles + j)),
        ],
        out_specs=pl.BlockSpec((FFN_TM, FFN_TN), lambda i, j: (i, j)),
        out_shape=jax.ShapeDtypeStruct((T_ALL, D_FF), BF16),
        scratch_shapes=[
            pltpu.VMEM((2, 2 * n_slab, FFN_MC + 2 * FFN_HALO, 128), F32),
            pltpu.VMEM((2, n_slab, FFN_MC, 128), F32),
        ],
        compiler_params=_cparams(2),
        name="ffn_up",
    )(h, w_in, w_in, conv_w, conv_w, conv_b, conv_b)


ATT_TN = 512
ATT_NQ = AH * AHD
ATT_NKV = AKV * AHD
ATT_DUP = 2 * ATT_NKV
ATT_N = ATT_NQ + 2 * ATT_DUP
ATT_Q_TILES = ATT_NQ // ATT_TN


def _dup_heads(w):
    lead = w.shape[:-1]
    w = w.reshape(lead + (AKV, 1, AHD))
    return jnp.broadcast_to(w, lead + (AKV, 2, AHD)).reshape(lead + (ATT_DUP,))


def _undup_heads(x):
    return x.reshape(x.shape[:-1] + (AKV, 2, AHD))[..., 0, :].reshape(x.shape[:-1] + (ATT_NKV,))


def _attn_proj_kernel(h_ref, w_ref, cos_ref, slo_ref, shi_ref, q_ref, kd_ref, vd_ref, k32_ref, v32_ref):
    i = pl.program_id(0)
    j = pl.program_id(1)
    is_lat = i * PROJ_TM >= T_CTX

    def chunks(epilogue):
        for m in range(PROJ_TM // PROJ_MC):
            rows = slice(m * PROJ_MC, (m + 1) * PROJ_MC)
            epilogue(jnp.dot(h_ref[rows, :], w_ref[...], preferred_element_type=F32), rows)

    def roped(o_ref):
        def ep(acc, rows):
            for s in range(ATT_TN // 128):
                cols = slice(s * 128, (s + 1) * 128)
                y = _rope_slab(acc[:, cols], cos_ref[rows, :], slo_ref[rows, :], shi_ref[rows, :], AHD // 4)
                o_ref[rows, cols] = y.astype(BF16)
        return ep

    def plain(o_ref, o32_ref=None):
        def ep(acc, rows):
            o_ref[rows, :] = acc.astype(BF16)
            if o32_ref is not None:
                o32_ref[rows, :] = acc
        return ep

    cases = (
        (j < ATT_Q_TILES, True, roped(q_ref)),
        (j < ATT_Q_TILES, False, plain(q_ref)),
        (j == ATT_Q_TILES, True, roped(kd_ref)),
        (j == ATT_Q_TILES, False, plain(kd_ref, k32_ref)),
        (j == ATT_Q_TILES + 1, True, plain(vd_ref)),
        (j == ATT_Q_TILES + 1, False, plain(vd_ref, v32_ref)),
    )
    for on_tile, on_lat, epilogue in cases:
        @pl.when(jnp.logical_and(on_tile, is_lat == on_lat))
        def _():
            chunks(epilogue)


def _attn_proj(h, w_all, tabs):
    cos, slo, shi = tabs
    n_pos = LAT_L // PROJ_TM
    n_ctx = T_CTX // PROJ_TM
    tab_spec = pl.BlockSpec((PROJ_TM, 128), lambda i, j: (i % n_pos, 0))
    row_spec = pl.BlockSpec((PROJ_TM, ATT_DUP), lambda i, j: (i, 0))
    ctx_spec = pl.BlockSpec((PROJ_TM, ATT_DUP), lambda i, j: (jnp.minimum(i, n_ctx - 1), 0))
    return pl.pallas_call(
        _attn_proj_kernel,
        grid=(T_ALL // PROJ_TM, ATT_N // ATT_TN),
        in_specs=[
            pl.BlockSpec((PROJ_TM, D), lambda i, j: (i, 0)),
            pl.BlockSpec((D, ATT_TN), lambda i, j: (0, j)),
            tab_spec, tab_spec, tab_spec,
        ],
        out_specs=[
            pl.BlockSpec((PROJ_TM, ATT_TN), lambda i, j: (i, jnp.minimum(j, ATT_Q_TILES - 1))),
            row_spec, row_spec, ctx_spec, ctx_spec,
        ],
        out_shape=[
            jax.ShapeDtypeStruct((T_ALL, ATT_NQ), BF16),
            jax.ShapeDtypeStruct((T_ALL, ATT_DUP), BF16),
            jax.ShapeDtypeStruct((T_ALL, ATT_DUP), BF16),
            jax.ShapeDtypeStruct((T_CTX, ATT_DUP), F32),
            jax.ShapeDtypeStruct((T_CTX, ATT_DUP), F32),
        ],
        compiler_params=_cparams(2),
        name="attn_proj",
    )(h, w_all, cos, slo, shi)


def _head_scores(q_ref, kh, k_dup):
    nq = q_ref.shape[0]
    nk = k_dup.shape[0]
    low = lax.broadcasted_iota(jnp.int32, (nk, 128), 1) < AHD
    zero = jnp.zeros_like(k_dup)
    k2 = jnp.concatenate([jnp.where(low, k_dup, zero), jnp.where(low, zero, k_dup)], axis=0)
    base = kh * AG * AHD
    lhs = jnp.concatenate([q_ref[:, base:base + 128], q_ref[:, base + 128:base + 256]], axis=0)
    s = lax.dot_general(lhs, k2, (((1,), (1,)), ((), ())), preferred_element_type=F32)
    return [s[0:nq, 0:nk], s[0:nq, nk:], s[nq:, 0:nk], s[nq:, nk:]]


def _row_max(s):
    m = s[:, 0:128]
    for c in range(128, s.shape[1], 128):
        m = jnp.maximum(m, s[:, c:c + 128])
    return jnp.max(m, axis=-1, keepdims=True)


def _softmax_pv_store(scores, v_dup, sink_ref, kh, o_ref):
    nq = scores[0].shape[0]
    nk = v_dup.shape[0]
    v_aug = jnp.concatenate([v_dup, jnp.ones((nk, 128), BF16)], axis=1)
    ps, tails = [], []
    for g, s in enumerate(scores):
        sink = sink_ref[kh * AG + g]
        m = jnp.maximum(_row_max(s), sink)
        ps.append(jnp.exp(s - m).astype(BF16))
        tails.append(jnp.exp(sink - m))
    pv = jnp.dot(jnp.concatenate(ps, axis=0), v_aug, preferred_element_type=F32)
    outs = []
    for g in range(AG):
        rows = slice(g * nq, (g + 1) * nq)
        outs.append(pv[rows, 0:128] / (pv[rows, 128:256] + tails[g]))
    low = lax.broadcasted_iota(jnp.int32, (nq, 128), 1) < AHD
    base = kh * AG * AHD
    o_ref[:, base:base + 128] = jnp.where(low, outs[0], outs[1]).astype(BF16)
    o_ref[:, base + 128:base + 256] = jnp.where(low, outs[2], outs[3]).astype(BF16)


def _attend_all(q_ref, sink_ref, o_ref, kv_of, bias_of):
    def scores_of(kh):
        k_dup, _ = kv_of(kh)
        return bias_of(_head_scores(q_ref, kh, k_dup))

    nxt = scores_of(0)
    for kh in range(AKV):
        cur = nxt
        if kh + 1 < AKV:
            nxt = scores_of(kh + 1)
        _softmax_pv_store(cur, kv_of(kh)[1], sink_ref, kh, o_ref)


def _ctx_attn_kernel(sink_ref, q_ref, kd_ref, vd_ref, o_ref):
    def kv_of(kh):
        cols = slice(kh * 128, (kh + 1) * 128)
        return kd_ref[:, cols], vd_ref[:, cols]

    _attend_all(q_ref, sink_ref, o_ref, kv_of, lambda scores: scores)


def _ctx_attn(sink, q, kd, vd, o_prev):
    return pl.pallas_call(
        lambda sink_ref, q_ref, kd_ref, vd_ref, prev_ref, o_ref: _ctx_attn_kernel(
            sink_ref, q_ref, kd_ref, vd_ref, o_ref),
        grid=(CTX_B,),
        in_specs=[
            pl.BlockSpec(memory_space=pltpu.SMEM),
            pl.BlockSpec((CTX_L, ATT_NQ), lambda b: (b, 0)),
            pl.BlockSpec((CTX_L, ATT_DUP), lambda b: (b, 0)),
            pl.BlockSpec((CTX_L, ATT_DUP), lambda b: (b, 0)),
            pl.BlockSpec(memory_space=pl.ANY),
        ],
        out_specs=pl.BlockSpec((CTX_L, ATT_NQ), lambda b: (b, 0)),
        out_shape=jax.ShapeDtypeStruct((T_ALL, ATT_NQ), BF16),
        input_output_aliases={4: 0},
        compiler_params=_cparams(1),
        name="ctx_attn",
    )(sink, q, kd, vd, o_prev)


BAND = 3 * ABLK


def _lat_attn_kernel(sink_ref, q_ref, kd_ref, vd_ref, ck_ref, cv_ref, o_ref):
    n = pl.program_id(1)
    start = pl.multiple_of(jnp.clip((n - 1) * ABLK, 0, LAT_L - BAND), ABLK)
    qpos = n * ABLK + lax.broadcasted_iota(jnp.int32, (ABLK, BAND), 0)
    kpos = start + lax.broadcasted_iota(jnp.int32, (ABLK, BAND), 1)
    band_bias = jnp.where(jnp.abs(qpos - kpos) <= WINDOW, 0.0, NEG_INF)

    def kv_of(kh):
        cols = slice(kh * 128, (kh + 1) * 128)
        k = jnp.concatenate([kd_ref[pl.ds(start, BAND), cols], ck_ref[0, :, cols]], axis=0)
        v = jnp.concatenate([vd_ref[pl.ds(start, BAND), cols], cv_ref[0, :, cols]], axis=0)
        return k, v

    def bias_of(scores):
        return [jnp.concatenate([s[:, :BAND] + band_bias, s[:, BAND:]], axis=1) for s in scores]

    _attend_all(q_ref, sink_ref, o_ref, kv_of, bias_of)


def _lat_attn(sink, q, kd, vd, cache_kd, cache_vd, o_prev):
    n_q = LAT_L // ABLK
    rb0 = T_CTX // ABLK
    sb0 = T_CTX // LAT_L
    seq_spec = pl.BlockSpec((LAT_L, ATT_DUP), lambda b, n: (sb0 + b, 0))
    cache_spec = pl.BlockSpec((1, CTX_L, ATT_DUP), lambda b, n: (b, 0, 0))
    return pl.pallas_call(
        lambda sink_ref, q_ref, kd_ref, vd_ref, ck_ref, cv_ref, prev_ref, o_ref: _lat_attn_kernel(
            sink_ref, q_ref, kd_ref, vd_ref, ck_ref, cv_ref, o_ref),
        grid=(LAT_B, n_q),
        in_specs=[
            pl.BlockSpec(memory_space=pltpu.SMEM),
            pl.BlockSpec((ABLK, ATT_NQ), lambda b, n: (rb0 + b * n_q + n, 0)),
            seq_spec, seq_spec, cache_spec, cache_spec,
            pl.BlockSpec(memory_space=pl.ANY),
        ],
        out_specs=pl.BlockSpec((ABLK, ATT_NQ), lambda b, n: (rb0 + b * n_q + n, 0)),
        out_shape=jax.ShapeDtypeStruct((T_ALL, ATT_NQ), BF16),
        input_output_aliases={6: 0},
        compiler_params=_cparams(2),
        name="lat_attn",
    )(sink, q, kd, vd, cache_kd, cache_vd, o_prev)


def _log_sigmoid(x):
    return -(jnp.maximum(-x, 0.0) + jnp.log1p(jnp.exp(-jnp.abs(x))))


def kernel(x_prompt, x_sample, state_ret_fwd, state_ret_bwd, cache_attn_k, cache_attn_v, c, c_ctx, ada_w, ada_b, norm_mix, norm_ffn, ret_wq, ret_wk, ret_wv, ret_wg_fwd, ret_wg_bwd, ret_wo, ret_decay_fwd, ret_decay_bwd, attn_wq, attn_wk, attn_wv, attn_wo, attn_sink, ffn_w_in, ffn_conv_w, ffn_conv_b, ffn_w_out, norm_final):
    x_ctx = x_prompt.reshape(T_CTX, D)
    x_lat = x_sample.reshape(T_LAT, D)

    cv = jnp.concatenate([c, c_ctx[None, :], jnp.zeros((MOD_ROWS - LAT_B - 1, D), F32)], axis=0)
    mods = _adaln(cv, ada_w, ada_b).reshape(ada_w.shape[0], MOD_ROWS, N_MOD, D)

    h = _first_mod(x_ctx, x_lat, norm_mix[0:1], mods[0])
    w_ret = jnp.concatenate([ret_wq[0], ret_wk[0], ret_wv[0], ret_wg_fwd[0], ret_wg_bwd[0]], axis=1).astype(BF16)
    qkvg = _ret_proj(h, w_ret, _rope_tables(RDK))
    lg = jnp.stack([_log_sigmoid(ret_decay_fwd[0].astype(F32)), _log_sigmoid(ret_decay_bwd[0].astype(F32))])
    y0 = jnp.zeros((T_ALL, RH * RDV), BF16)
    y0, s_f, s_b = _ret_scan(qkvg, lg, y0, n_blocks=T_CTX // RET_BLOCK, n_seq=RET_BLOCK // CTX_L,
                             seq_len=CTX_L, row_block0=0, emit_state=True)
    (y0,) = _ret_scan(qkvg, lg, y0, n_blocks=LAT_B, n_seq=1, seq_len=LAT_L, row_block0=T_CTX // RET_BLOCK,
                      s0f=state_ret_fwd, s0b=state_ret_bwd)
    x, h = _res_matmul(y0, ret_wo[0].astype(BF16), (x_ctx, x_lat), mods[0], 2,
                       norm_ffn[0:1], mods[0], (3, 4))
    act = _ffn_up(h, ffn_w_in[0].astype(BF16), ffn_conv_w[0], ffn_conv_b[0:1])
    x, h = _res_matmul(act, ffn_w_out[0].astype(BF16), x, mods[0], 5, norm_mix[1:2], mods[1], (0, 1))

    w_att = jnp.concatenate([attn_wq[0] * (AHD ** -0.5), _dup_heads(attn_wk[0]), _dup_heads(attn_wv[0])],
                            axis=1).astype(BF16)
    cos, slo, shi = _rope_tables(AHD)
    tabs = tuple(jnp.tile(t, (1, 128 // AHD)) for t in (cos, slo, shi))
    q, kd, vd, k32, v32 = _attn_proj(h, w_att, tabs)
    sink = attn_sink[0].astype(F32)
    cache_kd = _dup_heads(cache_attn_k[:, 0].reshape(LAT_B, CTX_L, ATT_NKV)).astype(BF16)
    cache_vd = _dup_heads(cache_attn_v[:, 0].reshape(LAT_B, CTX_L, ATT_NKV)).astype(BF16)
    a = _lat_attn(sink, q, kd, vd, cache_kd, cache_vd, jnp.zeros((T_ALL, ATT_NQ), BF16))
    a = _ctx_attn(sink, q, kd, vd, a)
    x, h = _res_matmul(a, attn_wo[0].astype(BF16), x, mods[1], 2, norm_ffn[1:2], mods[1], (3, 4))
    act = _ffn_up(h, ffn_w_in[1].astype(BF16), ffn_conv_w[1], ffn_conv_b[1:2])
    y_ctx, y_lat = _res_matmul(act, ffn_w_out[1].astype(BF16), x, mods[1], 5, norm_final[None, :])

    y_prompt = y_ctx.reshape(CTX_B, CTX_L, D)
    y_sample = y_lat.reshape(LAT_B, LAT_L, D)
    new_k = _undup_heads(k32).reshape(CTX_B, 1, CTX_L, AKV, AHD)
    new_v = _undup_heads(v32).reshape(CTX_B, 1, CTX_L, AKV, AHD)
    return y_prompt, y_sample, s_f, s_b, new_k, new_v
```

```python
import functools

import jax
import jax.numpy as jnp
from jax import lax
from jax.experimental import pallas as pl
from jax.experimental.pallas import tpu as pltpu

D = 1024
CTX_B, CTX_L = 32, 256
LAT_B, LAT_L = 8, 2048
T_CTX = CTX_B * CTX_L
T_LAT = LAT_B * LAT_L
T_ALL = T_CTX + T_LAT
GRID_W = 64
N_MOD = 6
MOD_ROWS = 16
CTX_MOD_ROW = LAT_B
RH, RDK, RDV = 4, 256, 512
RC = 256
AH, AKV, AG, AHD = 16, 4, 4, 64
WINDOW = 128
ABLK = 128
D_FF = 2816
ROPE_BASE = 10000.0
EPS = 1e-6
NEG_INF = -1e30

BF16 = jnp.bfloat16
F32 = jnp.float32

VMEM_LIMIT_BYTES = 56 * 1024 * 1024
NORM_ROWS = 32


def _cparams(n_axes):
    return pltpu.CompilerParams(dimension_semantics=("arbitrary",) * n_axes,
                                vmem_limit_bytes=VMEM_LIMIT_BYTES)


def _silu(x):
    return x / (1.0 + jnp.exp(-x))


def _mod_row(i, tm):
    r0 = i * tm
    return jnp.where(r0 < T_CTX, CTX_MOD_ROW, (r0 - T_CTX) // LAT_L)


def _modulate(x, g_ref, mod_ref, shift_k, scale_k):
    ms = jnp.mean(x * x, axis=-1, keepdims=True)
    y = x * lax.rsqrt(ms + EPS) * g_ref[...]
    return y * (1.0 + mod_ref[0, scale_k:scale_k + 1, :]) + mod_ref[0, shift_k:shift_k + 1, :]


MOD_TM = 512
N_CTX_BLK = T_CTX // MOD_TM


def _ctx_blk(i):
    return jnp.minimum(i, N_CTX_BLK - 1)


def _lat_blk(i):
    return jnp.maximum(i - N_CTX_BLK, 0)


def _first_mod_kernel(xc_ref, xl_ref, g_ref, mod_ref, h_ref):
    i = pl.program_id(0)

    def run(x_ref):
        for r in range(0, MOD_TM, NORM_ROWS):
            rows = slice(r, r + NORM_ROWS)
            h_ref[rows, :] = _modulate(x_ref[rows, :], g_ref, mod_ref, 0, 1).astype(BF16)

    @pl.when(i < N_CTX_BLK)
    def _():
        run(xc_ref)

    @pl.when(i >= N_CTX_BLK)
    def _():
        run(xl_ref)


def _first_mod(x_ctx, x_lat, gain, mods_l):
    return pl.pallas_call(
        _first_mod_kernel,
        grid=(T_ALL // MOD_TM,),
        in_specs=[
            pl.BlockSpec((MOD_TM, D), lambda i: (_ctx_blk(i), 0)),
            pl.BlockSpec((MOD_TM, D), lambda i: (_lat_blk(i), 0)),
            pl.BlockSpec((1, D), lambda i: (0, 0)),
            pl.BlockSpec((1, N_MOD, D), lambda i: (_mod_row(i, MOD_TM), 0, 0)),
        ],
        out_specs=pl.BlockSpec((MOD_TM, D), lambda i: (i, 0)),
        out_shape=jax.ShapeDtypeStruct((T_ALL, D), BF16),
        compiler_params=_cparams(1),
        name="first_mod",
    )(x_ctx, x_lat, gain, mods_l)


ADA_TN = 512


def _adaln_kernel(cv_ref, w_ref, b_ref, o_ref):
    s = _silu(cv_ref[...]).astype(BF16)
    w = w_ref[0].astype(BF16)
    o_ref[0] = jnp.dot(s, w, preferred_element_type=F32) + b_ref[0]


def _adaln(cv, ada_w, ada_b):
    depth = ada_w.shape[0]
    n = N_MOD * D
    return pl.pallas_call(
        _adaln_kernel,
        grid=(depth, n // ADA_TN),
        in_specs=[
            pl.BlockSpec((MOD_ROWS, D), lambda l, j: (0, 0)),
            pl.BlockSpec((1, D, ADA_TN), lambda l, j: (l, 0, j)),
            pl.BlockSpec((1, 1, ADA_TN), lambda l, j: (l, 0, j)),
        ],
        out_specs=pl.BlockSpec((1, MOD_ROWS, ADA_TN), lambda l, j: (l, 0, j)),
        out_shape=jax.ShapeDtypeStruct((depth, MOD_ROWS, n), F32),
        compiler_params=_cparams(2),
        name="adaln",
    )(cv, ada_w, ada_b.reshape(depth, 1, n))


def _rope_tables(head_dim):
    half = head_dim // 2
    n_freq = half // 2
    inv = ROPE_BASE ** (-jnp.arange(n_freq, dtype=F32) / n_freq)
    t = jnp.arange(LAT_L)
    rows = (t // GRID_W).astype(F32)[:, None] * inv[None, :]
    cols = (t % GRID_W).astype(F32)[:, None] * inv[None, :]
    zero = jnp.zeros_like(rows)
    cos = jnp.concatenate([jnp.cos(rows)] * 2 + [jnp.cos(cols)] * 2, axis=-1)
    sin_lo = jnp.concatenate([-jnp.sin(rows), zero, -jnp.sin(cols), zero], axis=-1)
    sin_hi = jnp.concatenate([zero, jnp.sin(rows), zero, jnp.sin(cols)], axis=-1)
    return cos, sin_lo, sin_hi


def _rope_slab(x, cos, sin_lo, sin_hi, n_freq):
    up = pltpu.roll(x, 128 - n_freq, 1)
    dn = pltpu.roll(x, n_freq, 1)
    return x * cos + up * sin_lo + dn * sin_hi


PROJ_TM = 1024
PROJ_MC = 256
RET_TN = 1024
RET_N = 2 * RH * RDK + 3 * RH * RDV


def _ret_proj_kernel(h_ref, w_ref, cos_ref, slo_ref, shi_ref, o_ref):
    i = pl.program_id(0)
    j = pl.program_id(1)
    is_lat = i * PROJ_TM >= T_CTX

    def chunks(epilogue):
        for m in range(PROJ_TM // PROJ_MC):
            rows = slice(m * PROJ_MC, (m + 1) * PROJ_MC)
            acc = jnp.dot(h_ref[rows, :], w_ref[...], preferred_element_type=F32)
            epilogue(acc, rows)

    def plain(acc, rows):
        o_ref[rows, :] = acc.astype(BF16)

    def gate(acc, rows):
        o_ref[rows, :] = _silu(acc).astype(BF16)

    def rope(scale):
        def ep(acc, rows):
            for s in range(RET_TN // 128):
                cols = slice(s * 128, (s + 1) * 128)
                tcols = slice((s % 2) * 128, (s % 2 + 1) * 128)
                x = acc[:, cols]
                if scale != 1.0:
                    x = x * scale
                y = _rope_slab(x, cos_ref[rows, tcols], slo_ref[rows, tcols], shi_ref[rows, tcols], RDK // 4)
                o_ref[rows, cols] = y.astype(BF16)
        return ep

    def scaled(scale):
        def ep(acc, rows):
            o_ref[rows, :] = (acc * scale).astype(BF16)
        return ep

    k_scale = RDK ** -0.5

    @pl.when(jnp.logical_and(j == 0, is_lat))
    def _():
        chunks(rope(1.0))

    @pl.when(jnp.logical_and(j == 0, jnp.logical_not(is_lat)))
    def _():
        chunks(plain)

    @pl.when(jnp.logical_and(j == 1, is_lat))
    def _():
        chunks(rope(k_scale))

    @pl.when(jnp.logical_and(j == 1, jnp.logical_not(is_lat)))
    def _():
        chunks(scaled(k_scale))

    @pl.when(jnp.logical_and(j >= 2, j < 4))
    def _():
        chunks(plain)

    @pl.when(j >= 4)
    def _():
        chunks(gate)


def _ret_proj(h, w_all, tabs):
    cos, slo, shi = tabs
    n_pos = LAT_L // PROJ_TM
    tab_spec = pl.BlockSpec((PROJ_TM, RDK), lambda i, j: (i % n_pos, 0))
    return pl.pallas_call(
        _ret_proj_kernel,
        grid=(T_ALL // PROJ_TM, RET_N // RET_TN),
        in_specs=[
            pl.BlockSpec((PROJ_TM, D), lambda i, j: (i, 0)),
            pl.BlockSpec((D, RET_TN), lambda i, j: (0, j)),
            tab_spec, tab_spec, tab_spec,
        ],
        out_specs=pl.BlockSpec((PROJ_TM, RET_TN), lambda i, j: (i, j)),
        out_shape=jax.ShapeDtypeStruct((T_ALL, RET_N), BF16),
        compiler_params=_cparams(2),
        name="ret_proj",
    )(h, w_all, cos, slo, shi)


RET_BLOCK = 2048


def _ret_scan_kernel(n_seq, seq_len, has_init, emit_state, lg_ref, q_ref, k_ref, v_ref, gf_ref, gb_ref, *rest):
    rest = list(rest)
    s0f_ref = rest.pop(0) if has_init else None
    s0b_ref = rest.pop(0) if has_init else None
    y_ref = rest.pop(0)
    sf_ref = rest.pop(0) if emit_state else None
    sb_ref = rest.pop(0) if emit_state else None
    ybuf_ref, stf_ref, stb_ref = rest

    hd = pl.program_id(1)
    n_chunks = seq_len // RC
    idx_col = lax.broadcasted_iota(jnp.int32, (RC, 1), 0).astype(F32)
    idx_row = lax.broadcasted_iota(jnp.int32, (1, RC), 1).astype(F32)
    diff = idx_col - idx_row

    lg_f = lg_ref[0, hd]
    lg_b = lg_ref[1, hd]
    dirs = {
        "f": dict(dmat=jnp.where(diff >= 0, jnp.exp(lg_f * jnp.maximum(diff, 0.0)), 0.0),
                  xi=jnp.exp(lg_f * (idx_col + 1.0)), zeta=jnp.exp(lg_f * (RC - 1.0 - idx_row)),
                  gch=jnp.exp(jnp.ones((1, RDV), F32) * (lg_f * RC)),
                  st_ref=stf_ref, s0_ref=s0f_ref, gate_ref=gf_ref, out_ref=sf_ref),
        "b": dict(dmat=jnp.where(diff <= 0, jnp.exp(lg_b * jnp.maximum(-diff, 0.0)), 0.0),
                  xi=jnp.exp(lg_b * (RC - idx_col)), zeta=jnp.exp(lg_b * idx_row),
                  gch=jnp.exp(jnp.ones((1, RDV), F32) * (lg_b * RC)),
                  st_ref=stb_ref, s0_ref=s0b_ref, gate_ref=gb_ref, out_ref=sb_ref),
    }

    def shared(rows):
        q = q_ref[rows, :]
        k = k_ref[rows, :]
        s = lax.dot_general(q, k, (((1,), (1,)), ((), ())), preferred_element_type=F32)
        return q, s, k.astype(F32).T, v_ref[rows, :]

    def chunk_dir(d, sh, rows, first, last, sq):
        q, s, kt, v = sh
        sm = (s * d["dmat"]).astype(BF16)
        kzt = (kt * d["zeta"]).astype(BF16)
        both = jnp.dot(jnp.concatenate([sm, kzt], axis=0), v, preferred_element_type=F32)
        o, upd = both[:RC], both[RC:]
        st = None
        if has_init and first:
            st = d["s0_ref"][0, 0, 0]
        elif not first:
            st = d["st_ref"][...]
        if st is not None:
            o = o + jnp.dot(q, st.astype(BF16), preferred_element_type=F32) * d["xi"]
            upd = st * d["gch"] + upd
        if last and emit_state:
            d["out_ref"][sq, 0, 0] = upd
        elif not last:
            d["st_ref"][...] = upd
        o = o * lax.rsqrt(jnp.mean(o * o, axis=-1, keepdims=True) + EPS)
        return o * d["gate_ref"][rows, :].astype(F32)

    for sq in range(n_seq):
        half_done = set()

        def emit(c, val):
            rows = slice(sq * seq_len + c * RC, sq * seq_len + (c + 1) * RC)
            if c in half_done:
                y_ref[rows, :] = (ybuf_ref[rows, :] + val).astype(BF16)
            else:
                ybuf_ref[rows, :] = val
                half_done.add(c)

        for t in range(n_chunks):
            cf, cb = t, n_chunks - 1 - t
            rows_f = slice(sq * seq_len + cf * RC, sq * seq_len + (cf + 1) * RC)
            rows_b = slice(sq * seq_len + cb * RC, sq * seq_len + (cb + 1) * RC)
            sh_f = shared(rows_f)
            sh_b = sh_f if cb == cf else shared(rows_b)
            first, last = t == 0, t == n_chunks - 1
            of = chunk_dir(dirs["f"], sh_f, rows_f, first, last, sq)
            ob = chunk_dir(dirs["b"], sh_b, rows_b, first, last, sq)
            if cf == cb:
                y_ref[rows_f, :] = (of + ob).astype(BF16)
            else:
                emit(cf, of)
                emit(cb, ob)


def _ret_scan(qkvg, lg, y_prev, *, n_blocks, n_seq, seq_len, row_block0, s0f=None, s0b=None, emit_state=False):
    has_init = s0f is not None
    kq = RDK
    n_qk = RH * RDK // kq
    in_specs = [
        pl.BlockSpec(memory_space=pltpu.SMEM),
        pl.BlockSpec((RET_BLOCK, RDK), lambda b, h: (row_block0 + b, h)),
        pl.BlockSpec((RET_BLOCK, RDK), lambda b, h: (row_block0 + b, n_qk + h)),
        pl.BlockSpec((RET_BLOCK, RDV), lambda b, h: (row_block0 + b, n_qk + h)),
        pl.BlockSpec((RET_BLOCK, RDV), lambda b, h: (row_block0 + b, n_qk + RH + h)),
        pl.BlockSpec((RET_BLOCK, RDV), lambda b, h: (row_block0 + b, n_qk + 2 * RH + h)),
    ]
    args = [lg, qkvg, qkvg, qkvg, qkvg, qkvg]
    if has_init:
        st_spec = pl.BlockSpec((1, 1, 1, RDK, RDV), lambda b, h: (b, 0, h, 0, 0))
        in_specs += [st_spec, st_spec]
        args += [s0f, s0b]
    in_specs.append(pl.BlockSpec(memory_space=pl.ANY))
    args.append(y_prev)
    y_shape = jax.ShapeDtypeStruct((T_ALL, RH * RDV), BF16)
    out_specs = [pl.BlockSpec((RET_BLOCK, RDV), lambda b, h: (row_block0 + b, h))]
    out_shape = [y_shape]
    if emit_state:
        so_spec = pl.BlockSpec((n_seq, 1, 1, RDK, RDV), lambda b, h: (b, 0, h, 0, 0))
        so_shape = jax.ShapeDtypeStruct((n_blocks * n_seq, 1, RH, RDK, RDV), F32)
        out_specs += [so_spec, so_spec]
        out_shape += [so_shape, so_shape]

    def body(*refs):
        refs = list(refs)
        n_in = len(in_specs)
        ins = refs[:n_in - 1]
        _ret_scan_kernel(n_seq, seq_len, has_init, emit_state, *ins, *refs[n_in:])

    return pl.pallas_call(
        body,
        grid=(n_blocks, RH),
        in_specs=in_specs,
        out_specs=out_specs,
        out_shape=out_shape,
        scratch_shapes=[pltpu.VMEM((RET_BLOCK, RDV), F32), pltpu.VMEM((RDK, RDV), F32),
                        pltpu.VMEM((RDK, RDV), F32)],
        input_output_aliases={len(args) - 1: 0},
        compiler_params=_cparams(2),
        name="ret_scan_ctx" if emit_state else "ret_scan_lat",
    )(*args)


RES_TM = MOD_TM
RES_MC = 256


def _res_kernel(gate_k, split_x, next_mod, a_ref, w_ref, *rest):
    rest = list(rest)
    x_refs = [rest.pop(0), rest.pop(0)] if split_x else [rest.pop(0)]
    mod_ref, gain_ref, nmod_ref, xo_ref, h_ref = rest
    gate = mod_ref[0, gate_k:gate_k + 1, :]

    def run(x_ref):
        for m in range(RES_TM // RES_MC):
            rows = slice(m * RES_MC, (m + 1) * RES_MC)
            acc = jnp.dot(a_ref[rows, :], w_ref[...], preferred_element_type=F32)
            xn = x_ref[rows, :] + gate * acc
            xo_ref[rows, :] = xn
            h_ref[rows, :] = _modulate(xn, gain_ref, nmod_ref, *next_mod).astype(BF16)

    if not split_x:
        run(x_refs[0])
        return
    is_ctx = pl.program_id(0) < N_CTX_BLK

    @pl.when(is_ctx)
    def _():
        run(x_refs[0])

    @pl.when(jnp.logical_not(is_ctx))
    def _():
        run(x_refs[1])


def _res_matmul(a, w, x, mods_l, gate_k, gain, next_mods, next_mod):
    kdim = a.shape[1]
    split_x = isinstance(x, tuple)
    row_spec = pl.BlockSpec((RES_TM, D), lambda i: (i, 0))
    mod_spec = pl.BlockSpec((1, N_MOD, D), lambda i: (_mod_row(i, RES_TM), 0, 0))
    in_specs = [pl.BlockSpec((RES_TM, kdim), lambda i: (i, 0)), pl.BlockSpec((kdim, D), lambda i: (0, 0))]
    args = [a, w]
    if split_x:
        in_specs += [pl.BlockSpec((RES_TM, D), lambda i: (_ctx_blk(i), 0)),
                     pl.BlockSpec((RES_TM, D), lambda i: (_lat_blk(i), 0))]
        args += list(x)
    else:
        in_specs.append(row_spec)
        args.append(x)
    in_specs += [mod_spec, pl.BlockSpec((1, D), lambda i: (0, 0)), mod_spec]
    args += [mods_l, gain, next_mods]
    return pl.pallas_call(
        functools.partial(_res_kernel, gate_k, split_x, next_mod),
        grid=(T_ALL // RES_TM,),
        in_specs=in_specs,
        out_specs=[row_spec, row_spec],
        out_shape=[jax.ShapeDtypeStruct((T_ALL, D), F32), jax.ShapeDtypeStruct((T_ALL, D), BF16)],
        compiler_params=_cparams(1),
        name="res_matmul_k%d" % kdim,
    )(*args)


FFN_TM = 1024
FFN_MC = 512
FFN_TN = 256
FFN_HALO = 16
FFN_TILES = D_FF // FFN_TN
FFN_CTX_BLK = T_CTX // FFN_TM
FFN_EPI_ROWS = 256


def _ffn_kernel(gate_k, next_mod, h_ref, hp_ref, hn_ref, wu_ref, wg_ref, cwu_ref, cwg_ref, cbu_ref, cbg_ref,
                wo_ref, x_ref, mod_ref, gain_ref, *rest):
    rest = list(rest)
    nmod_ref = rest.pop(0) if next_mod is not None else None
    out_a, out_b, a_scr, t_scr, acc_ref = rest
    i = pl.program_id(0)
    j = pl.program_id(1)
    is_ctx = i < FFN_CTX_BLK
    first_half = (i - FFN_CTX_BLK) % (LAT_L // FFN_TM) == 0
    zeros = jnp.zeros((FFN_HALO, D), BF16)
    n_slab = FFN_TN // 128

    @pl.when(j == 0)
    def _():
        acc_ref[...] = jnp.zeros((FFN_TM, D), F32)

    def chunk(c, mc, lo, hi):
        slot = c % 2
        r0 = c * mc
        hm = jnp.concatenate([lo, h_ref[r0:r0 + mc, :], hi], axis=0)
        n_ext = mc + 2 * FFN_HALO
        half = mc // 2
        for k, w_ref in enumerate((wu_ref, wg_ref)):
            a = jnp.dot(hm, w_ref[...], preferred_element_type=F32)
            for s in range(n_slab):
                a_scr[slot, k * n_slab + s, 0:n_ext, :] = a[:, s * 128:(s + 1) * 128]

        def conv(slab, cw_ref, cb_ref, lanes):
            def rows(off):
                return a_scr[slot, slab, pl.ds(FFN_HALO + off, half, stride=2), :]
            w0, w1, w2 = cw_ref[0:1, lanes], cw_ref[1:2, lanes], cw_ref[2:3, lanes]
            b = cb_ref[:, lanes]
            om1, ev, od, ep1 = rows(-1), rows(0), rows(1), rows(2)
            return om1 * w0 + ev * w1 + od * w2 + b, ev * w0 + od * w1 + ep1 * w2 + b

        acts = []
        for s in range(n_slab):
            lanes = slice(s * 128, (s + 1) * 128)
            u_ev, u_od = conv(s, cwu_ref, cbu_ref, lanes)
            g_ev, g_od = conv(n_slab + s, cwg_ref, cbg_ref, lanes)
            t_scr[slot, s, pl.ds(0, half, stride=2), :] = _silu(g_ev) * u_ev
            t_scr[slot, s, pl.ds(1, half, stride=2), :] = _silu(g_od) * u_od
            acts.append(t_scr[slot, s, 0:mc, :].astype(BF16))
        act = jnp.concatenate(acts, axis=1)
        acc_ref[r0:r0 + mc, :] += jnp.dot(act, wo_ref[...], preferred_element_type=F32)

    @pl.when(is_ctx)
    def _():
        for c in range(FFN_TM // CTX_L):
            chunk(c, CTX_L, zeros, zeros)

    @pl.when(jnp.logical_not(is_ctx))
    def _():
        n = FFN_TM // FFN_MC
        for c in range(n):
            r0 = c * FFN_MC
            lo = jnp.where(first_half, zeros, hp_ref[...]) if c == 0 else h_ref[r0 - FFN_HALO:r0, :]
            hi = (jnp.where(first_half, hn_ref[...], zeros) if c == n - 1
                  else h_ref[r0 + FFN_MC:r0 + FFN_MC + FFN_HALO, :])
            chunk(c, FFN_MC, lo, hi)

    def finish(y_ref):
        gate = mod_ref[0, gate_k:gate_k + 1, :]
        for r in range(0, FFN_TM, FFN_EPI_ROWS):
            rows = slice(r, r + FFN_EPI_ROWS)
            xn = x_ref[rows, :] + gate * acc_ref[rows, :]
            if next_mod is not None:
                out_a[rows, :] = xn
                out_b[rows, :] = _modulate(xn, gain_ref, nmod_ref, *next_mod).astype(BF16)
            else:
                ms = jnp.mean(xn * xn, axis=-1, keepdims=True)
                y_ref[rows, :] = xn * lax.rsqrt(ms + EPS) * gain_ref[...]

    last = j == FFN_TILES - 1
    if next_mod is not None:
        @pl.when(last)
        def _():
            finish(None)
    else:
        @pl.when(jnp.logical_and(last, is_ctx))
        def _():
            finish(out_a)

        @pl.when(jnp.logical_and(last, jnp.logical_not(is_ctx)))
        def _():
            finish(out_b)


def _ffn(h, w_in, conv_w, conv_b, w_out, x, mods_l, gate_k, gain, next_mods=None, next_mod=None):
    n_slab = FFN_TN // 128
    halo_blocks = FFN_TM // FFN_HALO
    last_halo = T_ALL // FFN_HALO - 1
    row_spec = pl.BlockSpec((FFN_TM, D), lambda i, j: (i, 0))
    mod_spec = pl.BlockSpec((1, N_MOD, D), lambda i, j: (_mod_row(i, FFN_TM), 0, 0))
    in_specs = [
        row_spec,
        pl.BlockSpec((FFN_HALO, D), lambda i, j: (jnp.maximum(i * halo_blocks - 1, 0), 0)),
        pl.BlockSpec((FFN_HALO, D), lambda i, j: (jnp.minimum((i + 1) * halo_blocks, last_halo), 0)),
        pl.BlockSpec((D, FFN_TN), lambda i, j: (0, j)),
        pl.BlockSpec((D, FFN_TN), lambda i, j: (0, FFN_TILES + j)),
        pl.BlockSpec((3, FFN_TN), lambda i, j: (0, j)),
        pl.BlockSpec((3, FFN_TN), lambda i, j: (0, FFN_TILES + j)),
        pl.BlockSpec((1, FFN_TN), lambda i, j: (0, j)),
        pl.BlockSpec((1, FFN_TN), lambda i, j: (0, FFN_TILES + j)),
        pl.BlockSpec((FFN_TN, D), lambda i, j: (j, 0)),
        row_spec,
        mod_spec,
        pl.BlockSpec((1, D), lambda i, j: (0, 0)),
    ]
    args = [h, h, h, w_in, w_in, conv_w, conv_w, conv_b, conv_b, w_out, x, mods_l, gain]
    if next_mod is not None:
        in_specs.append(mod_spec)
        args.append(next_mods)
        out_specs = [row_spec, row_spec]
        out_shape = [jax.ShapeDtypeStruct((T_ALL, D), F32), jax.ShapeDtypeStruct((T_ALL, D), BF16)]
    else:
        out_specs = [
            pl.BlockSpec((FFN_TM, D), lambda i, j: (jnp.minimum(i, FFN_CTX_BLK - 1), 0)),
            pl.BlockSpec((FFN_TM, D), lambda i, j: (jnp.maximum(i - FFN_CTX_BLK, 0), 0)),
        ]
        out_shape = [jax.ShapeDtypeStruct((T_CTX, D), F32), jax.ShapeDtypeStruct((T_LAT, D), F32)]
    return pl.pallas_call(
        functools.partial(_ffn_kernel, gate_k, next_mod),
        grid=(T_ALL // FFN_TM, FFN_TILES),
        in_specs=in_specs,
        out_specs=out_specs,
        out_shape=out_shape,
        scratch_shapes=[
            pltpu.VMEM((2, 2 * n_slab, FFN_MC + 2 * FFN_HALO, 128), F32),
            pltpu.VMEM((2, n_slab, FFN_MC, 128), F32),
            pltpu.VMEM((FFN_TM, D), F32),
        ],
        compiler_params=_cparams(2),
        name="ffn" if next_mod is not None else "ffn_final",
    )(*args)


ATT_TN = 512
ATT_NQ = AH * AHD
ATT_NKV = AKV * AHD
ATT_DUP = 2 * ATT_NKV
ATT_N = ATT_NQ + 2 * ATT_DUP
ATT_Q_TILES = ATT_NQ // ATT_TN


def _dup_heads(w):
    lead = w.shape[:-1]
    w = w.reshape(lead + (AKV, 1, AHD))
    return jnp.broadcast_to(w, lead + (AKV, 2, AHD)).reshape(lead + (ATT_DUP,))


def _undup_heads(x):
    return x.reshape(x.shape[:-1] + (AKV, 2, AHD))[..., 0, :].reshape(x.shape[:-1] + (ATT_NKV,))


def _attn_proj_kernel(h_ref, w_ref, cos_ref, slo_ref, shi_ref, q_ref, kd_ref, vd_ref, k32_ref, v32_ref):
    i = pl.program_id(0)
    j = pl.program_id(1)
    is_lat = i * PROJ_TM >= T_CTX

    def chunks(epilogue):
        for m in range(PROJ_TM // PROJ_MC):
            rows = slice(m * PROJ_MC, (m + 1) * PROJ_MC)
            epilogue(jnp.dot(h_ref[rows, :], w_ref[...], preferred_element_type=F32), rows)

    def roped(o_ref):
        def ep(acc, rows):
            for s in range(ATT_TN // 128):
                cols = slice(s * 128, (s + 1) * 128)
                y = _rope_slab(acc[:, cols], cos_ref[rows, :], slo_ref[rows, :], shi_ref[rows, :], AHD // 4)
                o_ref[rows, cols] = y.astype(BF16)
        return ep

    def plain(o_ref, o32_ref=None):
        def ep(acc, rows):
            o_ref[rows, :] = acc.astype(BF16)
            if o32_ref is not None:
                o32_ref[rows, :] = acc
        return ep

    cases = (
        (j < ATT_Q_TILES, True, roped(q_ref)),
        (j < ATT_Q_TILES, False, plain(q_ref)),
        (j == ATT_Q_TILES, True, roped(kd_ref)),
        (j == ATT_Q_TILES, False, plain(kd_ref, k32_ref)),
        (j == ATT_Q_TILES + 1, True, plain(vd_ref)),
        (j == ATT_Q_TILES + 1, False, plain(vd_ref, v32_ref)),
    )
    for on_tile, on_lat, epilogue in cases:
        @pl.when(jnp.logical_and(on_tile, is_lat == on_lat))
        def _():
            chunks(epilogue)


def _attn_proj(h, w_all, tabs):
    cos, slo, shi = tabs
    n_pos = LAT_L // PROJ_TM
    n_ctx = T_CTX // PROJ_TM
    tab_spec = pl.BlockSpec((PROJ_TM, 128), lambda i, j: (i % n_pos, 0))
    row_spec = pl.BlockSpec((PROJ_TM, ATT_DUP), lambda i, j: (i, 0))
    ctx_spec = pl.BlockSpec((PROJ_TM, ATT_DUP), lambda i, j: (jnp.minimum(i, n_ctx - 1), 0))
    return pl.pallas_call(
        _attn_proj_kernel,
        grid=(T_ALL // PROJ_TM, ATT_N // ATT_TN),
        in_specs=[
            pl.BlockSpec((PROJ_TM, D), lambda i, j: (i, 0)),
            pl.BlockSpec((D, ATT_TN), lambda i, j: (0, j)),
            tab_spec, tab_spec, tab_spec,
        ],
        out_specs=[
            pl.BlockSpec((PROJ_TM, ATT_TN), lambda i, j: (i, jnp.minimum(j, ATT_Q_TILES - 1))),
            row_spec, row_spec, ctx_spec, ctx_spec,
        ],
        out_shape=[
            jax.ShapeDtypeStruct((T_ALL, ATT_NQ), BF16),
            jax.ShapeDtypeStruct((T_ALL, ATT_DUP), BF16),
            jax.ShapeDtypeStruct((T_ALL, ATT_DUP), BF16),
            jax.ShapeDtypeStruct((T_CTX, ATT_DUP), F32),
            jax.ShapeDtypeStruct((T_CTX, ATT_DUP), F32),
        ],
        compiler_params=_cparams(2),
        name="attn_proj",
    )(h, w_all, cos, slo, shi)


def _head_scores(q_ref, kh, k_dup):
    nq = q_ref.shape[0]
    nk = k_dup.shape[0]
    low = lax.broadcasted_iota(jnp.int32, (nk, 128), 1) < AHD
    zero = jnp.zeros_like(k_dup)
    k2 = jnp.concatenate([jnp.where(low, k_dup, zero), jnp.where(low, zero, k_dup)], axis=0)
    base = kh * AG * AHD
    lhs = jnp.concatenate([q_ref[:, base:base + 128], q_ref[:, base + 128:base + 256]], axis=0)
    s = lax.dot_general(lhs, k2, (((1,), (1,)), ((), ())), preferred_element_type=F32)
    return [s[0:nq, 0:nk], s[0:nq, nk:], s[nq:, 0:nk], s[nq:, nk:]]


def _row_max(s):
    m = s[:, 0:128]
    for c in range(128, s.shape[1], 128):
        m = jnp.maximum(m, s[:, c:c + 128])
    return jnp.max(m, axis=-1, keepdims=True)


def _softmax_pv_store(scores, v_dup, sink_ref, kh, o_ref):
    nq = scores[0].shape[0]
    nk = v_dup.shape[0]
    v_aug = jnp.concatenate([v_dup, jnp.ones((nk, 128), BF16)], axis=1)
    ps, tails = [], []
    for g, s in enumerate(scores):
        sink = sink_ref[kh * AG + g]
        m = jnp.maximum(_row_max(s), sink)
        ps.append(jnp.exp(s - m).astype(BF16))
        tails.append(jnp.exp(sink - m))
    pv = jnp.dot(jnp.concatenate(ps, axis=0), v_aug, preferred_element_type=F32)
    outs = []
    for g in range(AG):
        rows = slice(g * nq, (g + 1) * nq)
        outs.append(pv[rows, 0:128] / (pv[rows, 128:256] + tails[g]))
    low = lax.broadcasted_iota(jnp.int32, (nq, 128), 1) < AHD
    base = kh * AG * AHD
    o_ref[:, base:base + 128] = jnp.where(low, outs[0], outs[1]).astype(BF16)
    o_ref[:, base + 128:base + 256] = jnp.where(low, outs[2], outs[3]).astype(BF16)


def _attend_all(q_ref, sink_ref, o_ref, kv_of, bias_of):
    def scores_of(kh):
        k_dup, _ = kv_of(kh)
        return bias_of(_head_scores(q_ref, kh, k_dup))

    nxt = scores_of(0)
    for kh in range(AKV):
        cur = nxt
        if kh + 1 < AKV:
            nxt = scores_of(kh + 1)
        _softmax_pv_store(cur, kv_of(kh)[1], sink_ref, kh, o_ref)


def _ctx_attn_kernel(sink_ref, q_ref, kd_ref, vd_ref, o_ref):
    def kv_of(kh):
        cols = slice(kh * 128, (kh + 1) * 128)
        return kd_ref[:, cols], vd_ref[:, cols]

    _attend_all(q_ref, sink_ref, o_ref, kv_of, lambda scores: scores)


def _ctx_attn(sink, q, kd, vd, o_prev):
    return pl.pallas_call(
        lambda sink_ref, q_ref, kd_ref, vd_ref, prev_ref, o_ref: _ctx_attn_kernel(
            sink_ref, q_ref, kd_ref, vd_ref, o_ref),
        grid=(CTX_B,),
        in_specs=[
            pl.BlockSpec(memory_space=pltpu.SMEM),
            pl.BlockSpec((CTX_L, ATT_NQ), lambda b: (b, 0)),
            pl.BlockSpec((CTX_L, ATT_DUP), lambda b: (b, 0)),
            pl.BlockSpec((CTX_L, ATT_DUP), lambda b: (b, 0)),
            pl.BlockSpec(memory_space=pl.ANY),
        ],
        out_specs=pl.BlockSpec((CTX_L, ATT_NQ), lambda b: (b, 0)),
        out_shape=jax.ShapeDtypeStruct((T_ALL, ATT_NQ), BF16),
        input_output_aliases={4: 0},
        compiler_params=_cparams(1),
        name="ctx_attn",
    )(sink, q, kd, vd, o_prev)


BAND = 3 * ABLK


def _lat_attn_kernel(sink_ref, q_ref, kd_ref, vd_ref, ck_ref, cv_ref, o_ref):
    n = pl.program_id(1)
    start = pl.multiple_of(jnp.clip((n - 1) * ABLK, 0, LAT_L - BAND), ABLK)
    qpos = n * ABLK + lax.broadcasted_iota(jnp.int32, (ABLK, BAND), 0)
    kpos = start + lax.broadcasted_iota(jnp.int32, (ABLK, BAND), 1)
    band_bias = jnp.where(jnp.abs(qpos - kpos) <= WINDOW, 0.0, NEG_INF)

    def kv_of(kh):
        cols = slice(kh * 128, (kh + 1) * 128)
        k = jnp.concatenate([kd_ref[pl.ds(start, BAND), cols], ck_ref[0, :, cols]], axis=0)
        v = jnp.concatenate([vd_ref[pl.ds(start, BAND), cols], cv_ref[0, :, cols]], axis=0)
        return k, v

    def bias_of(scores):
        return [jnp.concatenate([s[:, :BAND] + band_bias, s[:, BAND:]], axis=1) for s in scores]

    _attend_all(q_ref, sink_ref, o_ref, kv_of, bias_of)


def _lat_attn(sink, q, kd, vd, cache_kd, cache_vd, o_prev):
    n_q = LAT_L // ABLK
    rb0 = T_CTX // ABLK
    sb0 = T_CTX // LAT_L
    seq_spec = pl.BlockSpec((LAT_L, ATT_DUP), lambda b, n: (sb0 + b, 0))
    cache_spec = pl.BlockSpec((1, CTX_L, ATT_DUP), lambda b, n: (b, 0, 0))
    return pl.pallas_call(
        lambda sink_ref, q_ref, kd_ref, vd_ref, ck_ref, cv_ref, prev_ref, o_ref: _lat_attn_kernel(
            sink_ref, q_ref, kd_ref, vd_ref, ck_ref, cv_ref, o_ref),
        grid=(LAT_B, n_q),
        in_specs=[
            pl.BlockSpec(memory_space=pltpu.SMEM),
            pl.BlockSpec((ABLK, ATT_NQ), lambda b, n: (rb0 + b * n_q + n, 0)),
            seq_spec, seq_spec, cache_spec, cache_spec,
            pl.BlockSpec(memory_space=pl.ANY),
        ],
        out_specs=pl.BlockSpec((ABLK, ATT_NQ), lambda b, n: (rb0 + b * n_q + n, 0)),
        out_shape=jax.ShapeDtypeStruct((T_ALL, ATT_NQ), BF16),
        input_output_aliases={6: 0},
        compiler_params=_cparams(2),
        name="lat_attn",
    )(sink, q, kd, vd, cache_kd, cache_vd, o_prev)


def _log_sigmoid(x):
    return -(jnp.maximum(-x, 0.0) + jnp.log1p(jnp.exp(-jnp.abs(x))))


def kernel(x_prompt, x_sample, state_ret_fwd, state_ret_bwd, cache_attn_k, cache_attn_v, c, c_ctx, ada_w, ada_b, norm_mix, norm_ffn, ret_wq, ret_wk, ret_wv, ret_wg_fwd, ret_wg_bwd, ret_wo, ret_decay_fwd, ret_decay_bwd, attn_wq, attn_wk, attn_wv, attn_wo, attn_sink, ffn_w_in, ffn_conv_w, ffn_conv_b, ffn_w_out, norm_final):
    x_ctx = x_prompt.reshape(T_CTX, D)
    x_lat = x_sample.reshape(T_LAT, D)

    cv = jnp.concatenate([c, c_ctx[None, :], jnp.zeros((MOD_ROWS - LAT_B - 1, D), F32)], axis=0)
    mods = _adaln(cv, ada_w, ada_b).reshape(ada_w.shape[0], MOD_ROWS, N_MOD, D)

    h = _first_mod(x_ctx, x_lat, norm_mix[0:1], mods[0])
    w_ret = jnp.concatenate([ret_wq[0], ret_wk[0], ret_wv[0], ret_wg_fwd[0], ret_wg_bwd[0]], axis=1).astype(BF16)
    qkvg = _ret_proj(h, w_ret, _rope_tables(RDK))
    lg = jnp.stack([_log_sigmoid(ret_decay_fwd[0].astype(F32)), _log_sigmoid(ret_decay_bwd[0].astype(F32))])
    y0 = jnp.zeros((T_ALL, RH * RDV), BF16)
    y0, s_f, s_b = _ret_scan(qkvg, lg, y0, n_blocks=T_CTX // RET_BLOCK, n_seq=RET_BLOCK // CTX_L,
                             seq_len=CTX_L, row_block0=0, emit_state=True)
    (y0,) = _ret_scan(qkvg, lg, y0, n_blocks=LAT_B, n_seq=1, seq_len=LAT_L, row_block0=T_CTX // RET_BLOCK,
                      s0f=state_ret_fwd, s0b=state_ret_bwd)
    x, h = _res_matmul(y0, ret_wo[0].astype(BF16), (x_ctx, x_lat), mods[0], 2,
                       norm_ffn[0:1], mods[0], (3, 4))
    x, h = _ffn(h, ffn_w_in[0].astype(BF16), ffn_conv_w[0], ffn_conv_b[0:1], ffn_w_out[0].astype(BF16),
                x, mods[0], 5, norm_mix[1:2], mods[1], (0, 1))

    w_att = jnp.concatenate([attn_wq[0] * (AHD ** -0.5), _dup_heads(attn_wk[0]), _dup_heads(attn_wv[0])],
                            axis=1).astype(BF16)
    cos, slo, shi = _rope_tables(AHD)
    tabs = tuple(jnp.tile(t, (1, 128 // AHD)) for t in (cos, slo, shi))
    q, kd, vd, k32, v32 = _attn_proj(h, w_att, tabs)
    sink = attn_sink[0].astype(F32)
    cache_kd = _dup_heads(cache_attn_k[:, 0].reshape(LAT_B, CTX_L, ATT_NKV)).astype(BF16)
    cache_vd = _dup_heads(cache_attn_v[:, 0].reshape(LAT_B, CTX_L, ATT_NKV)).astype(BF16)
    a = _lat_attn(sink, q, kd, vd, cache_kd, cache_vd, jnp.zeros((T_ALL, ATT_NQ), BF16))
    a = _ctx_attn(sink, q, kd, vd, a)
    x, h = _res_matmul(a, attn_wo[0].astype(BF16), x, mods[1], 2, norm_ffn[1:2], mods[1], (3, 4))
    y_ctx, y_lat = _ffn(h, ffn_w_in[1].astype(BF16), ffn_conv_w[1], ffn_conv_b[1:2], ffn_w_out[1].astype(BF16),
                        x, mods[1], 5, norm_final[None, :])

    y_prompt = y_ctx.reshape(CTX_B, CTX_L, D)
    y_sample = y_lat.reshape(LAT_B, LAT_L, D)
    new_k = _undup_heads(k32).reshape(CTX_B, 1, CTX_L, AKV, AHD)
    new_v = _undup_heads(v32).reshape(CTX_B, 1, CTX_L, AKV, AHD)
    return y_prompt, y_sample, s_f, s_b, new_k, new_v
```

```python
import functools

import jax
import jax.numpy as jnp
import numpy as np
from jax import lax
from jax.experimental import pallas as pl
from jax.experimental.pallas import tpu as pltpu

D = 1024
CTX_B, CTX_L = 32, 256
LAT_B, LAT_L = 8, 2048
T_CTX = CTX_B * CTX_L
T_LAT = LAT_B * LAT_L
T_ALL = T_CTX + T_LAT
GRID_W = 64
N_MOD = 6
MOD_ROWS = 16
CTX_MOD_ROW = LAT_B
RH, RDK, RDV = 4, 256, 512
RC = 256
AH, AKV, AG, AHD = 16, 4, 4, 64
WINDOW = 128
ABLK = 128
D_FF = 2816
ROPE_BASE = 10000.0
EPS = 1e-6
NEG_INF = -1e30

BF16 = jnp.bfloat16
F32 = jnp.float32

VMEM_LIMIT_BYTES = 56 * 1024 * 1024
NORM_ROWS = 32


def _cparams(n_axes):
    return pltpu.CompilerParams(dimension_semantics=("arbitrary",) * n_axes,
                                vmem_limit_bytes=VMEM_LIMIT_BYTES)


def _silu(x):
    return x / (1.0 + jnp.exp(-x))


def _mod_row(i, tm):
    r0 = i * tm
    return jnp.where(r0 < T_CTX, CTX_MOD_ROW, (r0 - T_CTX) // LAT_L)


def _modulate(x, g_ref, mod_ref, shift_k, scale_k):
    ms = jnp.mean(x * x, axis=-1, keepdims=True)
    y = x * lax.rsqrt(ms + EPS) * g_ref[...]
    return y * (1.0 + mod_ref[0, scale_k:scale_k + 1, :]) + mod_ref[0, shift_k:shift_k + 1, :]


MOD_TM = 512
N_CTX_BLK = T_CTX // MOD_TM


def _ctx_blk(i):
    return jnp.minimum(i, N_CTX_BLK - 1)


def _lat_blk(i):
    return jnp.maximum(i - N_CTX_BLK, 0)


def _first_mod_kernel(xc_ref, xl_ref, g_ref, mod_ref, h_ref):
    i = pl.program_id(0)

    def run(x_ref):
        for r in range(0, MOD_TM, NORM_ROWS):
            rows = slice(r, r + NORM_ROWS)
            h_ref[rows, :] = _modulate(x_ref[rows, :], g_ref, mod_ref, 0, 1).astype(BF16)

    @pl.when(i < N_CTX_BLK)
    def _():
        run(xc_ref)

    @pl.when(i >= N_CTX_BLK)
    def _():
        run(xl_ref)


def _first_mod(x_ctx, x_lat, gain, mods_l):
    return pl.pallas_call(
        _first_mod_kernel,
        grid=(T_ALL // MOD_TM,),
        in_specs=[
            pl.BlockSpec((MOD_TM, D), lambda i: (_ctx_blk(i), 0)),
            pl.BlockSpec((MOD_TM, D), lambda i: (_lat_blk(i), 0)),
            pl.BlockSpec((1, D), lambda i: (0, 0)),
            pl.BlockSpec((1, N_MOD, D), lambda i: (_mod_row(i, MOD_TM), 0, 0)),
        ],
        out_specs=pl.BlockSpec((MOD_TM, D), lambda i: (i, 0)),
        out_shape=jax.ShapeDtypeStruct((T_ALL, D), BF16),
        compiler_params=_cparams(1),
        name="first_mod",
    )(x_ctx, x_lat, gain, mods_l)


ADA_TN = 512


def _adaln_kernel(cv_ref, w_ref, b_ref, o_ref):
    s = _silu(cv_ref[...]).astype(BF16)
    w = w_ref[0].astype(BF16)
    o_ref[0] = jnp.dot(s, w, preferred_element_type=F32) + b_ref[0]


def _adaln(cv, ada_w, ada_b):
    depth = ada_w.shape[0]
    n = N_MOD * D
    return pl.pallas_call(
        _adaln_kernel,
        grid=(depth, n // ADA_TN),
        in_specs=[
            pl.BlockSpec((MOD_ROWS, D), lambda l, j: (0, 0)),
            pl.BlockSpec((1, D, ADA_TN), lambda l, j: (l, 0, j)),
            pl.BlockSpec((1, 1, ADA_TN), lambda l, j: (l, 0, j)),
        ],
        out_specs=pl.BlockSpec((1, MOD_ROWS, ADA_TN), lambda l, j: (l, 0, j)),
        out_shape=jax.ShapeDtypeStruct((depth, MOD_ROWS, n), F32),
        compiler_params=_cparams(2),
        name="adaln",
    )(cv, ada_w, ada_b.reshape(depth, 1, n))


def _rope_tables(head_dim):
    half = head_dim // 2
    n_freq = half // 2
    inv = (np.float32(ROPE_BASE) ** (-np.arange(n_freq, dtype=np.float32) / np.float32(n_freq))).astype(np.float32)
    t = np.arange(LAT_L)
    rows = (t // GRID_W).astype(np.float32)[:, None] * inv[None, :]
    cols = (t % GRID_W).astype(np.float32)[:, None] * inv[None, :]
    zero = np.zeros_like(rows)
    cos = np.concatenate([np.cos(rows)] * 2 + [np.cos(cols)] * 2, axis=-1)
    sin_lo = np.concatenate([-np.sin(rows), zero, -np.sin(cols), zero], axis=-1)
    sin_hi = np.concatenate([zero, np.sin(rows), zero, np.sin(cols)], axis=-1)
    return tuple(np.tile(x.astype(np.float32), (1, max(1, 128 // head_dim))) for x in (cos, sin_lo, sin_hi))


def _rope_slab(x, cos, sin_lo, sin_hi, n_freq):
    up = pltpu.roll(x, 128 - n_freq, 1)
    dn = pltpu.roll(x, n_freq, 1)
    return x * cos + up * sin_lo + dn * sin_hi


PROJ_TM = 1024
PROJ_MC = 256
RET_TN = 2048
RET_N = 2 * RH * RDK + 3 * RH * RDV


def _ret_proj_kernel(h_ref, w_ref, cos_ref, slo_ref, shi_ref, o_ref):
    i = pl.program_id(0)
    j = pl.program_id(1)
    is_lat = i * PROJ_TM >= T_CTX

    def chunks(epilogue):
        for m in range(PROJ_TM // PROJ_MC):
            rows = slice(m * PROJ_MC, (m + 1) * PROJ_MC)
            acc = jnp.dot(h_ref[rows, :], w_ref[...], preferred_element_type=F32)
            epilogue(acc, rows)

    def plain(acc, rows):
        o_ref[rows, :] = acc.astype(BF16)

    def gate(acc, rows):
        o_ref[rows, :] = _silu(acc).astype(BF16)

    def rope(acc, rows):
        for s in range(RET_TN // 128):
            cols = slice(s * 128, (s + 1) * 128)
            tcols = slice((s % 2) * 128, (s % 2 + 1) * 128)
            y = _rope_slab(acc[:, cols], cos_ref[rows, tcols], slo_ref[rows, tcols], shi_ref[rows, tcols], RDK // 4)
            o_ref[rows, cols] = y.astype(BF16)

    @pl.when(jnp.logical_and(j == 0, is_lat))
    def _():
        chunks(rope)

    @pl.when(jnp.logical_or(jnp.logical_and(j == 0, jnp.logical_not(is_lat)), j == 1))
    def _():
        chunks(plain)

    @pl.when(j >= 2)
    def _():
        chunks(gate)


def _ret_proj(h, w_all, tabs):
    cos, slo, shi = tabs
    n_pos = LAT_L // PROJ_TM
    tab_spec = pl.BlockSpec((PROJ_TM, RDK), lambda i, j: (i % n_pos, 0))
    return pl.pallas_call(
        _ret_proj_kernel,
        grid=(T_ALL // PROJ_TM, RET_N // RET_TN),
        in_specs=[
            pl.BlockSpec((PROJ_TM, D), lambda i, j: (i, 0)),
            pl.BlockSpec((D, RET_TN), lambda i, j: (0, j)),
            tab_spec, tab_spec, tab_spec,
        ],
        out_specs=pl.BlockSpec((PROJ_TM, RET_TN), lambda i, j: (i, j)),
        out_shape=jax.ShapeDtypeStruct((T_ALL, RET_N), BF16),
        compiler_params=_cparams(2),
        name="ret_proj",
    )(h, w_all, cos, slo, shi)


RET_BLOCK = 2048


def _ret_scan_kernel(n_seq, seq_len, has_init, emit_state, lg_ref, q_ref, k_ref, v_ref, gf_ref, gb_ref, *rest):
    rest = list(rest)
    s0f_ref = rest.pop(0) if has_init else None
    s0b_ref = rest.pop(0) if has_init else None
    y_ref = rest.pop(0)
    sf_ref = rest.pop(0) if emit_state else None
    sb_ref = rest.pop(0) if emit_state else None
    ybuf_ref, stf_ref, stb_ref = rest

    hd = pl.program_id(1)
    n_chunks = seq_len // RC
    idx_col = lax.broadcasted_iota(jnp.int32, (RC, 1), 0).astype(F32)
    idx_row = lax.broadcasted_iota(jnp.int32, (1, RC), 1).astype(F32)
    diff = idx_col - idx_row

    lg_f = lg_ref[0, hd]
    lg_b = lg_ref[1, hd]
    dirs = {
        "f": dict(dmat=jnp.where(diff >= 0, jnp.exp(lg_f * jnp.maximum(diff, 0.0)), 0.0),
                  xi=jnp.exp(lg_f * (idx_col + 1.0)), zeta=jnp.exp(lg_f * (RC - 1.0 - idx_row)),
                  gch=jnp.exp(jnp.ones((1, RDV), F32) * (lg_f * RC)),
                  st_ref=stf_ref, s0_ref=s0f_ref, gate_ref=gf_ref, out_ref=sf_ref),
        "b": dict(dmat=jnp.where(diff <= 0, jnp.exp(lg_b * jnp.maximum(-diff, 0.0)), 0.0),
                  xi=jnp.exp(lg_b * (RC - idx_col)), zeta=jnp.exp(lg_b * idx_row),
                  gch=jnp.exp(jnp.ones((1, RDV), F32) * (lg_b * RC)),
                  st_ref=stb_ref, s0_ref=s0b_ref, gate_ref=gb_ref, out_ref=sb_ref),
    }

    def shared(rows):
        q = q_ref[rows, :]
        k = k_ref[rows, :]
        s = lax.dot_general(q, k, (((1,), (1,)), ((), ())), preferred_element_type=F32)
        return q, s, k.astype(F32).T, v_ref[rows, :]

    def chunk_dir(d, sh, rows, first, last, sq):
        q, s, kt, v = sh
        sm = (s * d["dmat"]).astype(BF16)
        kzt = (kt * d["zeta"]).astype(BF16)
        both = jnp.dot(jnp.concatenate([sm, kzt], axis=0), v, preferred_element_type=F32)
        o, upd = both[:RC], both[RC:]
        st = None
        if has_init and first:
            st = d["s0_ref"][0, 0, 0]
        elif not first:
            st = d["st_ref"][...]
        if st is not None:
            o = o + jnp.dot(q, st.astype(BF16), preferred_element_type=F32) * d["xi"]
            upd = st * d["gch"] + upd
        if last and emit_state:
            d["out_ref"][sq, 0, 0] = upd
        elif not last:
            d["st_ref"][...] = upd
        o = o * lax.rsqrt(jnp.mean(o * o, axis=-1, keepdims=True) + EPS)
        return o * d["gate_ref"][rows, :].astype(F32)

    for sq in range(n_seq):
        half_done = set()

        def emit(c, val):
            rows = slice(sq * seq_len + c * RC, sq * seq_len + (c + 1) * RC)
            if c in half_done:
                y_ref[rows, :] = (ybuf_ref[rows, :] + val).astype(BF16)
            else:
                ybuf_ref[rows, :] = val
                half_done.add(c)

        for t in range(n_chunks):
            cf, cb = t, n_chunks - 1 - t
            rows_f = slice(sq * seq_len + cf * RC, sq * seq_len + (cf + 1) * RC)
            rows_b = slice(sq * seq_len + cb * RC, sq * seq_len + (cb + 1) * RC)
            sh_f = shared(rows_f)
            sh_b = sh_f if cb == cf else shared(rows_b)
            first, last = t == 0, t == n_chunks - 1
            of = chunk_dir(dirs["f"], sh_f, rows_f, first, last, sq)
            ob = chunk_dir(dirs["b"], sh_b, rows_b, first, last, sq)
            if cf == cb:
                y_ref[rows_f, :] = (of + ob).astype(BF16)
            else:
                emit(cf, of)
                emit(cb, ob)


def _ret_scan(qkvg, lg, y_prev, *, n_blocks, n_seq, seq_len, row_block0, s0f=None, s0b=None, emit_state=False):
    has_init = s0f is not None
    kq = RDK
    n_qk = RH * RDK // kq
    in_specs = [
        pl.BlockSpec(memory_space=pltpu.SMEM),
        pl.BlockSpec((RET_BLOCK, RDK), lambda b, h: (row_block0 + b, h)),
        pl.BlockSpec((RET_BLOCK, RDK), lambda b, h: (row_block0 + b, n_qk + h)),
        pl.BlockSpec((RET_BLOCK, RDV), lambda b, h: (row_block0 + b, n_qk + h)),
        pl.BlockSpec((RET_BLOCK, RDV), lambda b, h: (row_block0 + b, n_qk + RH + h)),
        pl.BlockSpec((RET_BLOCK, RDV), lambda b, h: (row_block0 + b, n_qk + 2 * RH + h)),
    ]
    args = [lg, qkvg, qkvg, qkvg, qkvg, qkvg]
    if has_init:
        st_spec = pl.BlockSpec((1, 1, 1, RDK, RDV), lambda b, h: (b, 0, h, 0, 0))
        in_specs += [st_spec, st_spec]
        args += [s0f, s0b]
    in_specs.append(pl.BlockSpec(memory_space=pl.ANY))
    args.append(y_prev)
    y_shape = jax.ShapeDtypeStruct((T_ALL, RH * RDV), BF16)
    out_specs = [pl.BlockSpec((RET_BLOCK, RDV), lambda b, h: (row_block0 + b, h))]
    out_shape = [y_shape]
    if emit_state:
        so_spec = pl.BlockSpec((n_seq, 1, 1, RDK, RDV), lambda b, h: (b, 0, h, 0, 0))
        so_shape = jax.ShapeDtypeStruct((n_blocks * n_seq, 1, RH, RDK, RDV), F32)
        out_specs += [so_spec, so_spec]
        out_shape += [so_shape, so_shape]

    def body(*refs):
        refs = list(refs)
        n_in = len(in_specs)
        ins = refs[:n_in - 1]
        _ret_scan_kernel(n_seq, seq_len, has_init, emit_state, *ins, *refs[n_in:])

    return pl.pallas_call(
        body,
        grid=(n_blocks, RH),
        in_specs=in_specs,
        out_specs=out_specs,
        out_shape=out_shape,
        scratch_shapes=[pltpu.VMEM((RET_BLOCK, RDV), F32), pltpu.VMEM((RDK, RDV), F32),
                        pltpu.VMEM((RDK, RDV), F32)],
        input_output_aliases={len(args) - 1: 0},
        compiler_params=_cparams(2),
        name="ret_scan_ctx" if emit_state else "ret_scan_lat",
    )(*args)


RES_TM = MOD_TM
RES_MC = 256


def _res_kernel(gate_k, split_x, next_mod, a_ref, w_ref, *rest):
    rest = list(rest)
    x_refs = [rest.pop(0), rest.pop(0)] if split_x else [rest.pop(0)]
    mod_ref = rest.pop(0)
    gain_ref = rest.pop(0)
    if next_mod is not None:
        nmod_ref = rest.pop(0)
        xo_ref, h_ref = rest
    else:
        yc_ref, yl_ref = rest
    i = pl.program_id(0)
    is_ctx = i < N_CTX_BLK
    gate = mod_ref[0, gate_k:gate_k + 1, :]

    def run(x_ref, y_ref):
        for m in range(RES_TM // RES_MC):
            rows = slice(m * RES_MC, (m + 1) * RES_MC)
            acc = jnp.dot(a_ref[rows, :], w_ref[...], preferred_element_type=F32)
            xn = x_ref[rows, :] + gate * acc
            if next_mod is not None:
                xo_ref[rows, :] = xn
                h_ref[rows, :] = _modulate(xn, gain_ref, nmod_ref, *next_mod).astype(BF16)
            else:
                ms = jnp.mean(xn * xn, axis=-1, keepdims=True)
                y_ref[rows, :] = xn * lax.rsqrt(ms + EPS) * gain_ref[...]

    if not split_x and next_mod is not None:
        run(x_refs[0], None)
        return

    @pl.when(is_ctx)
    def _():
        run(x_refs[0], None if next_mod is not None else yc_ref)

    @pl.when(jnp.logical_not(is_ctx))
    def _():
        run(x_refs[-1], None if next_mod is not None else yl_ref)


def _res_matmul(a, w, x, mods_l, gate_k, gain, next_mods=None, next_mod=None):
    kdim = a.shape[1]
    split_x = isinstance(x, tuple)
    row_spec = pl.BlockSpec((RES_TM, D), lambda i: (i, 0))
    ctx_spec = pl.BlockSpec((RES_TM, D), lambda i: (_ctx_blk(i), 0))
    lat_spec = pl.BlockSpec((RES_TM, D), lambda i: (_lat_blk(i), 0))
    mod_spec = pl.BlockSpec((1, N_MOD, D), lambda i: (_mod_row(i, RES_TM), 0, 0))
    in_specs = [pl.BlockSpec((RES_TM, kdim), lambda i: (i, 0)), pl.BlockSpec((kdim, D), lambda i: (0, 0))]
    args = [a, w]
    if split_x:
        in_specs += [ctx_spec, lat_spec]
        args += list(x)
    else:
        in_specs.append(row_spec)
        args.append(x)
    in_specs += [mod_spec, pl.BlockSpec((1, D), lambda i: (0, 0))]
    args += [mods_l, gain]
    if next_mod is not None:
        in_specs.append(mod_spec)
        args.append(next_mods)
        out_specs = [row_spec, row_spec]
        out_shape = [jax.ShapeDtypeStruct((T_ALL, D), F32), jax.ShapeDtypeStruct((T_ALL, D), BF16)]
    else:
        out_specs = [ctx_spec, lat_spec]
        out_shape = [jax.ShapeDtypeStruct((T_CTX, D), F32), jax.ShapeDtypeStruct((T_LAT, D), F32)]
    return pl.pallas_call(
        functools.partial(_res_kernel, gate_k, split_x, next_mod),
        grid=(T_ALL // RES_TM,),
        in_specs=in_specs,
        out_specs=out_specs,
        out_shape=out_shape,
        compiler_params=_cparams(1),
        name="res_matmul_k%d%s" % (kdim, "_final" if next_mod is None else ""),
    )(*args)


FFN_TM = 2048
FFN_MC = 512
FFN_TN = 256
FFN_HALO = 16
FFN_TILES = D_FF // FFN_TN
FFN_EXT = FFN_MC + (FFN_MC // CTX_L + 1) * FFN_HALO


def _ffn_up_kernel(h_ref, wu_ref, wg_ref, cwu_ref, cwg_ref, cbu_ref, cbg_ref, o_ref, a_scr, t_scr):
    is_ctx = pl.program_id(0) * FFN_TM < T_CTX
    zeros = jnp.zeros((FFN_HALO, D), BF16)
    n_slab = FFN_TN // 128

    def chunk(c, pieces, segments):
        slot = c % 2
        hm = jnp.concatenate(pieces, axis=0)
        n_ext = hm.shape[0]
        for k, w_ref in enumerate((wu_ref, wg_ref)):
            a = jnp.dot(hm, w_ref[...], preferred_element_type=F32)
            for s in range(n_slab):
                a_scr[slot, k * n_slab + s, 0:n_ext, :] = a[:, s * 128:(s + 1) * 128]

        def conv(slab, base, half, cw_ref, cb_ref, lanes):
            def rows(off):
                return a_scr[slot, slab, pl.ds(base + off, half, stride=2), :]
            w0, w1, w2 = cw_ref[0:1, lanes], cw_ref[1:2, lanes], cw_ref[2:3, lanes]
            b = cb_ref[:, lanes]
            om1, ev, od, ep1 = rows(-1), rows(0), rows(1), rows(2)
            return om1 * w0 + ev * w1 + od * w2 + b, ev * w0 + od * w1 + ep1 * w2 + b

        for s in range(n_slab):
            lanes = slice(s * 128, (s + 1) * 128)
            for base, r0, n_tok in segments:
                half = n_tok // 2
                t0 = r0 % FFN_MC
                u_ev, u_od = conv(s, base, half, cwu_ref, cbu_ref, lanes)
                g_ev, g_od = conv(n_slab + s, base, half, cwg_ref, cbg_ref, lanes)
                t_scr[slot, s, pl.ds(t0, half, stride=2), :] = _silu(g_ev) * u_ev
                t_scr[slot, s, pl.ds(t0 + 1, half, stride=2), :] = _silu(g_od) * u_od
            c0 = c * FFN_MC
            o_ref[c0:c0 + FFN_MC, lanes] = t_scr[slot, s, :, :].astype(BF16)

    @pl.when(is_ctx)
    def _():
        per_chunk = FFN_MC // CTX_L
        for c in range(FFN_TM // FFN_MC):
            pieces, segments = [zeros], []
            for q in range(per_chunk):
                r0 = (c * per_chunk + q) * CTX_L
                segments.append((FFN_HALO + q * (CTX_L + FFN_HALO), r0, CTX_L))
                pieces += [h_ref[r0:r0 + CTX_L, :], zeros]
            chunk(c, pieces, segments)

    @pl.when(jnp.logical_not(is_ctx))
    def _():
        n_chunks = FFN_TM // FFN_MC
        for c in range(n_chunks):
            r0 = c * FFN_MC
            lo = zeros if c == 0 else h_ref[r0 - FFN_HALO:r0, :]
            hi = zeros if c == n_chunks - 1 else h_ref[r0 + FFN_MC:r0 + FFN_MC + FFN_HALO, :]
            chunk(c, [lo, h_ref[r0:r0 + FFN_MC, :], hi], [(FFN_HALO, r0, FFN_MC)])


def _ffn_up(h, w_in, conv_w, conv_b):
    n_slab = FFN_TN // 128
    return pl.pallas_call(
        _ffn_up_kernel,
        grid=(T_ALL // FFN_TM, FFN_TILES),
        in_specs=[
            pl.BlockSpec((FFN_TM, D), lambda i, j: (i, 0)),
            pl.BlockSpec((D, FFN_TN), lambda i, j: (0, j)),
            pl.BlockSpec((D, FFN_TN), lambda i, j: (0, FFN_TILES + j)),
            pl.BlockSpec((3, FFN_TN), lambda i, j: (0, j)),
            pl.BlockSpec((3, FFN_TN), lambda i, j: (0, FFN_TILES + j)),
            pl.BlockSpec((1, FFN_TN), lambda i, j: (0, j)),
            pl.BlockSpec((1, FFN_TN), lambda i, j: (0, FFN_TILES + j)),
        ],
        out_specs=pl.BlockSpec((FFN_TM, FFN_TN), lambda i, j: (i, j)),
        out_shape=jax.ShapeDtypeStruct((T_ALL, D_FF), BF16),
        scratch_shapes=[
            pltpu.VMEM((2, 2 * n_slab, FFN_EXT, 128), F32),
            pltpu.VMEM((2, n_slab, FFN_MC, 128), F32),
        ],
        compiler_params=_cparams(2),
        name="ffn_up",
    )(h, w_in, w_in, conv_w, conv_w, conv_b, conv_b)


ATT_NQ = AH * AHD
ATT_NKV = AKV * AHD
ATT_DUP = 2 * ATT_NKV
ATT_N = ATT_NQ + 2 * ATT_DUP
ATT_TN = ATT_NQ
assert 2 * ATT_DUP == ATT_TN


def _dup_heads(w):
    lead = w.shape[:-1]
    w = w.reshape(lead + (AKV, 1, AHD))
    return jnp.broadcast_to(w, lead + (AKV, 2, AHD)).reshape(lead + (ATT_DUP,))


def _undup_heads(x):
    return x.reshape(x.shape[:-1] + (AKV, 2, AHD))[..., 0, :].reshape(x.shape[:-1] + (ATT_NKV,))


def _attn_proj_kernel(h_ref, w_ref, cos_ref, slo_ref, shi_ref, q_ref, kd_ref, vd_ref, k32_ref, v32_ref):
    i = pl.program_id(0)
    j = pl.program_id(1)
    is_lat = i * PROJ_TM >= T_CTX

    def chunks(epilogue):
        for m in range(PROJ_TM // PROJ_MC):
            rows = slice(m * PROJ_MC, (m + 1) * PROJ_MC)
            epilogue(jnp.dot(h_ref[rows, :], w_ref[...], preferred_element_type=F32), rows)

    def rope_into(o_ref, acc, rows, c0, n_cols):
        for s in range(n_cols // 128):
            y = _rope_slab(acc[:, c0 + s * 128:c0 + (s + 1) * 128],
                           cos_ref[rows, :], slo_ref[rows, :], shi_ref[rows, :], AHD // 4)
            o_ref[rows, s * 128:(s + 1) * 128] = y.astype(BF16)

    def q_lat(acc, rows):
        rope_into(q_ref, acc, rows, 0, ATT_NQ)

    def q_ctx(acc, rows):
        q_ref[rows, :] = acc.astype(BF16)

    def kv_lat(acc, rows):
        rope_into(kd_ref, acc, rows, 0, ATT_DUP)
        vd_ref[rows, :] = acc[:, ATT_DUP:].astype(BF16)

    def kv_ctx(acc, rows):
        kd_ref[rows, :] = acc[:, :ATT_DUP].astype(BF16)
        vd_ref[rows, :] = acc[:, ATT_DUP:].astype(BF16)
        k32_ref[rows, :] = acc[:, :ATT_DUP]
        v32_ref[rows, :] = acc[:, ATT_DUP:]

    for on_q, on_lat, epilogue in ((True, True, q_lat), (True, False, q_ctx),
                                   (False, True, kv_lat), (False, False, kv_ctx)):
        @pl.when(jnp.logical_and((j == 0) == on_q, is_lat == on_lat))
        def _():
            chunks(epilogue)


def _attn_proj(h, w_all, tabs):
    cos, slo, shi = tabs
    n_pos = LAT_L // PROJ_TM
    n_ctx = T_CTX // PROJ_TM
    tab_spec = pl.BlockSpec((PROJ_TM, 128), lambda i, j: (i % n_pos, 0))
    row_spec = pl.BlockSpec((PROJ_TM, ATT_DUP), lambda i, j: (i, 0))
    ctx_spec = pl.BlockSpec((PROJ_TM, ATT_DUP), lambda i, j: (jnp.minimum(i, n_ctx - 1), 0))
    return pl.pallas_call(
        _attn_proj_kernel,
        grid=(T_ALL // PROJ_TM, ATT_N // ATT_TN),
        in_specs=[
            pl.BlockSpec((PROJ_TM, D), lambda i, j: (i, 0)),
            pl.BlockSpec((D, ATT_TN), lambda i, j: (0, j)),
            tab_spec, tab_spec, tab_spec,
        ],
        out_specs=[
            pl.BlockSpec((PROJ_TM, ATT_NQ), lambda i, j: (i, 0)),
            row_spec, row_spec, ctx_spec, ctx_spec,
        ],
        out_shape=[
            jax.ShapeDtypeStruct((T_ALL, ATT_NQ), BF16),
            jax.ShapeDtypeStruct((T_ALL, ATT_DUP), BF16),
            jax.ShapeDtypeStruct((T_ALL, ATT_DUP), BF16),
            jax.ShapeDtypeStruct((T_CTX, ATT_DUP), F32),
            jax.ShapeDtypeStruct((T_CTX, ATT_DUP), F32),
        ],
        compiler_params=_cparams(2),
        name="attn_proj",
    )(h, w_all, cos, slo, shi)


def _head_scores(q_ref, kh, k_dup):
    nq = q_ref.shape[0]
    nk = k_dup.shape[0]
    low = lax.broadcasted_iota(jnp.int32, (nk, 128), 1) < AHD
    zero = jnp.zeros_like(k_dup)
    k2 = jnp.concatenate([jnp.where(low, k_dup, zero), jnp.where(low, zero, k_dup)], axis=0)
    base = kh * AG * AHD
    lhs = jnp.concatenate([q_ref[:, base:base + 128], q_ref[:, base + 128:base + 256]], axis=0)
    s = lax.dot_general(lhs, k2, (((1,), (1,)), ((), ())), preferred_element_type=F32)
    return [s[0:nq, 0:nk], s[0:nq, nk:], s[nq:, 0:nk], s[nq:, nk:]]


def _row_max(s):
    m = s[:, 0:128]
    for c in range(128, s.shape[1], 128):
        m = jnp.maximum(m, s[:, c:c + 128])
    return jnp.max(m, axis=-1, keepdims=True)


def _softmax_pv_store(scores, v_dup, sink_ref, kh, o_ref):
    nq = scores[0].shape[0]
    nk = v_dup.shape[0]
    v_aug = jnp.concatenate([v_dup, jnp.ones((nk, 128), BF16)], axis=1)
    ps, tails = [], []
    for g, s in enumerate(scores):
        sink = sink_ref[kh * AG + g]
        m = jnp.maximum(_row_max(s), sink)
        ps.append(jnp.exp(s - m).astype(BF16))
        tails.append(jnp.exp(sink - m))
    pv = jnp.dot(jnp.concatenate(ps, axis=0), v_aug, preferred_element_type=F32)
    outs = []
    for g in range(AG):
        rows = slice(g * nq, (g + 1) * nq)
        outs.append(pv[rows, 0:128] / (pv[rows, 128:256] + tails[g]))
    low = lax.broadcasted_iota(jnp.int32, (nq, 128), 1) < AHD
    base = kh * AG * AHD
    o_ref[:, base:base + 128] = jnp.where(low, outs[0], outs[1]).astype(BF16)
    o_ref[:, base + 128:base + 256] = jnp.where(low, outs[2], outs[3]).astype(BF16)


def _attend_all(q_ref, sink_ref, o_ref, kv_of, bias_of):
    def scores_of(kh):
        k_dup, _ = kv_of(kh)
        return bias_of(_head_scores(q_ref, kh, k_dup))

    nxt = scores_of(0)
    for kh in range(AKV):
        cur = nxt
        if kh + 1 < AKV:
            nxt = scores_of(kh + 1)
        _softmax_pv_store(cur, kv_of(kh)[1], sink_ref, kh, o_ref)


def _ctx_attn_kernel(sink_ref, q_ref, kd_ref, vd_ref, o_ref):
    def kv_of(kh):
        cols = slice(kh * 128, (kh + 1) * 128)
        return kd_ref[:, cols], vd_ref[:, cols]

    _attend_all(q_ref, sink_ref, o_ref, kv_of, lambda scores: scores)


def _ctx_attn(sink, q, kd, vd, o_prev):
    return pl.pallas_call(
        lambda sink_ref, q_ref, kd_ref, vd_ref, prev_ref, o_ref: _ctx_attn_kernel(
            sink_ref, q_ref, kd_ref, vd_ref, o_ref),
        grid=(CTX_B,),
        in_specs=[
            pl.BlockSpec(memory_space=pltpu.SMEM),
            pl.BlockSpec((CTX_L, ATT_NQ), lambda b: (b, 0)),
            pl.BlockSpec((CTX_L, ATT_DUP), lambda b: (b, 0)),
            pl.BlockSpec((CTX_L, ATT_DUP), lambda b: (b, 0)),
            pl.BlockSpec(memory_space=pl.ANY),
        ],
        out_specs=pl.BlockSpec((CTX_L, ATT_NQ), lambda b: (b, 0)),
        out_shape=jax.ShapeDtypeStruct((T_ALL, ATT_NQ), BF16),
        input_output_aliases={4: 0},
        compiler_params=_cparams(1),
        name="ctx_attn",
    )(sink, q, kd, vd, o_prev)


BAND = 3 * ABLK


def _lat_attn_kernel(sink_ref, q_ref, kd_ref, vd_ref, ck_ref, cv_ref, o_ref):
    n = pl.program_id(1)
    start = pl.multiple_of(jnp.clip((n - 1) * ABLK, 0, LAT_L - BAND), ABLK)
    qpos = n * ABLK + lax.broadcasted_iota(jnp.int32, (ABLK, BAND), 0)
    kpos = start + lax.broadcasted_iota(jnp.int32, (ABLK, BAND), 1)
    band_bias = jnp.where(jnp.abs(qpos - kpos) <= WINDOW, 0.0, NEG_INF)

    def kv_of(kh):
        cols = slice(kh * 128, (kh + 1) * 128)
        k = jnp.concatenate([kd_ref[pl.ds(start, BAND), cols], ck_ref[0, :, cols]], axis=0)
        v = jnp.concatenate([vd_ref[pl.ds(start, BAND), cols], cv_ref[0, :, cols]], axis=0)
        return k, v

    def bias_of(scores):
        return [jnp.concatenate([s[:, :BAND] + band_bias, s[:, BAND:]], axis=1) for s in scores]

    _attend_all(q_ref, sink_ref, o_ref, kv_of, bias_of)


def _lat_attn(sink, q, kd, vd, cache_kd, cache_vd, o_prev):
    n_q = LAT_L // ABLK
    rb0 = T_CTX // ABLK
    sb0 = T_CTX // LAT_L
    seq_spec = pl.BlockSpec((LAT_L, ATT_DUP), lambda b, n: (sb0 + b, 0))
    cache_spec = pl.BlockSpec((1, CTX_L, ATT_DUP), lambda b, n: (b, 0, 0))
    return pl.pallas_call(
        lambda sink_ref, q_ref, kd_ref, vd_ref, ck_ref, cv_ref, prev_ref, o_ref: _lat_attn_kernel(
            sink_ref, q_ref, kd_ref, vd_ref, ck_ref, cv_ref, o_ref),
        grid=(LAT_B, n_q),
        in_specs=[
            pl.BlockSpec(memory_space=pltpu.SMEM),
            pl.BlockSpec((ABLK, ATT_NQ), lambda b, n: (rb0 + b * n_q + n, 0)),
            seq_spec, seq_spec, cache_spec, cache_spec,
            pl.BlockSpec(memory_space=pl.ANY),
        ],
        out_specs=pl.BlockSpec((ABLK, ATT_NQ), lambda b, n: (rb0 + b * n_q + n, 0)),
        out_shape=jax.ShapeDtypeStruct((T_ALL, ATT_NQ), BF16),
        input_output_aliases={6: 0},
        compiler_params=_cparams(2),
        name="lat_attn",
    )(sink, q, kd, vd, cache_kd, cache_vd, o_prev)


def _log_sigmoid(x):
    return -(jnp.maximum(-x, 0.0) + jnp.log1p(jnp.exp(-jnp.abs(x))))


def kernel(x_prompt, x_sample, state_ret_fwd, state_ret_bwd, cache_attn_k, cache_attn_v, c, c_ctx, ada_w, ada_b, norm_mix, norm_ffn, ret_wq, ret_wk, ret_wv, ret_wg_fwd, ret_wg_bwd, ret_wo, ret_decay_fwd, ret_decay_bwd, attn_wq, attn_wk, attn_wv, attn_wo, attn_sink, ffn_w_in, ffn_conv_w, ffn_conv_b, ffn_w_out, norm_final):
    x_ctx = x_prompt.reshape(T_CTX, D)
    x_lat = x_sample.reshape(T_LAT, D)

    cv = jnp.concatenate([c, c_ctx[None, :], jnp.zeros((MOD_ROWS - LAT_B - 1, D), F32)], axis=0)
    mods = _adaln(cv, ada_w, ada_b).reshape(ada_w.shape[0], MOD_ROWS, N_MOD, D)

    h = _first_mod(x_ctx, x_lat, norm_mix[0:1], mods[0])
    w_ret = jnp.concatenate([ret_wq[0], ret_wk[0] * (RDK ** -0.5), ret_wv[0], ret_wg_fwd[0], ret_wg_bwd[0]],
                            axis=1).astype(BF16)
    qkvg = _ret_proj(h, w_ret, _rope_tables(RDK))
    lg = jnp.stack([_log_sigmoid(ret_decay_fwd[0].astype(F32)), _log_sigmoid(ret_decay_bwd[0].astype(F32))])
    y0 = jnp.zeros((T_ALL, RH * RDV), BF16)
    y0, s_f, s_b = _ret_scan(qkvg, lg, y0, n_blocks=T_CTX // RET_BLOCK, n_seq=RET_BLOCK // CTX_L,
                             seq_len=CTX_L, row_block0=0, emit_state=True)
    (y0,) = _ret_scan(qkvg, lg, y0, n_blocks=LAT_B, n_seq=1, seq_len=LAT_L, row_block0=T_CTX // RET_BLOCK,
                      s0f=state_ret_fwd, s0b=state_ret_bwd)
    x, h = _res_matmul(y0, ret_wo[0].astype(BF16), (x_ctx, x_lat), mods[0], 2,
                       norm_ffn[0:1], mods[0], (3, 4))
    act = _ffn_up(h, ffn_w_in[0].astype(BF16), ffn_conv_w[0], ffn_conv_b[0:1])
    x, h = _res_matmul(act, ffn_w_out[0].astype(BF16), x, mods[0], 5, norm_mix[1:2], mods[1], (0, 1))

    w_att = jnp.concatenate([attn_wq[0] * (AHD ** -0.5), _dup_heads(attn_wk[0]), _dup_heads(attn_wv[0])],
                            axis=1).astype(BF16)
    q, kd, vd, k32, v32 = _attn_proj(h, w_att, _rope_tables(AHD))
    sink = attn_sink[0].astype(F32)
    cache_kd = _dup_heads(cache_attn_k[:, 0].reshape(LAT_B, CTX_L, ATT_NKV)).astype(BF16)
    cache_vd = _dup_heads(cache_attn_v[:, 0].reshape(LAT_B, CTX_L, ATT_NKV)).astype(BF16)
    a = _lat_attn(sink, q, kd, vd, cache_kd, cache_vd, jnp.zeros((T_ALL, ATT_NQ), BF16))
    a = _ctx_attn(sink, q, kd, vd, a)
    x, h = _res_matmul(a, attn_wo[0].astype(BF16), x, mods[1], 2, norm_ffn[1:2], mods[1], (3, 4))
    act = _ffn_up(h, ffn_w_in[1].astype(BF16), ffn_conv_w[1], ffn_conv_b[1:2])
    y_ctx, y_lat = _res_matmul(act, ffn_w_out[1].astype(BF16), x, mods[1], 5, norm_final[None, :])

    y_prompt = y_ctx.reshape(CTX_B, CTX_L, D)
    y_sample = y_lat.reshape(LAT_B, LAT_L, D)
    new_k = _undup_heads(k32).reshape(CTX_B, 1, CTX_L, AKV, AHD)
    new_v = _undup_heads(v32).reshape(CTX_B, 1, CTX_L, AKV, AHD)
    return y_prompt, y_sample, s_f, s_b, new_k, new_v
```

```python
import functools

import jax
import jax.numpy as jnp
import numpy as np
from jax import lax
from jax.experimental import pallas as pl
from jax.experimental.pallas import tpu as pltpu

D = 1024
CTX_B, CTX_L = 32, 256
LAT_B, LAT_L = 8, 2048
T_CTX = CTX_B * CTX_L
T_LAT = LAT_B * LAT_L
T_ALL = T_CTX + T_LAT
GRID_W = 64
N_MOD = 6
MOD_ROWS = 16
CTX_MOD_ROW = LAT_B
RH, RDK, RDV = 4, 256, 512
RC = 256
AH, AKV, AG, AHD = 16, 4, 4, 64
WINDOW = 128
ABLK = 128
D_FF = 2816
ROPE_BASE = 10000.0
EPS = 1e-6
NEG_INF = -1e30

BF16 = jnp.bfloat16
F32 = jnp.float32

VMEM_LIMIT_BYTES = 56 * 1024 * 1024
NORM_ROWS = 32


def _cparams(n_axes):
    return pltpu.CompilerParams(dimension_semantics=("arbitrary",) * n_axes,
                                vmem_limit_bytes=VMEM_LIMIT_BYTES)


def _silu(x):
    return x / (1.0 + jnp.exp(-x))


def _mod_row(i, tm):
    r0 = i * tm
    return jnp.where(r0 < T_CTX, CTX_MOD_ROW, (r0 - T_CTX) // LAT_L)


def _modulate(x, g_ref, mod_ref, shift_k, scale_k):
    ms = jnp.mean(x * x, axis=-1, keepdims=True)
    y = x * lax.rsqrt(ms + EPS) * g_ref[...]
    return y * (1.0 + mod_ref[0, scale_k:scale_k + 1, :]) + mod_ref[0, shift_k:shift_k + 1, :]


MOD_TM = 1024
N_CTX_BLK = T_CTX // MOD_TM


def _ctx_blk(i):
    return jnp.minimum(i, N_CTX_BLK - 1)


def _lat_blk(i):
    return jnp.maximum(i - N_CTX_BLK, 0)


def _first_mod_kernel(xc_ref, xl_ref, g_ref, mod_ref, h_ref):
    i = pl.program_id(0)

    def run(x_ref):
        for r in range(0, MOD_TM, NORM_ROWS):
            rows = slice(r, r + NORM_ROWS)
            h_ref[rows, :] = _modulate(x_ref[rows, :], g_ref, mod_ref, 0, 1).astype(BF16)

    @pl.when(i < N_CTX_BLK)
    def _():
        run(xc_ref)

    @pl.when(i >= N_CTX_BLK)
    def _():
        run(xl_ref)


def _first_mod(x_ctx, x_lat, gain, mods_l):
    return pl.pallas_call(
        _first_mod_kernel,
        grid=(T_ALL // MOD_TM,),
        in_specs=[
            pl.BlockSpec((MOD_TM, D), lambda i: (_ctx_blk(i), 0)),
            pl.BlockSpec((MOD_TM, D), lambda i: (_lat_blk(i), 0)),
            pl.BlockSpec((1, D), lambda i: (0, 0)),
            pl.BlockSpec((1, N_MOD, D), lambda i: (_mod_row(i, MOD_TM), 0, 0)),
        ],
        out_specs=pl.BlockSpec((MOD_TM, D), lambda i: (i, 0)),
        out_shape=jax.ShapeDtypeStruct((T_ALL, D), BF16),
        compiler_params=_cparams(1),
        name="first_mod",
    )(x_ctx, x_lat, gain, mods_l)


ADA_TN = 512


def _adaln_kernel(cv_ref, w_ref, b_ref, o_ref):
    s = _silu(cv_ref[...]).astype(BF16)
    w = w_ref[0].astype(BF16)
    o_ref[0] = jnp.dot(s, w, preferred_element_type=F32) + b_ref[0]


def _adaln(cv, ada_w, ada_b):
    depth = ada_w.shape[0]
    n = N_MOD * D
    return pl.pallas_call(
        _adaln_kernel,
        grid=(depth, n // ADA_TN),
        in_specs=[
            pl.BlockSpec((MOD_ROWS, D), lambda l, j: (0, 0)),
            pl.BlockSpec((1, D, ADA_TN), lambda l, j: (l, 0, j)),
            pl.BlockSpec((1, 1, ADA_TN), lambda l, j: (l, 0, j)),
        ],
        out_specs=pl.BlockSpec((1, MOD_ROWS, ADA_TN), lambda l, j: (l, 0, j)),
        out_shape=jax.ShapeDtypeStruct((depth, MOD_ROWS, n), F32),
        compiler_params=_cparams(2),
        name="adaln",
    )(cv, ada_w, ada_b.reshape(depth, 1, n))


def _rope_tables(head_dim):
    half = head_dim // 2
    n_freq = half // 2
    inv = (np.float32(ROPE_BASE) ** (-np.arange(n_freq, dtype=np.float32) / np.float32(n_freq))).astype(np.float32)
    t = np.arange(LAT_L)
    rows = (t // GRID_W).astype(np.float32)[:, None] * inv[None, :]
    cols = (t % GRID_W).astype(np.float32)[:, None] * inv[None, :]
    zero = np.zeros_like(rows)
    cos = np.concatenate([np.cos(rows)] * 2 + [np.cos(cols)] * 2, axis=-1)
    sin_lo = np.concatenate([-np.sin(rows), zero, -np.sin(cols), zero], axis=-1)
    sin_hi = np.concatenate([zero, np.sin(rows), zero, np.sin(cols)], axis=-1)
    return tuple(np.tile(x.astype(np.float32), (1, max(1, 128 // head_dim))) for x in (cos, sin_lo, sin_hi))


def _rope_slab(x, cos, sin_lo, sin_hi, n_freq):
    up = pltpu.roll(x, 128 - n_freq, 1)
    dn = pltpu.roll(x, n_freq, 1)
    return x * cos + up * sin_lo + dn * sin_hi


PROJ_TM = 1024
PROJ_MC = 256
RET_NQK = 2 * RH * RDK
RET_N = RET_NQK + 3 * RH * RDV
RET_TN = RET_N // 2


def _ret_proj_kernel(h_ref, w_ref, cos_ref, slo_ref, shi_ref, o_ref):
    i = pl.program_id(0)
    j = pl.program_id(1)
    is_lat = i * PROJ_TM >= T_CTX

    def chunks(epilogue):
        for m in range(PROJ_TM // PROJ_MC):
            rows = slice(m * PROJ_MC, (m + 1) * PROJ_MC)
            acc = jnp.dot(h_ref[rows, :], w_ref[...], preferred_element_type=F32)
            epilogue(acc, rows)

    def plain(acc, rows):
        o_ref[rows, :] = acc.astype(BF16)

    def gate(acc, rows):
        o_ref[rows, :] = _silu(acc).astype(BF16)

    def rope_qk(acc, rows):
        for s in range(RET_NQK // 128):
            cols = slice(s * 128, (s + 1) * 128)
            tcols = slice((s % 2) * 128, (s % 2 + 1) * 128)
            y = _rope_slab(acc[:, cols], cos_ref[rows, tcols], slo_ref[rows, tcols], shi_ref[rows, tcols], RDK // 4)
            o_ref[rows, cols] = y.astype(BF16)
        o_ref[rows, RET_NQK:] = acc[:, RET_NQK:].astype(BF16)

    @pl.when(jnp.logical_and(j == 0, is_lat))
    def _():
        chunks(rope_qk)

    @pl.when(jnp.logical_and(j == 0, jnp.logical_not(is_lat)))
    def _():
        chunks(plain)

    @pl.when(j == 1)
    def _():
        chunks(gate)


def _ret_proj(h, w_all, tabs):
    cos, slo, shi = tabs
    n_pos = LAT_L // PROJ_TM
    tab_spec = pl.BlockSpec((PROJ_TM, RDK), lambda i, j: (i % n_pos, 0))
    return pl.pallas_call(
        _ret_proj_kernel,
        grid=(T_ALL // PROJ_TM, RET_N // RET_TN),
        in_specs=[
            pl.BlockSpec((PROJ_TM, D), lambda i, j: (i, 0)),
            pl.BlockSpec((D, RET_TN), lambda i, j: (0, j)),
            tab_spec, tab_spec, tab_spec,
        ],
        out_specs=pl.BlockSpec((PROJ_TM, RET_TN), lambda i, j: (i, j)),
        out_shape=jax.ShapeDtypeStruct((T_ALL, RET_N), BF16),
        compiler_params=_cparams(2),
        name="ret_proj",
    )(h, w_all, cos, slo, shi)


RET_BLOCK = 2048


def _ret_scan_kernel(n_seq, seq_len, has_init, emit_state, lg_ref, q_ref, k_ref, v_ref, gf_ref, gb_ref, *rest):
    rest = list(rest)
    s0f_ref = rest.pop(0) if has_init else None
    s0b_ref = rest.pop(0) if has_init else None
    y_ref = rest.pop(0)
    sf_ref = rest.pop(0) if emit_state else None
    sb_ref = rest.pop(0) if emit_state else None
    ybuf_ref, stf_ref, stb_ref = rest

    hd = pl.program_id(1)
    n_chunks = seq_len // RC
    idx_col = lax.broadcasted_iota(jnp.int32, (RC, 1), 0).astype(F32)
    idx_row = lax.broadcasted_iota(jnp.int32, (1, RC), 1).astype(F32)
    diff = idx_col - idx_row

    lg_f = lg_ref[0, hd]
    lg_b = lg_ref[1, hd]
    dirs = {
        "f": dict(dmat=jnp.where(diff >= 0, jnp.exp(lg_f * jnp.maximum(diff, 0.0)), 0.0),
                  xi=jnp.exp(lg_f * (idx_col + 1.0)), zeta=jnp.exp(lg_f * (RC - 1.0 - idx_row)),
                  gch=jnp.exp(jnp.ones((1, RDV), F32) * (lg_f * RC)),
                  st_ref=stf_ref, s0_ref=s0f_ref, gate_ref=gf_ref, out_ref=sf_ref),
        "b": dict(dmat=jnp.where(diff <= 0, jnp.exp(lg_b * jnp.maximum(-diff, 0.0)), 0.0),
                  xi=jnp.exp(lg_b * (RC - idx_col)), zeta=jnp.exp(lg_b * idx_row),
                  gch=jnp.exp(jnp.ones((1, RDV), F32) * (lg_b * RC)),
                  st_ref=stb_ref, s0_ref=s0b_ref, gate_ref=gb_ref, out_ref=sb_ref),
    }

    def shared(rows):
        q = q_ref[rows, :]
        k = k_ref[rows, :]
        s = lax.dot_general(q, k, (((1,), (1,)), ((), ())), preferred_element_type=F32)
        return q, s, k.astype(F32).T, v_ref[rows, :]

    def chunk_dir(d, sh, rows, first, last, sq):
        q, s, kt, v = sh
        sm = (s * d["dmat"]).astype(BF16)
        kzt = (kt * d["zeta"]).astype(BF16)
        both = jnp.dot(jnp.concatenate([sm, kzt], axis=0), v, preferred_element_type=F32)
        o, upd = both[:RC], both[RC:]
        st = None
        if has_init and first:
            st = d["s0_ref"][0, 0, 0]
        elif not first:
            st = d["st_ref"][...]
        if st is not None:
            o = o + jnp.dot(q, st.astype(BF16), preferred_element_type=F32) * d["xi"]
            upd = st * d["gch"] + upd
        if last and emit_state:
            d["out_ref"][sq, 0, 0] = upd
        elif not last:
            d["st_ref"][...] = upd
        o = o * lax.rsqrt(jnp.mean(o * o, axis=-1, keepdims=True) + EPS)
        return o * d["gate_ref"][rows, :].astype(F32)

    for sq in range(n_seq):
        half_done = set()

        def emit(c, val):
            rows = slice(sq * seq_len + c * RC, sq * seq_len + (c + 1) * RC)
            if c in half_done:
                y_ref[rows, :] = (ybuf_ref[rows, :] + val).astype(BF16)
            else:
                ybuf_ref[rows, :] = val
                half_done.add(c)

        for t in range(n_chunks):
            cf, cb = t, n_chunks - 1 - t
            rows_f = slice(sq * seq_len + cf * RC, sq * seq_len + (cf + 1) * RC)
            rows_b = slice(sq * seq_len + cb * RC, sq * seq_len + (cb + 1) * RC)
            sh_f = shared(rows_f)
            sh_b = sh_f if cb == cf else shared(rows_b)
            first, last = t == 0, t == n_chunks - 1
            of = chunk_dir(dirs["f"], sh_f, rows_f, first, last, sq)
            ob = chunk_dir(dirs["b"], sh_b, rows_b, first, last, sq)
            if cf == cb:
                y_ref[rows_f, :] = (of + ob).astype(BF16)
            else:
                emit(cf, of)
                emit(cb, ob)


def _ret_scan(qkvg, lg, y_prev, *, n_blocks, n_seq, seq_len, row_block0, s0f=None, s0b=None, emit_state=False):
    has_init = s0f is not None
    kq = RDK
    n_qk = RH * RDK // kq
    in_specs = [
        pl.BlockSpec(memory_space=pltpu.SMEM),
        pl.BlockSpec((RET_BLOCK, RDK), lambda b, h: (row_block0 + b, h)),
        pl.BlockSpec((RET_BLOCK, RDK), lambda b, h: (row_block0 + b, n_qk + h)),
        pl.BlockSpec((RET_BLOCK, RDV), lambda b, h: (row_block0 + b, n_qk + h)),
        pl.BlockSpec((RET_BLOCK, RDV), lambda b, h: (row_block0 + b, n_qk + RH + h)),
        pl.BlockSpec((RET_BLOCK, RDV), lambda b, h: (row_block0 + b, n_qk + 2 * RH + h)),
    ]
    args = [lg, qkvg, qkvg, qkvg, qkvg, qkvg]
    if has_init:
        st_spec = pl.BlockSpec((1, 1, 1, RDK, RDV), lambda b, h: (b, 0, h, 0, 0))
        in_specs += [st_spec, st_spec]
        args += [s0f, s0b]
    in_specs.append(pl.BlockSpec(memory_space=pl.ANY))
    args.append(y_prev)
    y_shape = jax.ShapeDtypeStruct((T_ALL, RH * RDV), BF16)
    out_specs = [pl.BlockSpec((RET_BLOCK, RDV), lambda b, h: (row_block0 + b, h))]
    out_shape = [y_shape]
    if emit_state:
        so_spec = pl.BlockSpec((n_seq, 1, 1, RDK, RDV), lambda b, h: (b, 0, h, 0, 0))
        so_shape = jax.ShapeDtypeStruct((n_blocks * n_seq, 1, RH, RDK, RDV), F32)
        out_specs += [so_spec, so_spec]
        out_shape += [so_shape, so_shape]

    def body(*refs):
        refs = list(refs)
        n_in = len(in_specs)
        ins = refs[:n_in - 1]
        _ret_scan_kernel(n_seq, seq_len, has_init, emit_state, *ins, *refs[n_in:])

    return pl.pallas_call(
        body,
        grid=(n_blocks, RH),
        in_specs=in_specs,
        out_specs=out_specs,
        out_shape=out_shape,
        scratch_shapes=[pltpu.VMEM((RET_BLOCK, RDV), F32), pltpu.VMEM((RDK, RDV), F32),
                        pltpu.VMEM((RDK, RDV), F32)],
        input_output_aliases={len(args) - 1: 0},
        compiler_params=_cparams(2),
        name="ret_scan_ctx" if emit_state else "ret_scan_lat",
    )(*args)


RES_TM = MOD_TM
RES_MC = 256


def _res_kernel(gate_k, split_x, next_mod, a_ref, w_ref, *rest):
    rest = list(rest)
    x_refs = [rest.pop(0), rest.pop(0)] if split_x else [rest.pop(0)]
    mod_ref = rest.pop(0)
    gain_ref = rest.pop(0)
    if next_mod is not None:
        nmod_ref = rest.pop(0)
        xo_ref, h_ref = rest
    else:
        yc_ref, yl_ref = rest
    i = pl.program_id(0)
    is_ctx = i < N_CTX_BLK
    gate = mod_ref[0, gate_k:gate_k + 1, :]

    def run(x_ref, y_ref):
        for m in range(RES_TM // RES_MC):
            rows = slice(m * RES_MC, (m + 1) * RES_MC)
            acc = jnp.dot(a_ref[rows, :], w_ref[...], preferred_element_type=F32)
            xn = x_ref[rows, :] + gate * acc
            if next_mod is not None:
                xo_ref[rows, :] = xn
                h_ref[rows, :] = _modulate(xn, gain_ref, nmod_ref, *next_mod).astype(BF16)
            else:
                ms = jnp.mean(xn * xn, axis=-1, keepdims=True)
                y_ref[rows, :] = xn * lax.rsqrt(ms + EPS) * gain_ref[...]

    if not split_x and next_mod is not None:
        run(x_refs[0], None)
        return

    @pl.when(is_ctx)
    def _():
        run(x_refs[0], None if next_mod is not None else yc_ref)

    @pl.when(jnp.logical_not(is_ctx))
    def _():
        run(x_refs[-1], None if next_mod is not None else yl_ref)


def _res_matmul(a, w, x, mods_l, gate_k, gain, next_mods=None, next_mod=None):
    kdim = a.shape[1]
    split_x = isinstance(x, tuple)
    row_spec = pl.BlockSpec((RES_TM, D), lambda i: (i, 0))
    ctx_spec = pl.BlockSpec((RES_TM, D), lambda i: (_ctx_blk(i), 0))
    lat_spec = pl.BlockSpec((RES_TM, D), lambda i: (_lat_blk(i), 0))
    mod_spec = pl.BlockSpec((1, N_MOD, D), lambda i: (_mod_row(i, RES_TM), 0, 0))
    in_specs = [pl.BlockSpec((RES_TM, kdim), lambda i: (i, 0)), pl.BlockSpec((kdim, D), lambda i: (0, 0))]
    args = [a, w]
    if split_x:
        in_specs += [ctx_spec, lat_spec]
        args += list(x)
    else:
        in_specs.append(row_spec)
        args.append(x)
    in_specs += [mod_spec, pl.BlockSpec((1, D), lambda i: (0, 0))]
    args += [mods_l, gain]
    if next_mod is not None:
        in_specs.append(mod_spec)
        args.append(next_mods)
        out_specs = [row_spec, row_spec]
        out_shape = [jax.ShapeDtypeStruct((T_ALL, D), F32), jax.ShapeDtypeStruct((T_ALL, D), BF16)]
    else:
        out_specs = [ctx_spec, lat_spec]
        out_shape = [jax.ShapeDtypeStruct((T_CTX, D), F32), jax.ShapeDtypeStruct((T_LAT, D), F32)]
    return pl.pallas_call(
        functools.partial(_res_kernel, gate_k, split_x, next_mod),
        grid=(T_ALL // RES_TM,),
        in_specs=in_specs,
        out_specs=out_specs,
        out_shape=out_shape,
        compiler_params=_cparams(1),
        name="res_matmul_k%d%s" % (kdim, "_final" if next_mod is None else ""),
    )(*args)


FFN_TM = 4096
FFN_MC = 512
FFN_TN = 256
FFN_HALO = 16
FFN_TILES = D_FF // FFN_TN
FFN_EXT = FFN_MC + (FFN_MC // CTX_L + 1) * FFN_HALO


def _ffn_up_kernel(h_ref, wu_ref, wg_ref, cwu_ref, cwg_ref, cbu_ref, cbg_ref, o_ref, a_scr, t_scr):
    is_ctx = pl.program_id(0) * FFN_TM < T_CTX
    zeros = jnp.zeros((FFN_HALO, D), BF16)
    n_slab = FFN_TN // 128

    def chunk(c, pieces, segments):
        slot = c % 2
        hm = jnp.concatenate(pieces, axis=0)
        n_ext = hm.shape[0]
        for k, w_ref in enumerate((wu_ref, wg_ref)):
            a = jnp.dot(hm, w_ref[...], preferred_element_type=F32)
            for s in range(n_slab):
                a_scr[slot, k * n_slab + s, 0:n_ext, :] = a[:, s * 128:(s + 1) * 128]

        def conv(slab, base, half, cw_ref, cb_ref, lanes):
            def rows(off):
                return a_scr[slot, slab, pl.ds(base + off, half, stride=2), :]
            w0, w1, w2 = cw_ref[0:1, lanes], cw_ref[1:2, lanes], cw_ref[2:3, lanes]
            b = cb_ref[:, lanes]
            om1, ev, od, ep1 = rows(-1), rows(0), rows(1), rows(2)
            return om1 * w0 + ev * w1 + od * w2 + b, ev * w0 + od * w1 + ep1 * w2 + b

        for s in range(n_slab):
            lanes = slice(s * 128, (s + 1) * 128)
            for base, r0, n_tok in segments:
                half = n_tok // 2
                t0 = r0 % FFN_MC
                u_ev, u_od = conv(s, base, half, cwu_ref, cbu_ref, lanes)
                g_ev, g_od = conv(n_slab + s, base, half, cwg_ref, cbg_ref, lanes)
                t_scr[slot, s, pl.ds(t0, half, stride=2), :] = _silu(g_ev) * u_ev
                t_scr[slot, s, pl.ds(t0 + 1, half, stride=2), :] = _silu(g_od) * u_od
            c0 = c * FFN_MC
            o_ref[c0:c0 + FFN_MC, lanes] = t_scr[slot, s, :, :].astype(BF16)

    @pl.when(is_ctx)
    def _():
        per_chunk = FFN_MC // CTX_L
        for c in range(FFN_TM // FFN_MC):
            pieces, segments = [zeros], []
            for q in range(per_chunk):
                r0 = (c * per_chunk + q) * CTX_L
                segments.append((FFN_HALO + q * (CTX_L + FFN_HALO), r0, CTX_L))
                pieces += [h_ref[r0:r0 + CTX_L, :], zeros]
            chunk(c, pieces, segments)

    @pl.when(jnp.logical_not(is_ctx))
    def _():
        n_chunks = FFN_TM // FFN_MC
        for c in range(n_chunks):
            r0 = c * FFN_MC
            lo = zeros if r0 % LAT_L == 0 else h_ref[r0 - FFN_HALO:r0, :]
            hi = zeros if (r0 + FFN_MC) % LAT_L == 0 else h_ref[r0 + FFN_MC:r0 + FFN_MC + FFN_HALO, :]
            chunk(c, [lo, h_ref[r0:r0 + FFN_MC, :], hi], [(FFN_HALO, r0, FFN_MC)])


def _ffn_up(h, w_in, conv_w, conv_b):
    n_slab = FFN_TN // 128
    return pl.pallas_call(
        _ffn_up_kernel,
        grid=(T_ALL // FFN_TM, FFN_TILES),
        in_specs=[
            pl.BlockSpec((FFN_TM, D), lambda i, j: (i, 0)),
            pl.BlockSpec((D, FFN_TN), lambda i, j: (0, j)),
            pl.BlockSpec((D, FFN_TN), lambda i, j: (0, FFN_TILES + j)),
            pl.BlockSpec((3, FFN_TN), lambda i, j: (0, j)),
            pl.BlockSpec((3, FFN_TN), lambda i, j: (0, FFN_TILES + j)),
            pl.BlockSpec((1, FFN_TN), lambda i, j: (0, j)),
            pl.BlockSpec((1, FFN_TN), lambda i, j: (0, FFN_TILES + j)),
        ],
        out_specs=pl.BlockSpec((FFN_TM, FFN_TN), lambda i, j: (i, j)),
        out_shape=jax.ShapeDtypeStruct((T_ALL, D_FF), BF16),
        scratch_shapes=[
            pltpu.VMEM((2, 2 * n_slab, FFN_EXT, 128), F32),
            pltpu.VMEM((2, n_slab, FFN_MC, 128), F32),
        ],
        compiler_params=_cparams(2),
        name="ffn_up",
    )(h, w_in, w_in, conv_w, conv_w, conv_b, conv_b)


ATT_NQ = AH * AHD
ATT_NKV = AKV * AHD
ATT_DUP = 2 * ATT_NKV
ATT_N = ATT_NQ + 2 * ATT_DUP
ATT_TN = ATT_NQ
assert 2 * ATT_DUP == ATT_TN


def _dup_heads(w):
    lead = w.shape[:-1]
    w = w.reshape(lead + (AKV, 1, AHD))
    return jnp.broadcast_to(w, lead + (AKV, 2, AHD)).reshape(lead + (ATT_DUP,))


def _undup_heads(x):
    return x.reshape(x.shape[:-1] + (AKV, 2, AHD))[..., 0, :].reshape(x.shape[:-1] + (ATT_NKV,))


def _attn_proj_kernel(h_ref, w_ref, cos_ref, slo_ref, shi_ref, q_ref, kd_ref, vd_ref, k32_ref, v32_ref):
    i = pl.program_id(0)
    j = pl.program_id(1)
    is_lat = i * PROJ_TM >= T_CTX

    def chunks(epilogue):
        for m in range(PROJ_TM // PROJ_MC):
            rows = slice(m * PROJ_MC, (m + 1) * PROJ_MC)
            epilogue(jnp.dot(h_ref[rows, :], w_ref[...], preferred_element_type=F32), rows)

    def rope_into(o_ref, acc, rows, c0, n_cols):
        for s in range(n_cols // 128):
            y = _rope_slab(acc[:, c0 + s * 128:c0 + (s + 1) * 128],
                           cos_ref[rows, :], slo_ref[rows, :], shi_ref[rows, :], AHD // 4)
            o_ref[rows, s * 128:(s + 1) * 128] = y.astype(BF16)

    def q_lat(acc, rows):
        rope_into(q_ref, acc, rows, 0, ATT_NQ)

    def q_ctx(acc, rows):
        q_ref[rows, :] = acc.astype(BF16)

    def kv_lat(acc, rows):
        rope_into(kd_ref, acc, rows, 0, ATT_DUP)
        vd_ref[rows, :] = acc[:, ATT_DUP:].astype(BF16)

    def kv_ctx(acc, rows):
        kd_ref[rows, :] = acc[:, :ATT_DUP].astype(BF16)
        vd_ref[rows, :] = acc[:, ATT_DUP:].astype(BF16)
        k32_ref[rows, :] = acc[:, :ATT_DUP]
        v32_ref[rows, :] = acc[:, ATT_DUP:]

    for on_q, on_lat, epilogue in ((True, True, q_lat), (True, False, q_ctx),
                                   (False, True, kv_lat), (False, False, kv_ctx)):
        @pl.when(jnp.logical_and((j == 0) == on_q, is_lat == on_lat))
        def _():
            chunks(epilogue)


def _attn_proj(h, w_all, tabs):
    cos, slo, shi = tabs
    n_pos = LAT_L // PROJ_TM
    n_ctx = T_CTX // PROJ_TM
    tab_spec = pl.BlockSpec((PROJ_TM, 128), lambda i, j: (i % n_pos, 0))
    row_spec = pl.BlockSpec((PROJ_TM, ATT_DUP), lambda i, j: (i, 0))
    ctx_spec = pl.BlockSpec((PROJ_TM, ATT_DUP), lambda i, j: (jnp.minimum(i, n_ctx - 1), 0))
    return pl.pallas_call(
        _attn_proj_kernel,
        grid=(T_ALL // PROJ_TM, ATT_N // ATT_TN),
        in_specs=[
            pl.BlockSpec((PROJ_TM, D), lambda i, j: (i, 0)),
            pl.BlockSpec((D, ATT_TN), lambda i, j: (0, j)),
            tab_spec, tab_spec, tab_spec,
        ],
        out_specs=[
            pl.BlockSpec((PROJ_TM, ATT_NQ), lambda i, j: (i, 0)),
            row_spec, row_spec, ctx_spec, ctx_spec,
        ],
        out_shape=[
            jax.ShapeDtypeStruct((T_ALL, ATT_NQ), BF16),
            jax.ShapeDtypeStruct((T_ALL, ATT_DUP), BF16),
            jax.ShapeDtypeStruct((T_ALL, ATT_DUP), BF16),
            jax.ShapeDtypeStruct((T_CTX, ATT_DUP), F32),
            jax.ShapeDtypeStruct((T_CTX, ATT_DUP), F32),
        ],
        compiler_params=_cparams(2),
        name="attn_proj",
    )(h, w_all, cos, slo, shi)


def _head_scores(q_ref, kh, k_dup):
    nq = q_ref.shape[0]
    nk = k_dup.shape[0]
    low = lax.broadcasted_iota(jnp.int32, (nk, 128), 1) < AHD
    zero = jnp.zeros_like(k_dup)
    k2 = jnp.concatenate([jnp.where(low, k_dup, zero), jnp.where(low, zero, k_dup)], axis=0)
    base = kh * AG * AHD
    lhs = jnp.concatenate([q_ref[:, base:base + 128], q_ref[:, base + 128:base + 256]], axis=0)
    s = lax.dot_general(lhs, k2, (((1,), (1,)), ((), ())), preferred_element_type=F32)
    return [s[0:nq, 0:nk], s[0:nq, nk:], s[nq:, 0:nk], s[nq:, nk:]]


def _row_max(s):
    m = s[:, 0:128]
    for c in range(128, s.shape[1], 128):
        m = jnp.maximum(m, s[:, c:c + 128])
    return jnp.max(m, axis=-1, keepdims=True)


def _softmax_pv_store(scores, v_dup, sink_ref, kh, o_ref):
    nq = scores[0].shape[0]
    nk = v_dup.shape[0]
    v_aug = jnp.concatenate([v_dup, jnp.ones((nk, 128), BF16)], axis=1)
    ps, tails = [], []
    for g, s in enumerate(scores):
        sink = sink_ref[kh * AG + g]
        m = jnp.maximum(_row_max(s), sink)
        ps.append(jnp.exp(s - m).astype(BF16))
        tails.append(jnp.exp(sink - m))
    pv = jnp.dot(jnp.concatenate(ps, axis=0), v_aug, preferred_element_type=F32)
    outs = []
    for g in range(AG):
        rows = slice(g * nq, (g + 1) * nq)
        outs.append(pv[rows, 0:128] / (pv[rows, 128:256] + tails[g]))
    low = lax.broadcasted_iota(jnp.int32, (nq, 128), 1) < AHD
    base = kh * AG * AHD
    o_ref[:, base:base + 128] = jnp.where(low, outs[0], outs[1]).astype(BF16)
    o_ref[:, base + 128:base + 256] = jnp.where(low, outs[2], outs[3]).astype(BF16)


def _attend_all(q_ref, sink_ref, o_ref, kv_of, bias_of):
    def scores_of(kh):
        k_dup, _ = kv_of(kh)
        return bias_of(_head_scores(q_ref, kh, k_dup))

    nxt = scores_of(0)
    for kh in range(AKV):
        cur = nxt
        if kh + 1 < AKV:
            nxt = scores_of(kh + 1)
        _softmax_pv_store(cur, kv_of(kh)[1], sink_ref, kh, o_ref)


def _ctx_attn_kernel(sink_ref, q_ref, kd_ref, vd_ref, o_ref):
    def kv_of(kh):
        cols = slice(kh * 128, (kh + 1) * 128)
        return kd_ref[:, cols], vd_ref[:, cols]

    _attend_all(q_ref, sink_ref, o_ref, kv_of, lambda scores: scores)


def _ctx_attn(sink, q, kd, vd, o_prev):
    return pl.pallas_call(
        lambda sink_ref, q_ref, kd_ref, vd_ref, prev_ref, o_ref: _ctx_attn_kernel(
            sink_ref, q_ref, kd_ref, vd_ref, o_ref),
        grid=(CTX_B,),
        in_specs=[
            pl.BlockSpec(memory_space=pltpu.SMEM),
            pl.BlockSpec((CTX_L, ATT_NQ), lambda b: (b, 0)),
            pl.BlockSpec((CTX_L, ATT_DUP), lambda b: (b, 0)),
            pl.BlockSpec((CTX_L, ATT_DUP), lambda b: (b, 0)),
            pl.BlockSpec(memory_space=pl.ANY),
        ],
        out_specs=pl.BlockSpec((CTX_L, ATT_NQ), lambda b: (b, 0)),
        out_shape=jax.ShapeDtypeStruct((T_ALL, ATT_NQ), BF16),
        input_output_aliases={4: 0},
        compiler_params=_cparams(1),
        name="ctx_attn",
    )(sink, q, kd, vd, o_prev)


BAND = 3 * ABLK


def _lat_attn_kernel(sink_ref, q_ref, kd_ref, vd_ref, ck_ref, cv_ref, o_ref):
    n = pl.program_id(1)
    start = pl.multiple_of(jnp.clip((n - 1) * ABLK, 0, LAT_L - BAND), ABLK)
    qpos = n * ABLK + lax.broadcasted_iota(jnp.int32, (ABLK, BAND), 0)
    kpos = start + lax.broadcasted_iota(jnp.int32, (ABLK, BAND), 1)
    band_bias = jnp.where(jnp.abs(qpos - kpos) <= WINDOW, 0.0, NEG_INF)

    def kv_of(kh):
        cols = slice(kh * 128, (kh + 1) * 128)
        k = jnp.concatenate([kd_ref[pl.ds(start, BAND), cols], ck_ref[0, :, cols]], axis=0)
        v = jnp.concatenate([vd_ref[pl.ds(start, BAND), cols], cv_ref[0, :, cols]], axis=0)
        return k, v

    def bias_of(scores):
        return [jnp.concatenate([s[:, :BAND] + band_bias, s[:, BAND:]], axis=1) for s in scores]

    _attend_all(q_ref, sink_ref, o_ref, kv_of, bias_of)


def _lat_attn(sink, q, kd, vd, cache_kd, cache_vd, o_prev):
    n_q = LAT_L // ABLK
    rb0 = T_CTX // ABLK
    sb0 = T_CTX // LAT_L
    seq_spec = pl.BlockSpec((LAT_L, ATT_DUP), lambda b, n: (sb0 + b, 0))
    cache_spec = pl.BlockSpec((1, CTX_L, ATT_DUP), lambda b, n: (b, 0, 0))
    return pl.pallas_call(
        lambda sink_ref, q_ref, kd_ref, vd_ref, ck_ref, cv_ref, prev_ref, o_ref: _lat_attn_kernel(
            sink_ref, q_ref, kd_ref, vd_ref, ck_ref, cv_ref, o_ref),
        grid=(LAT_B, n_q),
        in_specs=[
            pl.BlockSpec(memory_space=pltpu.SMEM),
            pl.BlockSpec((ABLK, ATT_NQ), lambda b, n: (rb0 + b * n_q + n, 0)),
            seq_spec, seq_spec, cache_spec, cache_spec,
            pl.BlockSpec(memory_space=pl.ANY),
        ],
        out_specs=pl.BlockSpec((ABLK, ATT_NQ), lambda b, n: (rb0 + b * n_q + n, 0)),
        out_shape=jax.ShapeDtypeStruct((T_ALL, ATT_NQ), BF16),
        input_output_aliases={6: 0},
        compiler_params=_cparams(2),
        name="lat_attn",
    )(sink, q, kd, vd, cache_kd, cache_vd, o_prev)


def _log_sigmoid(x):
    return -(jnp.maximum(-x, 0.0) + jnp.log1p(jnp.exp(-jnp.abs(x))))


def kernel(x_prompt, x_sample, state_ret_fwd, state_ret_bwd, cache_attn_k, cache_attn_v, c, c_ctx, ada_w, ada_b, norm_mix, norm_ffn, ret_wq, ret_wk, ret_wv, ret_wg_fwd, ret_wg_bwd, ret_wo, ret_decay_fwd, ret_decay_bwd, attn_wq, attn_wk, attn_wv, attn_wo, attn_sink, ffn_w_in, ffn_conv_w, ffn_conv_b, ffn_w_out, norm_final):
    x_ctx = x_prompt.reshape(T_CTX, D)
    x_lat = x_sample.reshape(T_LAT, D)

    cv = jnp.concatenate([c, c_ctx[None, :], jnp.zeros((MOD_ROWS - LAT_B - 1, D), F32)], axis=0)
    mods = _adaln(cv, ada_w, ada_b).reshape(ada_w.shape[0], MOD_ROWS, N_MOD, D)

    h = _first_mod(x_ctx, x_lat, norm_mix[0:1], mods[0])
    w_ret = jnp.concatenate([ret_wq[0], ret_wk[0] * (RDK ** -0.5), ret_wv[0], ret_wg_fwd[0], ret_wg_bwd[0]],
                            axis=1).astype(BF16)
    qkvg = _ret_proj(h, w_ret, _rope_tables(RDK))
    lg = jnp.stack([_log_sigmoid(ret_decay_fwd[0].astype(F32)), _log_sigmoid(ret_decay_bwd[0].astype(F32))])
    y0 = jnp.zeros((T_ALL, RH * RDV), BF16)
    y0, s_f, s_b = _ret_scan(qkvg, lg, y0, n_blocks=T_CTX // RET_BLOCK, n_seq=RET_BLOCK // CTX_L,
                             seq_len=CTX_L, row_block0=0, emit_state=True)
    (y0,) = _ret_scan(qkvg, lg, y0, n_blocks=LAT_B, n_seq=1, seq_len=LAT_L, row_block0=T_CTX // RET_BLOCK,
                      s0f=state_ret_fwd, s0b=state_ret_bwd)
    x, h = _res_matmul(y0, ret_wo[0].astype(BF16), (x_ctx, x_lat), mods[0], 2,
                       norm_ffn[0:1], mods[0], (3, 4))
    act = _ffn_up(h, ffn_w_in[0].astype(BF16), ffn_conv_w[0], ffn_conv_b[0:1])
    x, h = _res_matmul(act, ffn_w_out[0].astype(BF16), x, mods[0], 5, norm_mix[1:2], mods[1], (0, 1))

    w_att = jnp.concatenate([attn_wq[0] * (AHD ** -0.5), _dup_heads(attn_wk[0]), _dup_heads(attn_wv[0])],
                            axis=1).astype(BF16)
    q, kd, vd, k32, v32 = _attn_proj(h, w_att, _rope_tables(AHD))
    sink = attn_sink[0].astype(F32)
    cache_kd = _dup_heads(cache_attn_k[:, 0].reshape(LAT_B, CTX_L, ATT_NKV)).astype(BF16)
    cache_vd = _dup_heads(cache_attn_v[:, 0].reshape(LAT_B, CTX_L, ATT_NKV)).astype(BF16)
    a = _lat_attn(sink, q, kd, vd, cache_kd, cache_vd, jnp.zeros((T_ALL, ATT_NQ), BF16))
    a = _ctx_attn(sink, q, kd, vd, a)
    x, h = _res_matmul(a, attn_wo[0].astype(BF16), x, mods[1], 2, norm_ffn[1:2], mods[1], (3, 4))
    act = _ffn_up(h, ffn_w_in[1].astype(BF16), ffn_conv_w[1], ffn_conv_b[1:2])
    y_ctx, y_lat = _res_matmul(act, ffn_w_out[1].astype(BF16), x, mods[1], 5, norm_final[None, :])

    y_prompt = y_ctx.reshape(CTX_B, CTX_L, D)
    y_sample = y_lat.reshape(LAT_B, LAT_L, D)
    new_k = _undup_heads(k32).reshape(CTX_B, 1, CTX_L, AKV, AHD)
    new_v = _undup_heads(v32).reshape(CTX_B, 1, CTX_L, AKV, AHD)
    return y_prompt, y_sample, s_f, s_b, new_k, new_v
```

```python
import functools

import jax
import jax.numpy as jnp
import numpy as np
from jax import lax
from jax.experimental import pallas as pl
from jax.experimental.pallas import tpu as pltpu

D = 1024
CTX_B, CTX_L = 32, 256
LAT_B, LAT_L = 8, 2048
T_CTX = CTX_B * CTX_L
T_LAT = LAT_B * LAT_L
T_ALL = T_CTX + T_LAT
GRID_W = 64
N_MOD = 6
MOD_ROWS = 16
CTX_MOD_ROW = LAT_B
RH, RDK, RDV = 4, 256, 512
RC = 256
AH, AKV, AG, AHD = 16, 4, 4, 64
WINDOW = 128
ABLK = 128
D_FF = 2816
ROPE_BASE = 10000.0
EPS = 1e-6
NEG_INF = -1e30

BF16 = jnp.bfloat16
F32 = jnp.float32

VMEM_LIMIT_BYTES = 56 * 1024 * 1024
NORM_ROWS = 32


def _cparams(n_axes):
    return pltpu.CompilerParams(dimension_semantics=("arbitrary",) * n_axes,
                                vmem_limit_bytes=VMEM_LIMIT_BYTES)


def _silu(x):
    return x / (1.0 + jnp.exp(-x))


def _mod_row(i, tm):
    r0 = i * tm
    return jnp.where(r0 < T_CTX, CTX_MOD_ROW, (r0 - T_CTX) // LAT_L)


def _modulate(x, g_ref, mod_ref, shift_k, scale_k):
    ms = jnp.mean(x * x, axis=-1, keepdims=True)
    y = x * lax.rsqrt(ms + EPS) * g_ref[...]
    return y * (1.0 + mod_ref[0, scale_k:scale_k + 1, :]) + mod_ref[0, shift_k:shift_k + 1, :]


MOD_TM = 1024
N_CTX_BLK = T_CTX // MOD_TM


def _ctx_blk(i):
    return jnp.minimum(i, N_CTX_BLK - 1)


def _lat_blk(i):
    return jnp.maximum(i - N_CTX_BLK, 0)


def _first_mod_kernel(xc_ref, xl_ref, g_ref, mod_ref, h_ref):
    i = pl.program_id(0)

    def run(x_ref):
        for r in range(0, MOD_TM, NORM_ROWS):
            rows = slice(r, r + NORM_ROWS)
            h_ref[rows, :] = _modulate(x_ref[rows, :], g_ref, mod_ref, 0, 1).astype(BF16)

    @pl.when(i < N_CTX_BLK)
    def _():
        run(xc_ref)

    @pl.when(i >= N_CTX_BLK)
    def _():
        run(xl_ref)


def _first_mod(x_ctx, x_lat, gain, mods_l):
    return pl.pallas_call(
        _first_mod_kernel,
        grid=(T_ALL // MOD_TM,),
        in_specs=[
            pl.BlockSpec((MOD_TM, D), lambda i: (_ctx_blk(i), 0)),
            pl.BlockSpec((MOD_TM, D), lambda i: (_lat_blk(i), 0)),
            pl.BlockSpec((1, D), lambda i: (0, 0)),
            pl.BlockSpec((1, N_MOD, D), lambda i: (_mod_row(i, MOD_TM), 0, 0)),
        ],
        out_specs=pl.BlockSpec((MOD_TM, D), lambda i: (i, 0)),
        out_shape=jax.ShapeDtypeStruct((T_ALL, D), BF16),
        compiler_params=_cparams(1),
        name="first_mod",
    )(x_ctx, x_lat, gain, mods_l)


ADA_TN = 512


def _adaln_kernel(cv_ref, w_ref, b_ref, o_ref):
    s = _silu(cv_ref[...]).astype(BF16)
    w = w_ref[0].astype(BF16)
    o_ref[0] = jnp.dot(s, w, preferred_element_type=F32) + b_ref[0]


def _adaln(cv, ada_w, ada_b):
    depth = ada_w.shape[0]
    n = N_MOD * D
    return pl.pallas_call(
        _adaln_kernel,
        grid=(depth, n // ADA_TN),
        in_specs=[
            pl.BlockSpec((MOD_ROWS, D), lambda l, j: (0, 0)),
            pl.BlockSpec((1, D, ADA_TN), lambda l, j: (l, 0, j)),
            pl.BlockSpec((1, 1, ADA_TN), lambda l, j: (l, 0, j)),
        ],
        out_specs=pl.BlockSpec((1, MOD_ROWS, ADA_TN), lambda l, j: (l, 0, j)),
        out_shape=jax.ShapeDtypeStruct((depth, MOD_ROWS, n), F32),
        compiler_params=_cparams(2),
        name="adaln",
    )(cv, ada_w, ada_b.reshape(depth, 1, n))


def _rope_tables(head_dim):
    half = head_dim // 2
    n_freq = half // 2
    inv = (np.float32(ROPE_BASE) ** (-np.arange(n_freq, dtype=np.float32) / np.float32(n_freq))).astype(np.float32)
    t = np.arange(LAT_L)
    rows = (t // GRID_W).astype(np.float32)[:, None] * inv[None, :]
    cols = (t % GRID_W).astype(np.float32)[:, None] * inv[None, :]
    zero = np.zeros_like(rows)
    cos = np.concatenate([np.cos(rows)] * 2 + [np.cos(cols)] * 2, axis=-1)
    sin_lo = np.concatenate([-np.sin(rows), zero, -np.sin(cols), zero], axis=-1)
    sin_hi = np.concatenate([zero, np.sin(rows), zero, np.sin(cols)], axis=-1)
    return tuple(np.tile(x.astype(np.float32), (1, max(1, 128 // head_dim))) for x in (cos, sin_lo, sin_hi))


def _rope_slab(x, cos, sin_lo, sin_hi, n_freq):
    up = pltpu.roll(x, 128 - n_freq, 1)
    dn = pltpu.roll(x, n_freq, 1)
    return x * cos + up * sin_lo + dn * sin_hi


PROJ_TM = 1024
PROJ_MC = 256
RET_NQK = 2 * RH * RDK
RET_N = RET_NQK + 3 * RH * RDV
RET_TN = RET_N // 2


def _ret_proj_kernel(h_ref, w_ref, cos_ref, slo_ref, shi_ref, o_ref):
    i = pl.program_id(0)
    j = pl.program_id(1)
    is_lat = i * PROJ_TM >= T_CTX

    def chunks(epilogue):
        for m in range(PROJ_TM // PROJ_MC):
            rows = slice(m * PROJ_MC, (m + 1) * PROJ_MC)
            acc = jnp.dot(h_ref[rows, :], w_ref[...], preferred_element_type=F32)
            epilogue(acc, rows)

    def plain(acc, rows):
        o_ref[rows, :] = acc.astype(BF16)

    def gate(acc, rows):
        o_ref[rows, :] = _silu(acc).astype(BF16)

    def rope_qk(acc, rows):
        for s in range(RET_NQK // 128):
            cols = slice(s * 128, (s + 1) * 128)
            tcols = slice((s % 2) * 128, (s % 2 + 1) * 128)
            y = _rope_slab(acc[:, cols], cos_ref[rows, tcols], slo_ref[rows, tcols], shi_ref[rows, tcols], RDK // 4)
            o_ref[rows, cols] = y.astype(BF16)
        o_ref[rows, RET_NQK:] = acc[:, RET_NQK:].astype(BF16)

    @pl.when(jnp.logical_and(j == 0, is_lat))
    def _():
        chunks(rope_qk)

    @pl.when(jnp.logical_and(j == 0, jnp.logical_not(is_lat)))
    def _():
        chunks(plain)

    @pl.when(j == 1)
    def _():
        chunks(gate)


def _ret_proj(h, w_all, tabs):
    cos, slo, shi = tabs
    n_pos = LAT_L // PROJ_TM
    tab_spec = pl.BlockSpec((PROJ_TM, RDK), lambda i, j: (i % n_pos, 0))
    return pl.pallas_call(
        _ret_proj_kernel,
        grid=(T_ALL // PROJ_TM, RET_N // RET_TN),
        in_specs=[
            pl.BlockSpec((PROJ_TM, D), lambda i, j: (i, 0)),
            pl.BlockSpec((D, RET_TN), lambda i, j: (0, j)),
            tab_spec, tab_spec, tab_spec,
        ],
        out_specs=pl.BlockSpec((PROJ_TM, RET_TN), lambda i, j: (i, j)),
        out_shape=jax.ShapeDtypeStruct((T_ALL, RET_N), BF16),
        compiler_params=_cparams(2),
        name="ret_proj",
    )(h, w_all, cos, slo, shi)


RET_BLOCK = 2048


def _ret_scan_kernel(n_seq, seq_len, has_init, emit_state, lg_ref, q_ref, k_ref, v_ref, gf_ref, gb_ref, *rest):
    rest = list(rest)
    s0f_ref = rest.pop(0) if has_init else None
    s0b_ref = rest.pop(0) if has_init else None
    y_ref = rest.pop(0)
    sf_ref = rest.pop(0) if emit_state else None
    sb_ref = rest.pop(0) if emit_state else None
    ybuf_ref, stf_ref, stb_ref = rest

    hd = pl.program_id(1)
    n_chunks = seq_len // RC
    idx_col = lax.broadcasted_iota(jnp.int32, (RC, 1), 0).astype(F32)
    idx_row = lax.broadcasted_iota(jnp.int32, (1, RC), 1).astype(F32)
    diff = idx_col - idx_row

    lg_f = lg_ref[0, hd]
    lg_b = lg_ref[1, hd]
    dirs = {
        "f": dict(dmat=jnp.where(diff >= 0, jnp.exp(lg_f * jnp.maximum(diff, 0.0)), 0.0),
                  xi=jnp.exp(lg_f * (idx_col + 1.0)), zeta=jnp.exp(lg_f * (RC - 1.0 - idx_row)),
                  gch=jnp.exp(jnp.ones((1, RDV), F32) * (lg_f * RC)),
                  st_ref=stf_ref, s0_ref=s0f_ref, gate_ref=gf_ref, out_ref=sf_ref),
        "b": dict(dmat=jnp.where(diff <= 0, jnp.exp(lg_b * jnp.maximum(-diff, 0.0)), 0.0),
                  xi=jnp.exp(lg_b * (RC - idx_col)), zeta=jnp.exp(lg_b * idx_row),
                  gch=jnp.exp(jnp.ones((1, RDV), F32) * (lg_b * RC)),
                  st_ref=stb_ref, s0_ref=s0b_ref, gate_ref=gb_ref, out_ref=sb_ref),
    }

    def shared(rows):
        q = q_ref[rows, :]
        k = k_ref[rows, :]
        s = lax.dot_general(q, k, (((1,), (1,)), ((), ())), preferred_element_type=F32)
        return q, s, k.astype(F32).T, v_ref[rows, :]

    def chunk_dir(d, sh, rows, first, last, sq):
        q, s, kt, v = sh
        sm = (s * d["dmat"]).astype(BF16)
        kzt = (kt * d["zeta"]).astype(BF16)
        both = jnp.dot(jnp.concatenate([sm, kzt], axis=0), v, preferred_element_type=F32)
        o, upd = both[:RC], both[RC:]
        st = None
        if has_init and first:
            st = d["s0_ref"][0, 0, 0]
        elif not first:
            st = d["st_ref"][...]
        if st is not None:
            o = o + jnp.dot(q, st.astype(BF16), preferred_element_type=F32) * d["xi"]
            upd = st * d["gch"] + upd
        if last and emit_state:
            d["out_ref"][sq, 0, 0] = upd
        elif not last:
            d["st_ref"][...] = upd
        o = o * lax.rsqrt(jnp.mean(o * o, axis=-1, keepdims=True) + EPS)
        return o * d["gate_ref"][rows, :].astype(F32)

    for sq in range(n_seq):
        half_done = set()

        def emit(c, val):
            rows = slice(sq * seq_len + c * RC, sq * seq_len + (c + 1) * RC)
            if c in half_done:
                y_ref[rows, :] = (ybuf_ref[rows, :] + val).astype(BF16)
            else:
                ybuf_ref[rows, :] = val
                half_done.add(c)

        for t in range(n_chunks):
            cf, cb = t, n_chunks - 1 - t
            rows_f = slice(sq * seq_len + cf * RC, sq * seq_len + (cf + 1) * RC)
            rows_b = slice(sq * seq_len + cb * RC, sq * seq_len + (cb + 1) * RC)
            sh_f = shared(rows_f)
            sh_b = sh_f if cb == cf else shared(rows_b)
            first, last = t == 0, t == n_chunks - 1
            of = chunk_dir(dirs["f"], sh_f, rows_f, first, last, sq)
            ob = chunk_dir(dirs["b"], sh_b, rows_b, first, last, sq)
            if cf == cb:
                y_ref[rows_f, :] = (of + ob).astype(BF16)
            else:
                emit(cf, of)
                emit(cb, ob)


RET_CTX_BLK = T_CTX // RET_BLOCK
RET_CTX_SEQS = RET_BLOCK // CTX_L


def _ret_scan(qkvg, lg, s0f, s0b):
    n_qk = RH
    ctx_last = RET_CTX_BLK - 1

    def lat_b(b):
        return jnp.maximum(b - RET_CTX_BLK, 0)

    s0_spec = pl.BlockSpec((1, 1, 1, RDK, RDV), lambda b, h: (lat_b(b), 0, h, 0, 0))
    so_spec = pl.BlockSpec((RET_CTX_SEQS, 1, 1, RDK, RDV),
                           lambda b, h: (jnp.minimum(b, ctx_last), 0, jnp.where(b <= ctx_last, h, RH - 1), 0, 0))
    so_shape = jax.ShapeDtypeStruct((CTX_B, 1, RH, RDK, RDV), F32)

    def body(lg_ref, q_ref, k_ref, v_ref, gf_ref, gb_ref, s0f_ref, s0b_ref, y_ref, sf_ref, sb_ref, *scratch):
        is_ctx = pl.program_id(0) < RET_CTX_BLK

        @pl.when(is_ctx)
        def _():
            _ret_scan_kernel(RET_CTX_SEQS, CTX_L, False, True, lg_ref, q_ref, k_ref, v_ref, gf_ref, gb_ref,
                             y_ref, sf_ref, sb_ref, *scratch)

        @pl.when(jnp.logical_not(is_ctx))
        def _():
            _ret_scan_kernel(1, LAT_L, True, False, lg_ref, q_ref, k_ref, v_ref, gf_ref, gb_ref,
                             s0f_ref, s0b_ref, y_ref, *scratch)

    return pl.pallas_call(
        body,
        grid=(T_ALL // RET_BLOCK, RH),
        in_specs=[
            pl.BlockSpec(memory_space=pltpu.SMEM),
            pl.BlockSpec((RET_BLOCK, RDK), lambda b, h: (b, h)),
            pl.BlockSpec((RET_BLOCK, RDK), lambda b, h: (b, n_qk + h)),
            pl.BlockSpec((RET_BLOCK, RDV), lambda b, h: (b, n_qk + h)),
            pl.BlockSpec((RET_BLOCK, RDV), lambda b, h: (b, n_qk + RH + h)),
            pl.BlockSpec((RET_BLOCK, RDV), lambda b, h: (b, n_qk + 2 * RH + h)),
            s0_spec, s0_spec,
        ],
        out_specs=[pl.BlockSpec((RET_BLOCK, RDV), lambda b, h: (b, h)), so_spec, so_spec],
        out_shape=[jax.ShapeDtypeStruct((T_ALL, RH * RDV), BF16), so_shape, so_shape],
        scratch_shapes=[pltpu.VMEM((RET_BLOCK, RDV), F32), pltpu.VMEM((RDK, RDV), F32),
                        pltpu.VMEM((RDK, RDV), F32)],
        compiler_params=_cparams(2),
        name="ret_scan",
    )(lg, qkvg, qkvg, qkvg, qkvg, qkvg, s0f, s0b)


RES_TM = MOD_TM
RES_MC = 256


def _res_kernel(gate_k, split_x, next_mod, a_ref, w_ref, *rest):
    rest = list(rest)
    x_refs = [rest.pop(0), rest.pop(0)] if split_x else [rest.pop(0)]
    mod_ref = rest.pop(0)
    gain_ref = rest.pop(0)
    if next_mod is not None:
        nmod_ref = rest.pop(0)
        xo_ref, h_ref = rest
    else:
        yc_ref, yl_ref = rest
    i = pl.program_id(0)
    is_ctx = i < N_CTX_BLK
    gate = mod_ref[0, gate_k:gate_k + 1, :]

    def run(x_ref, y_ref):
        for m in range(RES_TM // RES_MC):
            rows = slice(m * RES_MC, (m + 1) * RES_MC)
            acc = jnp.dot(a_ref[rows, :], w_ref[...], preferred_element_type=F32)
            xn = x_ref[rows, :] + gate * acc
            if next_mod is not None:
                xo_ref[rows, :] = xn
                h_ref[rows, :] = _modulate(xn, gain_ref, nmod_ref, *next_mod).astype(BF16)
            else:
                ms = jnp.mean(xn * xn, axis=-1, keepdims=True)
                y_ref[rows, :] = xn * lax.rsqrt(ms + EPS) * gain_ref[...]

    if not split_x and next_mod is not None:
        run(x_refs[0], None)
        return

    @pl.when(is_ctx)
    def _():
        run(x_refs[0], None if next_mod is not None else yc_ref)

    @pl.when(jnp.logical_not(is_ctx))
    def _():
        run(x_refs[-1], None if next_mod is not None else yl_ref)


def _res_matmul(a, w, x, mods_l, gate_k, gain, next_mods=None, next_mod=None):
    kdim = a.shape[1]
    split_x = isinstance(x, tuple)
    row_spec = pl.BlockSpec((RES_TM, D), lambda i: (i, 0))
    ctx_spec = pl.BlockSpec((RES_TM, D), lambda i: (_ctx_blk(i), 0))
    lat_spec = pl.BlockSpec((RES_TM, D), lambda i: (_lat_blk(i), 0))
    mod_spec = pl.BlockSpec((1, N_MOD, D), lambda i: (_mod_row(i, RES_TM), 0, 0))
    in_specs = [pl.BlockSpec((RES_TM, kdim), lambda i: (i, 0)), pl.BlockSpec((kdim, D), lambda i: (0, 0))]
    args = [a, w]
    if split_x:
        in_specs += [ctx_spec, lat_spec]
        args += list(x)
    else:
        in_specs.append(row_spec)
        args.append(x)
    in_specs += [mod_spec, pl.BlockSpec((1, D), lambda i: (0, 0))]
    args += [mods_l, gain]
    if next_mod is not None:
        in_specs.append(mod_spec)
        args.append(next_mods)
        out_specs = [row_spec, row_spec]
        out_shape = [jax.ShapeDtypeStruct((T_ALL, D), F32), jax.ShapeDtypeStruct((T_ALL, D), BF16)]
    else:
        out_specs = [ctx_spec, lat_spec]
        out_shape = [jax.ShapeDtypeStruct((T_CTX, D), F32), jax.ShapeDtypeStruct((T_LAT, D), F32)]
    return pl.pallas_call(
        functools.partial(_res_kernel, gate_k, split_x, next_mod),
        grid=(T_ALL // RES_TM,),
        in_specs=in_specs,
        out_specs=out_specs,
        out_shape=out_shape,
        compiler_params=_cparams(1),
        name="res_matmul_k%d%s" % (kdim, "_final" if next_mod is None else ""),
    )(*args)


FFN_TM = 2048
FFN_MC = 512
FFN_TN = 256
FFN_HALO = 16
FFN_TILES = D_FF // FFN_TN
FFN_EXT = FFN_MC + (FFN_MC // CTX_L + 1) * FFN_HALO


def _ffn_up_kernel(h_ref, w_ref, cwu_ref, cwg_ref, cbu_ref, cbg_ref, o_ref, a_scr, t_scr):
    is_ctx = pl.program_id(0) * FFN_TM < T_CTX
    zeros = jnp.zeros((FFN_HALO, D), BF16)
    n_slab = FFN_TN // 128

    def chunk(c, pieces, segments):
        slot = c % 2
        out_r0 = segments[0][1]
        out_n = sum(n_tok for _, _, n_tok in segments)
        hm = jnp.concatenate(pieces, axis=0)
        n_ext = hm.shape[0]
        a = jnp.dot(hm, w_ref[...], preferred_element_type=F32)
        for s in range(2 * n_slab):
            a_scr[slot, s, 0:n_ext, :] = a[:, s * 128:(s + 1) * 128]

        def conv(slab, base, half, cw_ref, cb_ref, lanes):
            def rows(off):
                return a_scr[slot, slab, pl.ds(base + off, half, stride=2), :]
            w0, w1, w2 = cw_ref[0:1, lanes], cw_ref[1:2, lanes], cw_ref[2:3, lanes]
            b = cb_ref[:, lanes]
            om1, ev, od, ep1 = rows(-1), rows(0), rows(1), rows(2)
            return om1 * w0 + ev * w1 + od * w2 + b, ev * w0 + od * w1 + ep1 * w2 + b

        for s in range(n_slab):
            lanes = slice(s * 128, (s + 1) * 128)
            for base, r0, n_tok in segments:
                half = n_tok // 2
                t0 = r0 - out_r0
                u_ev, u_od = conv(s, base, half, cwu_ref, cbu_ref, lanes)
                g_ev, g_od = conv(n_slab + s, base, half, cwg_ref, cbg_ref, lanes)
                t_scr[slot, s, pl.ds(t0, half, stride=2), :] = _silu(g_ev) * u_ev
                t_scr[slot, s, pl.ds(t0 + 1, half, stride=2), :] = _silu(g_od) * u_od
            o_ref[out_r0:out_r0 + out_n, lanes] = t_scr[slot, s, 0:out_n, :].astype(BF16)

    sizes = [FFN_MC] * (FFN_TM // FFN_MC)
    starts = [sum(sizes[:c]) for c in range(len(sizes))]

    @pl.when(is_ctx)
    def _():
        for c, (c0, mc) in enumerate(zip(starts, sizes)):
            pieces, segments = [zeros], []
            for q in range(mc // CTX_L):
                r0 = c0 + q * CTX_L
                segments.append((FFN_HALO + q * (CTX_L + FFN_HALO), r0, CTX_L))
                pieces += [h_ref[r0:r0 + CTX_L, :], zeros]
            chunk(c, pieces, segments)

    @pl.when(jnp.logical_not(is_ctx))
    def _():
        for c, (r0, mc) in enumerate(zip(starts, sizes)):
            lo = zeros if r0 % LAT_L == 0 else h_ref[r0 - FFN_HALO:r0, :]
            hi = zeros if (r0 + mc) % LAT_L == 0 else h_ref[r0 + mc:r0 + mc + FFN_HALO, :]
            chunk(c, [lo, h_ref[r0:r0 + mc, :], hi], [(FFN_HALO, r0, mc)])


def _ffn_tile_major(w_in):
    return w_in.reshape(D, 2, FFN_TILES, FFN_TN).transpose(0, 2, 1, 3).reshape(D, 2 * D_FF)


def _ffn_up(h, w_in_tiles, conv_w, conv_b):
    n_slab = FFN_TN // 128
    return pl.pallas_call(
        _ffn_up_kernel,
        grid=(T_ALL // FFN_TM, FFN_TILES),
        in_specs=[
            pl.BlockSpec((FFN_TM, D), lambda i, j: (i, 0)),
            pl.BlockSpec((D, 2 * FFN_TN), lambda i, j: (0, j)),
            pl.BlockSpec((3, FFN_TN), lambda i, j: (0, j)),
            pl.BlockSpec((3, FFN_TN), lambda i, j: (0, FFN_TILES + j)),
            pl.BlockSpec((1, FFN_TN), lambda i, j: (0, j)),
            pl.BlockSpec((1, FFN_TN), lambda i, j: (0, FFN_TILES + j)),
        ],
        out_specs=pl.BlockSpec((FFN_TM, FFN_TN), lambda i, j: (i, j)),
        out_shape=jax.ShapeDtypeStruct((T_ALL, D_FF), BF16),
        scratch_shapes=[
            pltpu.VMEM((2, 2 * n_slab, FFN_EXT, 128), F32),
            pltpu.VMEM((2, n_slab, FFN_MC, 128), F32),
        ],
        compiler_params=_cparams(2),
        name="ffn_up",
    )(h, w_in_tiles, conv_w, conv_w, conv_b, conv_b)


ATT_NQ = AH * AHD
ATT_NKV = AKV * AHD
ATT_DUP = 2 * ATT_NKV
ATT_N = ATT_NQ + 2 * ATT_DUP
ATT_TN = ATT_NQ
assert 2 * ATT_DUP == ATT_TN


def _dup_heads(w):
    lead = w.shape[:-1]
    w = w.reshape(lead + (AKV, 1, AHD))
    return jnp.broadcast_to(w, lead + (AKV, 2, AHD)).reshape(lead + (ATT_DUP,))


def _undup_heads(x):
    return x.reshape(x.shape[:-1] + (AKV, 2, AHD))[..., 0, :].reshape(x.shape[:-1] + (ATT_NKV,))


def _attn_proj_kernel(h_ref, w_ref, cos_ref, slo_ref, shi_ref, q_ref, kd_ref, vd_ref, k32_ref, v32_ref):
    i = pl.program_id(0)
    j = pl.program_id(1)
    is_lat = i * PROJ_TM >= T_CTX

    def chunks(epilogue):
        for m in range(PROJ_TM // PROJ_MC):
            rows = slice(m * PROJ_MC, (m + 1) * PROJ_MC)
            epilogue(jnp.dot(h_ref[rows, :], w_ref[...], preferred_element_type=F32), rows)

    def rope_into(o_ref, acc, rows, c0, n_cols):
        for s in range(n_cols // 128):
            y = _rope_slab(acc[:, c0 + s * 128:c0 + (s + 1) * 128],
                           cos_ref[rows, :], slo_ref[rows, :], shi_ref[rows, :], AHD // 4)
            o_ref[rows, s * 128:(s + 1) * 128] = y.astype(BF16)

    def q_lat(acc, rows):
        rope_into(q_ref, acc, rows, 0, ATT_NQ)

    def q_ctx(acc, rows):
        q_ref[rows, :] = acc.astype(BF16)

    def kv_lat(acc, rows):
        rope_into(kd_ref, acc, rows, 0, ATT_DUP)
        vd_ref[rows, :] = acc[:, ATT_DUP:].astype(BF16)

    def kv_ctx(acc, rows):
        kd_ref[rows, :] = acc[:, :ATT_DUP].astype(BF16)
        vd_ref[rows, :] = acc[:, ATT_DUP:].astype(BF16)
        k32_ref[rows, :] = acc[:, :ATT_DUP]
        v32_ref[rows, :] = acc[:, ATT_DUP:]

    for on_q, on_lat, epilogue in ((True, True, q_lat), (True, False, q_ctx),
                                   (False, True, kv_lat), (False, False, kv_ctx)):
        @pl.when(jnp.logical_and((j == 0) == on_q, is_lat == on_lat))
        def _():
            chunks(epilogue)


def _attn_proj(h, w_all, tabs):
    cos, slo, shi = tabs
    n_pos = LAT_L // PROJ_TM
    n_ctx = T_CTX // PROJ_TM
    tab_spec = pl.BlockSpec((PROJ_TM, 128), lambda i, j: (i % n_pos, 0))
    row_spec = pl.BlockSpec((PROJ_TM, ATT_DUP), lambda i, j: (i, 0))
    ctx_spec = pl.BlockSpec((PROJ_TM, ATT_DUP), lambda i, j: (jnp.minimum(i, n_ctx - 1), 0))
    return pl.pallas_call(
        _attn_proj_kernel,
        grid=(T_ALL // PROJ_TM, ATT_N // ATT_TN),
        in_specs=[
            pl.BlockSpec((PROJ_TM, D), lambda i, j: (i, 0)),
            pl.BlockSpec((D, ATT_TN), lambda i, j: (0, j)),
            tab_spec, tab_spec, tab_spec,
        ],
        out_specs=[
            pl.BlockSpec((PROJ_TM, ATT_NQ), lambda i, j: (i, 0)),
            row_spec, row_spec, ctx_spec, ctx_spec,
        ],
        out_shape=[
            jax.ShapeDtypeStruct((T_ALL, ATT_NQ), BF16),
            jax.ShapeDtypeStruct((T_ALL, ATT_DUP), BF16),
            jax.ShapeDtypeStruct((T_ALL, ATT_DUP), BF16),
            jax.ShapeDtypeStruct((T_CTX, ATT_DUP), F32),
            jax.ShapeDtypeStruct((T_CTX, ATT_DUP), F32),
        ],
        compiler_params=_cparams(2),
        name="attn_proj",
    )(h, w_all, cos, slo, shi)


def _head_scores(q_ref, kh, k_dup):
    nq = q_ref.shape[0]
    nk = k_dup.shape[0]
    low = lax.broadcasted_iota(jnp.int32, (nk, 128), 1) < AHD
    zero = jnp.zeros_like(k_dup)
    k2 = jnp.concatenate([jnp.where(low, k_dup, zero), jnp.where(low, zero, k_dup)], axis=0)
    base = kh * AG * AHD
    lhs = jnp.concatenate([q_ref[:, base:base + 128], q_ref[:, base + 128:base + 256]], axis=0)
    s = lax.dot_general(lhs, k2, (((1,), (1,)), ((), ())), preferred_element_type=F32)
    return [s[0:nq, 0:nk], s[0:nq, nk:], s[nq:, 0:nk], s[nq:, nk:]]


def _row_max(s):
    m = s[:, 0:128]
    for c in range(128, s.shape[1], 128):
        m = jnp.maximum(m, s[:, c:c + 128])
    return jnp.max(m, axis=-1, keepdims=True)


def _softmax_pv_store(scores, v_dup, sink_ref, kh, o_ref):
    nq = scores[0].shape[0]
    nk = v_dup.shape[0]
    v_aug = jnp.concatenate([v_dup, jnp.ones((nk, 128), BF16)], axis=1)
    ps, tails = [], []
    for g, s in enumerate(scores):
        sink = sink_ref[kh * AG + g]
        m = jnp.maximum(_row_max(s), sink)
        ps.append(jnp.exp(s - m).astype(BF16))
        tails.append(jnp.exp(sink - m))
    pv = jnp.dot(jnp.concatenate(ps, axis=0), v_aug, preferred_element_type=F32)
    outs = []
    for g in range(AG):
        rows = slice(g * nq, (g + 1) * nq)
        outs.append(pv[rows, 0:128] / (pv[rows, 128:256] + tails[g]))
    low = lax.broadcasted_iota(jnp.int32, (nq, 128), 1) < AHD
    base = kh * AG * AHD
    o_ref[:, base:base + 128] = jnp.where(low, outs[0], outs[1]).astype(BF16)
    o_ref[:, base + 128:base + 256] = jnp.where(low, outs[2], outs[3]).astype(BF16)


def _attend_all(q_ref, sink_ref, o_ref, kv_of, bias_of):
    def scores_of(kh):
        k_dup, _ = kv_of(kh)
        return bias_of(_head_scores(q_ref, kh, k_dup))

    nxt = scores_of(0)
    for kh in range(AKV):
        cur = nxt
        if kh + 1 < AKV:
            nxt = scores_of(kh + 1)
        _softmax_pv_store(cur, kv_of(kh)[1], sink_ref, kh, o_ref)


def _ctx_attn_kernel(sink_ref, q_ref, kd_ref, vd_ref, o_ref):
    def kv_of(kh):
        cols = slice(kh * 128, (kh + 1) * 128)
        return kd_ref[:, cols], vd_ref[:, cols]

    _attend_all(q_ref, sink_ref, o_ref, kv_of, lambda scores: scores)


BAND = 3 * ABLK
ATT_CTX_STEPS = T_CTX // ABLK
ATT_LAT_QBLK = LAT_L // ABLK


def _lat_attn_kernel(sink_ref, q_ref, kd_ref, vd_ref, ck_ref, cv_ref, o_ref):
    n = (pl.program_id(0) - ATT_CTX_STEPS) % ATT_LAT_QBLK
    start = pl.multiple_of(jnp.clip((n - 1) * ABLK, 0, LAT_L - BAND), ABLK)
    qpos = n * ABLK + lax.broadcasted_iota(jnp.int32, (ABLK, BAND), 0)
    kpos = start + lax.broadcasted_iota(jnp.int32, (ABLK, BAND), 1)
    band_bias = jnp.where(jnp.abs(qpos - kpos) <= WINDOW, 0.0, NEG_INF)

    def kv_of(kh):
        cols = slice(kh * 128, (kh + 1) * 128)
        k = jnp.concatenate([kd_ref[pl.ds(start, BAND), cols], ck_ref[0, :, cols]], axis=0)
        v = jnp.concatenate([vd_ref[pl.ds(start, BAND), cols], cv_ref[0, :, cols]], axis=0)
        return k, v

    def bias_of(scores):
        return [jnp.concatenate([s[:, :BAND] + band_bias, s[:, BAND:]], axis=1) for s in scores]

    _attend_all(q_ref, sink_ref, o_ref, kv_of, bias_of)


def _attention(sink, q, kd, vd, cache_kd, cache_vd):
    def lat_seq(t):
        return jnp.maximum(t - ATT_CTX_STEPS, 0) // ATT_LAT_QBLK

    row_spec = pl.BlockSpec((ABLK, ATT_NQ), lambda t: (t, 0))
    ctx_spec = pl.BlockSpec((CTX_L, ATT_DUP), lambda t: (jnp.minimum(t // (CTX_L // ABLK), CTX_B - 1), 0))
    seq_spec = pl.BlockSpec((LAT_L, ATT_DUP), lambda t: (T_CTX // LAT_L + lat_seq(t), 0))
    cache_spec = pl.BlockSpec((1, CTX_L, ATT_DUP), lambda t: (lat_seq(t), 0, 0))

    def body(sink_ref, q_ref, kdc_ref, vdc_ref, kdl_ref, vdl_ref, ck_ref, cv_ref, o_ref):
        is_ctx = pl.program_id(0) < ATT_CTX_STEPS

        @pl.when(is_ctx)
        def _():
            _ctx_attn_kernel(sink_ref, q_ref, kdc_ref, vdc_ref, o_ref)

        @pl.when(jnp.logical_not(is_ctx))
        def _():
            _lat_attn_kernel(sink_ref, q_ref, kdl_ref, vdl_ref, ck_ref, cv_ref, o_ref)

    return pl.pallas_call(
        body,
        grid=(T_ALL // ABLK,),
        in_specs=[pl.BlockSpec(memory_space=pltpu.SMEM), row_spec,
                  ctx_spec, ctx_spec, seq_spec, seq_spec, cache_spec, cache_spec],
        out_specs=row_spec,
        out_shape=jax.ShapeDtypeStruct((T_ALL, ATT_NQ), BF16),
        compiler_params=_cparams(1),
        name="attention",
    )(sink, q, kd, vd, kd, vd, cache_kd, cache_vd)


def _log_sigmoid(x):
    return -(jnp.maximum(-x, 0.0) + jnp.log1p(jnp.exp(-jnp.abs(x))))


def kernel(x_prompt, x_sample, state_ret_fwd, state_ret_bwd, cache_attn_k, cache_attn_v, c, c_ctx, ada_w, ada_b, norm_mix, norm_ffn, ret_wq, ret_wk, ret_wv, ret_wg_fwd, ret_wg_bwd, ret_wo, ret_decay_fwd, ret_decay_bwd, attn_wq, attn_wk, attn_wv, attn_wo, attn_sink, ffn_w_in, ffn_conv_w, ffn_conv_b, ffn_w_out, norm_final):
    x_ctx = x_prompt.reshape(T_CTX, D)
    x_lat = x_sample.reshape(T_LAT, D)

    cv = jnp.concatenate([c, c_ctx[None, :], jnp.zeros((MOD_ROWS - LAT_B - 1, D), F32)], axis=0)
    mods = _adaln(cv, ada_w, ada_b).reshape(ada_w.shape[0], MOD_ROWS, N_MOD, D)

    h = _first_mod(x_ctx, x_lat, norm_mix[0:1], mods[0])
    w_ret = jnp.concatenate([ret_wq[0], ret_wk[0] * (RDK ** -0.5), ret_wv[0], ret_wg_fwd[0], ret_wg_bwd[0]],
                            axis=1).astype(BF16)
    qkvg = _ret_proj(h, w_ret, _rope_tables(RDK))
    lg = jnp.stack([_log_sigmoid(ret_decay_fwd[0].astype(F32)), _log_sigmoid(ret_decay_bwd[0].astype(F32))])
    y0, s_f, s_b = _ret_scan(qkvg, lg, state_ret_fwd, state_ret_bwd)
    x, h = _res_matmul(y0, ret_wo[0].astype(BF16), (x_ctx, x_lat), mods[0], 2,
                       norm_ffn[0:1], mods[0], (3, 4))
    act = _ffn_up(h, _ffn_tile_major(ffn_w_in[0]).astype(BF16), ffn_conv_w[0], ffn_conv_b[0:1])
    x, h = _res_matmul(act, ffn_w_out[0].astype(BF16), x, mods[0], 5, norm_mix[1:2], mods[1], (0, 1))

    w_att = jnp.concatenate([attn_wq[0] * (AHD ** -0.5), _dup_heads(attn_wk[0]), _dup_heads(attn_wv[0])],
                            axis=1).astype(BF16)
    q, kd, vd, k32, v32 = _attn_proj(h, w_att, _rope_tables(AHD))
    sink = attn_sink[0].astype(F32)
    cache_kd = _dup_heads(cache_attn_k[:, 0].reshape(LAT_B, CTX_L, ATT_NKV)).astype(BF16)
    cache_vd = _dup_heads(cache_attn_v[:, 0].reshape(LAT_B, CTX_L, ATT_NKV)).astype(BF16)
    a = _attention(sink, q, kd, vd, cache_kd, cache_vd)
    x, h = _res_matmul(a, attn_wo[0].astype(BF16), x, mods[1], 2, norm_ffn[1:2], mods[1], (3, 4))
    act = _ffn_up(h, _ffn_tile_major(ffn_w_in[1]).astype(BF16), ffn_conv_w[1], ffn_conv_b[1:2])
    y_ctx, y_lat = _res_matmul(act, ffn_w_out[1].astype(BF16), x, mods[1], 5, norm_final[None, :])

    y_prompt = y_ctx.reshape(CTX_B, CTX_L, D)
    y_sample = y_lat.reshape(LAT_B, LAT_L, D)
    new_k = _undup_heads(k32).reshape(CTX_B, 1, CTX_L, AKV, AHD)
    new_v = _undup_heads(v32).reshape(CTX_B, 1, CTX_L, AKV, AHD)
    return y_prompt, y_sample, s_f, s_b, new_k, new_v
```

```python
import functools

import jax
import jax.numpy as jnp
import numpy as np
from jax import lax
from jax.experimental import pallas as pl
from jax.experimental.pallas import tpu as pltpu

D = 1024
CTX_B, CTX_L = 32, 256
LAT_B, LAT_L = 8, 2048
T_CTX = CTX_B * CTX_L
T_LAT = LAT_B * LAT_L
T_ALL = T_CTX + T_LAT
GRID_W = 64
N_MOD = 6
MOD_ROWS = 16
CTX_MOD_ROW = LAT_B
RH, RDK, RDV = 4, 256, 512
RC = 256
AH, AKV, AG, AHD = 16, 4, 4, 64
WINDOW = 128
ABLK = 128
D_FF = 2816
ROPE_BASE = 10000.0
EPS = 1e-6
NEG_INF = -1e30

BF16 = jnp.bfloat16
F32 = jnp.float32

VMEM_LIMIT_BYTES = 56 * 1024 * 1024
NORM_ROWS = 32


def _cparams(n_axes):
    return pltpu.CompilerParams(dimension_semantics=("arbitrary",) * n_axes,
                                vmem_limit_bytes=VMEM_LIMIT_BYTES)


def _silu(x):
    return x / (1.0 + jnp.exp(-x))


def _mod_row(i, tm):
    r0 = i * tm
    return jnp.where(r0 < T_CTX, CTX_MOD_ROW, (r0 - T_CTX) // LAT_L)


def _modulate(x, g_ref, mod_ref, shift_k, scale_k):
    ms = jnp.mean(x * x, axis=-1, keepdims=True)
    y = x * lax.rsqrt(ms + EPS) * g_ref[...]
    return y * (1.0 + mod_ref[0, scale_k:scale_k + 1, :]) + mod_ref[0, shift_k:shift_k + 1, :]


MOD_TM = 1024
N_CTX_BLK = T_CTX // MOD_TM


def _ctx_blk(i):
    return jnp.minimum(i, N_CTX_BLK - 1)


def _lat_blk(i):
    return jnp.maximum(i - N_CTX_BLK, 0)


def _first_mod_kernel(xc_ref, xl_ref, g_ref, mod_ref, h_ref):
    i = pl.program_id(0)

    def run(x_ref):
        for r in range(0, MOD_TM, NORM_ROWS):
            rows = slice(r, r + NORM_ROWS)
            h_ref[rows, :] = _modulate(x_ref[rows, :], g_ref, mod_ref, 0, 1).astype(BF16)

    @pl.when(i < N_CTX_BLK)
    def _():
        run(xc_ref)

    @pl.when(i >= N_CTX_BLK)
    def _():
        run(xl_ref)


def _first_mod(x_ctx, x_lat, gain, mods_l):
    return pl.pallas_call(
        _first_mod_kernel,
        grid=(T_ALL // MOD_TM,),
        in_specs=[
            pl.BlockSpec((MOD_TM, D), lambda i: (_ctx_blk(i), 0)),
            pl.BlockSpec((MOD_TM, D), lambda i: (_lat_blk(i), 0)),
            pl.BlockSpec((1, D), lambda i: (0, 0)),
            pl.BlockSpec((1, N_MOD, D), lambda i: (_mod_row(i, MOD_TM), 0, 0)),
        ],
        out_specs=pl.BlockSpec((MOD_TM, D), lambda i: (i, 0)),
        out_shape=jax.ShapeDtypeStruct((T_ALL, D), BF16),
        compiler_params=_cparams(1),
        name="first_mod",
    )(x_ctx, x_lat, gain, mods_l)


ADA_TN = 512


def _adaln_kernel(cv_ref, w_ref, b_ref, o_ref):
    s = _silu(cv_ref[...]).astype(BF16)
    w = w_ref[0].astype(BF16)
    o_ref[0] = jnp.dot(s, w, preferred_element_type=F32) + b_ref[0]


def _adaln(cv, ada_w, ada_b):
    depth = ada_w.shape[0]
    n = N_MOD * D
    return pl.pallas_call(
        _adaln_kernel,
        grid=(depth, n // ADA_TN),
        in_specs=[
            pl.BlockSpec((MOD_ROWS, D), lambda l, j: (0, 0)),
            pl.BlockSpec((1, D, ADA_TN), lambda l, j: (l, 0, j)),
            pl.BlockSpec((1, 1, ADA_TN), lambda l, j: (l, 0, j)),
        ],
        out_specs=pl.BlockSpec((1, MOD_ROWS, ADA_TN), lambda l, j: (l, 0, j)),
        out_shape=jax.ShapeDtypeStruct((depth, MOD_ROWS, n), F32),
        compiler_params=_cparams(2),
        name="adaln",
    )(cv, ada_w, ada_b.reshape(depth, 1, n))


def _rope_tables(head_dim):
    half = head_dim // 2
    n_freq = half // 2
    inv = (np.float32(ROPE_BASE) ** (-np.arange(n_freq, dtype=np.float32) / np.float32(n_freq))).astype(np.float32)
    t = np.arange(LAT_L)
    rows = (t // GRID_W).astype(np.float32)[:, None] * inv[None, :]
    cols = (t % GRID_W).astype(np.float32)[:, None] * inv[None, :]
    zero = np.zeros_like(rows)
    cos = np.concatenate([np.cos(rows)] * 2 + [np.cos(cols)] * 2, axis=-1)
    sin_lo = np.concatenate([-np.sin(rows), zero, -np.sin(cols), zero], axis=-1)
    sin_hi = np.concatenate([zero, np.sin(rows), zero, np.sin(cols)], axis=-1)
    return tuple(np.tile(x.astype(np.float32), (1, max(1, 128 // head_dim))) for x in (cos, sin_lo, sin_hi))


def _rope_slab(x, cos, sin_lo, sin_hi, n_freq):
    up = pltpu.roll(x, 128 - n_freq, 1)
    dn = pltpu.roll(x, n_freq, 1)
    return x * cos + up * sin_lo + dn * sin_hi


PROJ_TM = 1024
PROJ_MC = 256
RET_NQK = 2 * RH * RDK
RET_N = RET_NQK + 3 * RH * RDV
RET_TN = RET_N // 2


def _ret_proj_kernel(h_ref, w_ref, cos_ref, slo_ref, shi_ref, o_ref):
    i = pl.program_id(0)
    j = pl.program_id(1)
    is_lat = i * PROJ_TM >= T_CTX

    def chunks(epilogue):
        for m in range(PROJ_TM // PROJ_MC):
            rows = slice(m * PROJ_MC, (m + 1) * PROJ_MC)
            acc = jnp.dot(h_ref[rows, :], w_ref[...], preferred_element_type=F32)
            epilogue(acc, rows)

    def plain(acc, rows):
        o_ref[rows, :] = acc.astype(BF16)

    def gate(acc, rows):
        o_ref[rows, :] = _silu(acc).astype(BF16)

    def rope_qk(acc, rows):
        for s in range(RET_NQK // 128):
            cols = slice(s * 128, (s + 1) * 128)
            tcols = slice((s % 2) * 128, (s % 2 + 1) * 128)
            y = _rope_slab(acc[:, cols], cos_ref[rows, tcols], slo_ref[rows, tcols], shi_ref[rows, tcols], RDK // 4)
            o_ref[rows, cols] = y.astype(BF16)
        o_ref[rows, RET_NQK:] = acc[:, RET_NQK:].astype(BF16)

    @pl.when(jnp.logical_and(j == 0, is_lat))
    def _():
        chunks(rope_qk)

    @pl.when(jnp.logical_and(j == 0, jnp.logical_not(is_lat)))
    def _():
        chunks(plain)

    @pl.when(j == 1)
    def _():
        chunks(gate)


def _ret_proj(h, w_all, tabs):
    cos, slo, shi = tabs
    n_pos = LAT_L // PROJ_TM
    tab_spec = pl.BlockSpec((PROJ_TM, RDK), lambda i, j: (i % n_pos, 0))
    return pl.pallas_call(
        _ret_proj_kernel,
        grid=(T_ALL // PROJ_TM, RET_N // RET_TN),
        in_specs=[
            pl.BlockSpec((PROJ_TM, D), lambda i, j: (i, 0)),
            pl.BlockSpec((D, RET_TN), lambda i, j: (0, j)),
            tab_spec, tab_spec, tab_spec,
        ],
        out_specs=pl.BlockSpec((PROJ_TM, RET_TN), lambda i, j: (i, j)),
        out_shape=jax.ShapeDtypeStruct((T_ALL, RET_N), BF16),
        compiler_params=_cparams(2),
        name="ret_proj",
    )(h, w_all, cos, slo, shi)


RET_BLOCK = 2048


def _ret_scan_kernel(n_seq, seq_len, has_init, emit_state, lg_ref, q_ref, k_ref, v_ref, gf_ref, gb_ref, *rest):
    rest = list(rest)
    s0f_ref = rest.pop(0) if has_init else None
    s0b_ref = rest.pop(0) if has_init else None
    y_ref = rest.pop(0)
    sf_ref = rest.pop(0) if emit_state else None
    sb_ref = rest.pop(0) if emit_state else None
    ybuf_ref, stf_ref, stb_ref = rest

    hd = pl.program_id(1)
    n_chunks = seq_len // RC
    idx_col = lax.broadcasted_iota(jnp.int32, (RC, 1), 0).astype(F32)
    idx_row = lax.broadcasted_iota(jnp.int32, (1, RC), 1).astype(F32)
    diff = idx_col - idx_row

    lg_f = lg_ref[0, hd]
    lg_b = lg_ref[1, hd]
    dirs = {
        "f": dict(dmat=jnp.where(diff >= 0, jnp.exp(lg_f * jnp.maximum(diff, 0.0)), 0.0),
                  xi=jnp.exp(lg_f * (idx_col + 1.0)), zeta=jnp.exp(lg_f * (RC - 1.0 - idx_row)),
                  gch=jnp.exp(jnp.ones((1, RDV), F32) * (lg_f * RC)),
                  st_ref=stf_ref, s0_ref=s0f_ref, gate_ref=gf_ref, out_ref=sf_ref),
        "b": dict(dmat=jnp.where(diff <= 0, jnp.exp(lg_b * jnp.maximum(-diff, 0.0)), 0.0),
                  xi=jnp.exp(lg_b * (RC - idx_col)), zeta=jnp.exp(lg_b * idx_row),
                  gch=jnp.exp(jnp.ones((1, RDV), F32) * (lg_b * RC)),
                  st_ref=stb_ref, s0_ref=s0b_ref, gate_ref=gb_ref, out_ref=sb_ref),
    }

    def shared(rows):
        q = q_ref[rows, :]
        k = k_ref[rows, :]
        s = lax.dot_general(q, k, (((1,), (1,)), ((), ())), preferred_element_type=F32)
        return q, s, k.astype(F32).T, v_ref[rows, :]

    def chunk_dir(d, sh, rows, first, last, sq):
        q, s, kt, v = sh
        sm = (s * d["dmat"]).astype(BF16)
        kzt = (kt * d["zeta"]).astype(BF16)
        both = jnp.dot(jnp.concatenate([sm, kzt], axis=0), v, preferred_element_type=F32)
        o, upd = both[:RC], both[RC:]
        st = None
        if has_init and first:
            st = d["s0_ref"][0, 0, 0]
        elif not first:
            st = d["st_ref"][...]
        if st is not None:
            o = o + jnp.dot(q, st.astype(BF16), preferred_element_type=F32) * d["xi"]
            upd = st * d["gch"] + upd
        if last and emit_state:
            d["out_ref"][sq, 0, 0] = upd
        elif not last:
            d["st_ref"][...] = upd
        o = o * lax.rsqrt(jnp.mean(o * o, axis=-1, keepdims=True) + EPS)
        return o * d["gate_ref"][rows, :].astype(F32)

    for sq in range(n_seq):
        half_done = set()

        def emit(c, val):
            rows = slice(sq * seq_len + c * RC, sq * seq_len + (c + 1) * RC)
            if c in half_done:
                y_ref[rows, :] = (ybuf_ref[rows, :] + val).astype(BF16)
            else:
                ybuf_ref[rows, :] = val
                half_done.add(c)

        for t in range(n_chunks):
            cf, cb = t, n_chunks - 1 - t
            rows_f = slice(sq * seq_len + cf * RC, sq * seq_len + (cf + 1) * RC)
            rows_b = slice(sq * seq_len + cb * RC, sq * seq_len + (cb + 1) * RC)
            sh_f = shared(rows_f)
            sh_b = sh_f if cb == cf else shared(rows_b)
            first, last = t == 0, t == n_chunks - 1
            of = chunk_dir(dirs["f"], sh_f, rows_f, first, last, sq)
            ob = chunk_dir(dirs["b"], sh_b, rows_b, first, last, sq)
            if cf == cb:
                y_ref[rows_f, :] = (of + ob).astype(BF16)
            else:
                emit(cf, of)
                emit(cb, ob)


RET_CTX_BLK = T_CTX // RET_BLOCK
RET_CTX_SEQS = RET_BLOCK // CTX_L


def _ret_scan(qkvg, lg, s0f, s0b):
    n_qk = RH
    ctx_last = RET_CTX_BLK - 1

    def lat_b(b):
        return jnp.maximum(b - RET_CTX_BLK, 0)

    s0_spec = pl.BlockSpec((1, 1, 1, RDK, RDV), lambda b, h: (lat_b(b), 0, h, 0, 0))
    so_spec = pl.BlockSpec((RET_CTX_SEQS, 1, 1, RDK, RDV),
                           lambda b, h: (jnp.minimum(b, ctx_last), 0, jnp.where(b <= ctx_last, h, RH - 1), 0, 0))
    so_shape = jax.ShapeDtypeStruct((CTX_B, 1, RH, RDK, RDV), F32)

    def body(lg_ref, q_ref, k_ref, v_ref, gf_ref, gb_ref, s0f_ref, s0b_ref, y_ref, sf_ref, sb_ref, *scratch):
        is_ctx = pl.program_id(0) < RET_CTX_BLK

        @pl.when(is_ctx)
        def _():
            _ret_scan_kernel(RET_CTX_SEQS, CTX_L, False, True, lg_ref, q_ref, k_ref, v_ref, gf_ref, gb_ref,
                             y_ref, sf_ref, sb_ref, *scratch)

        @pl.when(jnp.logical_not(is_ctx))
        def _():
            _ret_scan_kernel(1, LAT_L, True, False, lg_ref, q_ref, k_ref, v_ref, gf_ref, gb_ref,
                             s0f_ref, s0b_ref, y_ref, *scratch)

    return pl.pallas_call(
        body,
        grid=(T_ALL // RET_BLOCK, RH),
        in_specs=[
            pl.BlockSpec(memory_space=pltpu.SMEM),
            pl.BlockSpec((RET_BLOCK, RDK), lambda b, h: (b, h)),
            pl.BlockSpec((RET_BLOCK, RDK), lambda b, h: (b, n_qk + h)),
            pl.BlockSpec((RET_BLOCK, RDV), lambda b, h: (b, n_qk + h)),
            pl.BlockSpec((RET_BLOCK, RDV), lambda b, h: (b, n_qk + RH + h)),
            pl.BlockSpec((RET_BLOCK, RDV), lambda b, h: (b, n_qk + 2 * RH + h)),
            s0_spec, s0_spec,
        ],
        out_specs=[pl.BlockSpec((RET_BLOCK, RDV), lambda b, h: (b, h)), so_spec, so_spec],
        out_shape=[jax.ShapeDtypeStruct((T_ALL, RH * RDV), BF16), so_shape, so_shape],
        scratch_shapes=[pltpu.VMEM((RET_BLOCK, RDV), F32), pltpu.VMEM((RDK, RDV), F32),
                        pltpu.VMEM((RDK, RDV), F32)],
        compiler_params=_cparams(2),
        name="ret_scan",
    )(lg, qkvg, qkvg, qkvg, qkvg, qkvg, s0f, s0b)


RES_TM = MOD_TM
RES_MC = 256


def _res_kernel(gate_k, split_x, next_mod, a_ref, w_ref, *rest):
    rest = list(rest)
    x_refs = [rest.pop(0), rest.pop(0)] if split_x else [rest.pop(0)]
    mod_ref = rest.pop(0)
    gain_ref = rest.pop(0)
    if next_mod is not None:
        nmod_ref = rest.pop(0)
        xo_ref, h_ref = rest
    else:
        yc_ref, yl_ref = rest
    i = pl.program_id(0)
    is_ctx = i < N_CTX_BLK
    gate = mod_ref[0, gate_k:gate_k + 1, :]

    def run(x_ref, y_ref):
        for m in range(RES_TM // RES_MC):
            rows = slice(m * RES_MC, (m + 1) * RES_MC)
            acc = jnp.dot(a_ref[rows, :], w_ref[...], preferred_element_type=F32)
            xn = x_ref[rows, :] + gate * acc
            if next_mod is not None:
                xo_ref[rows, :] = xn
                h_ref[rows, :] = _modulate(xn, gain_ref, nmod_ref, *next_mod).astype(BF16)
            else:
                ms = jnp.mean(xn * xn, axis=-1, keepdims=True)
                y_ref[rows, :] = xn * lax.rsqrt(ms + EPS) * gain_ref[...]

    if not split_x and next_mod is not None:
        run(x_refs[0], None)
        return

    @pl.when(is_ctx)
    def _():
        run(x_refs[0], None if next_mod is not None else yc_ref)

    @pl.when(jnp.logical_not(is_ctx))
    def _():
        run(x_refs[-1], None if next_mod is not None else yl_ref)


def _res_matmul(a, w, x, mods_l, gate_k, gain, next_mods=None, next_mod=None):
    kdim = a.shape[1]
    split_x = isinstance(x, tuple)
    row_spec = pl.BlockSpec((RES_TM, D), lambda i: (i, 0))
    ctx_spec = pl.BlockSpec((RES_TM, D), lambda i: (_ctx_blk(i), 0))
    lat_spec = pl.BlockSpec((RES_TM, D), lambda i: (_lat_blk(i), 0))
    mod_spec = pl.BlockSpec((1, N_MOD, D), lambda i: (_mod_row(i, RES_TM), 0, 0))
    in_specs = [pl.BlockSpec((RES_TM, kdim), lambda i: (i, 0)), pl.BlockSpec((kdim, D), lambda i: (0, 0))]
    args = [a, w]
    if split_x:
        in_specs += [ctx_spec, lat_spec]
        args += list(x)
    else:
        in_specs.append(row_spec)
        args.append(x)
    in_specs += [mod_spec, pl.BlockSpec((1, D), lambda i: (0, 0))]
    args += [mods_l, gain]
    if next_mod is not None:
        in_specs.append(mod_spec)
        args.append(next_mods)
        out_specs = [row_spec, row_spec]
        out_shape = [jax.ShapeDtypeStruct((T_ALL, D), F32), jax.ShapeDtypeStruct((T_ALL, D), BF16)]
    else:
        out_specs = [ctx_spec, lat_spec]
        out_shape = [jax.ShapeDtypeStruct((T_CTX, D), F32), jax.ShapeDtypeStruct((T_LAT, D), F32)]
    return pl.pallas_call(
        functools.partial(_res_kernel, gate_k, split_x, next_mod),
        grid=(T_ALL // RES_TM,),
        in_specs=in_specs,
        out_specs=out_specs,
        out_shape=out_shape,
        compiler_params=_cparams(1),
        name="res_matmul_k%d%s" % (kdim, "_final" if next_mod is None else ""),
    )(*args)


FFN_TM = 2048
FFN_MC = 512
FFN_TN = 256
FFN_HALO = 16
FFN_TILES = D_FF // FFN_TN
FFN_EXT = FFN_MC + (FFN_MC // CTX_L + 1) * FFN_HALO


def _ffn_up_kernel(h_ref, wu_ref, wg_ref, cwu_ref, cwg_ref, cbu_ref, cbg_ref, o_ref, a_scr, t_scr):
    is_ctx = pl.program_id(0) * FFN_TM < T_CTX
    zeros = jnp.zeros((FFN_HALO, D), BF16)
    n_slab = FFN_TN // 128

    def chunk(c, pieces, segments):
        slot = c % 2
        out_r0 = segments[0][1]
        out_n = sum(n_tok for _, _, n_tok in segments)
        hm = jnp.concatenate(pieces, axis=0)
        n_ext = hm.shape[0]
        for k, w_ref in enumerate((wu_ref, wg_ref)):
            a = jnp.dot(hm, w_ref[...], preferred_element_type=F32)
            for s in range(n_slab):
                a_scr[slot, k * n_slab + s, 0:n_ext, :] = a[:, s * 128:(s + 1) * 128]

        def conv(slab, base, half, cw_ref, cb_ref, lanes):
            def rows(off):
                return a_scr[slot, slab, pl.ds(base + off, half, stride=2), :]
            w0, w1, w2 = cw_ref[0:1, lanes], cw_ref[1:2, lanes], cw_ref[2:3, lanes]
            b = cb_ref[:, lanes]
            om1, ev, od, ep1 = rows(-1), rows(0), rows(1), rows(2)
            return om1 * w0 + ev * w1 + od * w2 + b, ev * w0 + od * w1 + ep1 * w2 + b

        for s in range(n_slab):
            lanes = slice(s * 128, (s + 1) * 128)
            for base, r0, n_tok in segments:
                half = n_tok // 2
                t0 = r0 - out_r0
                u_ev, u_od = conv(s, base, half, cwu_ref, cbu_ref, lanes)
                g_ev, g_od = conv(n_slab + s, base, half, cwg_ref, cbg_ref, lanes)
                t_scr[slot, s, pl.ds(t0, half, stride=2), :] = _silu(g_ev) * u_ev
                t_scr[slot, s, pl.ds(t0 + 1, half, stride=2), :] = _silu(g_od) * u_od
            o_ref[out_r0:out_r0 + out_n, lanes] = t_scr[slot, s, 0:out_n, :].astype(BF16)

    sizes = [FFN_MC] * (FFN_TM // FFN_MC)
    starts = [sum(sizes[:c]) for c in range(len(sizes))]

    @pl.when(is_ctx)
    def _():
        for c, (c0, mc) in enumerate(zip(starts, sizes)):
            pieces, segments = [zeros], []
            for q in range(mc // CTX_L):
                r0 = c0 + q * CTX_L
                segments.append((FFN_HALO + q * (CTX_L + FFN_HALO), r0, CTX_L))
                pieces += [h_ref[r0:r0 + CTX_L, :], zeros]
            chunk(c, pieces, segments)

    @pl.when(jnp.logical_not(is_ctx))
    def _():
        for c, (r0, mc) in enumerate(zip(starts, sizes)):
            lo = zeros if r0 % LAT_L == 0 else h_ref[r0 - FFN_HALO:r0, :]
            hi = zeros if (r0 + mc) % LAT_L == 0 else h_ref[r0 + mc:r0 + mc + FFN_HALO, :]
            chunk(c, [lo, h_ref[r0:r0 + mc, :], hi], [(FFN_HALO, r0, mc)])


def _ffn_up(h, w_in, conv_w, conv_b):
    n_slab = FFN_TN // 128
    return pl.pallas_call(
        _ffn_up_kernel,
        grid=(T_ALL // FFN_TM, FFN_TILES),
        in_specs=[
            pl.BlockSpec((FFN_TM, D), lambda i, j: (i, 0)),
            pl.BlockSpec((D, FFN_TN), lambda i, j: (0, j)),
            pl.BlockSpec((D, FFN_TN), lambda i, j: (0, FFN_TILES + j)),
            pl.BlockSpec((3, FFN_TN), lambda i, j: (0, j)),
            pl.BlockSpec((3, FFN_TN), lambda i, j: (0, FFN_TILES + j)),
            pl.BlockSpec((1, FFN_TN), lambda i, j: (0, j)),
            pl.BlockSpec((1, FFN_TN), lambda i, j: (0, FFN_TILES + j)),
        ],
        out_specs=pl.BlockSpec((FFN_TM, FFN_TN), lambda i, j: (i, j)),
        out_shape=jax.ShapeDtypeStruct((T_ALL, D_FF), BF16),
        scratch_shapes=[
            pltpu.VMEM((2, 2 * n_slab, FFN_EXT, 128), F32),
            pltpu.VMEM((2, n_slab, FFN_MC, 128), F32),
        ],
        compiler_params=_cparams(2),
        name="ffn_up",
    )(h, w_in, w_in, conv_w, conv_w, conv_b, conv_b)


ATT_NQ = AH * AHD
ATT_NKV = AKV * AHD
ATT_DUP = 2 * ATT_NKV
ATT_N = ATT_NQ + 2 * ATT_DUP
ATT_TN = ATT_NQ
assert 2 * ATT_DUP == ATT_TN


def _dup_heads(w):
    lead = w.shape[:-1]
    w = w.reshape(lead + (AKV, AHD))
    return jnp.concatenate([w, w], axis=-1).reshape(lead + (ATT_DUP,))


def _undup_heads(x):
    return x.reshape(x.shape[:-1] + (AKV, 2, AHD))[..., 0, :].reshape(x.shape[:-1] + (ATT_NKV,))


def _attn_proj_kernel(h_ref, w_ref, cos_ref, slo_ref, shi_ref, q_ref, kd_ref, vd_ref, k32_ref, v32_ref):
    i = pl.program_id(0)
    j = pl.program_id(1)
    is_lat = i * PROJ_TM >= T_CTX

    def chunks(epilogue):
        for m in range(PROJ_TM // PROJ_MC):
            rows = slice(m * PROJ_MC, (m + 1) * PROJ_MC)
            epilogue(jnp.dot(h_ref[rows, :], w_ref[...], preferred_element_type=F32), rows)

    def rope_into(o_ref, acc, rows, c0, n_cols):
        for s in range(n_cols // 128):
            y = _rope_slab(acc[:, c0 + s * 128:c0 + (s + 1) * 128],
                           cos_ref[rows, :], slo_ref[rows, :], shi_ref[rows, :], AHD // 4)
            o_ref[rows, s * 128:(s + 1) * 128] = y.astype(BF16)

    def q_lat(acc, rows):
        rope_into(q_ref, acc, rows, 0, ATT_NQ)

    def q_ctx(acc, rows):
        q_ref[rows, :] = acc.astype(BF16)

    def kv_lat(acc, rows):
        rope_into(kd_ref, acc, rows, 0, ATT_DUP)
        vd_ref[rows, :] = acc[:, ATT_DUP:].astype(BF16)

    def kv_ctx(acc, rows):
        kd_ref[rows, :] = acc[:, :ATT_DUP].astype(BF16)
        vd_ref[rows, :] = acc[:, ATT_DUP:].astype(BF16)
        k32_ref[rows, :] = acc[:, :ATT_DUP]
        v32_ref[rows, :] = acc[:, ATT_DUP:]

    for on_q, on_lat, epilogue in ((True, True, q_lat), (True, False, q_ctx),
                                   (False, True, kv_lat), (False, False, kv_ctx)):
        @pl.when(jnp.logical_and((j == 0) == on_q, is_lat == on_lat))
        def _():
            chunks(epilogue)


def _attn_proj(h, w_all, tabs):
    cos, slo, shi = tabs
    n_pos = LAT_L // PROJ_TM
    n_ctx = T_CTX // PROJ_TM
    tab_spec = pl.BlockSpec((PROJ_TM, 128), lambda i, j: (i % n_pos, 0))
    row_spec = pl.BlockSpec((PROJ_TM, ATT_DUP), lambda i, j: (i, 0))
    ctx_spec = pl.BlockSpec((PROJ_TM, ATT_DUP), lambda i, j: (jnp.minimum(i, n_ctx - 1), 0))
    return pl.pallas_call(
        _attn_proj_kernel,
        grid=(T_ALL // PROJ_TM, ATT_N // ATT_TN),
        in_specs=[
            pl.BlockSpec((PROJ_TM, D), lambda i, j: (i, 0)),
            pl.BlockSpec((D, ATT_TN), lambda i, j: (0, j)),
            tab_spec, tab_spec, tab_spec,
        ],
        out_specs=[
            pl.BlockSpec((PROJ_TM, ATT_NQ), lambda i, j: (i, 0)),
            row_spec, row_spec, ctx_spec, ctx_spec,
        ],
        out_shape=[
            jax.ShapeDtypeStruct((T_ALL, ATT_NQ), BF16),
            jax.ShapeDtypeStruct((T_ALL, ATT_DUP), BF16),
            jax.ShapeDtypeStruct((T_ALL, ATT_DUP), BF16),
            jax.ShapeDtypeStruct((T_CTX, ATT_DUP), F32),
            jax.ShapeDtypeStruct((T_CTX, ATT_DUP), F32),
        ],
        compiler_params=_cparams(2),
        name="attn_proj",
    )(h, w_all, cos, slo, shi)


def _head_scores(q_ref, kh, k_dup):
    nq = q_ref.shape[0]
    nk = k_dup.shape[0]
    low = lax.broadcasted_iota(jnp.int32, (nk, 128), 1) < AHD
    zero = jnp.zeros_like(k_dup)
    k2 = jnp.concatenate([jnp.where(low, k_dup, zero), jnp.where(low, zero, k_dup)], axis=0)
    base = kh * AG * AHD
    lhs = jnp.concatenate([q_ref[:, base:base + 128], q_ref[:, base + 128:base + 256]], axis=0)
    s = lax.dot_general(lhs, k2, (((1,), (1,)), ((), ())), preferred_element_type=F32)
    return [s[0:nq, 0:nk], s[0:nq, nk:], s[nq:, 0:nk], s[nq:, nk:]]


def _row_max(s):
    m = s[:, 0:128]
    for c in range(128, s.shape[1], 128):
        m = jnp.maximum(m, s[:, c:c + 128])
    return jnp.max(m, axis=-1, keepdims=True)


def _softmax_pv_store(scores, v_dup, sink_ref, kh, o_ref):
    nq = scores[0].shape[0]
    nk = v_dup.shape[0]
    v_aug = jnp.concatenate([v_dup, jnp.ones((nk, 128), BF16)], axis=1)
    ps, tails = [], []
    for g, s in enumerate(scores):
        sink = sink_ref[kh * AG + g]
        m = jnp.maximum(_row_max(s), sink)
        ps.append(jnp.exp(s - m).astype(BF16))
        tails.append(jnp.exp(sink - m))
    pv = jnp.dot(jnp.concatenate(ps, axis=0), v_aug, preferred_element_type=F32)
    outs = []
    for g in range(AG):
        rows = slice(g * nq, (g + 1) * nq)
        outs.append(pv[rows, 0:128] / (pv[rows, 128:256] + tails[g]))
    low = lax.broadcasted_iota(jnp.int32, (nq, 128), 1) < AHD
    base = kh * AG * AHD
    o_ref[:, base:base + 128] = jnp.where(low, outs[0], outs[1]).astype(BF16)
    o_ref[:, base + 128:base + 256] = jnp.where(low, outs[2], outs[3]).astype(BF16)


def _attend_all(q_ref, sink_ref, o_ref, kv_of, bias_of):
    def scores_of(kh):
        k_dup, _ = kv_of(kh)
        return bias_of(_head_scores(q_ref, kh, k_dup))

    nxt = scores_of(0)
    for kh in range(AKV):
        cur = nxt
        if kh + 1 < AKV:
            nxt = scores_of(kh + 1)
        _softmax_pv_store(cur, kv_of(kh)[1], sink_ref, kh, o_ref)


def _ctx_attn_kernel(sink_ref, q_ref, kd_ref, vd_ref, o_ref):
    def kv_of(kh):
        cols = slice(kh * 128, (kh + 1) * 128)
        return kd_ref[:, cols], vd_ref[:, cols]

    _attend_all(q_ref, sink_ref, o_ref, kv_of, lambda scores: scores)


BAND = 3 * ABLK
ATT_CTX_STEPS = T_CTX // ABLK
ATT_LAT_QBLK = LAT_L // ABLK


def _lat_attn_kernel(sink_ref, q_ref, kd_ref, vd_ref, ck_ref, cv_ref, o_ref):
    n = (pl.program_id(0) - ATT_CTX_STEPS) % ATT_LAT_QBLK
    start = pl.multiple_of(jnp.clip((n - 1) * ABLK, 0, LAT_L - BAND), ABLK)
    qpos = n * ABLK + lax.broadcasted_iota(jnp.int32, (ABLK, BAND), 0)
    kpos = start + lax.broadcasted_iota(jnp.int32, (ABLK, BAND), 1)
    band_bias = jnp.where(jnp.abs(qpos - kpos) <= WINDOW, 0.0, NEG_INF)

    def kv_of(kh):
        cols = slice(kh * 128, (kh + 1) * 128)
        k = jnp.concatenate([kd_ref[pl.ds(start, BAND), cols], ck_ref[0, :, cols]], axis=0)
        v = jnp.concatenate([vd_ref[pl.ds(start, BAND), cols], cv_ref[0, :, cols]], axis=0)
        return k, v

    def bias_of(scores):
        return [jnp.concatenate([s[:, :BAND] + band_bias, s[:, BAND:]], axis=1) for s in scores]

    _attend_all(q_ref, sink_ref, o_ref, kv_of, bias_of)


def _attention(sink, q, kd, vd, cache_kd, cache_vd):
    def lat_seq(t):
        return jnp.maximum(t - ATT_CTX_STEPS, 0) // ATT_LAT_QBLK

    row_spec = pl.BlockSpec((ABLK, ATT_NQ), lambda t: (t, 0))
    ctx_spec = pl.BlockSpec((CTX_L, ATT_DUP), lambda t: (jnp.minimum(t // (CTX_L // ABLK), CTX_B - 1), 0))
    seq_spec = pl.BlockSpec((LAT_L, ATT_DUP), lambda t: (T_CTX // LAT_L + lat_seq(t), 0))
    cache_spec = pl.BlockSpec((1, CTX_L, ATT_DUP), lambda t: (lat_seq(t), 0, 0))

    def body(sink_ref, q_ref, kdc_ref, vdc_ref, kdl_ref, vdl_ref, ck_ref, cv_ref, o_ref):
        is_ctx = pl.program_id(0) < ATT_CTX_STEPS

        @pl.when(is_ctx)
        def _():
            _ctx_attn_kernel(sink_ref, q_ref, kdc_ref, vdc_ref, o_ref)

        @pl.when(jnp.logical_not(is_ctx))
        def _():
            _lat_attn_kernel(sink_ref, q_ref, kdl_ref, vdl_ref, ck_ref, cv_ref, o_ref)

    return pl.pallas_call(
        body,
        grid=(T_ALL // ABLK,),
        in_specs=[pl.BlockSpec(memory_space=pltpu.SMEM), row_spec,
                  ctx_spec, ctx_spec, seq_spec, seq_spec, cache_spec, cache_spec],
        out_specs=row_spec,
        out_shape=jax.ShapeDtypeStruct((T_ALL, ATT_NQ), BF16),
        compiler_params=_cparams(1),
        name="attention",
    )(sink, q, kd, vd, kd, vd, cache_kd, cache_vd)


def _log_sigmoid(x):
    return -(jnp.maximum(-x, 0.0) + jnp.log1p(jnp.exp(-jnp.abs(x))))


def kernel(x_prompt, x_sample, state_ret_fwd, state_ret_bwd, cache_attn_k, cache_attn_v, c, c_ctx, ada_w, ada_b, norm_mix, norm_ffn, ret_wq, ret_wk, ret_wv, ret_wg_fwd, ret_wg_bwd, ret_wo, ret_decay_fwd, ret_decay_bwd, attn_wq, attn_wk, attn_wv, attn_wo, attn_sink, ffn_w_in, ffn_conv_w, ffn_conv_b, ffn_w_out, norm_final):
    x_ctx = x_prompt.reshape(T_CTX, D)
    x_lat = x_sample.reshape(T_LAT, D)

    cv = jnp.concatenate([c, c_ctx[None, :], jnp.zeros((MOD_ROWS - LAT_B - 1, D), F32)], axis=0)
    mods = _adaln(cv, ada_w, ada_b).reshape(ada_w.shape[0], MOD_ROWS, N_MOD, D)

    h = _first_mod(x_ctx, x_lat, norm_mix[0:1], mods[0])
    w_ret = jnp.concatenate([ret_wq[0], ret_wk[0] * (RDK ** -0.5), ret_wv[0], ret_wg_fwd[0], ret_wg_bwd[0]],
                            axis=1).astype(BF16)
    qkvg = _ret_proj(h, w_ret, _rope_tables(RDK))
    lg = jnp.stack([_log_sigmoid(ret_decay_fwd[0].astype(F32)), _log_sigmoid(ret_decay_bwd[0].astype(F32))])
    y0, s_f, s_b = _ret_scan(qkvg, lg, state_ret_fwd, state_ret_bwd)
    x, h = _res_matmul(y0, ret_wo[0].astype(BF16), (x_ctx, x_lat), mods[0], 2,
                       norm_ffn[0:1], mods[0], (3, 4))
    act = _ffn_up(h, ffn_w_in[0].astype(BF16), ffn_conv_w[0], ffn_conv_b[0:1])
    x, h = _res_matmul(act, ffn_w_out[0].astype(BF16), x, mods[0], 5, norm_mix[1:2], mods[1], (0, 1))

    w_att = jnp.concatenate([attn_wq[0] * (AHD ** -0.5), _dup_heads(attn_wk[0]), _dup_heads(attn_wv[0])],
                            axis=1).astype(BF16)
    q, kd, vd, k32, v32 = _attn_proj(h, w_att, _rope_tables(AHD))
    sink = attn_sink[0].astype(F32)
    cache_kd = _dup_heads(cache_attn_k[:, 0].reshape(LAT_B, CTX_L, ATT_NKV)).astype(BF16)
    cache_vd = _dup_heads(cache_attn_v[:, 0].reshape(LAT_B, CTX_L, ATT_NKV)).astype(BF16)
    a = _attention(sink, q, kd, vd, cache_kd, cache_vd)
    x, h = _res_matmul(a, attn_wo[0].astype(BF16), x, mods[1], 2, norm_ffn[1:2], mods[1], (3, 4))
    act = _ffn_up(h, ffn_w_in[1].astype(BF16), ffn_conv_w[1], ffn_conv_b[1:2])
    y_ctx, y_lat = _res_matmul(act, ffn_w_out[1].astype(BF16), x, mods[1], 5, norm_final[None, :])

    y_prompt = y_ctx.reshape(CTX_B, CTX_L, D)
    y_sample = y_lat.reshape(LAT_B, LAT_L, D)
    new_k = _undup_heads(k32).reshape(CTX_B, 1, CTX_L, AKV, AHD)
    new_v = _undup_heads(v32).reshape(CTX_B, 1, CTX_L, AKV, AHD)
    return y_prompt, y_sample, s_f, s_b, new_k, new_v
```

```python
import functools

import jax
import jax.numpy as jnp
import numpy as np
from jax import lax
from jax.experimental import pallas as pl
from jax.experimental.pallas import tpu as pltpu

D = 1024
CTX_B, CTX_L = 32, 256
LAT_B, LAT_L = 8, 2048
T_CTX = CTX_B * CTX_L
T_LAT = LAT_B * LAT_L
T_ALL = T_CTX + T_LAT
GRID_W = 64
N_MOD = 6
MOD_ROWS = 16
CTX_MOD_ROW = LAT_B
RH, RDK, RDV = 4, 256, 512
RC = 256
AH, AKV, AG, AHD = 16, 4, 4, 64
WINDOW = 128
ABLK = 128
D_FF = 2816
ROPE_BASE = 10000.0
EPS = 1e-6
NEG_INF = -1e30

BF16 = jnp.bfloat16
F32 = jnp.float32

VMEM_LIMIT_BYTES = 56 * 1024 * 1024
NORM_ROWS = 32


def _cparams(n_axes):
    return pltpu.CompilerParams(dimension_semantics=("arbitrary",) * n_axes,
                                vmem_limit_bytes=VMEM_LIMIT_BYTES)


def _silu(x):
    return x / (1.0 + jnp.exp(-x))


def _mod_row(i, tm):
    r0 = i * tm
    return jnp.where(r0 < T_CTX, CTX_MOD_ROW, (r0 - T_CTX) // LAT_L)


def _modulate(x, g_ref, mod_ref, shift_k, scale_k):
    ms = jnp.mean(x * x, axis=-1, keepdims=True)
    y = x * lax.rsqrt(ms + EPS) * g_ref[...]
    return y * (1.0 + mod_ref[0, scale_k:scale_k + 1, :]) + mod_ref[0, shift_k:shift_k + 1, :]


MOD_TM = 1024
N_CTX_BLK = T_CTX // MOD_TM


def _ctx_blk(i):
    return jnp.minimum(i, N_CTX_BLK - 1)


def _lat_blk(i):
    return jnp.maximum(i - N_CTX_BLK, 0)


def _first_mod_kernel(xc_ref, xl_ref, g_ref, mod_ref, h_ref):
    i = pl.program_id(0)

    def run(x_ref):
        for r in range(0, MOD_TM, NORM_ROWS):
            rows = slice(r, r + NORM_ROWS)
            h_ref[rows, :] = _modulate(x_ref[rows, :], g_ref, mod_ref, 0, 1).astype(BF16)

    @pl.when(i < N_CTX_BLK)
    def _():
        run(xc_ref)

    @pl.when(i >= N_CTX_BLK)
    def _():
        run(xl_ref)


def _first_mod(x_ctx, x_lat, gain, mods_l):
    return pl.pallas_call(
        _first_mod_kernel,
        grid=(T_ALL // MOD_TM,),
        in_specs=[
            pl.BlockSpec((MOD_TM, D), lambda i: (_ctx_blk(i), 0)),
            pl.BlockSpec((MOD_TM, D), lambda i: (_lat_blk(i), 0)),
            pl.BlockSpec((1, D), lambda i: (0, 0)),
            pl.BlockSpec((1, N_MOD, D), lambda i: (_mod_row(i, MOD_TM), 0, 0)),
        ],
        out_specs=pl.BlockSpec((MOD_TM, D), lambda i: (i, 0)),
        out_shape=jax.ShapeDtypeStruct((T_ALL, D), BF16),
        compiler_params=_cparams(1),
        name="first_mod",
    )(x_ctx, x_lat, gain, mods_l)


ADA_TN = 512


def _adaln_kernel(cv_ref, w_ref, b_ref, o_ref):
    s = _silu(cv_ref[...]).astype(BF16)
    w = w_ref[0].astype(BF16)
    o_ref[0] = jnp.dot(s, w, preferred_element_type=F32) + b_ref[0]


def _adaln(cv, ada_w, ada_b):
    depth = ada_w.shape[0]
    n = N_MOD * D
    return pl.pallas_call(
        _adaln_kernel,
        grid=(depth, n // ADA_TN),
        in_specs=[
            pl.BlockSpec((MOD_ROWS, D), lambda l, j: (0, 0)),
            pl.BlockSpec((1, D, ADA_TN), lambda l, j: (l, 0, j)),
            pl.BlockSpec((1, 1, ADA_TN), lambda l, j: (l, 0, j)),
        ],
        out_specs=pl.BlockSpec((1, MOD_ROWS, ADA_TN), lambda l, j: (l, 0, j)),
        out_shape=jax.ShapeDtypeStruct((depth, MOD_ROWS, n), F32),
        compiler_params=_cparams(2),
        name="adaln",
    )(cv, ada_w, ada_b.reshape(depth, 1, n))


def _rope_tables(head_dim):
    half = head_dim // 2
    n_freq = half // 2
    inv = (np.float32(ROPE_BASE) ** (-np.arange(n_freq, dtype=np.float32) / np.float32(n_freq))).astype(np.float32)
    t = np.arange(LAT_L)
    rows = (t // GRID_W).astype(np.float32)[:, None] * inv[None, :]
    cols = (t % GRID_W).astype(np.float32)[:, None] * inv[None, :]
    zero = np.zeros_like(rows)
    cos = np.concatenate([np.cos(rows)] * 2 + [np.cos(cols)] * 2, axis=-1)
    sin_lo = np.concatenate([-np.sin(rows), zero, -np.sin(cols), zero], axis=-1)
    sin_hi = np.concatenate([zero, np.sin(rows), zero, np.sin(cols)], axis=-1)
    return tuple(np.tile(x.astype(np.float32), (1, max(1, 128 // head_dim))) for x in (cos, sin_lo, sin_hi))


def _rope_slab(x, cos, sin_lo, sin_hi, n_freq):
    up = pltpu.roll(x, 128 - n_freq, 1)
    dn = pltpu.roll(x, n_freq, 1)
    return x * cos + up * sin_lo + dn * sin_hi


PROJ_TM = 1024
PROJ_MC = 256
RET_NQK = 2 * RH * RDK
RET_N = RET_NQK + 3 * RH * RDV
RET_TN = RET_N // 2


def _ret_proj_kernel(h_ref, w_ref, cos_ref, slo_ref, shi_ref, o_ref):
    i = pl.program_id(0)
    j = pl.program_id(1)
    is_lat = i * PROJ_TM >= T_CTX

    def chunks(epilogue):
        for m in range(PROJ_TM // PROJ_MC):
            rows = slice(m * PROJ_MC, (m + 1) * PROJ_MC)
            acc = jnp.dot(h_ref[rows, :], w_ref[...], preferred_element_type=F32)
            epilogue(acc, rows)

    def plain(acc, rows):
        o_ref[rows, :] = acc.astype(BF16)

    def gate(acc, rows):
        o_ref[rows, :] = _silu(acc).astype(BF16)

    def rope_qk(acc, rows):
        for s in range(RET_NQK // 128):
            cols = slice(s * 128, (s + 1) * 128)
            tcols = slice((s % 2) * 128, (s % 2 + 1) * 128)
            y = _rope_slab(acc[:, cols], cos_ref[rows, tcols], slo_ref[rows, tcols], shi_ref[rows, tcols], RDK // 4)
            o_ref[rows, cols] = y.astype(BF16)
        o_ref[rows, RET_NQK:] = acc[:, RET_NQK:].astype(BF16)

    @pl.when(jnp.logical_and(j == 0, is_lat))
    def _():
        chunks(rope_qk)

    @pl.when(jnp.logical_and(j == 0, jnp.logical_not(is_lat)))
    def _():
        chunks(plain)

    @pl.when(j == 1)
    def _():
        chunks(gate)


def _ret_proj(h, w_all, tabs):
    cos, slo, shi = tabs
    n_pos = LAT_L // PROJ_TM
    tab_spec = pl.BlockSpec((PROJ_TM, RDK), lambda i, j: (i % n_pos, 0))
    return pl.pallas_call(
        _ret_proj_kernel,
        grid=(T_ALL // PROJ_TM, RET_N // RET_TN),
        in_specs=[
            pl.BlockSpec((PROJ_TM, D), lambda i, j: (i, 0)),
            pl.BlockSpec((D, RET_TN), lambda i, j: (0, j)),
            tab_spec, tab_spec, tab_spec,
        ],
        out_specs=pl.BlockSpec((PROJ_TM, RET_TN), lambda i, j: (i, j)),
        out_shape=jax.ShapeDtypeStruct((T_ALL, RET_N), BF16),
        compiler_params=_cparams(2),
        name="ret_proj",
    )(h, w_all, cos, slo, shi)


RET_BLOCK = 2048


def _ret_scan_kernel(n_seq, seq_len, has_init, emit_state, lg_ref, q_ref, k_ref, v_ref, gf_ref, gb_ref, *rest):
    rest = list(rest)
    s0f_ref = rest.pop(0) if has_init else None
    s0b_ref = rest.pop(0) if has_init else None
    y_ref = rest.pop(0)
    sf_ref = rest.pop(0) if emit_state else None
    sb_ref = rest.pop(0) if emit_state else None
    ybuf_ref, stf_ref, stb_ref = rest

    hd = pl.program_id(1)
    n_chunks = seq_len // RC
    idx_col = lax.broadcasted_iota(jnp.int32, (RC, 1), 0).astype(F32)
    idx_row = lax.broadcasted_iota(jnp.int32, (1, RC), 1).astype(F32)
    diff = idx_col - idx_row

    lg_f = lg_ref[0, hd]
    lg_b = lg_ref[1, hd]
    dirs = {
        "f": dict(dmat=jnp.where(diff >= 0, jnp.exp(lg_f * jnp.maximum(diff, 0.0)), 0.0),
                  xi=jnp.exp(lg_f * (idx_col + 1.0)), zeta=jnp.exp(lg_f * (RC - 1.0 - idx_row)),
                  gch=jnp.exp(jnp.ones((1, RDV), F32) * (lg_f * RC)),
                  st_ref=stf_ref, s0_ref=s0f_ref, gate_ref=gf_ref, out_ref=sf_ref),
        "b": dict(dmat=jnp.where(diff <= 0, jnp.exp(lg_b * jnp.maximum(-diff, 0.0)), 0.0),
                  xi=jnp.exp(lg_b * (RC - idx_col)), zeta=jnp.exp(lg_b * idx_row),
                  gch=jnp.exp(jnp.ones((1, RDV), F32) * (lg_b * RC)),
                  st_ref=stb_ref, s0_ref=s0b_ref, gate_ref=gb_ref, out_ref=sb_ref),
    }

    def shared(rows):
        q = q_ref[rows, :]
        k = k_ref[rows, :]
        s = lax.dot_general(q, k, (((1,), (1,)), ((), ())), preferred_element_type=F32)
        return q, s, k.astype(F32).T, v_ref[rows, :]

    def chunk_dir(d, sh, rows, first, last, sq):
        q, s, kt, v = sh
        sm = (s * d["dmat"]).astype(BF16)
        kzt = (kt * d["zeta"]).astype(BF16)
        both = jnp.dot(jnp.concatenate([sm, kzt], axis=0), v, preferred_element_type=F32)
        o, upd = both[:RC], both[RC:]
        st = None
        if has_init and first:
            st = d["s0_ref"][0, 0, 0]
        elif not first:
            st = d["st_ref"][...]
        if st is not None:
            o = o + jnp.dot(q, st.astype(BF16), preferred_element_type=F32) * d["xi"]
            upd = st * d["gch"] + upd
        if last and emit_state:
            d["out_ref"][sq, 0, 0] = upd
        elif not last:
            d["st_ref"][...] = upd
        o = o * lax.rsqrt(jnp.mean(o * o, axis=-1, keepdims=True) + EPS)
        return o * d["gate_ref"][rows, :].astype(F32)

    for sq in range(n_seq):
        half_done = set()

        def emit(c, val):
            rows = slice(sq * seq_len + c * RC, sq * seq_len + (c + 1) * RC)
            if c in half_done:
                y_ref[rows, :] = (ybuf_ref[rows, :] + val).astype(BF16)
            else:
                ybuf_ref[rows, :] = val
                half_done.add(c)

        for t in range(n_chunks):
            cf, cb = t, n_chunks - 1 - t
            rows_f = slice(sq * seq_len + cf * RC, sq * seq_len + (cf + 1) * RC)
            rows_b = slice(sq * seq_len + cb * RC, sq * seq_len + (cb + 1) * RC)
            sh_f = shared(rows_f)
            sh_b = sh_f if cb == cf else shared(rows_b)
            first, last = t == 0, t == n_chunks - 1
            of = chunk_dir(dirs["f"], sh_f, rows_f, first, last, sq)
            ob = chunk_dir(dirs["b"], sh_b, rows_b, first, last, sq)
            if cf == cb:
                y_ref[rows_f, :] = (of + ob).astype(BF16)
            else:
                emit(cf, of)
                emit(cb, ob)


RET_CTX_BLK = T_CTX // RET_BLOCK
RET_CTX_SEQS = RET_BLOCK // CTX_L


def _ret_scan(qkvg, lg, s0f, s0b):
    n_qk = RH
    ctx_last = RET_CTX_BLK - 1

    def lat_b(b):
        return jnp.maximum(b - RET_CTX_BLK, 0)

    s0_spec = pl.BlockSpec((1, 1, 1, RDK, RDV), lambda b, h: (lat_b(b), 0, h, 0, 0))
    so_spec = pl.BlockSpec((RET_CTX_SEQS, 1, 1, RDK, RDV),
                           lambda b, h: (jnp.minimum(b, ctx_last), 0, jnp.where(b <= ctx_last, h, RH - 1), 0, 0))
    so_shape = jax.ShapeDtypeStruct((CTX_B, 1, RH, RDK, RDV), F32)

    def body(lg_ref, q_ref, k_ref, v_ref, gf_ref, gb_ref, s0f_ref, s0b_ref, y_ref, sf_ref, sb_ref, *scratch):
        is_ctx = pl.program_id(0) < RET_CTX_BLK

        @pl.when(is_ctx)
        def _():
            _ret_scan_kernel(RET_CTX_SEQS, CTX_L, False, True, lg_ref, q_ref, k_ref, v_ref, gf_ref, gb_ref,
                             y_ref, sf_ref, sb_ref, *scratch)

        @pl.when(jnp.logical_not(is_ctx))
        def _():
            _ret_scan_kernel(1, LAT_L, True, False, lg_ref, q_ref, k_ref, v_ref, gf_ref, gb_ref,
                             s0f_ref, s0b_ref, y_ref, *scratch)

    return pl.pallas_call(
        body,
        grid=(T_ALL // RET_BLOCK, RH),
        in_specs=[
            pl.BlockSpec(memory_space=pltpu.SMEM),
            pl.BlockSpec((RET_BLOCK, RDK), lambda b, h: (b, h)),
            pl.BlockSpec((RET_BLOCK, RDK), lambda b, h: (b, n_qk + h)),
            pl.BlockSpec((RET_BLOCK, RDV), lambda b, h: (b, n_qk + h)),
            pl.BlockSpec((RET_BLOCK, RDV), lambda b, h: (b, n_qk + RH + h)),
            pl.BlockSpec((RET_BLOCK, RDV), lambda b, h: (b, n_qk + 2 * RH + h)),
            s0_spec, s0_spec,
        ],
        out_specs=[pl.BlockSpec((RET_BLOCK, RDV), lambda b, h: (b, h)), so_spec, so_spec],
        out_shape=[jax.ShapeDtypeStruct((T_ALL, RH * RDV), BF16), so_shape, so_shape],
        scratch_shapes=[pltpu.VMEM((RET_BLOCK, RDV), F32), pltpu.VMEM((RDK, RDV), F32),
                        pltpu.VMEM((RDK, RDV), F32)],
        compiler_params=_cparams(2),
        name="ret_scan",
    )(lg, qkvg, qkvg, qkvg, qkvg, qkvg, s0f, s0b)


RES_TM = MOD_TM
RES_MC = 256


def _res_kernel(gate_k, split_x, next_mod, a_ref, w_ref, *rest):
    rest = list(rest)
    x_refs = [rest.pop(0), rest.pop(0)] if split_x else [rest.pop(0)]
    mod_ref = rest.pop(0)
    gain_ref = rest.pop(0)
    if next_mod is not None:
        nmod_ref = rest.pop(0)
        xo_ref, h_ref = rest
    else:
        yc_ref, yl_ref = rest
    i = pl.program_id(0)
    is_ctx = i < N_CTX_BLK
    gate = mod_ref[0, gate_k:gate_k + 1, :]

    def run(x_ref, y_ref):
        for m in range(RES_TM // RES_MC):
            rows = slice(m * RES_MC, (m + 1) * RES_MC)
            acc = jnp.dot(a_ref[rows, :], w_ref[0], preferred_element_type=F32)
            xn = x_ref[rows, :] + gate * acc
            if next_mod is not None:
                xo_ref[rows, :] = xn
                h_ref[rows, :] = _modulate(xn, gain_ref, nmod_ref, *next_mod).astype(BF16)
            else:
                ms = jnp.mean(xn * xn, axis=-1, keepdims=True)
                y_ref[rows, :] = xn * lax.rsqrt(ms + EPS) * gain_ref[...]

    if not split_x and next_mod is not None:
        run(x_refs[0], None)
        return

    @pl.when(is_ctx)
    def _():
        run(x_refs[0], None if next_mod is not None else yc_ref)

    @pl.when(jnp.logical_not(is_ctx))
    def _():
        run(x_refs[-1], None if next_mod is not None else yl_ref)


def _res_matmul(a, w, layer, x, mods_l, gate_k, gain, next_mods=None, next_mod=None):
    kdim = a.shape[1]
    split_x = isinstance(x, tuple)
    row_spec = pl.BlockSpec((RES_TM, D), lambda i: (i, 0))
    ctx_spec = pl.BlockSpec((RES_TM, D), lambda i: (_ctx_blk(i), 0))
    lat_spec = pl.BlockSpec((RES_TM, D), lambda i: (_lat_blk(i), 0))
    mod_spec = pl.BlockSpec((1, N_MOD, D), lambda i: (_mod_row(i, RES_TM), 0, 0))
    in_specs = [pl.BlockSpec((RES_TM, kdim), lambda i: (i, 0)),
                pl.BlockSpec((1, kdim, D), lambda i: (layer, 0, 0))]
    args = [a, w]
    if split_x:
        in_specs += [ctx_spec, lat_spec]
        args += list(x)
    else:
        in_specs.append(row_spec)
        args.append(x)
    in_specs += [mod_spec, pl.BlockSpec((1, D), lambda i: (0, 0))]
    args += [mods_l, gain]
    if next_mod is not None:
        in_specs.append(mod_spec)
        args.append(next_mods)
        out_specs = [row_spec, row_spec]
        out_shape = [jax.ShapeDtypeStruct((T_ALL, D), F32), jax.ShapeDtypeStruct((T_ALL, D), BF16)]
    else:
        out_specs = [ctx_spec, lat_spec]
        out_shape = [jax.ShapeDtypeStruct((T_CTX, D), F32), jax.ShapeDtypeStruct((T_LAT, D), F32)]
    return pl.pallas_call(
        functools.partial(_res_kernel, gate_k, split_x, next_mod),
        grid=(T_ALL // RES_TM,),
        in_specs=in_specs,
        out_specs=out_specs,
        out_shape=out_shape,
        compiler_params=_cparams(1),
        name="res_matmul_k%d%s" % (kdim, "_final" if next_mod is None else ""),
    )(*args)


FFN_TM = 2048
FFN_MC = 512
FFN_TN = 256
FFN_HALO = 16
FFN_TILES = D_FF // FFN_TN
FFN_EXT = FFN_MC + (FFN_MC // CTX_L + 1) * FFN_HALO


def _ffn_up_kernel(h_ref, wu_ref, wg_ref, cwu_ref, cwg_ref, cbu_ref, cbg_ref, o_ref, a_scr, t_scr):
    is_ctx = pl.program_id(0) * FFN_TM < T_CTX
    zeros = jnp.zeros((FFN_HALO, D), BF16)
    n_slab = FFN_TN // 128

    def chunk(c, pieces, segments):
        slot = c % 2
        out_r0 = segments[0][1]
        out_n = sum(n_tok for _, _, n_tok in segments)
        hm = jnp.concatenate(pieces, axis=0)
        n_ext = hm.shape[0]
        for k, w_ref in enumerate((wu_ref, wg_ref)):
            a = jnp.dot(hm, w_ref[0], preferred_element_type=F32)
            for s in range(n_slab):
                a_scr[slot, k * n_slab + s, 0:n_ext, :] = a[:, s * 128:(s + 1) * 128]

        def conv(slab, base, half, cw_ref, cb_ref, lanes):
            def rows(off):
                return a_scr[slot, slab, pl.ds(base + off, half, stride=2), :]
            w0, w1, w2 = cw_ref[0, 0:1, lanes], cw_ref[0, 1:2, lanes], cw_ref[0, 2:3, lanes]
            b = cb_ref[0, :, lanes]
            om1, ev, od, ep1 = rows(-1), rows(0), rows(1), rows(2)
            return om1 * w0 + ev * w1 + od * w2 + b, ev * w0 + od * w1 + ep1 * w2 + b

        for s in range(n_slab):
            lanes = slice(s * 128, (s + 1) * 128)
            for base, r0, n_tok in segments:
                half = n_tok // 2
                t0 = r0 - out_r0
                u_ev, u_od = conv(s, base, half, cwu_ref, cbu_ref, lanes)
                g_ev, g_od = conv(n_slab + s, base, half, cwg_ref, cbg_ref, lanes)
                t_scr[slot, s, pl.ds(t0, half, stride=2), :] = _silu(g_ev) * u_ev
                t_scr[slot, s, pl.ds(t0 + 1, half, stride=2), :] = _silu(g_od) * u_od
            o_ref[out_r0:out_r0 + out_n, lanes] = t_scr[slot, s, 0:out_n, :].astype(BF16)

    sizes = [FFN_MC] * (FFN_TM // FFN_MC)
    starts = [sum(sizes[:c]) for c in range(len(sizes))]

    @pl.when(is_ctx)
    def _():
        for c, (c0, mc) in enumerate(zip(starts, sizes)):
            pieces, segments = [zeros], []
            for q in range(mc // CTX_L):
                r0 = c0 + q * CTX_L
                segments.append((FFN_HALO + q * (CTX_L + FFN_HALO), r0, CTX_L))
                pieces += [h_ref[r0:r0 + CTX_L, :], zeros]
            chunk(c, pieces, segments)

    @pl.when(jnp.logical_not(is_ctx))
    def _():
        for c, (r0, mc) in enumerate(zip(starts, sizes)):
            lo = zeros if r0 % LAT_L == 0 else h_ref[r0 - FFN_HALO:r0, :]
            hi = zeros if (r0 + mc) % LAT_L == 0 else h_ref[r0 + mc:r0 + mc + FFN_HALO, :]
            chunk(c, [lo, h_ref[r0:r0 + mc, :], hi], [(FFN_HALO, r0, mc)])


def _ffn_up(h, w_in, conv_w, conv_b, layer):
    n_slab = FFN_TN // 128
    return pl.pallas_call(
        _ffn_up_kernel,
        grid=(T_ALL // FFN_TM, FFN_TILES),
        in_specs=[
            pl.BlockSpec((FFN_TM, D), lambda i, j: (i, 0)),
            pl.BlockSpec((1, D, FFN_TN), lambda i, j: (layer, 0, j)),
            pl.BlockSpec((1, D, FFN_TN), lambda i, j: (layer, 0, FFN_TILES + j)),
            pl.BlockSpec((1, 3, FFN_TN), lambda i, j: (layer, 0, j)),
            pl.BlockSpec((1, 3, FFN_TN), lambda i, j: (layer, 0, FFN_TILES + j)),
            pl.BlockSpec((1, 1, FFN_TN), lambda i, j: (layer, 0, j)),
            pl.BlockSpec((1, 1, FFN_TN), lambda i, j: (layer, 0, FFN_TILES + j)),
        ],
        out_specs=pl.BlockSpec((FFN_TM, FFN_TN), lambda i, j: (i, j)),
        out_shape=jax.ShapeDtypeStruct((T_ALL, D_FF), BF16),
        scratch_shapes=[
            pltpu.VMEM((2, 2 * n_slab, FFN_EXT, 128), F32),
            pltpu.VMEM((2, n_slab, FFN_MC, 128), F32),
        ],
        compiler_params=_cparams(2),
        name="ffn_up",
    )(h, w_in, w_in, conv_w, conv_w, conv_b, conv_b)


ATT_NQ = AH * AHD
ATT_NKV = AKV * AHD
ATT_DUP = 2 * ATT_NKV
ATT_N = ATT_NQ + 2 * ATT_DUP
ATT_TN = ATT_NQ
assert 2 * ATT_DUP == ATT_TN


def _dup_heads(w):
    lead = w.shape[:-1]
    w = w.reshape(lead + (AKV, AHD))
    return jnp.concatenate([w, w], axis=-1).reshape(lead + (ATT_DUP,))


def _attn_proj_kernel(h_ref, w_ref, cos_ref, slo_ref, shi_ref, q_ref, kd_ref, vd_ref, k32_ref, v32_ref):
    i = pl.program_id(0)
    j = pl.program_id(1)
    is_lat = i * PROJ_TM >= T_CTX

    def chunks(epilogue):
        for m in range(PROJ_TM // PROJ_MC):
            rows = slice(m * PROJ_MC, (m + 1) * PROJ_MC)
            epilogue(jnp.dot(h_ref[rows, :], w_ref[...], preferred_element_type=F32), rows)

    def rope_into(o_ref, acc, rows, c0, n_cols):
        for s in range(n_cols // 128):
            y = _rope_slab(acc[:, c0 + s * 128:c0 + (s + 1) * 128],
                           cos_ref[rows, :], slo_ref[rows, :], shi_ref[rows, :], AHD // 4)
            o_ref[rows, s * 128:(s + 1) * 128] = y.astype(BF16)

    def q_lat(acc, rows):
        rope_into(q_ref, acc, rows, 0, ATT_NQ)

    def q_ctx(acc, rows):
        q_ref[rows, :] = acc.astype(BF16)

    def kv_lat(acc, rows):
        rope_into(kd_ref, acc, rows, 0, ATT_DUP)
        vd_ref[rows, :] = acc[:, ATT_DUP:].astype(BF16)

    def undup_into(o_ref, acc, rows, c0):
        low = lax.broadcasted_iota(jnp.int32, (PROJ_MC, 128), 1) < AHD
        for s in range(ATT_NKV // 128):
            a = acc[:, c0 + 2 * s * 128:c0 + (2 * s + 1) * 128]
            b = acc[:, c0 + (2 * s + 1) * 128:c0 + (2 * s + 2) * 128]
            o_ref[rows, s * 128:(s + 1) * 128] = jnp.where(low, a, b)

    def kv_ctx(acc, rows):
        kd_ref[rows, :] = acc[:, :ATT_DUP].astype(BF16)
        vd_ref[rows, :] = acc[:, ATT_DUP:].astype(BF16)
        undup_into(k32_ref, acc, rows, 0)
        undup_into(v32_ref, acc, rows, ATT_DUP)

    for on_q, on_lat, epilogue in ((True, True, q_lat), (True, False, q_ctx),
                                   (False, True, kv_lat), (False, False, kv_ctx)):
        @pl.when(jnp.logical_and((j == 0) == on_q, is_lat == on_lat))
        def _():
            chunks(epilogue)


def _attn_proj(h, w_all, tabs):
    cos, slo, shi = tabs
    n_pos = LAT_L // PROJ_TM
    n_ctx = T_CTX // PROJ_TM
    tab_spec = pl.BlockSpec((PROJ_TM, 128), lambda i, j: (i % n_pos, 0))
    row_spec = pl.BlockSpec((PROJ_TM, ATT_DUP), lambda i, j: (i, 0))
    ctx_spec = pl.BlockSpec((PROJ_TM, ATT_NKV), lambda i, j: (jnp.minimum(i, n_ctx - 1), 0))
    return pl.pallas_call(
        _attn_proj_kernel,
        grid=(T_ALL // PROJ_TM, ATT_N // ATT_TN),
        in_specs=[
            pl.BlockSpec((PROJ_TM, D), lambda i, j: (i, 0)),
            pl.BlockSpec((D, ATT_TN), lambda i, j: (0, j)),
            tab_spec, tab_spec, tab_spec,
        ],
        out_specs=[
            pl.BlockSpec((PROJ_TM, ATT_NQ), lambda i, j: (i, 0)),
            row_spec, row_spec, ctx_spec, ctx_spec,
        ],
        out_shape=[
            jax.ShapeDtypeStruct((T_ALL, ATT_NQ), BF16),
            jax.ShapeDtypeStruct((T_ALL, ATT_DUP), BF16),
            jax.ShapeDtypeStruct((T_ALL, ATT_DUP), BF16),
            jax.ShapeDtypeStruct((T_CTX, ATT_NKV), F32),
            jax.ShapeDtypeStruct((T_CTX, ATT_NKV), F32),
        ],
        compiler_params=_cparams(2),
        name="attn_proj",
    )(h, w_all, cos, slo, shi)


def _head_scores(q_ref, kh, k_dup):
    nq = q_ref.shape[0]
    nk = k_dup.shape[0]
    low = lax.broadcasted_iota(jnp.int32, (nk, 128), 1) < AHD
    zero = jnp.zeros_like(k_dup)
    k2 = jnp.concatenate([jnp.where(low, k_dup, zero), jnp.where(low, zero, k_dup)], axis=0)
    base = kh * AG * AHD
    lhs = jnp.concatenate([q_ref[:, base:base + 128], q_ref[:, base + 128:base + 256]], axis=0)
    s = lax.dot_general(lhs, k2, (((1,), (1,)), ((), ())), preferred_element_type=F32)
    return [s[0:nq, 0:nk], s[0:nq, nk:], s[nq:, 0:nk], s[nq:, nk:]]


def _row_max(s):
    m = s[:, 0:128]
    for c in range(128, s.shape[1], 128):
        m = jnp.maximum(m, s[:, c:c + 128])
    return jnp.max(m, axis=-1, keepdims=True)


def _softmax_pv_store(scores, v_dup, sink_ref, kh, o_ref):
    nq = scores[0].shape[0]
    nk = v_dup.shape[0]
    v_aug = jnp.concatenate([v_dup, jnp.ones((nk, 128), BF16)], axis=1)
    ps, tails = [], []
    for g, s in enumerate(scores):
        sink = sink_ref[kh * AG + g]
        m = jnp.maximum(_row_max(s), sink)
        ps.append(jnp.exp(s - m).astype(BF16))
        tails.append(jnp.exp(sink - m))
    pv = jnp.dot(jnp.concatenate(ps, axis=0), v_aug, preferred_element_type=F32)
    outs = []
    for g in range(AG):
        rows = slice(g * nq, (g + 1) * nq)
        outs.append(pv[rows, 0:128] / (pv[rows, 128:256] + tails[g]))
    low = lax.broadcasted_iota(jnp.int32, (nq, 128), 1) < AHD
    base = kh * AG * AHD
    o_ref[:, base:base + 128] = jnp.where(low, outs[0], outs[1]).astype(BF16)
    o_ref[:, base + 128:base + 256] = jnp.where(low, outs[2], outs[3]).astype(BF16)


def _attend_all(q_ref, sink_ref, o_ref, kv_of, bias_of):
    def scores_of(kh):
        k_dup, _ = kv_of(kh)
        return bias_of(_head_scores(q_ref, kh, k_dup))

    nxt = scores_of(0)
    for kh in range(AKV):
        cur = nxt
        if kh + 1 < AKV:
            nxt = scores_of(kh + 1)
        _softmax_pv_store(cur, kv_of(kh)[1], sink_ref, kh, o_ref)


def _ctx_attn_kernel(sink_ref, q_ref, kd_ref, vd_ref, o_ref):
    def kv_of(kh):
        cols = slice(kh * 128, (kh + 1) * 128)
        return kd_ref[:, cols], vd_ref[:, cols]

    _attend_all(q_ref, sink_ref, o_ref, kv_of, lambda scores: scores)


BAND = 3 * ABLK
ATT_CTX_STEPS = T_CTX // ABLK
ATT_LAT_QBLK = LAT_L // ABLK


def _lat_attn_kernel(sink_ref, q_ref, kd_ref, vd_ref, ck_ref, cv_ref, o_ref):
    n = (pl.program_id(0) - ATT_CTX_STEPS) % ATT_LAT_QBLK
    start = pl.multiple_of(jnp.clip((n - 1) * ABLK, 0, LAT_L - BAND), ABLK)
    qpos = n * ABLK + lax.broadcasted_iota(jnp.int32, (ABLK, BAND), 0)
    kpos = start + lax.broadcasted_iota(jnp.int32, (ABLK, BAND), 1)
    band_bias = jnp.where(jnp.abs(qpos - kpos) <= WINDOW, 0.0, NEG_INF)

    def kv_of(kh):
        cols = slice(kh * 128, (kh + 1) * 128)
        k = jnp.concatenate([kd_ref[pl.ds(start, BAND), cols], ck_ref[0, :, cols]], axis=0)
        v = jnp.concatenate([vd_ref[pl.ds(start, BAND), cols], cv_ref[0, :, cols]], axis=0)
        return k, v

    def bias_of(scores):
        return [jnp.concatenate([s[:, :BAND] + band_bias, s[:, BAND:]], axis=1) for s in scores]

    _attend_all(q_ref, sink_ref, o_ref, kv_of, bias_of)


def _attention(sink, q, kd, vd, cache_kd, cache_vd):
    def lat_seq(t):
        return jnp.maximum(t - ATT_CTX_STEPS, 0) // ATT_LAT_QBLK

    row_spec = pl.BlockSpec((ABLK, ATT_NQ), lambda t: (t, 0))
    ctx_spec = pl.BlockSpec((CTX_L, ATT_DUP), lambda t: (jnp.minimum(t // (CTX_L // ABLK), CTX_B - 1), 0))
    seq_spec = pl.BlockSpec((LAT_L, ATT_DUP), lambda t: (T_CTX // LAT_L + lat_seq(t), 0))
    cache_spec = pl.BlockSpec((1, CTX_L, ATT_DUP), lambda t: (lat_seq(t), 0, 0))

    def body(sink_ref, q_ref, kdc_ref, vdc_ref, kdl_ref, vdl_ref, ck_ref, cv_ref, o_ref):
        is_ctx = pl.program_id(0) < ATT_CTX_STEPS

        @pl.when(is_ctx)
        def _():
            _ctx_attn_kernel(sink_ref, q_ref, kdc_ref, vdc_ref, o_ref)

        @pl.when(jnp.logical_not(is_ctx))
        def _():
            _lat_attn_kernel(sink_ref, q_ref, kdl_ref, vdl_ref, ck_ref, cv_ref, o_ref)

    return pl.pallas_call(
        body,
        grid=(T_ALL // ABLK,),
        in_specs=[pl.BlockSpec(memory_space=pltpu.SMEM), row_spec,
                  ctx_spec, ctx_spec, seq_spec, seq_spec, cache_spec, cache_spec],
        out_specs=row_spec,
        out_shape=jax.ShapeDtypeStruct((T_ALL, ATT_NQ), BF16),
        compiler_params=_cparams(1),
        name="attention",
    )(sink, q, kd, vd, kd, vd, cache_kd, cache_vd)


def _log_sigmoid(x):
    return -(jnp.maximum(-x, 0.0) + jnp.log1p(jnp.exp(-jnp.abs(x))))


def kernel(x_prompt, x_sample, state_ret_fwd, state_ret_bwd, cache_attn_k, cache_attn_v, c, c_ctx, ada_w, ada_b, norm_mix, norm_ffn, ret_wq, ret_wk, ret_wv, ret_wg_fwd, ret_wg_bwd, ret_wo, ret_decay_fwd, ret_decay_bwd, attn_wq, attn_wk, attn_wv, attn_wo, attn_sink, ffn_w_in, ffn_conv_w, ffn_conv_b, ffn_w_out, norm_final):
    x_ctx = x_prompt.reshape(T_CTX, D)
    x_lat = x_sample.reshape(T_LAT, D)

    cv = jnp.concatenate([c, c_ctx[None, :], jnp.zeros((MOD_ROWS - LAT_B - 1, D), F32)], axis=0)
    mods = _adaln(cv, ada_w, ada_b).reshape(ada_w.shape[0], MOD_ROWS, N_MOD, D)

    h = _first_mod(x_ctx, x_lat, norm_mix[0:1], mods[0])
    w_ret = jnp.concatenate([ret_wq[0], ret_wk[0] * (RDK ** -0.5), ret_wv[0], ret_wg_fwd[0], ret_wg_bwd[0]],
                            axis=1).astype(BF16)
    qkvg = _ret_proj(h, w_ret, _rope_tables(RDK))
    lg = jnp.stack([_log_sigmoid(ret_decay_fwd[0].astype(F32)), _log_sigmoid(ret_decay_bwd[0].astype(F32))])
    y0, s_f, s_b = _ret_scan(qkvg, lg, state_ret_fwd, state_ret_bwd)
    w_in = ffn_w_in.astype(BF16)
    w_out = ffn_w_out.astype(BF16)
    conv_b = ffn_conv_b[:, None, :]
    x, h = _res_matmul(y0, ret_wo.astype(BF16), 0, (x_ctx, x_lat), mods[0], 2,
                       norm_ffn[0:1], mods[0], (3, 4))
    act = _ffn_up(h, w_in, ffn_conv_w, conv_b, 0)
    x, h = _res_matmul(act, w_out, 0, x, mods[0], 5, norm_mix[1:2], mods[1], (0, 1))

    w_att = jnp.concatenate([attn_wq[0] * (AHD ** -0.5), _dup_heads(attn_wk[0]), _dup_heads(attn_wv[0])],
                            axis=1).astype(BF16)
    q, kd, vd, k32, v32 = _attn_proj(h, w_att, _rope_tables(AHD))
    sink = attn_sink[0].astype(F32)
    cache_kd = _dup_heads(cache_attn_k[:, 0].reshape(LAT_B, CTX_L, ATT_NKV)).astype(BF16)
    cache_vd = _dup_heads(cache_attn_v[:, 0].reshape(LAT_B, CTX_L, ATT_NKV)).astype(BF16)
    a = _attention(sink, q, kd, vd, cache_kd, cache_vd)
    x, h = _res_matmul(a, attn_wo.astype(BF16), 0, x, mods[1], 2, norm_ffn[1:2], mods[1], (3, 4))
    act = _ffn_up(h, w_in, ffn_conv_w, conv_b, 1)
    y_ctx, y_lat = _res_matmul(act, w_out, 1, x, mods[1], 5, norm_final[None, :])

    y_prompt = y_ctx.reshape(CTX_B, CTX_L, D)
    y_sample = y_lat.reshape(LAT_B, LAT_L, D)
    new_k = k32.reshape(CTX_B, 1, CTX_L, AKV, AHD)
    new_v = v32.reshape(CTX_B, 1, CTX_L, AKV, AHD)
    return y_prompt, y_sample, s_f, s_b, new_k, new_v
```

```python
import functools

import jax
import jax.numpy as jnp
import numpy as np
from jax import lax
from jax.experimental import pallas as pl
from jax.experimental.pallas import tpu as pltpu

D = 1024
CTX_B, CTX_L = 32, 256
LAT_B, LAT_L = 8, 2048
T_CTX = CTX_B * CTX_L
T_LAT = LAT_B * LAT_L
T_ALL = T_CTX + T_LAT
GRID_W = 64
N_MOD = 6
MOD_ROWS = 16
CTX_MOD_ROW = LAT_B
RH, RDK, RDV = 4, 256, 512
RC = 256
AH, AKV, AG, AHD = 16, 4, 4, 64
WINDOW = 128
ABLK = 256
D_FF = 2816
ROPE_BASE = 10000.0
EPS = 1e-6
NEG_INF = -1e30

BF16 = jnp.bfloat16
F32 = jnp.float32

VMEM_LIMIT_BYTES = 56 * 1024 * 1024
NORM_ROWS = 32


def _cparams(n_axes):
    return pltpu.CompilerParams(dimension_semantics=("arbitrary",) * n_axes,
                                vmem_limit_bytes=VMEM_LIMIT_BYTES)


def _silu(x):
    return x / (1.0 + jnp.exp(-x))


def _mod_row(i, tm):
    r0 = i * tm
    return jnp.where(r0 < T_CTX, CTX_MOD_ROW, (r0 - T_CTX) // LAT_L)


def _modulate(x, g_ref, mod_ref, shift_k, scale_k):
    ms = jnp.mean(x * x, axis=-1, keepdims=True)
    y = x * lax.rsqrt(ms + EPS) * g_ref[...]
    return y * (1.0 + mod_ref[0, scale_k:scale_k + 1, :]) + mod_ref[0, shift_k:shift_k + 1, :]


MOD_TM = 1024
N_CTX_BLK = T_CTX // MOD_TM


def _ctx_blk(i):
    return jnp.minimum(i, N_CTX_BLK - 1)


def _lat_blk(i):
    return jnp.maximum(i - N_CTX_BLK, 0)


def _first_mod_kernel(xc_ref, xl_ref, g_ref, mod_ref, h_ref):
    i = pl.program_id(0)

    def run(x_ref):
        for r in range(0, MOD_TM, NORM_ROWS):
            rows = slice(r, r + NORM_ROWS)
            h_ref[rows, :] = _modulate(x_ref[rows, :], g_ref, mod_ref, 0, 1).astype(BF16)

    @pl.when(i < N_CTX_BLK)
    def _():
        run(xc_ref)

    @pl.when(i >= N_CTX_BLK)
    def _():
        run(xl_ref)


def _first_mod(x_ctx, x_lat, gain, mods_l):
    return pl.pallas_call(
        _first_mod_kernel,
        grid=(T_ALL // MOD_TM,),
        in_specs=[
            pl.BlockSpec((MOD_TM, D), lambda i: (_ctx_blk(i), 0)),
            pl.BlockSpec((MOD_TM, D), lambda i: (_lat_blk(i), 0)),
            pl.BlockSpec((1, D), lambda i: (0, 0)),
            pl.BlockSpec((1, N_MOD, D), lambda i: (_mod_row(i, MOD_TM), 0, 0)),
        ],
        out_specs=pl.BlockSpec((MOD_TM, D), lambda i: (i, 0)),
        out_shape=jax.ShapeDtypeStruct((T_ALL, D), BF16),
        compiler_params=_cparams(1),
        name="first_mod",
    )(x_ctx, x_lat, gain, mods_l)


ADA_TN = 512


def _adaln_kernel(cv_ref, w_ref, b_ref, o_ref):
    s = _silu(cv_ref[...]).astype(BF16)
    w = w_ref[0].astype(BF16)
    o_ref[0] = jnp.dot(s, w, preferred_element_type=F32) + b_ref[0]


def _adaln(cv, ada_w, ada_b):
    depth = ada_w.shape[0]
    n = N_MOD * D
    return pl.pallas_call(
        _adaln_kernel,
        grid=(depth, n // ADA_TN),
        in_specs=[
            pl.BlockSpec((MOD_ROWS, D), lambda l, j: (0, 0)),
            pl.BlockSpec((1, D, ADA_TN), lambda l, j: (l, 0, j)),
            pl.BlockSpec((1, 1, ADA_TN), lambda l, j: (l, 0, j)),
        ],
        out_specs=pl.BlockSpec((1, MOD_ROWS, ADA_TN), lambda l, j: (l, 0, j)),
        out_shape=jax.ShapeDtypeStruct((depth, MOD_ROWS, n), F32),
        compiler_params=_cparams(2),
        name="adaln",
    )(cv, ada_w, ada_b.reshape(depth, 1, n))


def _rope_tables(head_dim):
    half = head_dim // 2
    n_freq = half // 2
    inv = (np.float32(ROPE_BASE) ** (-np.arange(n_freq, dtype=np.float32) / np.float32(n_freq))).astype(np.float32)
    t = np.arange(LAT_L)
    rows = (t // GRID_W).astype(np.float32)[:, None] * inv[None, :]
    cols = (t % GRID_W).astype(np.float32)[:, None] * inv[None, :]
    zero = np.zeros_like(rows)
    cos = np.concatenate([np.cos(rows)] * 2 + [np.cos(cols)] * 2, axis=-1)
    sin_lo = np.concatenate([-np.sin(rows), zero, -np.sin(cols), zero], axis=-1)
    sin_hi = np.concatenate([zero, np.sin(rows), zero, np.sin(cols)], axis=-1)
    return tuple(np.tile(x.astype(np.float32), (1, max(1, 128 // head_dim))) for x in (cos, sin_lo, sin_hi))


def _rope_slab(x, cos, sin_lo, sin_hi, n_freq):
    up = pltpu.roll(x, 128 - n_freq, 1)
    dn = pltpu.roll(x, n_freq, 1)
    return x * cos + up * sin_lo + dn * sin_hi


PROJ_TM = 1024
PROJ_MC = 256
RET_NQK = 2 * RH * RDK
RET_N = RET_NQK + 3 * RH * RDV
RET_TN = RET_N // 2


def _ret_proj_kernel(h_ref, w_ref, cos_ref, slo_ref, shi_ref, o_ref):
    i = pl.program_id(0)
    j = pl.program_id(1)
    is_lat = i * PROJ_TM >= T_CTX

    def chunks(epilogue):
        for m in range(PROJ_TM // PROJ_MC):
            rows = slice(m * PROJ_MC, (m + 1) * PROJ_MC)
            acc = jnp.dot(h_ref[rows, :], w_ref[...], preferred_element_type=F32)
            epilogue(acc, rows)

    def plain(acc, rows):
        o_ref[rows, :] = acc.astype(BF16)

    def gate(acc, rows):
        o_ref[rows, :] = _silu(acc).astype(BF16)

    def rope_qk(acc, rows):
        for s in range(RET_NQK // 128):
            cols = slice(s * 128, (s + 1) * 128)
            tcols = slice((s % 2) * 128, (s % 2 + 1) * 128)
            y = _rope_slab(acc[:, cols], cos_ref[rows, tcols], slo_ref[rows, tcols], shi_ref[rows, tcols], RDK // 4)
            o_ref[rows, cols] = y.astype(BF16)
        o_ref[rows, RET_NQK:] = acc[:, RET_NQK:].astype(BF16)

    @pl.when(jnp.logical_and(j == 0, is_lat))
    def _():
        chunks(rope_qk)

    @pl.when(jnp.logical_and(j == 0, jnp.logical_not(is_lat)))
    def _():
        chunks(plain)

    @pl.when(j == 1)
    def _():
        chunks(gate)


def _ret_proj(h, w_all, tabs):
    cos, slo, shi = tabs
    n_pos = LAT_L // PROJ_TM
    tab_spec = pl.BlockSpec((PROJ_TM, RDK), lambda i, j: (i % n_pos, 0))
    return pl.pallas_call(
        _ret_proj_kernel,
        grid=(T_ALL // PROJ_TM, RET_N // RET_TN),
        in_specs=[
            pl.BlockSpec((PROJ_TM, D), lambda i, j: (i, 0)),
            pl.BlockSpec((D, RET_TN), lambda i, j: (0, j)),
            tab_spec, tab_spec, tab_spec,
        ],
        out_specs=pl.BlockSpec((PROJ_TM, RET_TN), lambda i, j: (i, j)),
        out_shape=jax.ShapeDtypeStruct((T_ALL, RET_N), BF16),
        compiler_params=_cparams(2),
        name="ret_proj",
    )(h, w_all, cos, slo, shi)


RET_BLOCK = 2048


def _ret_scan_kernel(n_seq, seq_len, has_init, emit_state, lg_ref, q_ref, k_ref, v_ref, gf_ref, gb_ref, *rest):
    rest = list(rest)
    s0f_ref = rest.pop(0) if has_init else None
    s0b_ref = rest.pop(0) if has_init else None
    y_ref = rest.pop(0)
    sf_ref = rest.pop(0) if emit_state else None
    sb_ref = rest.pop(0) if emit_state else None
    ybuf_ref, stf_ref, stb_ref = rest

    hd = pl.program_id(1)
    n_chunks = seq_len // RC
    idx_col = lax.broadcasted_iota(jnp.int32, (RC, 1), 0).astype(F32)
    idx_row = lax.broadcasted_iota(jnp.int32, (1, RC), 1).astype(F32)
    diff = idx_col - idx_row

    lg_f = lg_ref[0, hd]
    lg_b = lg_ref[1, hd]
    dirs = {
        "f": dict(dmat=jnp.where(diff >= 0, jnp.exp(lg_f * jnp.maximum(diff, 0.0)), 0.0),
                  xi=jnp.exp(lg_f * (idx_col + 1.0)), zeta=jnp.exp(lg_f * (RC - 1.0 - idx_row)),
                  gch=jnp.exp(jnp.ones((1, RDV), F32) * (lg_f * RC)),
                  st_ref=stf_ref, s0_ref=s0f_ref, gate_ref=gf_ref, out_ref=sf_ref),
        "b": dict(dmat=jnp.where(diff <= 0, jnp.exp(lg_b * jnp.maximum(-diff, 0.0)), 0.0),
                  xi=jnp.exp(lg_b * (RC - idx_col)), zeta=jnp.exp(lg_b * idx_row),
                  gch=jnp.exp(jnp.ones((1, RDV), F32) * (lg_b * RC)),
                  st_ref=stb_ref, s0_ref=s0b_ref, gate_ref=gb_ref, out_ref=sb_ref),
    }

    def shared(rows):
        q = q_ref[rows, :]
        k = k_ref[rows, :]
        s = lax.dot_general(q, k, (((1,), (1,)), ((), ())), preferred_element_type=F32)
        return q, s, k.astype(F32).T, v_ref[rows, :]

    def chunk_dir(d, sh, rows, first, last, sq):
        q, s, kt, v = sh
        sm = (s * d["dmat"]).astype(BF16)
        kzt = (kt * d["zeta"]).astype(BF16)
        both = jnp.dot(jnp.concatenate([sm, kzt], axis=0), v, preferred_element_type=F32)
        o, upd = both[:RC], both[RC:]
        st = None
        if has_init and first:
            st = d["s0_ref"][0, 0, 0]
        elif not first:
            st = d["st_ref"][...]
        if st is not None:
            o = o + jnp.dot(q, st.astype(BF16), preferred_element_type=F32) * d["xi"]
            upd = st * d["gch"] + upd
        if last and emit_state:
            d["out_ref"][sq, 0, 0] = upd
        elif not last:
            d["st_ref"][...] = upd
        o = o * lax.rsqrt(jnp.mean(o * o, axis=-1, keepdims=True) + EPS)
        return o * d["gate_ref"][rows, :].astype(F32)

    for sq in range(n_seq):
        half_done = set()

        def emit(c, val):
            rows = slice(sq * seq_len + c * RC, sq * seq_len + (c + 1) * RC)
            if c in half_done:
                y_ref[rows, :] = (ybuf_ref[rows, :] + val).astype(BF16)
            else:
                ybuf_ref[rows, :] = val
                half_done.add(c)

        for t in range(n_chunks):
            cf, cb = t, n_chunks - 1 - t
            rows_f = slice(sq * seq_len + cf * RC, sq * seq_len + (cf + 1) * RC)
            rows_b = slice(sq * seq_len + cb * RC, sq * seq_len + (cb + 1) * RC)
            sh_f = shared(rows_f)
            sh_b = sh_f if cb == cf else shared(rows_b)
            first, last = t == 0, t == n_chunks - 1
            of = chunk_dir(dirs["f"], sh_f, rows_f, first, last, sq)
            ob = chunk_dir(dirs["b"], sh_b, rows_b, first, last, sq)
            if cf == cb:
                y_ref[rows_f, :] = (of + ob).astype(BF16)
            else:
                emit(cf, of)
                emit(cb, ob)


RET_CTX_BLK = T_CTX // RET_BLOCK
RET_CTX_SEQS = RET_BLOCK // CTX_L


def _ret_scan(qkvg, lg, s0f, s0b):
    n_qk = RH
    ctx_last = RET_CTX_BLK - 1

    def lat_b(b):
        return jnp.maximum(b - RET_CTX_BLK, 0)

    s0_spec = pl.BlockSpec((1, 1, 1, RDK, RDV), lambda b, h: (lat_b(b), 0, h, 0, 0))
    so_spec = pl.BlockSpec((RET_CTX_SEQS, 1, 1, RDK, RDV),
                           lambda b, h: (jnp.minimum(b, ctx_last), 0, jnp.where(b <= ctx_last, h, RH - 1), 0, 0))
    so_shape = jax.ShapeDtypeStruct((CTX_B, 1, RH, RDK, RDV), F32)

    def body(lg_ref, q_ref, k_ref, v_ref, gf_ref, gb_ref, s0f_ref, s0b_ref, y_ref, sf_ref, sb_ref, *scratch):
        is_ctx = pl.program_id(0) < RET_CTX_BLK

        @pl.when(is_ctx)
        def _():
            _ret_scan_kernel(RET_CTX_SEQS, CTX_L, False, True, lg_ref, q_ref, k_ref, v_ref, gf_ref, gb_ref,
                             y_ref, sf_ref, sb_ref, *scratch)

        @pl.when(jnp.logical_not(is_ctx))
        def _():
            _ret_scan_kernel(1, LAT_L, True, False, lg_ref, q_ref, k_ref, v_ref, gf_ref, gb_ref,
                             s0f_ref, s0b_ref, y_ref, *scratch)

    return pl.pallas_call(
        body,
        grid=(T_ALL // RET_BLOCK, RH),
        in_specs=[
            pl.BlockSpec(memory_space=pltpu.SMEM),
            pl.BlockSpec((RET_BLOCK, RDK), lambda b, h: (b, h)),
            pl.BlockSpec((RET_BLOCK, RDK), lambda b, h: (b, n_qk + h)),
            pl.BlockSpec((RET_BLOCK, RDV), lambda b, h: (b, n_qk + h)),
            pl.BlockSpec((RET_BLOCK, RDV), lambda b, h: (b, n_qk + RH + h)),
            pl.BlockSpec((RET_BLOCK, RDV), lambda b, h: (b, n_qk + 2 * RH + h)),
            s0_spec, s0_spec,
        ],
        out_specs=[pl.BlockSpec((RET_BLOCK, RDV), lambda b, h: (b, h)), so_spec, so_spec],
        out_shape=[jax.ShapeDtypeStruct((T_ALL, RH * RDV), BF16), so_shape, so_shape],
        scratch_shapes=[pltpu.VMEM((RET_BLOCK, RDV), F32), pltpu.VMEM((RDK, RDV), F32),
                        pltpu.VMEM((RDK, RDV), F32)],
        compiler_params=_cparams(2),
        name="ret_scan",
    )(lg, qkvg, qkvg, qkvg, qkvg, qkvg, s0f, s0b)


RES_TM = MOD_TM
RES_MC = 256


def _res_kernel(gate_k, split_x, next_mod, a_ref, w_ref, *rest):
    rest = list(rest)
    x_refs = [rest.pop(0), rest.pop(0)] if split_x else [rest.pop(0)]
    mod_ref = rest.pop(0)
    gain_ref = rest.pop(0)
    if next_mod is not None:
        nmod_ref = rest.pop(0)
        xo_ref, h_ref = rest
    else:
        yc_ref, yl_ref = rest
    i = pl.program_id(0)
    is_ctx = i < N_CTX_BLK
    gate = mod_ref[0, gate_k:gate_k + 1, :]

    def run(x_ref, y_ref):
        for m in range(RES_TM // RES_MC):
            rows = slice(m * RES_MC, (m + 1) * RES_MC)
            acc = jnp.dot(a_ref[rows, :], w_ref[0], preferred_element_type=F32)
            xn = x_ref[rows, :] + gate * acc
            if next_mod is not None:
                xo_ref[rows, :] = xn
                h_ref[rows, :] = _modulate(xn, gain_ref, nmod_ref, *next_mod).astype(BF16)
            else:
                ms = jnp.mean(xn * xn, axis=-1, keepdims=True)
                y_ref[rows, :] = xn * lax.rsqrt(ms + EPS) * gain_ref[...]

    if not split_x and next_mod is not None:
        run(x_refs[0], None)
        return

    @pl.when(is_ctx)
    def _():
        run(x_refs[0], None if next_mod is not None else yc_ref)

    @pl.when(jnp.logical_not(is_ctx))
    def _():
        run(x_refs[-1], None if next_mod is not None else yl_ref)


def _res_matmul(a, w, layer, x, mods_l, gate_k, gain, next_mods=None, next_mod=None):
    kdim = a.shape[1]
    split_x = isinstance(x, tuple)
    row_spec = pl.BlockSpec((RES_TM, D), lambda i: (i, 0))
    ctx_spec = pl.BlockSpec((RES_TM, D), lambda i: (_ctx_blk(i), 0))
    lat_spec = pl.BlockSpec((RES_TM, D), lambda i: (_lat_blk(i), 0))
    mod_spec = pl.BlockSpec((1, N_MOD, D), lambda i: (_mod_row(i, RES_TM), 0, 0))
    in_specs = [pl.BlockSpec((RES_TM, kdim), lambda i: (i, 0)),
                pl.BlockSpec((1, kdim, D), lambda i: (layer, 0, 0))]
    args = [a, w]
    if split_x:
        in_specs += [ctx_spec, lat_spec]
        args += list(x)
    else:
        in_specs.append(row_spec)
        args.append(x)
    in_specs += [mod_spec, pl.BlockSpec((1, D), lambda i: (0, 0))]
    args += [mods_l, gain]
    if next_mod is not None:
        in_specs.append(mod_spec)
        args.append(next_mods)
        out_specs = [row_spec, row_spec]
        out_shape = [jax.ShapeDtypeStruct((T_ALL, D), F32), jax.ShapeDtypeStruct((T_ALL, D), BF16)]
    else:
        out_specs = [ctx_spec, lat_spec]
        out_shape = [jax.ShapeDtypeStruct((T_CTX, D), F32), jax.ShapeDtypeStruct((T_LAT, D), F32)]
    return pl.pallas_call(
        functools.partial(_res_kernel, gate_k, split_x, next_mod),
        grid=(T_ALL // RES_TM,),
        in_specs=in_specs,
        out_specs=out_specs,
        out_shape=out_shape,
        compiler_params=_cparams(1),
        name="res_matmul_k%d%s" % (kdim, "_final" if next_mod is None else ""),
    )(*args)


FFN_TM = 2048
FFN_MC = 512
FFN_TN = 256
FFN_HALO = 16
FFN_TILES = D_FF // FFN_TN
FFN_EXT = FFN_MC + (FFN_MC // CTX_L + 1) * FFN_HALO


def _ffn_up_kernel(h_ref, wu_ref, wg_ref, cwu_ref, cwg_ref, cbu_ref, cbg_ref, o_ref, a_scr, t_scr):
    is_ctx = pl.program_id(0) * FFN_TM < T_CTX
    zeros = jnp.zeros((FFN_HALO, D), BF16)
    n_slab = FFN_TN // 128

    def chunk(c, pieces, segments):
        slot = c % 2
        out_r0 = segments[0][1]
        out_n = sum(n_tok for _, _, n_tok in segments)
        hm = jnp.concatenate(pieces, axis=0)
        n_ext = hm.shape[0]
        for k, w_ref in enumerate((wu_ref, wg_ref)):
            a = jnp.dot(hm, w_ref[0], preferred_element_type=F32)
            for s in range(n_slab):
                a_scr[slot, k * n_slab + s, 0:n_ext, :] = a[:, s * 128:(s + 1) * 128]

        def conv(slab, base, half, cw_ref, cb_ref, lanes):
            def rows(off):
                return a_scr[slot, slab, pl.ds(base + off, half, stride=2), :]
            w0, w1, w2 = cw_ref[0, 0:1, lanes], cw_ref[0, 1:2, lanes], cw_ref[0, 2:3, lanes]
            b = cb_ref[0, :, lanes]
            om1, ev, od, ep1 = rows(-1), rows(0), rows(1), rows(2)
            return om1 * w0 + ev * w1 + od * w2 + b, ev * w0 + od * w1 + ep1 * w2 + b

        for s in range(n_slab):
            lanes = slice(s * 128, (s + 1) * 128)
            for base, r0, n_tok in segments:
                half = n_tok // 2
                t0 = r0 - out_r0
                u_ev, u_od = conv(s, base, half, cwu_ref, cbu_ref, lanes)
                g_ev, g_od = conv(n_slab + s, base, half, cwg_ref, cbg_ref, lanes)
                t_scr[slot, s, pl.ds(t0, half, stride=2), :] = _silu(g_ev) * u_ev
                t_scr[slot, s, pl.ds(t0 + 1, half, stride=2), :] = _silu(g_od) * u_od
            o_ref[out_r0:out_r0 + out_n, lanes] = t_scr[slot, s, 0:out_n, :].astype(BF16)

    sizes = [FFN_MC] * (FFN_TM // FFN_MC)
    starts = [sum(sizes[:c]) for c in range(len(sizes))]

    @pl.when(is_ctx)
    def _():
        for c, (c0, mc) in enumerate(zip(starts, sizes)):
            pieces, segments = [zeros], []
            for q in range(mc // CTX_L):
                r0 = c0 + q * CTX_L
                segments.append((FFN_HALO + q * (CTX_L + FFN_HALO), r0, CTX_L))
                pieces += [h_ref[r0:r0 + CTX_L, :], zeros]
            chunk(c, pieces, segments)

    @pl.when(jnp.logical_not(is_ctx))
    def _():
        for c, (r0, mc) in enumerate(zip(starts, sizes)):
            lo = zeros if r0 % LAT_L == 0 else h_ref[r0 - FFN_HALO:r0, :]
            hi = zeros if (r0 + mc) % LAT_L == 0 else h_ref[r0 + mc:r0 + mc + FFN_HALO, :]
            chunk(c, [lo, h_ref[r0:r0 + mc, :], hi], [(FFN_HALO, r0, mc)])


def _ffn_up(h, w_in, conv_w, conv_b, layer):
    n_slab = FFN_TN // 128
    return pl.pallas_call(
        _ffn_up_kernel,
        grid=(T_ALL // FFN_TM, FFN_TILES),
        in_specs=[
            pl.BlockSpec((FFN_TM, D), lambda i, j: (i, 0)),
            pl.BlockSpec((1, D, FFN_TN), lambda i, j: (layer, 0, j)),
            pl.BlockSpec((1, D, FFN_TN), lambda i, j: (layer, 0, FFN_TILES + j)),
            pl.BlockSpec((1, 3, FFN_TN), lambda i, j: (layer, 0, j)),
            pl.BlockSpec((1, 3, FFN_TN), lambda i, j: (layer, 0, FFN_TILES + j)),
            pl.BlockSpec((1, 1, FFN_TN), lambda i, j: (layer, 0, j)),
            pl.BlockSpec((1, 1, FFN_TN), lambda i, j: (layer, 0, FFN_TILES + j)),
        ],
        out_specs=pl.BlockSpec((FFN_TM, FFN_TN), lambda i, j: (i, j)),
        out_shape=jax.ShapeDtypeStruct((T_ALL, D_FF), BF16),
        scratch_shapes=[
            pltpu.VMEM((2, 2 * n_slab, FFN_EXT, 128), F32),
            pltpu.VMEM((2, n_slab, FFN_MC, 128), F32),
        ],
        compiler_params=_cparams(2),
        name="ffn_up",
    )(h, w_in, w_in, conv_w, conv_w, conv_b, conv_b)


ATT_NQ = AH * AHD
ATT_NKV = AKV * AHD
ATT_DUP = 2 * ATT_NKV
ATT_N = ATT_NQ + 2 * ATT_DUP
ATT_TN = ATT_NQ
assert 2 * ATT_DUP == ATT_TN


def _dup_heads(w):
    lead = w.shape[:-1]
    w = w.reshape(lead + (AKV, AHD))
    return jnp.concatenate([w, w], axis=-1).reshape(lead + (ATT_DUP,))


def _attn_proj_kernel(h_ref, w_ref, cos_ref, slo_ref, shi_ref, q_ref, kd_ref, vd_ref, k32_ref, v32_ref):
    i = pl.program_id(0)
    j = pl.program_id(1)
    is_lat = i * PROJ_TM >= T_CTX

    def chunks(epilogue):
        for m in range(PROJ_TM // PROJ_MC):
            rows = slice(m * PROJ_MC, (m + 1) * PROJ_MC)
            epilogue(jnp.dot(h_ref[rows, :], w_ref[...], preferred_element_type=F32), rows)

    def rope_into(o_ref, acc, rows, c0, n_cols):
        for s in range(n_cols // 128):
            y = _rope_slab(acc[:, c0 + s * 128:c0 + (s + 1) * 128],
                           cos_ref[rows, :], slo_ref[rows, :], shi_ref[rows, :], AHD // 4)
            o_ref[rows, s * 128:(s + 1) * 128] = y.astype(BF16)

    def q_lat(acc, rows):
        rope_into(q_ref, acc, rows, 0, ATT_NQ)

    def q_ctx(acc, rows):
        q_ref[rows, :] = acc.astype(BF16)

    def kv_lat(acc, rows):
        rope_into(kd_ref, acc, rows, 0, ATT_DUP)
        vd_ref[rows, :] = acc[:, ATT_DUP:].astype(BF16)

    def undup_into(o_ref, acc, rows, c0):
        low = lax.broadcasted_iota(jnp.int32, (PROJ_MC, 128), 1) < AHD
        for s in range(ATT_NKV // 128):
            a = acc[:, c0 + 2 * s * 128:c0 + (2 * s + 1) * 128]
            b = acc[:, c0 + (2 * s + 1) * 128:c0 + (2 * s + 2) * 128]
            o_ref[rows, s * 128:(s + 1) * 128] = jnp.where(low, a, b)

    def kv_ctx(acc, rows):
        kd_ref[rows, :] = acc[:, :ATT_DUP].astype(BF16)
        vd_ref[rows, :] = acc[:, ATT_DUP:].astype(BF16)
        undup_into(k32_ref, acc, rows, 0)
        undup_into(v32_ref, acc, rows, ATT_DUP)

    for on_q, on_lat, epilogue in ((True, True, q_lat), (True, False, q_ctx),
                                   (False, True, kv_lat), (False, False, kv_ctx)):
        @pl.when(jnp.logical_and((j == 0) == on_q, is_lat == on_lat))
        def _():
            chunks(epilogue)


def _attn_proj(h, w_all, tabs):
    cos, slo, shi = tabs
    n_pos = LAT_L // PROJ_TM
    n_ctx = T_CTX // PROJ_TM
    tab_spec = pl.BlockSpec((PROJ_TM, 128), lambda i, j: (i % n_pos, 0))
    row_spec = pl.BlockSpec((PROJ_TM, ATT_DUP), lambda i, j: (i, 0))
    ctx_spec = pl.BlockSpec((PROJ_TM, ATT_NKV), lambda i, j: (jnp.minimum(i, n_ctx - 1), 0))
    return pl.pallas_call(
        _attn_proj_kernel,
        grid=(T_ALL // PROJ_TM, ATT_N // ATT_TN),
        in_specs=[
            pl.BlockSpec((PROJ_TM, D), lambda i, j: (i, 0)),
            pl.BlockSpec((D, ATT_TN), lambda i, j: (0, j)),
            tab_spec, tab_spec, tab_spec,
        ],
        out_specs=[
            pl.BlockSpec((PROJ_TM, ATT_NQ), lambda i, j: (i, 0)),
            row_spec, row_spec, ctx_spec, ctx_spec,
        ],
        out_shape=[
            jax.ShapeDtypeStruct((T_ALL, ATT_NQ), BF16),
            jax.ShapeDtypeStruct((T_ALL, ATT_DUP), BF16),
            jax.ShapeDtypeStruct((T_ALL, ATT_DUP), BF16),
            jax.ShapeDtypeStruct((T_CTX, ATT_NKV), F32),
            jax.ShapeDtypeStruct((T_CTX, ATT_NKV), F32),
        ],
        compiler_params=_cparams(2),
        name="attn_proj",
    )(h, w_all, cos, slo, shi)


def _head_scores(q_ref, kh, k_dup):
    nq = q_ref.shape[0]
    nk = k_dup.shape[0]
    low = lax.broadcasted_iota(jnp.int32, (nk, 128), 1) < AHD
    zero = jnp.zeros_like(k_dup)
    k2 = jnp.concatenate([jnp.where(low, k_dup, zero), jnp.where(low, zero, k_dup)], axis=0)
    base = kh * AG * AHD
    lhs = jnp.concatenate([q_ref[:, base:base + 128], q_ref[:, base + 128:base + 256]], axis=0)
    s = lax.dot_general(lhs, k2, (((1,), (1,)), ((), ())), preferred_element_type=F32)
    return [s[0:nq, 0:nk], s[0:nq, nk:], s[nq:, 0:nk], s[nq:, nk:]]


def _row_max(s):
    m = s[:, 0:128]
    for c in range(128, s.shape[1], 128):
        m = jnp.maximum(m, s[:, c:c + 128])
    return jnp.max(m, axis=-1, keepdims=True)


def _softmax_pv_store(scores, v_dup, sink_ref, kh, o_ref):
    nq = scores[0].shape[0]
    nk = v_dup.shape[0]
    v_aug = jnp.concatenate([v_dup, jnp.ones((nk, 128), BF16)], axis=1)
    ps, tails = [], []
    for g, s in enumerate(scores):
        sink = sink_ref[kh * AG + g]
        m = jnp.maximum(_row_max(s), sink)
        ps.append(jnp.exp(s - m).astype(BF16))
        tails.append(jnp.exp(sink - m))
    pv = jnp.dot(jnp.concatenate(ps, axis=0), v_aug, preferred_element_type=F32)
    outs = []
    for g in range(AG):
        rows = slice(g * nq, (g + 1) * nq)
        outs.append(pv[rows, 0:128] / (pv[rows, 128:256] + tails[g]))
    low = lax.broadcasted_iota(jnp.int32, (nq, 128), 1) < AHD
    base = kh * AG * AHD
    o_ref[:, base:base + 128] = jnp.where(low, outs[0], outs[1]).astype(BF16)
    o_ref[:, base + 128:base + 256] = jnp.where(low, outs[2], outs[3]).astype(BF16)


def _attend_all(q_ref, sink_ref, o_ref, kv_of, bias_of):
    def scores_of(kh):
        k_dup, _ = kv_of(kh)
        return bias_of(_head_scores(q_ref, kh, k_dup))

    nxt = scores_of(0)
    for kh in range(AKV):
        cur = nxt
        if kh + 1 < AKV:
            nxt = scores_of(kh + 1)
        _softmax_pv_store(cur, kv_of(kh)[1], sink_ref, kh, o_ref)


def _ctx_attn_kernel(sink_ref, q_ref, kd_ref, vd_ref, o_ref):
    def kv_of(kh):
        cols = slice(kh * 128, (kh + 1) * 128)
        return kd_ref[:, cols], vd_ref[:, cols]

    _attend_all(q_ref, sink_ref, o_ref, kv_of, lambda scores: scores)


BAND = ABLK + 2 * WINDOW
ATT_CTX_STEPS = T_CTX // ABLK
ATT_LAT_QBLK = LAT_L // ABLK


def _lat_attn_kernel(sink_ref, q_ref, kd_ref, vd_ref, ck_ref, cv_ref, o_ref):
    n = (pl.program_id(0) - ATT_CTX_STEPS) % ATT_LAT_QBLK
    start = pl.multiple_of(jnp.clip(n * ABLK - WINDOW, 0, LAT_L - BAND), WINDOW)
    qpos = n * ABLK + lax.broadcasted_iota(jnp.int32, (ABLK, BAND), 0)
    kpos = start + lax.broadcasted_iota(jnp.int32, (ABLK, BAND), 1)
    band_bias = jnp.where(jnp.abs(qpos - kpos) <= WINDOW, 0.0, NEG_INF)

    def kv_of(kh):
        cols = slice(kh * 128, (kh + 1) * 128)
        k = jnp.concatenate([kd_ref[pl.ds(start, BAND), cols], ck_ref[0, :, cols]], axis=0)
        v = jnp.concatenate([vd_ref[pl.ds(start, BAND), cols], cv_ref[0, :, cols]], axis=0)
        return k, v

    def bias_of(scores):
        return [jnp.concatenate([s[:, :BAND] + band_bias, s[:, BAND:]], axis=1) for s in scores]

    _attend_all(q_ref, sink_ref, o_ref, kv_of, bias_of)


def _attention(sink, q, kd, vd, cache_kd, cache_vd):
    def lat_seq(t):
        return jnp.maximum(t - ATT_CTX_STEPS, 0) // ATT_LAT_QBLK

    row_spec = pl.BlockSpec((ABLK, ATT_NQ), lambda t: (t, 0))
    ctx_spec = pl.BlockSpec((CTX_L, ATT_DUP), lambda t: (jnp.minimum(t // (CTX_L // ABLK), CTX_B - 1), 0))
    seq_spec = pl.BlockSpec((LAT_L, ATT_DUP), lambda t: (T_CTX // LAT_L + lat_seq(t), 0))
    cache_spec = pl.BlockSpec((1, CTX_L, ATT_DUP), lambda t: (lat_seq(t), 0, 0))

    def body(sink_ref, q_ref, kdc_ref, vdc_ref, kdl_ref, vdl_ref, ck_ref, cv_ref, o_ref):
        is_ctx = pl.program_id(0) < ATT_CTX_STEPS

        @pl.when(is_ctx)
        def _():
            _ctx_attn_kernel(sink_ref, q_ref, kdc_ref, vdc_ref, o_ref)

        @pl.when(jnp.logical_not(is_ctx))
        def _():
            _lat_attn_kernel(sink_ref, q_ref, kdl_ref, vdl_ref, ck_ref, cv_ref, o_ref)

    return pl.pallas_call(
        body,
        grid=(T_ALL // ABLK,),
        in_specs=[pl.BlockSpec(memory_space=pltpu.SMEM), row_spec,
                  ctx_spec, ctx_spec, seq_spec, seq_spec, cache_spec, cache_spec],
        out_specs=row_spec,
        out_shape=jax.ShapeDtypeStruct((T_ALL, ATT_NQ), BF16),
        compiler_params=_cparams(1),
        name="attention",
    )(sink, q, kd, vd, kd, vd, cache_kd, cache_vd)


def _log_sigmoid(x):
    return -(jnp.maximum(-x, 0.0) + jnp.log1p(jnp.exp(-jnp.abs(x))))


def kernel(x_prompt, x_sample, state_ret_fwd, state_ret_bwd, cache_attn_k, cache_attn_v, c, c_ctx, ada_w, ada_b, norm_mix, norm_ffn, ret_wq, ret_wk, ret_wv, ret_wg_fwd, ret_wg_bwd, ret_wo, ret_decay_fwd, ret_decay_bwd, attn_wq, attn_wk, attn_wv, attn_wo, attn_sink, ffn_w_in, ffn_conv_w, ffn_conv_b, ffn_w_out, norm_final):
    x_ctx = x_prompt.reshape(T_CTX, D)
    x_lat = x_sample.reshape(T_LAT, D)

    cv = jnp.concatenate([c, c_ctx[None, :], jnp.zeros((MOD_ROWS - LAT_B - 1, D), F32)], axis=0)
    mods = _adaln(cv, ada_w, ada_b).reshape(ada_w.shape[0], MOD_ROWS, N_MOD, D)

    h = _first_mod(x_ctx, x_lat, norm_mix[0:1], mods[0])
    w_ret = jnp.concatenate([ret_wq[0], ret_wk[0] * (RDK ** -0.5), ret_wv[0], ret_wg_fwd[0], ret_wg_bwd[0]],
                            axis=1).astype(BF16)
    qkvg = _ret_proj(h, w_ret, _rope_tables(RDK))
    lg = jnp.stack([_log_sigmoid(ret_decay_fwd[0].astype(F32)), _log_sigmoid(ret_decay_bwd[0].astype(F32))])
    y0, s_f, s_b = _ret_scan(qkvg, lg, state_ret_fwd, state_ret_bwd)
    w_in = ffn_w_in.astype(BF16)
    w_out = ffn_w_out.astype(BF16)
    conv_b = ffn_conv_b[:, None, :]
    x, h = _res_matmul(y0, ret_wo.astype(BF16), 0, (x_ctx, x_lat), mods[0], 2,
                       norm_ffn[0:1], mods[0], (3, 4))
    act = _ffn_up(h, w_in, ffn_conv_w, conv_b, 0)
    x, h = _res_matmul(act, w_out, 0, x, mods[0], 5, norm_mix[1:2], mods[1], (0, 1))

    w_att = jnp.concatenate([attn_wq[0] * (AHD ** -0.5), _dup_heads(attn_wk[0]), _dup_heads(attn_wv[0])],
                            axis=1).astype(BF16)
    q, kd, vd, k32, v32 = _attn_proj(h, w_att, _rope_tables(AHD))
    sink = attn_sink[0].astype(F32)
    cache_kd = _dup_heads(cache_attn_k[:, 0].reshape(LAT_B, CTX_L, ATT_NKV)).astype(BF16)
    cache_vd = _dup_heads(cache_attn_v[:, 0].reshape(LAT_B, CTX_L, ATT_NKV)).astype(BF16)
    a = _attention(sink, q, kd, vd, cache_kd, cache_vd)
    x, h = _res_matmul(a, attn_wo.astype(BF16), 0, x, mods[1], 2, norm_ffn[1:2], mods[1], (3, 4))
    act = _ffn_up(h, w_in, ffn_conv_w, conv_b, 1)
    y_ctx, y_lat = _res_matmul(act, w_out, 1, x, mods[1], 5, norm_final[None, :])

    y_prompt = y_ctx.reshape(CTX_B, CTX_L, D)
    y_sample = y_lat.reshape(LAT_B, LAT_L, D)
    new_k = k32.reshape(CTX_B, 1, CTX_L, AKV, AHD)
    new_v = v32.reshape(CTX_B, 1, CTX_L, AKV, AHD)
    return y_prompt, y_sample, s_f, s_b, new_k, new_v
```

```python
import functools

import jax
import jax.numpy as jnp
import numpy as np
from jax import lax
from jax.experimental import pallas as pl
from jax.experimental.pallas import tpu as pltpu

D = 1024
CTX_B, CTX_L = 32, 256
LAT_B, LAT_L = 8, 2048
T_CTX = CTX_B * CTX_L
T_LAT = LAT_B * LAT_L
T_ALL = T_CTX + T_LAT
GRID_W = 64
N_MOD = 6
MOD_ROWS = 16
CTX_MOD_ROW = LAT_B
RH, RDK, RDV = 4, 256, 512
RC = 256
AH, AKV, AG, AHD = 16, 4, 4, 64
WINDOW = 128
ABLK = 256
D_FF = 2816
ROPE_BASE = 10000.0
EPS = 1e-6
NEG_INF = -1e30

BF16 = jnp.bfloat16
F32 = jnp.float32

VMEM_LIMIT_BYTES = 56 * 1024 * 1024
NORM_ROWS = 32


def _cparams(n_axes):
    return pltpu.CompilerParams(dimension_semantics=("arbitrary",) * n_axes,
                                vmem_limit_bytes=VMEM_LIMIT_BYTES)


def _silu(x):
    return x / (1.0 + jnp.exp(-x))


def _mod_row(i, tm):
    r0 = i * tm
    return jnp.where(r0 < T_CTX, CTX_MOD_ROW, (r0 - T_CTX) // LAT_L)


def _modulate(x, g_ref, mod_ref, shift_k, scale_k):
    ms = jnp.mean(x * x, axis=-1, keepdims=True)
    y = x * lax.rsqrt(ms + EPS) * g_ref[...]
    return y * (1.0 + mod_ref[0, scale_k:scale_k + 1, :]) + mod_ref[0, shift_k:shift_k + 1, :]


MOD_TM = 1024
N_CTX_BLK = T_CTX // MOD_TM


def _ctx_blk(i):
    return jnp.minimum(i, N_CTX_BLK - 1)


def _lat_blk(i):
    return jnp.maximum(i - N_CTX_BLK, 0)


def _first_mod_kernel(xc_ref, xl_ref, g_ref, mod_ref, h_ref):
    i = pl.program_id(0)

    def run(x_ref):
        for r in range(0, MOD_TM, NORM_ROWS):
            rows = slice(r, r + NORM_ROWS)
            h_ref[rows, :] = _modulate(x_ref[rows, :], g_ref, mod_ref, 0, 1).astype(BF16)

    @pl.when(i < N_CTX_BLK)
    def _():
        run(xc_ref)

    @pl.when(i >= N_CTX_BLK)
    def _():
        run(xl_ref)


def _first_mod(x_ctx, x_lat, gain, mods_l):
    return pl.pallas_call(
        _first_mod_kernel,
        grid=(T_ALL // MOD_TM,),
        in_specs=[
            pl.BlockSpec((MOD_TM, D), lambda i: (_ctx_blk(i), 0)),
            pl.BlockSpec((MOD_TM, D), lambda i: (_lat_blk(i), 0)),
            pl.BlockSpec((1, D), lambda i: (0, 0)),
            pl.BlockSpec((1, N_MOD, D), lambda i: (_mod_row(i, MOD_TM), 0, 0)),
        ],
        out_specs=pl.BlockSpec((MOD_TM, D), lambda i: (i, 0)),
        out_shape=jax.ShapeDtypeStruct((T_ALL, D), BF16),
        compiler_params=_cparams(1),
        name="first_mod",
    )(x_ctx, x_lat, gain, mods_l)


ADA_TN = 1024


def _adaln_kernel(cv_ref, w_ref, b_ref, o_ref):
    s = _silu(cv_ref[...]).astype(BF16)
    w = w_ref[0].astype(BF16)
    o_ref[0] = jnp.dot(s, w, preferred_element_type=F32) + b_ref[0]


def _adaln(cv, ada_w, ada_b):
    depth = ada_w.shape[0]
    n = N_MOD * D
    return pl.pallas_call(
        _adaln_kernel,
        grid=(depth, n // ADA_TN),
        in_specs=[
            pl.BlockSpec((MOD_ROWS, D), lambda l, j: (0, 0)),
            pl.BlockSpec((1, D, ADA_TN), lambda l, j: (l, 0, j)),
            pl.BlockSpec((1, 1, ADA_TN), lambda l, j: (l, 0, j)),
        ],
        out_specs=pl.BlockSpec((1, MOD_ROWS, ADA_TN), lambda l, j: (l, 0, j)),
        out_shape=jax.ShapeDtypeStruct((depth, MOD_ROWS, n), F32),
        compiler_params=_cparams(2),
        name="adaln",
    )(cv, ada_w, ada_b.reshape(depth, 1, n))


def _rope_tables(head_dim):
    half = head_dim // 2
    n_freq = half // 2
    inv = (np.float32(ROPE_BASE) ** (-np.arange(n_freq, dtype=np.float32) / np.float32(n_freq))).astype(np.float32)
    t = np.arange(LAT_L)
    rows = (t // GRID_W).astype(np.float32)[:, None] * inv[None, :]
    cols = (t % GRID_W).astype(np.float32)[:, None] * inv[None, :]
    zero = np.zeros_like(rows)
    cos = np.concatenate([np.cos(rows)] * 2 + [np.cos(cols)] * 2, axis=-1)
    sin_lo = np.concatenate([-np.sin(rows), zero, -np.sin(cols), zero], axis=-1)
    sin_hi = np.concatenate([zero, np.sin(rows), zero, np.sin(cols)], axis=-1)
    return tuple(np.tile(x.astype(np.float32), (1, max(1, 128 // head_dim))) for x in (cos, sin_lo, sin_hi))


def _rope_slab(x, cos, sin_lo, sin_hi, n_freq):
    up = pltpu.roll(x, 128 - n_freq, 1)
    dn = pltpu.roll(x, n_freq, 1)
    return x * cos + up * sin_lo + dn * sin_hi


PROJ_TM = 1024
PROJ_MC = 256
RET_NQK = 2 * RH * RDK
RET_N = RET_NQK + 3 * RH * RDV
RET_TN = RET_N // 2


def _ret_proj_kernel(h_ref, w_ref, cos_ref, slo_ref, shi_ref, o_ref):
    i = pl.program_id(0)
    j = pl.program_id(1)
    is_lat = i * PROJ_TM >= T_CTX

    def chunks(epilogue):
        for m in range(PROJ_TM // PROJ_MC):
            rows = slice(m * PROJ_MC, (m + 1) * PROJ_MC)
            acc = jnp.dot(h_ref[rows, :], w_ref[...], preferred_element_type=F32)
            epilogue(acc, rows)

    def plain(acc, rows):
        o_ref[rows, :] = acc.astype(BF16)

    def gate(acc, rows):
        o_ref[rows, :] = _silu(acc).astype(BF16)

    def rope_qk(acc, rows):
        for s in range(RET_NQK // 128):
            cols = slice(s * 128, (s + 1) * 128)
            tcols = slice((s % 2) * 128, (s % 2 + 1) * 128)
            y = _rope_slab(acc[:, cols], cos_ref[rows, tcols], slo_ref[rows, tcols], shi_ref[rows, tcols], RDK // 4)
            o_ref[rows, cols] = y.astype(BF16)
        o_ref[rows, RET_NQK:] = acc[:, RET_NQK:].astype(BF16)

    @pl.when(jnp.logical_and(j == 0, is_lat))
    def _():
        chunks(rope_qk)

    @pl.when(jnp.logical_and(j == 0, jnp.logical_not(is_lat)))
    def _():
        chunks(plain)

    @pl.when(j == 1)
    def _():
        chunks(gate)


def _ret_proj(h, w_all, tabs):
    cos, slo, shi = tabs
    n_pos = LAT_L // PROJ_TM
    tab_spec = pl.BlockSpec((PROJ_TM, RDK), lambda i, j: (i % n_pos, 0))
    return pl.pallas_call(
        _ret_proj_kernel,
        grid=(T_ALL // PROJ_TM, RET_N // RET_TN),
        in_specs=[
            pl.BlockSpec((PROJ_TM, D), lambda i, j: (i, 0)),
            pl.BlockSpec((D, RET_TN), lambda i, j: (0, j)),
            tab_spec, tab_spec, tab_spec,
        ],
        out_specs=pl.BlockSpec((PROJ_TM, RET_TN), lambda i, j: (i, j)),
        out_shape=jax.ShapeDtypeStruct((T_ALL, RET_N), BF16),
        compiler_params=_cparams(2),
        name="ret_proj",
    )(h, w_all, cos, slo, shi)


RET_BLOCK = 2048


def _ret_scan_kernel(n_seq, seq_len, has_init, emit_state, lg_ref, q_ref, k_ref, v_ref, gf_ref, gb_ref, *rest):
    rest = list(rest)
    s0f_ref = rest.pop(0) if has_init else None
    s0b_ref = rest.pop(0) if has_init else None
    y_ref = rest.pop(0)
    sf_ref = rest.pop(0) if emit_state else None
    sb_ref = rest.pop(0) if emit_state else None
    ybuf_ref, stf_ref, stb_ref = rest

    hd = pl.program_id(1)
    n_chunks = seq_len // RC
    idx_col = lax.broadcasted_iota(jnp.int32, (RC, 1), 0).astype(F32)
    idx_row = lax.broadcasted_iota(jnp.int32, (1, RC), 1).astype(F32)
    diff = idx_col - idx_row

    lg_f = lg_ref[0, hd]
    lg_b = lg_ref[1, hd]
    dirs = {
        "f": dict(dmat=jnp.where(diff >= 0, jnp.exp(lg_f * jnp.maximum(diff, 0.0)), 0.0),
                  xi=jnp.exp(lg_f * (idx_col + 1.0)), zeta=jnp.exp(lg_f * (RC - 1.0 - idx_row)),
                  gch=jnp.exp(jnp.ones((1, RDV), F32) * (lg_f * RC)),
                  st_ref=stf_ref, s0_ref=s0f_ref, gate_ref=gf_ref, out_ref=sf_ref),
        "b": dict(dmat=jnp.where(diff <= 0, jnp.exp(lg_b * jnp.maximum(-diff, 0.0)), 0.0),
                  xi=jnp.exp(lg_b * (RC - idx_col)), zeta=jnp.exp(lg_b * idx_row),
                  gch=jnp.exp(jnp.ones((1, RDV), F32) * (lg_b * RC)),
                  st_ref=stb_ref, s0_ref=s0b_ref, gate_ref=gb_ref, out_ref=sb_ref),
    }

    def shared(rows):
        q = q_ref[rows, :]
        k = k_ref[rows, :]
        s = lax.dot_general(q, k, (((1,), (1,)), ((), ())), preferred_element_type=F32)
        return q, s, k.astype(F32).T, v_ref[rows, :]

    def chunk_dir(d, sh, rows, first, last, sq):
        q, s, kt, v = sh
        sm = (s * d["dmat"]).astype(BF16)
        kzt = (kt * d["zeta"]).astype(BF16)
        both = jnp.dot(jnp.concatenate([sm, kzt], axis=0), v, preferred_element_type=F32)
        o, upd = both[:RC], both[RC:]
        st = None
        if has_init and first:
            st = d["s0_ref"][0, 0, 0]
        elif not first:
            st = d["st_ref"][...]
        if st is not None:
            o = o + jnp.dot(q, st.astype(BF16), preferred_element_type=F32) * d["xi"]
            upd = st * d["gch"] + upd
        if last and emit_state:
            d["out_ref"][sq, 0, 0] = upd
        elif not last:
            d["st_ref"][...] = upd
        o = o * lax.rsqrt(jnp.mean(o * o, axis=-1, keepdims=True) + EPS)
        return o * d["gate_ref"][rows, :].astype(F32)

    for sq in range(n_seq):
        half_done = set()

        def emit(c, val):
            rows = slice(sq * seq_len + c * RC, sq * seq_len + (c + 1) * RC)
            if c in half_done:
                y_ref[rows, :] = (ybuf_ref[rows, :] + val).astype(BF16)
            else:
                ybuf_ref[rows, :] = val
                half_done.add(c)

        for t in range(n_chunks):
            cf, cb = t, n_chunks - 1 - t
            rows_f = slice(sq * seq_len + cf * RC, sq * seq_len + (cf + 1) * RC)
            rows_b = slice(sq * seq_len + cb * RC, sq * seq_len + (cb + 1) * RC)
            sh_f = shared(rows_f)
            sh_b = sh_f if cb == cf else shared(rows_b)
            first, last = t == 0, t == n_chunks - 1
            of = chunk_dir(dirs["f"], sh_f, rows_f, first, last, sq)
            ob = chunk_dir(dirs["b"], sh_b, rows_b, first, last, sq)
            if cf == cb:
                y_ref[rows_f, :] = (of + ob).astype(BF16)
            else:
                emit(cf, of)
                emit(cb, ob)


RET_CTX_BLK = T_CTX // RET_BLOCK
RET_CTX_SEQS = RET_BLOCK // CTX_L


def _ret_scan(qkvg, lg, s0f, s0b):
    n_qk = RH
    ctx_last = RET_CTX_BLK - 1

    def lat_b(b):
        return jnp.maximum(b - RET_CTX_BLK, 0)

    s0_spec = pl.BlockSpec((1, 1, 1, RDK, RDV), lambda b, h: (lat_b(b), 0, h, 0, 0))
    so_spec = pl.BlockSpec((RET_CTX_SEQS, 1, 1, RDK, RDV),
                           lambda b, h: (jnp.minimum(b, ctx_last), 0, jnp.where(b <= ctx_last, h, RH - 1), 0, 0))
    so_shape = jax.ShapeDtypeStruct((CTX_B, 1, RH, RDK, RDV), F32)

    def body(lg_ref, q_ref, k_ref, v_ref, gf_ref, gb_ref, s0f_ref, s0b_ref, y_ref, sf_ref, sb_ref, *scratch):
        is_ctx = pl.program_id(0) < RET_CTX_BLK

        @pl.when(is_ctx)
        def _():
            _ret_scan_kernel(RET_CTX_SEQS, CTX_L, False, True, lg_ref, q_ref, k_ref, v_ref, gf_ref, gb_ref,
                             y_ref, sf_ref, sb_ref, *scratch)

        @pl.when(jnp.logical_not(is_ctx))
        def _():
            _ret_scan_kernel(1, LAT_L, True, False, lg_ref, q_ref, k_ref, v_ref, gf_ref, gb_ref,
                             s0f_ref, s0b_ref, y_ref, *scratch)

    return pl.pallas_call(
        body,
        grid=(T_ALL // RET_BLOCK, RH),
        in_specs=[
            pl.BlockSpec(memory_space=pltpu.SMEM),
            pl.BlockSpec((RET_BLOCK, RDK), lambda b, h: (b, h)),
            pl.BlockSpec((RET_BLOCK, RDK), lambda b, h: (b, n_qk + h)),
            pl.BlockSpec((RET_BLOCK, RDV), lambda b, h: (b, n_qk + h)),
            pl.BlockSpec((RET_BLOCK, RDV), lambda b, h: (b, n_qk + RH + h)),
            pl.BlockSpec((RET_BLOCK, RDV), lambda b, h: (b, n_qk + 2 * RH + h)),
            s0_spec, s0_spec,
        ],
        out_specs=[pl.BlockSpec((RET_BLOCK, RDV), lambda b, h: (b, h)), so_spec, so_spec],
        out_shape=[jax.ShapeDtypeStruct((T_ALL, RH * RDV), BF16), so_shape, so_shape],
        scratch_shapes=[pltpu.VMEM((RET_BLOCK, RDV), F32), pltpu.VMEM((RDK, RDV), F32),
                        pltpu.VMEM((RDK, RDV), F32)],
        compiler_params=_cparams(2),
        name="ret_scan",
    )(lg, qkvg, qkvg, qkvg, qkvg, qkvg, s0f, s0b)


RES_TM = MOD_TM
RES_MC = 256


def _res_kernel(gate_k, split_x, next_mod, a_ref, w_ref, *rest):
    rest = list(rest)
    x_refs = [rest.pop(0), rest.pop(0)] if split_x else [rest.pop(0)]
    mod_ref = rest.pop(0)
    gain_ref = rest.pop(0)
    if next_mod is not None:
        nmod_ref = rest.pop(0)
        xo_ref, h_ref = rest
    else:
        yc_ref, yl_ref = rest
    i = pl.program_id(0)
    is_ctx = i < N_CTX_BLK
    gate = mod_ref[0, gate_k:gate_k + 1, :]

    def run(x_ref, y_ref):
        for m in range(RES_TM // RES_MC):
            rows = slice(m * RES_MC, (m + 1) * RES_MC)
            acc = jnp.dot(a_ref[rows, :], w_ref[0], preferred_element_type=F32)
            xn = x_ref[rows, :] + gate * acc
            if next_mod is not None:
                xo_ref[rows, :] = xn
                h_ref[rows, :] = _modulate(xn, gain_ref, nmod_ref, *next_mod).astype(BF16)
            else:
                ms = jnp.mean(xn * xn, axis=-1, keepdims=True)
                y_ref[rows, :] = xn * lax.rsqrt(ms + EPS) * gain_ref[...]

    if not split_x and next_mod is not None:
        run(x_refs[0], None)
        return

    @pl.when(is_ctx)
    def _():
        run(x_refs[0], None if next_mod is not None else yc_ref)

    @pl.when(jnp.logical_not(is_ctx))
    def _():
        run(x_refs[-1], None if next_mod is not None else yl_ref)


def _res_matmul(a, w, layer, x, mods_l, gate_k, gain, next_mods=None, next_mod=None):
    kdim = a.shape[1]
    split_x = isinstance(x, tuple)
    row_spec = pl.BlockSpec((RES_TM, D), lambda i: (i, 0))
    ctx_spec = pl.BlockSpec((RES_TM, D), lambda i: (_ctx_blk(i), 0))
    lat_spec = pl.BlockSpec((RES_TM, D), lambda i: (_lat_blk(i), 0))
    mod_spec = pl.BlockSpec((1, N_MOD, D), lambda i: (_mod_row(i, RES_TM), 0, 0))
    in_specs = [pl.BlockSpec((RES_TM, kdim), lambda i: (i, 0)),
                pl.BlockSpec((1, kdim, D), lambda i: (layer, 0, 0))]
    args = [a, w]
    if split_x:
        in_specs += [ctx_spec, lat_spec]
        args += list(x)
    else:
        in_specs.append(row_spec)
        args.append(x)
    in_specs += [mod_spec, pl.BlockSpec((1, D), lambda i: (0, 0))]
    args += [mods_l, gain]
    if next_mod is not None:
        in_specs.append(mod_spec)
        args.append(next_mods)
        out_specs = [row_spec, row_spec]
        out_shape = [jax.ShapeDtypeStruct((T_ALL, D), F32), jax.ShapeDtypeStruct((T_ALL, D), BF16)]
    else:
        out_specs = [ctx_spec, lat_spec]
        out_shape = [jax.ShapeDtypeStruct((T_CTX, D), F32), jax.ShapeDtypeStruct((T_LAT, D), F32)]
    return pl.pallas_call(
        functools.partial(_res_kernel, gate_k, split_x, next_mod),
        grid=(T_ALL // RES_TM,),
        in_specs=in_specs,
        out_specs=out_specs,
        out_shape=out_shape,
        compiler_params=_cparams(1),
        name="res_matmul_k%d%s" % (kdim, "_final" if next_mod is None else ""),
    )(*args)


FFN_TM = 2048
FFN_MC = 512
FFN_TN = 256
FFN_HALO = 16
FFN_TILES = D_FF // FFN_TN
FFN_EXT = FFN_MC + (FFN_MC // CTX_L + 1) * FFN_HALO


def _ffn_up_kernel(h_ref, wu_ref, wg_ref, cwu_ref, cwg_ref, cbu_ref, cbg_ref, o_ref, a_scr, t_scr):
    is_ctx = pl.program_id(0) * FFN_TM < T_CTX
    zeros = jnp.zeros((FFN_HALO, D), BF16)
    n_slab = FFN_TN // 128

    def chunk(c, pieces, segments):
        slot = c % 2
        out_r0 = segments[0][1]
        out_n = sum(n_tok for _, _, n_tok in segments)
        hm = jnp.concatenate(pieces, axis=0)
        n_ext = hm.shape[0]
        for k, w_ref in enumerate((wu_ref, wg_ref)):
            a = jnp.dot(hm, w_ref[0], preferred_element_type=F32)
            for s in range(n_slab):
                a_scr[slot, k * n_slab + s, 0:n_ext, :] = a[:, s * 128:(s + 1) * 128]

        def conv(slab, base, half, cw_ref, cb_ref, lanes):
            def rows(off):
                return a_scr[slot, slab, pl.ds(base + off, half, stride=2), :]
            w0, w1, w2 = cw_ref[0, 0:1, lanes], cw_ref[0, 1:2, lanes], cw_ref[0, 2:3, lanes]
            b = cb_ref[0, :, lanes]
            om1, ev, od, ep1 = rows(-1), rows(0), rows(1), rows(2)
            return om1 * w0 + ev * w1 + od * w2 + b, ev * w0 + od * w1 + ep1 * w2 + b

        for s in range(n_slab):
            lanes = slice(s * 128, (s + 1) * 128)
            for base, r0, n_tok in segments:
                half = n_tok // 2
                t0 = r0 - out_r0
                u_ev, u_od = conv(s, base, half, cwu_ref, cbu_ref, lanes)
                g_ev, g_od = conv(n_slab + s, base, half, cwg_ref, cbg_ref, lanes)
                t_scr[slot, s, pl.ds(t0, half, stride=2), :] = _silu(g_ev) * u_ev
                t_scr[slot, s, pl.ds(t0 + 1, half, stride=2), :] = _silu(g_od) * u_od
            o_ref[out_r0:out_r0 + out_n, lanes] = t_scr[slot, s, 0:out_n, :].astype(BF16)

    sizes = [FFN_MC] * (FFN_TM // FFN_MC)
    starts = [sum(sizes[:c]) for c in range(len(sizes))]

    @pl.when(is_ctx)
    def _():
        for c, (c0, mc) in enumerate(zip(starts, sizes)):
            pieces, segments = [zeros], []
            for q in range(mc // CTX_L):
                r0 = c0 + q * CTX_L
                segments.append((FFN_HALO + q * (CTX_L + FFN_HALO), r0, CTX_L))
                pieces += [h_ref[r0:r0 + CTX_L, :], zeros]
            chunk(c, pieces, segments)

    @pl.when(jnp.logical_not(is_ctx))
    def _():
        for c, (r0, mc) in enumerate(zip(starts, sizes)):
            lo = zeros if r0 % LAT_L == 0 else h_ref[r0 - FFN_HALO:r0, :]
            hi = zeros if (r0 + mc) % LAT_L == 0 else h_ref[r0 + mc:r0 + mc + FFN_HALO, :]
            chunk(c, [lo, h_ref[r0:r0 + mc, :], hi], [(FFN_HALO, r0, mc)])


def _ffn_up(h, w_in, conv_w, conv_b, layer):
    n_slab = FFN_TN // 128
    return pl.pallas_call(
        _ffn_up_kernel,
        grid=(T_ALL // FFN_TM, FFN_TILES),
        in_specs=[
            pl.BlockSpec((FFN_TM, D), lambda i, j: (i, 0)),
            pl.BlockSpec((1, D, FFN_TN), lambda i, j: (layer, 0, j)),
            pl.BlockSpec((1, D, FFN_TN), lambda i, j: (layer, 0, FFN_TILES + j)),
            pl.BlockSpec((1, 3, FFN_TN), lambda i, j: (layer, 0, j)),
            pl.BlockSpec((1, 3, FFN_TN), lambda i, j: (layer, 0, FFN_TILES + j)),
            pl.BlockSpec((1, 1, FFN_TN), lambda i, j: (layer, 0, j)),
            pl.BlockSpec((1, 1, FFN_TN), lambda i, j: (layer, 0, FFN_TILES + j)),
        ],
        out_specs=pl.BlockSpec((FFN_TM, FFN_TN), lambda i, j: (i, j)),
        out_shape=jax.ShapeDtypeStruct((T_ALL, D_FF), BF16),
        scratch_shapes=[
            pltpu.VMEM((2, 2 * n_slab, FFN_EXT, 128), F32),
            pltpu.VMEM((2, n_slab, FFN_MC, 128), F32),
        ],
        compiler_params=_cparams(2),
        name="ffn_up",
    )(h, w_in, w_in, conv_w, conv_w, conv_b, conv_b)


ATT_NQ = AH * AHD
ATT_NKV = AKV * AHD
ATT_DUP = 2 * ATT_NKV
ATT_N = ATT_NQ + 2 * ATT_DUP
ATT_TN = ATT_NQ
assert 2 * ATT_DUP == ATT_TN


def _dup_heads(w):
    lead = w.shape[:-1]
    w = w.reshape(lead + (AKV, AHD))
    return jnp.concatenate([w, w], axis=-1).reshape(lead + (ATT_DUP,))


def _attn_proj_kernel(h_ref, w_ref, cos_ref, slo_ref, shi_ref, q_ref, kd_ref, vd_ref, k32_ref, v32_ref):
    i = pl.program_id(0)
    j = pl.program_id(1)
    is_lat = i * PROJ_TM >= T_CTX

    def chunks(epilogue):
        for m in range(PROJ_TM // PROJ_MC):
            rows = slice(m * PROJ_MC, (m + 1) * PROJ_MC)
            epilogue(jnp.dot(h_ref[rows, :], w_ref[...], preferred_element_type=F32), rows)

    def rope_into(o_ref, acc, rows, c0, n_cols):
        for s in range(n_cols // 128):
            y = _rope_slab(acc[:, c0 + s * 128:c0 + (s + 1) * 128],
                           cos_ref[rows, :], slo_ref[rows, :], shi_ref[rows, :], AHD // 4)
            o_ref[rows, s * 128:(s + 1) * 128] = y.astype(BF16)

    def q_lat(acc, rows):
        rope_into(q_ref, acc, rows, 0, ATT_NQ)

    def q_ctx(acc, rows):
        q_ref[rows, :] = acc.astype(BF16)

    def kv_lat(acc, rows):
        rope_into(kd_ref, acc, rows, 0, ATT_DUP)
        vd_ref[rows, :] = acc[:, ATT_DUP:].astype(BF16)

    def undup_into(o_ref, acc, rows, c0):
        low = lax.broadcasted_iota(jnp.int32, (PROJ_MC, 128), 1) < AHD
        for s in range(ATT_NKV // 128):
            a = acc[:, c0 + 2 * s * 128:c0 + (2 * s + 1) * 128]
            b = acc[:, c0 + (2 * s + 1) * 128:c0 + (2 * s + 2) * 128]
            o_ref[rows, s * 128:(s + 1) * 128] = jnp.where(low, a, b)

    def kv_ctx(acc, rows):
        kd_ref[rows, :] = acc[:, :ATT_DUP].astype(BF16)
        vd_ref[rows, :] = acc[:, ATT_DUP:].astype(BF16)
        undup_into(k32_ref, acc, rows, 0)
        undup_into(v32_ref, acc, rows, ATT_DUP)

    for on_q, on_lat, epilogue in ((True, True, q_lat), (True, False, q_ctx),
                                   (False, True, kv_lat), (False, False, kv_ctx)):
        @pl.when(jnp.logical_and((j == 0) == on_q, is_lat == on_lat))
        def _():
            chunks(epilogue)


def _attn_proj(h, w_all, tabs):
    cos, slo, shi = tabs
    n_pos = LAT_L // PROJ_TM
    n_ctx = T_CTX // PROJ_TM
    tab_spec = pl.BlockSpec((PROJ_TM, 128), lambda i, j: (i % n_pos, 0))
    row_spec = pl.BlockSpec((PROJ_TM, ATT_DUP), lambda i, j: (i, 0))
    ctx_spec = pl.BlockSpec((PROJ_TM, ATT_NKV), lambda i, j: (jnp.minimum(i, n_ctx - 1), 0))
    return pl.pallas_call(
        _attn_proj_kernel,
        grid=(T_ALL // PROJ_TM, ATT_N // ATT_TN),
        in_specs=[
            pl.BlockSpec((PROJ_TM, D), lambda i, j: (i, 0)),
            pl.BlockSpec((D, ATT_TN), lambda i, j: (0, j)),
            tab_spec, tab_spec, tab_spec,
        ],
        out_specs=[
            pl.BlockSpec((PROJ_TM, ATT_NQ), lambda i, j: (i, 0)),
            row_spec, row_spec, ctx_spec, ctx_spec,
        ],
        out_shape=[
            jax.ShapeDtypeStruct((T_ALL, ATT_NQ), BF16),
            jax.ShapeDtypeStruct((T_ALL, ATT_DUP), BF16),
            jax.ShapeDtypeStruct((T_ALL, ATT_DUP), BF16),
            jax.ShapeDtypeStruct((T_CTX, ATT_NKV), F32),
            jax.ShapeDtypeStruct((T_CTX, ATT_NKV), F32),
        ],
        compiler_params=_cparams(2),
        name="attn_proj",
    )(h, w_all, cos, slo, shi)


def _head_scores(q_ref, kh, k_dup):
    nq = q_ref.shape[0]
    nk = k_dup.shape[0]
    low = lax.broadcasted_iota(jnp.int32, (nk, 128), 1) < AHD
    zero = jnp.zeros_like(k_dup)
    k2 = jnp.concatenate([jnp.where(low, k_dup, zero), jnp.where(low, zero, k_dup)], axis=0)
    base = kh * AG * AHD
    lhs = jnp.concatenate([q_ref[:, base:base + 128], q_ref[:, base + 128:base + 256]], axis=0)
    s = lax.dot_general(lhs, k2, (((1,), (1,)), ((), ())), preferred_element_type=F32)
    return [s[0:nq, 0:nk], s[0:nq, nk:], s[nq:, 0:nk], s[nq:, nk:]]


def _row_max(s):
    m = s[:, 0:128]
    for c in range(128, s.shape[1], 128):
        m = jnp.maximum(m, s[:, c:c + 128])
    return jnp.max(m, axis=-1, keepdims=True)


def _softmax_pv_store(scores, v_dup, sink_ref, kh, o_ref):
    nq = scores[0].shape[0]
    nk = v_dup.shape[0]
    v_aug = jnp.concatenate([v_dup, jnp.ones((nk, 128), BF16)], axis=1)
    ps, tails = [], []
    for g, s in enumerate(scores):
        sink = sink_ref[kh * AG + g]
        m = jnp.maximum(_row_max(s), sink)
        ps.append(jnp.exp(s - m).astype(BF16))
        tails.append(jnp.exp(sink - m))
    pv = jnp.dot(jnp.concatenate(ps, axis=0), v_aug, preferred_element_type=F32)
    outs = []
    for g in range(AG):
        rows = slice(g * nq, (g + 1) * nq)
        outs.append(pv[rows, 0:128] / (pv[rows, 128:256] + tails[g]))
    low = lax.broadcasted_iota(jnp.int32, (nq, 128), 1) < AHD
    base = kh * AG * AHD
    o_ref[:, base:base + 128] = jnp.where(low, outs[0], outs[1]).astype(BF16)
    o_ref[:, base + 128:base + 256] = jnp.where(low, outs[2], outs[3]).astype(BF16)


def _attend_all(q_ref, sink_ref, o_ref, kv_of, bias_of):
    def scores_of(kh):
        k_dup, _ = kv_of(kh)
        return bias_of(_head_scores(q_ref, kh, k_dup))

    nxt = scores_of(0)
    for kh in range(AKV):
        cur = nxt
        if kh + 1 < AKV:
            nxt = scores_of(kh + 1)
        _softmax_pv_store(cur, kv_of(kh)[1], sink_ref, kh, o_ref)


def _ctx_attn_kernel(sink_ref, q_ref, kd_ref, vd_ref, o_ref):
    def kv_of(kh):
        cols = slice(kh * 128, (kh + 1) * 128)
        return kd_ref[:, cols], vd_ref[:, cols]

    _attend_all(q_ref, sink_ref, o_ref, kv_of, lambda scores: scores)


BAND = ABLK + 2 * WINDOW
ATT_CTX_STEPS = T_CTX // ABLK
ATT_LAT_QBLK = LAT_L // ABLK


def _lat_attn_kernel(sink_ref, q_ref, kd_ref, vd_ref, ck_ref, cv_ref, o_ref):
    n = (pl.program_id(0) - ATT_CTX_STEPS) % ATT_LAT_QBLK
    start = pl.multiple_of(jnp.clip(n * ABLK - WINDOW, 0, LAT_L - BAND), WINDOW)
    qpos = n * ABLK + lax.broadcasted_iota(jnp.int32, (ABLK, BAND), 0)
    kpos = start + lax.broadcasted_iota(jnp.int32, (ABLK, BAND), 1)
    band_bias = jnp.where(jnp.abs(qpos - kpos) <= WINDOW, 0.0, NEG_INF)

    def kv_of(kh):
        cols = slice(kh * 128, (kh + 1) * 128)
        k = jnp.concatenate([kd_ref[pl.ds(start, BAND), cols], ck_ref[0, :, cols]], axis=0)
        v = jnp.concatenate([vd_ref[pl.ds(start, BAND), cols], cv_ref[0, :, cols]], axis=0)
        return k, v

    def bias_of(scores):
        return [jnp.concatenate([s[:, :BAND] + band_bias, s[:, BAND:]], axis=1) for s in scores]

    _attend_all(q_ref, sink_ref, o_ref, kv_of, bias_of)


def _attention(sink, q, kd, vd, cache_kd, cache_vd):
    def lat_seq(t):
        return jnp.maximum(t - ATT_CTX_STEPS, 0) // ATT_LAT_QBLK

    row_spec = pl.BlockSpec((ABLK, ATT_NQ), lambda t: (t, 0))
    ctx_spec = pl.BlockSpec((CTX_L, ATT_DUP), lambda t: (jnp.minimum(t // (CTX_L // ABLK), CTX_B - 1), 0))
    seq_spec = pl.BlockSpec((LAT_L, ATT_DUP), lambda t: (T_CTX // LAT_L + lat_seq(t), 0))
    cache_spec = pl.BlockSpec((1, CTX_L, ATT_DUP), lambda t: (lat_seq(t), 0, 0))

    def body(sink_ref, q_ref, kdc_ref, vdc_ref, kdl_ref, vdl_ref, ck_ref, cv_ref, o_ref):
        is_ctx = pl.program_id(0) < ATT_CTX_STEPS

        @pl.when(is_ctx)
        def _():
            _ctx_attn_kernel(sink_ref, q_ref, kdc_ref, vdc_ref, o_ref)

        @pl.when(jnp.logical_not(is_ctx))
        def _():
            _lat_attn_kernel(sink_ref, q_ref, kdl_ref, vdl_ref, ck_ref, cv_ref, o_ref)

    return pl.pallas_call(
        body,
        grid=(T_ALL // ABLK,),
        in_specs=[pl.BlockSpec(memory_space=pltpu.SMEM), row_spec,
                  ctx_spec, ctx_spec, seq_spec, seq_spec, cache_spec, cache_spec],
        out_specs=row_spec,
        out_shape=jax.ShapeDtypeStruct((T_ALL, ATT_NQ), BF16),
        compiler_params=_cparams(1),
        name="attention",
    )(sink, q, kd, vd, kd, vd, cache_kd, cache_vd)


def _log_sigmoid(x):
    return -(jnp.maximum(-x, 0.0) + jnp.log1p(jnp.exp(-jnp.abs(x))))


def kernel(x_prompt, x_sample, state_ret_fwd, state_ret_bwd, cache_attn_k, cache_attn_v, c, c_ctx, ada_w, ada_b, norm_mix, norm_ffn, ret_wq, ret_wk, ret_wv, ret_wg_fwd, ret_wg_bwd, ret_wo, ret_decay_fwd, ret_decay_bwd, attn_wq, attn_wk, attn_wv, attn_wo, attn_sink, ffn_w_in, ffn_conv_w, ffn_conv_b, ffn_w_out, norm_final):
    assert x_prompt.shape == (CTX_B, CTX_L, D) and x_sample.shape == (LAT_B, LAT_L, D)
    assert state_ret_fwd.shape == state_ret_bwd.shape == (LAT_B, 1, RH, RDK, RDV)
    assert cache_attn_k.shape == cache_attn_v.shape == (LAT_B, 1, CTX_L, AKV, AHD)
    assert ada_w.shape == (2, D, N_MOD * D) and ffn_w_in.shape == (2, D, 2 * D_FF)
    assert all(t.dtype == F32 for t in (x_prompt, x_sample, state_ret_fwd, cache_attn_k, ada_w, ffn_w_in))
    x_ctx = x_prompt.reshape(T_CTX, D)
    x_lat = x_sample.reshape(T_LAT, D)

    cv = jnp.concatenate([c, c_ctx[None, :], jnp.zeros((MOD_ROWS - LAT_B - 1, D), F32)], axis=0)
    mods = _adaln(cv, ada_w, ada_b).reshape(ada_w.shape[0], MOD_ROWS, N_MOD, D)

    h = _first_mod(x_ctx, x_lat, norm_mix[0:1], mods[0])
    w_ret = jnp.concatenate([ret_wq[0], ret_wk[0] * (RDK ** -0.5), ret_wv[0], ret_wg_fwd[0], ret_wg_bwd[0]],
                            axis=1).astype(BF16)
    qkvg = _ret_proj(h, w_ret, _rope_tables(RDK))
    lg = jnp.stack([_log_sigmoid(ret_decay_fwd[0].astype(F32)), _log_sigmoid(ret_decay_bwd[0].astype(F32))])
    y0, s_f, s_b = _ret_scan(qkvg, lg, state_ret_fwd, state_ret_bwd)
    w_in = ffn_w_in.astype(BF16)
    w_out = ffn_w_out.astype(BF16)
    conv_b = ffn_conv_b[:, None, :]
    x, h = _res_matmul(y0, ret_wo.astype(BF16), 0, (x_ctx, x_lat), mods[0], 2,
                       norm_ffn[0:1], mods[0], (3, 4))
    act = _ffn_up(h, w_in, ffn_conv_w, conv_b, 0)
    x, h = _res_matmul(act, w_out, 0, x, mods[0], 5, norm_mix[1:2], mods[1], (0, 1))

    w_att = jnp.concatenate([attn_wq[0] * (AHD ** -0.5), _dup_heads(attn_wk[0]), _dup_heads(attn_wv[0])],
                            axis=1).astype(BF16)
    q, kd, vd, k32, v32 = _attn_proj(h, w_att, _rope_tables(AHD))
    sink = attn_sink[0].astype(F32)
    cache_kd = _dup_heads(cache_attn_k[:, 0].reshape(LAT_B, CTX_L, ATT_NKV)).astype(BF16)
    cache_vd = _dup_heads(cache_attn_v[:, 0].reshape(LAT_B, CTX_L, ATT_NKV)).astype(BF16)
    a = _attention(sink, q, kd, vd, cache_kd, cache_vd)
    x, h = _res_matmul(a, attn_wo.astype(BF16), 0, x, mods[1], 2, norm_ffn[1:2], mods[1], (3, 4))
    act = _ffn_up(h, w_in, ffn_conv_w, conv_b, 1)
    y_ctx, y_lat = _res_matmul(act, w_out, 1, x, mods[1], 5, norm_final[None, :])

    y_prompt = y_ctx.reshape(CTX_B, CTX_L, D)
    y_sample = y_lat.reshape(LAT_B, LAT_L, D)
    new_k = k32.reshape(CTX_B, 1, CTX_L, AKV, AHD)
    new_v = v32.reshape(CTX_B, 1, CTX_L, AKV, AHD)
    return y_prompt, y_sample, s_f, s_b, new_k, new_v
```

```python
import functools

import jax
import jax.numpy as jnp
import numpy as np
from jax import lax
from jax.experimental import pallas as pl
from jax.experimental.pallas import tpu as pltpu

D = 1024
CTX_B, CTX_L = 32, 256
LAT_B, LAT_L = 8, 2048
T_CTX = CTX_B * CTX_L
T_LAT = LAT_B * LAT_L
T_ALL = T_CTX + T_LAT
GRID_W = 64
N_MOD = 6
MOD_ROWS = 16
CTX_MOD_ROW = LAT_B
RH, RDK, RDV = 4, 256, 512
RC = 256
AH, AKV, AG, AHD = 16, 4, 4, 64
WINDOW = 128
ABLK = 256
D_FF = 2816
ROPE_BASE = 10000.0
EPS = 1e-6
NEG_INF = -1e30

BF16 = jnp.bfloat16
F32 = jnp.float32

VMEM_LIMIT_BYTES = 56 * 1024 * 1024
NORM_ROWS = 32


def _cparams(n_axes):
    return pltpu.CompilerParams(dimension_semantics=("arbitrary",) * n_axes,
                                vmem_limit_bytes=VMEM_LIMIT_BYTES)


def _silu(x):
    return x / (1.0 + jnp.exp(-x))


def _mod_row(i, tm):
    r0 = i * tm
    return jnp.where(r0 < T_CTX, CTX_MOD_ROW, (r0 - T_CTX) // LAT_L)


def _modulate(x, g_ref, mod_ref, shift_k, scale_k):
    ms = jnp.mean(x * x, axis=-1, keepdims=True)
    y = x * lax.rsqrt(ms + EPS) * g_ref[...]
    return y * (1.0 + mod_ref[0, scale_k:scale_k + 1, :]) + mod_ref[0, shift_k:shift_k + 1, :]


MOD_TM = 1024
N_CTX_BLK = T_CTX // MOD_TM


def _ctx_blk(i):
    return jnp.minimum(i, N_CTX_BLK - 1)


def _lat_blk(i):
    return jnp.maximum(i - N_CTX_BLK, 0)


def _first_mod_kernel(xc_ref, xl_ref, g_ref, mod_ref, h_ref):
    i = pl.program_id(0)

    def run(x_ref):
        for r in range(0, MOD_TM, NORM_ROWS):
            rows = slice(r, r + NORM_ROWS)
            h_ref[rows, :] = _modulate(x_ref[rows, :], g_ref, mod_ref, 0, 1).astype(BF16)

    @pl.when(i < N_CTX_BLK)
    def _():
        run(xc_ref)

    @pl.when(i >= N_CTX_BLK)
    def _():
        run(xl_ref)


def _first_mod(x_ctx, x_lat, gain, mods_l):
    return pl.pallas_call(
        _first_mod_kernel,
        grid=(T_ALL // MOD_TM,),
        in_specs=[
            pl.BlockSpec((MOD_TM, D), lambda i: (_ctx_blk(i), 0)),
            pl.BlockSpec((MOD_TM, D), lambda i: (_lat_blk(i), 0)),
            pl.BlockSpec((1, D), lambda i: (0, 0)),
            pl.BlockSpec((1, N_MOD, D), lambda i: (_mod_row(i, MOD_TM), 0, 0)),
        ],
        out_specs=pl.BlockSpec((MOD_TM, D), lambda i: (i, 0)),
        out_shape=jax.ShapeDtypeStruct((T_ALL, D), BF16),
        compiler_params=_cparams(1),
        name="first_mod",
    )(x_ctx, x_lat, gain, mods_l)


ADA_TN = 1024


def _adaln_kernel(cv_ref, w_ref, b_ref, o_ref):
    s = _silu(cv_ref[...]).astype(BF16)
    w = w_ref[0].astype(BF16)
    o_ref[0] = jnp.dot(s, w, preferred_element_type=F32) + b_ref[0]


def _adaln(cv, ada_w, ada_b):
    depth = ada_w.shape[0]
    n = N_MOD * D
    return pl.pallas_call(
        _adaln_kernel,
        grid=(depth, n // ADA_TN),
        in_specs=[
            pl.BlockSpec((MOD_ROWS, D), lambda l, j: (0, 0)),
            pl.BlockSpec((1, D, ADA_TN), lambda l, j: (l, 0, j)),
            pl.BlockSpec((1, 1, ADA_TN), lambda l, j: (l, 0, j)),
        ],
        out_specs=pl.BlockSpec((1, MOD_ROWS, ADA_TN), lambda l, j: (l, 0, j)),
        out_shape=jax.ShapeDtypeStruct((depth, MOD_ROWS, n), F32),
        compiler_params=_cparams(2),
        name="adaln",
    )(cv, ada_w, ada_b.reshape(depth, 1, n))


def _rope_tables(head_dim):
    half = head_dim // 2
    n_freq = half // 2
    inv = (np.float32(ROPE_BASE) ** (-np.arange(n_freq, dtype=np.float32) / np.float32(n_freq))).astype(np.float32)
    t = np.arange(LAT_L)
    rows = (t // GRID_W).astype(np.float32)[:, None] * inv[None, :]
    cols = (t % GRID_W).astype(np.float32)[:, None] * inv[None, :]
    zero = np.zeros_like(rows)
    cos = np.concatenate([np.cos(rows)] * 2 + [np.cos(cols)] * 2, axis=-1)
    sin_lo = np.concatenate([-np.sin(rows), zero, -np.sin(cols), zero], axis=-1)
    sin_hi = np.concatenate([zero, np.sin(rows), zero, np.sin(cols)], axis=-1)
    return tuple(np.tile(x.astype(np.float32), (1, max(1, 128 // head_dim))) for x in (cos, sin_lo, sin_hi))


def _rope_slab(x, cos, sin_lo, sin_hi, n_freq):
    up = pltpu.roll(x, 128 - n_freq, 1)
    dn = pltpu.roll(x, n_freq, 1)
    return x * cos + up * sin_lo + dn * sin_hi


PROJ_TM = 1024
PROJ_MC = 256
RET_NQK = 2 * RH * RDK
RET_N = RET_NQK + 3 * RH * RDV
RET_TN = RET_N // 2


def _ret_proj_kernel(h_ref, w_ref, cos_ref, slo_ref, shi_ref, o_ref):
    i = pl.program_id(0)
    j = pl.program_id(1)
    is_lat = i * PROJ_TM >= T_CTX

    def chunks(epilogue):
        for m in range(PROJ_TM // PROJ_MC):
            rows = slice(m * PROJ_MC, (m + 1) * PROJ_MC)
            acc = jnp.dot(h_ref[rows, :], w_ref[...], preferred_element_type=F32)
            epilogue(acc, rows)

    def plain(acc, rows):
        o_ref[rows, :] = acc.astype(BF16)

    def gate(acc, rows):
        o_ref[rows, :] = _silu(acc).astype(BF16)

    def rope_qk(acc, rows):
        for s in range(RET_NQK // 128):
            cols = slice(s * 128, (s + 1) * 128)
            tcols = slice((s % 2) * 128, (s % 2 + 1) * 128)
            y = _rope_slab(acc[:, cols], cos_ref[rows, tcols], slo_ref[rows, tcols], shi_ref[rows, tcols], RDK // 4)
            o_ref[rows, cols] = y.astype(BF16)
        o_ref[rows, RET_NQK:] = acc[:, RET_NQK:].astype(BF16)

    @pl.when(jnp.logical_and(j == 0, is_lat))
    def _():
        chunks(rope_qk)

    @pl.when(jnp.logical_and(j == 0, jnp.logical_not(is_lat)))
    def _():
        chunks(plain)

    @pl.when(j == 1)
    def _():
        chunks(gate)


def _ret_proj(h, w_all, tabs):
    cos, slo, shi = tabs
    n_pos = LAT_L // PROJ_TM
    tab_spec = pl.BlockSpec((PROJ_TM, RDK), lambda i, j: (i % n_pos, 0))
    return pl.pallas_call(
        _ret_proj_kernel,
        grid=(T_ALL // PROJ_TM, RET_N // RET_TN),
        in_specs=[
            pl.BlockSpec((PROJ_TM, D), lambda i, j: (i, 0)),
            pl.BlockSpec((D, RET_TN), lambda i, j: (0, j)),
            tab_spec, tab_spec, tab_spec,
        ],
        out_specs=pl.BlockSpec((PROJ_TM, RET_TN), lambda i, j: (i, j)),
        out_shape=jax.ShapeDtypeStruct((T_ALL, RET_N), BF16),
        compiler_params=_cparams(2),
        name="ret_proj",
    )(h, w_all, cos, slo, shi)


RET_BLOCK = 2048


def _ret_scan_kernel(n_seq, seq_len, has_init, emit_state, lg_ref, q_ref, k_ref, v_ref, gf_ref, gb_ref, *rest):
    rest = list(rest)
    s0f_ref = rest.pop(0) if has_init else None
    s0b_ref = rest.pop(0) if has_init else None
    y_ref = rest.pop(0)
    sf_ref = rest.pop(0) if emit_state else None
    sb_ref = rest.pop(0) if emit_state else None
    ybuf_ref, stf_ref, stb_ref, sbuf_ref, ktbuf_ref = rest

    hd = pl.program_id(1)
    n_chunks = seq_len // RC
    idx_col = lax.broadcasted_iota(jnp.int32, (RC, 1), 0).astype(F32)
    idx_row = lax.broadcasted_iota(jnp.int32, (1, RC), 1).astype(F32)
    diff = idx_col - idx_row

    lg_f = lg_ref[0, hd]
    lg_b = lg_ref[1, hd]
    dirs = {
        "f": dict(dmat=jnp.where(diff >= 0, jnp.exp(lg_f * jnp.maximum(diff, 0.0)), 0.0),
                  xi=jnp.exp(lg_f * (idx_col + 1.0)), zeta=jnp.exp(lg_f * (RC - 1.0 - idx_row)),
                  gch=jnp.exp(jnp.ones((1, RDV), F32) * (lg_f * RC)),
                  st_ref=stf_ref, s0_ref=s0f_ref, gate_ref=gf_ref, out_ref=sf_ref),
        "b": dict(dmat=jnp.where(diff <= 0, jnp.exp(lg_b * jnp.maximum(-diff, 0.0)), 0.0),
                  xi=jnp.exp(lg_b * (RC - idx_col)), zeta=jnp.exp(lg_b * idx_row),
                  gch=jnp.exp(jnp.ones((1, RDV), F32) * (lg_b * RC)),
                  st_ref=stb_ref, s0_ref=s0b_ref, gate_ref=gb_ref, out_ref=sb_ref),
    }

    def shared(rows, c, seen):
        q = q_ref[rows, :]
        if c in seen:
            return q, sbuf_ref[c], ktbuf_ref[c], v_ref[rows, :]
        k = k_ref[rows, :]
        s = lax.dot_general(q, k, (((1,), (1,)), ((), ())), preferred_element_type=F32)
        kt = k.astype(F32).T
        if n_chunks > 1:
            sbuf_ref[c] = s
            ktbuf_ref[c] = kt
            seen.add(c)
        return q, s, kt, v_ref[rows, :]

    def chunk_dir(d, sh, rows, first, last, sq):
        q, s, kt, v = sh
        sm = (s * d["dmat"]).astype(BF16)
        kzt = (kt * d["zeta"]).astype(BF16)
        both = jnp.dot(jnp.concatenate([sm, kzt], axis=0), v, preferred_element_type=F32)
        o, upd = both[:RC], both[RC:]
        st = None
        if has_init and first:
            st = d["s0_ref"][0, 0, 0]
        elif not first:
            st = d["st_ref"][...]
        if st is not None:
            o = o + jnp.dot(q, st.astype(BF16), preferred_element_type=F32) * d["xi"]
            upd = st * d["gch"] + upd
        if last and emit_state:
            d["out_ref"][sq, 0, 0] = upd
        elif not last:
            d["st_ref"][...] = upd
        o = o * lax.rsqrt(jnp.mean(o * o, axis=-1, keepdims=True) + EPS)
        return o * d["gate_ref"][rows, :].astype(F32)

    for sq in range(n_seq):
        half_done = set()
        seen = set()

        def emit(c, val):
            rows = slice(sq * seq_len + c * RC, sq * seq_len + (c + 1) * RC)
            if c in half_done:
                y_ref[rows, :] = (ybuf_ref[rows, :] + val).astype(BF16)
            else:
                ybuf_ref[rows, :] = val
                half_done.add(c)

        for t in range(n_chunks):
            cf, cb = t, n_chunks - 1 - t
            rows_f = slice(sq * seq_len + cf * RC, sq * seq_len + (cf + 1) * RC)
            rows_b = slice(sq * seq_len + cb * RC, sq * seq_len + (cb + 1) * RC)
            sh_f = shared(rows_f, cf, seen)
            sh_b = sh_f if cb == cf else shared(rows_b, cb, seen)
            first, last = t == 0, t == n_chunks - 1
            of = chunk_dir(dirs["f"], sh_f, rows_f, first, last, sq)
            ob = chunk_dir(dirs["b"], sh_b, rows_b, first, last, sq)
            if cf == cb:
                y_ref[rows_f, :] = (of + ob).astype(BF16)
            else:
                emit(cf, of)
                emit(cb, ob)


RET_CTX_BLK = T_CTX // RET_BLOCK
RET_CTX_SEQS = RET_BLOCK // CTX_L


def _ret_scan(qkvg, lg, s0f, s0b):
    n_qk = RH
    ctx_last = RET_CTX_BLK - 1

    def lat_b(b):
        return jnp.maximum(b - RET_CTX_BLK, 0)

    s0_spec = pl.BlockSpec((1, 1, 1, RDK, RDV), lambda b, h: (lat_b(b), 0, h, 0, 0))
    so_spec = pl.BlockSpec((RET_CTX_SEQS, 1, 1, RDK, RDV),
                           lambda b, h: (jnp.minimum(b, ctx_last), 0, jnp.where(b <= ctx_last, h, RH - 1), 0, 0))
    so_shape = jax.ShapeDtypeStruct((CTX_B, 1, RH, RDK, RDV), F32)

    def body(lg_ref, q_ref, k_ref, v_ref, gf_ref, gb_ref, s0f_ref, s0b_ref, y_ref, sf_ref, sb_ref, *scratch):
        is_ctx = pl.program_id(0) < RET_CTX_BLK

        @pl.when(is_ctx)
        def _():
            _ret_scan_kernel(RET_CTX_SEQS, CTX_L, False, True, lg_ref, q_ref, k_ref, v_ref, gf_ref, gb_ref,
                             y_ref, sf_ref, sb_ref, *scratch)

        @pl.when(jnp.logical_not(is_ctx))
        def _():
            _ret_scan_kernel(1, LAT_L, True, False, lg_ref, q_ref, k_ref, v_ref, gf_ref, gb_ref,
                             s0f_ref, s0b_ref, y_ref, *scratch)

    return pl.pallas_call(
        body,
        grid=(T_ALL // RET_BLOCK, RH),
        in_specs=[
            pl.BlockSpec(memory_space=pltpu.SMEM),
            pl.BlockSpec((RET_BLOCK, RDK), lambda b, h: (b, h)),
            pl.BlockSpec((RET_BLOCK, RDK), lambda b, h: (b, n_qk + h)),
            pl.BlockSpec((RET_BLOCK, RDV), lambda b, h: (b, n_qk + h)),
            pl.BlockSpec((RET_BLOCK, RDV), lambda b, h: (b, n_qk + RH + h)),
            pl.BlockSpec((RET_BLOCK, RDV), lambda b, h: (b, n_qk + 2 * RH + h)),
            s0_spec, s0_spec,
        ],
        out_specs=[pl.BlockSpec((RET_BLOCK, RDV), lambda b, h: (b, h)), so_spec, so_spec],
        out_shape=[jax.ShapeDtypeStruct((T_ALL, RH * RDV), BF16), so_shape, so_shape],
        scratch_shapes=[pltpu.VMEM((RET_BLOCK, RDV), F32), pltpu.VMEM((RDK, RDV), F32),
                        pltpu.VMEM((RDK, RDV), F32),
                        pltpu.VMEM((LAT_L // RC, RC, RC), F32), pltpu.VMEM((LAT_L // RC, RDK, RC), F32)],
        compiler_params=_cparams(2),
        name="ret_scan",
    )(lg, qkvg, qkvg, qkvg, qkvg, qkvg, s0f, s0b)


RES_TM = MOD_TM
RES_MC = 256


def _res_kernel(gate_k, split_x, next_mod, a_ref, w_ref, *rest):
    rest = list(rest)
    x_refs = [rest.pop(0), rest.pop(0)] if split_x else [rest.pop(0)]
    mod_ref = rest.pop(0)
    gain_ref = rest.pop(0)
    if next_mod is not None:
        nmod_ref = rest.pop(0)
        xo_ref, h_ref = rest
    else:
        yc_ref, yl_ref = rest
    i = pl.program_id(0)
    is_ctx = i < N_CTX_BLK
    gate = mod_ref[0, gate_k:gate_k + 1, :]

    def run(x_ref, y_ref):
        for m in range(RES_TM // RES_MC):
            rows = slice(m * RES_MC, (m + 1) * RES_MC)
            acc = jnp.dot(a_ref[rows, :], w_ref[0], preferred_element_type=F32)
            xn = x_ref[rows, :] + gate * acc
            if next_mod is not None:
                xo_ref[rows, :] = xn
                h_ref[rows, :] = _modulate(xn, gain_ref, nmod_ref, *next_mod).astype(BF16)
            else:
                ms = jnp.mean(xn * xn, axis=-1, keepdims=True)
                y_ref[rows, :] = xn * lax.rsqrt(ms + EPS) * gain_ref[...]

    if not split_x and next_mod is not None:
        run(x_refs[0], None)
        return

    @pl.when(is_ctx)
    def _():
        run(x_refs[0], None if next_mod is not None else yc_ref)

    @pl.when(jnp.logical_not(is_ctx))
    def _():
        run(x_refs[-1], None if next_mod is not None else yl_ref)


def _res_matmul(a, w, layer, x, mods_l, gate_k, gain, next_mods=None, next_mod=None):
    kdim = a.shape[1]
    split_x = isinstance(x, tuple)
    row_spec = pl.BlockSpec((RES_TM, D), lambda i: (i, 0))
    ctx_spec = pl.BlockSpec((RES_TM, D), lambda i: (_ctx_blk(i), 0))
    lat_spec = pl.BlockSpec((RES_TM, D), lambda i: (_lat_blk(i), 0))
    mod_spec = pl.BlockSpec((1, N_MOD, D), lambda i: (_mod_row(i, RES_TM), 0, 0))
    in_specs = [pl.BlockSpec((RES_TM, kdim), lambda i: (i, 0)),
                pl.BlockSpec((1, kdim, D), lambda i: (layer, 0, 0))]
    args = [a, w]
    if split_x:
        in_specs += [ctx_spec, lat_spec]
        args += list(x)
    else:
        in_specs.append(row_spec)
        args.append(x)
    in_specs += [mod_spec, pl.BlockSpec((1, D), lambda i: (0, 0))]
    args += [mods_l, gain]
    if next_mod is not None:
        in_specs.append(mod_spec)
        args.append(next_mods)
        out_specs = [row_spec, row_spec]
        out_shape = [jax.ShapeDtypeStruct((T_ALL, D), F32), jax.ShapeDtypeStruct((T_ALL, D), BF16)]
    else:
        out_specs = [ctx_spec, lat_spec]
        out_shape = [jax.ShapeDtypeStruct((T_CTX, D), F32), jax.ShapeDtypeStruct((T_LAT, D), F32)]
    return pl.pallas_call(
        functools.partial(_res_kernel, gate_k, split_x, next_mod),
        grid=(T_ALL // RES_TM,),
        in_specs=in_specs,
        out_specs=out_specs,
        out_shape=out_shape,
        compiler_params=_cparams(1),
        name="res_matmul_k%d%s" % (kdim, "_final" if next_mod is None else ""),
    )(*args)


FFN_TM = 2048
FFN_MC = 512
FFN_TN = 256
FFN_HALO = 16
FFN_TILES = D_FF // FFN_TN
FFN_EXT = FFN_MC + (FFN_MC // CTX_L + 1) * FFN_HALO


def _ffn_up_kernel(h_ref, wu_ref, wg_ref, cwu_ref, cwg_ref, cbu_ref, cbg_ref, o_ref, a_scr, t_scr):
    is_ctx = pl.program_id(0) * FFN_TM < T_CTX
    zeros = jnp.zeros((FFN_HALO, D), BF16)
    n_slab = FFN_TN // 128

    def chunk(c, pieces, segments):
        slot = c % 2
        out_r0 = segments[0][1]
        out_n = sum(n_tok for _, _, n_tok in segments)
        hm = jnp.concatenate(pieces, axis=0)
        n_ext = hm.shape[0]
        for k, w_ref in enumerate((wu_ref, wg_ref)):
            a = jnp.dot(hm, w_ref[0], preferred_element_type=F32)
            for s in range(n_slab):
                a_scr[slot, k * n_slab + s, 0:n_ext, :] = a[:, s * 128:(s + 1) * 128]

        def conv(slab, base, half, cw_ref, cb_ref, lanes):
            def rows(off):
                return a_scr[slot, slab, pl.ds(base + off, half, stride=2), :]
            w0, w1, w2 = cw_ref[0, 0:1, lanes], cw_ref[0, 1:2, lanes], cw_ref[0, 2:3, lanes]
            b = cb_ref[0, :, lanes]
            om1, ev, od, ep1 = rows(-1), rows(0), rows(1), rows(2)
            return om1 * w0 + ev * w1 + od * w2 + b, ev * w0 + od * w1 + ep1 * w2 + b

        for s in range(n_slab):
            lanes = slice(s * 128, (s + 1) * 128)
            for base, r0, n_tok in segments:
                half = n_tok // 2
                t0 = r0 - out_r0
                u_ev, u_od = conv(s, base, half, cwu_ref, cbu_ref, lanes)
                g_ev, g_od = conv(n_slab + s, base, half, cwg_ref, cbg_ref, lanes)
                t_scr[slot, s, pl.ds(t0, half, stride=2), :] = _silu(g_ev) * u_ev
                t_scr[slot, s, pl.ds(t0 + 1, half, stride=2), :] = _silu(g_od) * u_od
            o_ref[out_r0:out_r0 + out_n, lanes] = t_scr[slot, s, 0:out_n, :].astype(BF16)

    sizes = [FFN_MC] * (FFN_TM // FFN_MC)
    starts = [sum(sizes[:c]) for c in range(len(sizes))]

    @pl.when(is_ctx)
    def _():
        for c, (c0, mc) in enumerate(zip(starts, sizes)):
            pieces, segments = [zeros], []
            for q in range(mc // CTX_L):
                r0 = c0 + q * CTX_L
                segments.append((FFN_HALO + q * (CTX_L + FFN_HALO), r0, CTX_L))
                pieces += [h_ref[r0:r0 + CTX_L, :], zeros]
            chunk(c, pieces, segments)

    @pl.when(jnp.logical_not(is_ctx))
    def _():
        for c, (r0, mc) in enumerate(zip(starts, sizes)):
            lo = zeros if r0 % LAT_L == 0 else h_ref[r0 - FFN_HALO:r0, :]
            hi = zeros if (r0 + mc) % LAT_L == 0 else h_ref[r0 + mc:r0 + mc + FFN_HALO, :]
            chunk(c, [lo, h_ref[r0:r0 + mc, :], hi], [(FFN_HALO, r0, mc)])


def _ffn_up(h, w_in, conv_w, conv_b, layer):
    n_slab = FFN_TN // 128
    return pl.pallas_call(
        _ffn_up_kernel,
        grid=(T_ALL // FFN_TM, FFN_TILES),
        in_specs=[
            pl.BlockSpec((FFN_TM, D), lambda i, j: (i, 0)),
            pl.BlockSpec((1, D, FFN_TN), lambda i, j: (layer, 0, j)),
            pl.BlockSpec((1, D, FFN_TN), lambda i, j: (layer, 0, FFN_TILES + j)),
            pl.BlockSpec((1, 3, FFN_TN), lambda i, j: (layer, 0, j)),
            pl.BlockSpec((1, 3, FFN_TN), lambda i, j: (layer, 0, FFN_TILES + j)),
            pl.BlockSpec((1, 1, FFN_TN), lambda i, j: (layer, 0, j)),
            pl.BlockSpec((1, 1, FFN_TN), lambda i, j: (layer, 0, FFN_TILES + j)),
        ],
        out_specs=pl.BlockSpec((FFN_TM, FFN_TN), lambda i, j: (i, j)),
        out_shape=jax.ShapeDtypeStruct((T_ALL, D_FF), BF16),
        scratch_shapes=[
            pltpu.VMEM((2, 2 * n_slab, FFN_EXT, 128), F32),
            pltpu.VMEM((2, n_slab, FFN_MC, 128), F32),
        ],
        compiler_params=_cparams(2),
        name="ffn_up",
    )(h, w_in, w_in, conv_w, conv_w, conv_b, conv_b)


ATT_NQ = AH * AHD
ATT_NKV = AKV * AHD
ATT_DUP = 2 * ATT_NKV
ATT_N = ATT_NQ + 2 * ATT_DUP
ATT_TN = ATT_NQ
assert 2 * ATT_DUP == ATT_TN


def _dup_heads(w):
    lead = w.shape[:-1]
    w = w.reshape(lead + (AKV, AHD))
    return jnp.concatenate([w, w], axis=-1).reshape(lead + (ATT_DUP,))


def _attn_proj_kernel(h_ref, w_ref, cos_ref, slo_ref, shi_ref, q_ref, kd_ref, vd_ref, k32_ref, v32_ref):
    i = pl.program_id(0)
    j = pl.program_id(1)
    is_lat = i * PROJ_TM >= T_CTX

    def chunks(epilogue):
        for m in range(PROJ_TM // PROJ_MC):
            rows = slice(m * PROJ_MC, (m + 1) * PROJ_MC)
            epilogue(jnp.dot(h_ref[rows, :], w_ref[...], preferred_element_type=F32), rows)

    def rope_into(o_ref, acc, rows, c0, n_cols):
        for s in range(n_cols // 128):
            y = _rope_slab(acc[:, c0 + s * 128:c0 + (s + 1) * 128],
                           cos_ref[rows, :], slo_ref[rows, :], shi_ref[rows, :], AHD // 4)
            o_ref[rows, s * 128:(s + 1) * 128] = y.astype(BF16)

    def q_lat(acc, rows):
        rope_into(q_ref, acc, rows, 0, ATT_NQ)

    def q_ctx(acc, rows):
        q_ref[rows, :] = acc.astype(BF16)

    def kv_lat(acc, rows):
        rope_into(kd_ref, acc, rows, 0, ATT_DUP)
        vd_ref[rows, :] = acc[:, ATT_DUP:].astype(BF16)

    def undup_into(o_ref, acc, rows, c0):
        low = lax.broadcasted_iota(jnp.int32, (PROJ_MC, 128), 1) < AHD
        for s in range(ATT_NKV // 128):
            a = acc[:, c0 + 2 * s * 128:c0 + (2 * s + 1) * 128]
            b = acc[:, c0 + (2 * s + 1) * 128:c0 + (2 * s + 2) * 128]
            o_ref[rows, s * 128:(s + 1) * 128] = jnp.where(low, a, b)

    def kv_ctx(acc, rows):
        kd_ref[rows, :] = acc[:, :ATT_DUP].astype(BF16)
        vd_ref[rows, :] = acc[:, ATT_DUP:].astype(BF16)
        undup_into(k32_ref, acc, rows, 0)
        undup_into(v32_ref, acc, rows, ATT_DUP)

    for on_q, on_lat, epilogue in ((True, True, q_lat), (True, False, q_ctx),
                                   (False, True, kv_lat), (False, False, kv_ctx)):
        @pl.when(jnp.logical_and((j == 0) == on_q, is_lat == on_lat))
        def _():
            chunks(epilogue)


def _attn_proj(h, w_all, tabs):
    cos, slo, shi = tabs
    n_pos = LAT_L // PROJ_TM
    n_ctx = T_CTX // PROJ_TM
    tab_spec = pl.BlockSpec((PROJ_TM, 128), lambda i, j: (i % n_pos, 0))
    row_spec = pl.BlockSpec((PROJ_TM, ATT_DUP), lambda i, j: (i, 0))
    ctx_spec = pl.BlockSpec((PROJ_TM, ATT_NKV), lambda i, j: (jnp.minimum(i, n_ctx - 1), 0))
    return pl.pallas_call(
        _attn_proj_kernel,
        grid=(T_ALL // PROJ_TM, ATT_N // ATT_TN),
        in_specs=[
            pl.BlockSpec((PROJ_TM, D), lambda i, j: (i, 0)),
            pl.BlockSpec((D, ATT_TN), lambda i, j: (0, j)),
            tab_spec, tab_spec, tab_spec,
        ],
        out_specs=[
            pl.BlockSpec((PROJ_TM, ATT_NQ), lambda i, j: (i, 0)),
            row_spec, row_spec, ctx_spec, ctx_spec,
        ],
        out_shape=[
            jax.ShapeDtypeStruct((T_ALL, ATT_NQ), BF16),
            jax.ShapeDtypeStruct((T_ALL, ATT_DUP), BF16),
            jax.ShapeDtypeStruct((T_ALL, ATT_DUP), BF16),
            jax.ShapeDtypeStruct((T_CTX, ATT_NKV), F32),
            jax.ShapeDtypeStruct((T_CTX, ATT_NKV), F32),
        ],
        compiler_params=_cparams(2),
        name="attn_proj",
    )(h, w_all, cos, slo, shi)


def _head_scores(q_ref, kh, k_dup):
    nq = q_ref.shape[0]
    nk = k_dup.shape[0]
    low = lax.broadcasted_iota(jnp.int32, (nk, 128), 1) < AHD
    zero = jnp.zeros_like(k_dup)
    k2 = jnp.concatenate([jnp.where(low, k_dup, zero), jnp.where(low, zero, k_dup)], axis=0)
    base = kh * AG * AHD
    lhs = jnp.concatenate([q_ref[:, base:base + 128], q_ref[:, base + 128:base + 256]], axis=0)
    s = lax.dot_general(lhs, k2, (((1,), (1,)), ((), ())), preferred_element_type=F32)
    return [s[0:nq, 0:nk], s[0:nq, nk:], s[nq:, 0:nk], s[nq:, nk:]]


def _row_max(s):
    m = s[:, 0:128]
    for c in range(128, s.shape[1], 128):
        m = jnp.maximum(m, s[:, c:c + 128])
    return jnp.max(m, axis=-1, keepdims=True)


def _softmax_pv_store(scores, v_dup, sink_ref, kh, o_ref):
    nq = scores[0].shape[0]
    nk = v_dup.shape[0]
    v_aug = jnp.concatenate([v_dup, jnp.ones((nk, 128), BF16)], axis=1)
    ps, tails = [], []
    for g, s in enumerate(scores):
        sink = sink_ref[kh * AG + g]
        m = jnp.maximum(_row_max(s), sink)
        ps.append(jnp.exp(s - m).astype(BF16))
        tails.append(jnp.exp(sink - m))
    pv = jnp.dot(jnp.concatenate(ps, axis=0), v_aug, preferred_element_type=F32)
    outs = []
    for g in range(AG):
        rows = slice(g * nq, (g + 1) * nq)
        outs.append(pv[rows, 0:128] / (pv[rows, 128:256] + tails[g]))
    low = lax.broadcasted_iota(jnp.int32, (nq, 128), 1) < AHD
    base = kh * AG * AHD
    o_ref[:, base:base + 128] = jnp.where(low, outs[0], outs[1]).astype(BF16)
    o_ref[:, base + 128:base + 256] = jnp.where(low, outs[2], outs[3]).astype(BF16)


def _attend_all(q_ref, sink_ref, o_ref, kv_of, bias_of):
    def scores_of(kh):
        k_dup, _ = kv_of(kh)
        return bias_of(_head_scores(q_ref, kh, k_dup))

    nxt = scores_of(0)
    for kh in range(AKV):
        cur = nxt
        if kh + 1 < AKV:
            nxt = scores_of(kh + 1)
        _softmax_pv_store(cur, kv_of(kh)[1], sink_ref, kh, o_ref)


def _ctx_attn_kernel(sink_ref, q_ref, kd_ref, vd_ref, o_ref):
    def kv_of(kh):
        cols = slice(kh * 128, (kh + 1) * 128)
        return kd_ref[:, cols], vd_ref[:, cols]

    _attend_all(q_ref, sink_ref, o_ref, kv_of, lambda scores: scores)


BAND = ABLK + 2 * WINDOW
ATT_CTX_STEPS = T_CTX // ABLK
ATT_LAT_QBLK = LAT_L // ABLK


def _lat_attn_kernel(sink_ref, q_ref, kd_ref, vd_ref, ck_ref, cv_ref, o_ref):
    n = (pl.program_id(0) - ATT_CTX_STEPS) % ATT_LAT_QBLK
    start = pl.multiple_of(jnp.clip(n * ABLK - WINDOW, 0, LAT_L - BAND), WINDOW)
    qpos = n * ABLK + lax.broadcasted_iota(jnp.int32, (ABLK, BAND), 0)
    kpos = start + lax.broadcasted_iota(jnp.int32, (ABLK, BAND), 1)
    band_bias = jnp.where(jnp.abs(qpos - kpos) <= WINDOW, 0.0, NEG_INF)

    def kv_of(kh):
        cols = slice(kh * 128, (kh + 1) * 128)
        k = jnp.concatenate([kd_ref[pl.ds(start, BAND), cols], ck_ref[0, :, cols]], axis=0)
        v = jnp.concatenate([vd_ref[pl.ds(start, BAND), cols], cv_ref[0, :, cols]], axis=0)
        return k, v

    def bias_of(scores):
        return [jnp.concatenate([s[:, :BAND] + band_bias, s[:, BAND:]], axis=1) for s in scores]

    _attend_all(q_ref, sink_ref, o_ref, kv_of, bias_of)


def _attention(sink, q, kd, vd, cache_kd, cache_vd):
    def lat_seq(t):
        return jnp.maximum(t - ATT_CTX_STEPS, 0) // ATT_LAT_QBLK

    row_spec = pl.BlockSpec((ABLK, ATT_NQ), lambda t: (t, 0))
    ctx_spec = pl.BlockSpec((CTX_L, ATT_DUP), lambda t: (jnp.minimum(t // (CTX_L // ABLK), CTX_B - 1), 0))
    seq_spec = pl.BlockSpec((LAT_L, ATT_DUP), lambda t: (T_CTX // LAT_L + lat_seq(t), 0))
    cache_spec = pl.BlockSpec((1, CTX_L, ATT_DUP), lambda t: (lat_seq(t), 0, 0))

    def body(sink_ref, q_ref, kdc_ref, vdc_ref, kdl_ref, vdl_ref, ck_ref, cv_ref, o_ref):
        is_ctx = pl.program_id(0) < ATT_CTX_STEPS

        @pl.when(is_ctx)
        def _():
            _ctx_attn_kernel(sink_ref, q_ref, kdc_ref, vdc_ref, o_ref)

        @pl.when(jnp.logical_not(is_ctx))
        def _():
            _lat_attn_kernel(sink_ref, q_ref, kdl_ref, vdl_ref, ck_ref, cv_ref, o_ref)

    return pl.pallas_call(
        body,
        grid=(T_ALL // ABLK,),
        in_specs=[pl.BlockSpec(memory_space=pltpu.SMEM), row_spec,
                  ctx_spec, ctx_spec, seq_spec, seq_spec, cache_spec, cache_spec],
        out_specs=row_spec,
        out_shape=jax.ShapeDtypeStruct((T_ALL, ATT_NQ), BF16),
        compiler_params=_cparams(1),
        name="attention",
    )(sink, q, kd, vd, kd, vd, cache_kd, cache_vd)


def _log_sigmoid(x):
    return -(jnp.maximum(-x, 0.0) + jnp.log1p(jnp.exp(-jnp.abs(x))))


def kernel(x_prompt, x_sample, state_ret_fwd, state_ret_bwd, cache_attn_k, cache_attn_v, c, c_ctx, ada_w, ada_b, norm_mix, norm_ffn, ret_wq, ret_wk, ret_wv, ret_wg_fwd, ret_wg_bwd, ret_wo, ret_decay_fwd, ret_decay_bwd, attn_wq, attn_wk, attn_wv, attn_wo, attn_sink, ffn_w_in, ffn_conv_w, ffn_conv_b, ffn_w_out, norm_final):
    assert x_prompt.shape == (CTX_B, CTX_L, D) and x_sample.shape == (LAT_B, LAT_L, D)
    assert state_ret_fwd.shape == state_ret_bwd.shape == (LAT_B, 1, RH, RDK, RDV)
    assert cache_attn_k.shape == cache_attn_v.shape == (LAT_B, 1, CTX_L, AKV, AHD)
    assert ada_w.shape == (2, D, N_MOD * D) and ffn_w_in.shape == (2, D, 2 * D_FF)
    assert all(t.dtype == F32 for t in (x_prompt, x_sample, state_ret_fwd, cache_attn_k, ada_w, ffn_w_in))
    x_ctx = x_prompt.reshape(T_CTX, D)
    x_lat = x_sample.reshape(T_LAT, D)

    cv = jnp.concatenate([c, c_ctx[None, :], jnp.zeros((MOD_ROWS - LAT_B - 1, D), F32)], axis=0)
    mods = _adaln(cv, ada_w, ada_b).reshape(ada_w.shape[0], MOD_ROWS, N_MOD, D)

    h = _first_mod(x_ctx, x_lat, norm_mix[0:1], mods[0])
    w_ret = jnp.concatenate([ret_wq[0], ret_wk[0] * (RDK ** -0.5), ret_wv[0], ret_wg_fwd[0], ret_wg_bwd[0]],
                            axis=1).astype(BF16)
    qkvg = _ret_proj(h, w_ret, _rope_tables(RDK))
    lg = jnp.stack([_log_sigmoid(ret_decay_fwd[0].astype(F32)), _log_sigmoid(ret_decay_bwd[0].astype(F32))])
    y0, s_f, s_b = _ret_scan(qkvg, lg, state_ret_fwd, state_ret_bwd)
    w_in = ffn_w_in.astype(BF16)
    w_out = ffn_w_out.astype(BF16)
    conv_b = ffn_conv_b[:, None, :]
    x, h = _res_matmul(y0, ret_wo.astype(BF16), 0, (x_ctx, x_lat), mods[0], 2,
                       norm_ffn[0:1], mods[0], (3, 4))
    act = _ffn_up(h, w_in, ffn_conv_w, conv_b, 0)
    x, h = _res_matmul(act, w_out, 0, x, mods[0], 5, norm_mix[1:2], mods[1], (0, 1))

    w_att = jnp.concatenate([attn_wq[0] * (AHD ** -0.5), _dup_heads(attn_wk[0]), _dup_heads(attn_wv[0])],
                            axis=1).astype(BF16)
    q, kd, vd, k32, v32 = _attn_proj(h, w_att, _rope_tables(AHD))
    sink = attn_sink[0].astype(F32)
    cache_kd = _dup_heads(cache_attn_k[:, 0].reshape(LAT_B, CTX_L, ATT_NKV)).astype(BF16)
    cache_vd = _dup_heads(cache_attn_v[:, 0].reshape(LAT_B, CTX_L, ATT_NKV)).astype(BF16)
    a = _attention(sink, q, kd, vd, cache_kd, cache_vd)
    x, h = _res_matmul(a, attn_wo.astype(BF16), 0, x, mods[1], 2, norm_ffn[1:2], mods[1], (3, 4))
    act = _ffn_up(h, w_in, ffn_conv_w, conv_b, 1)
    y_ctx, y_lat = _res_matmul(act, w_out, 1, x, mods[1], 5, norm_final[None, :])

    y_prompt = y_ctx.reshape(CTX_B, CTX_L, D)
    y_sample = y_lat.reshape(LAT_B, LAT_L, D)
    new_k = k32.reshape(CTX_B, 1, CTX_L, AKV, AHD)
    new_v = v32.reshape(CTX_B, 1, CTX_L, AKV, AHD)
    return y_prompt, y_sample, s_f, s_b, new_k, new_v
```

```python
import functools

import jax
import jax.numpy as jnp
import numpy as np
from jax import lax
from jax.experimental import pallas as pl
from jax.experimental.pallas import tpu as pltpu

D = 1024
CTX_B, CTX_L = 32, 256
LAT_B, LAT_L = 8, 2048
T_CTX = CTX_B * CTX_L
T_LAT = LAT_B * LAT_L
T_ALL = T_CTX + T_LAT
GRID_W = 64
N_MOD = 6
MOD_ROWS = 16
CTX_MOD_ROW = LAT_B
RH, RDK, RDV = 4, 256, 512
RC = 256
AH, AKV, AG, AHD = 16, 4, 4, 64
WINDOW = 128
ABLK = 256
D_FF = 2816
ROPE_BASE = 10000.0
EPS = 1e-6
NEG_INF = -1e30

BF16 = jnp.bfloat16
F32 = jnp.float32

VMEM_LIMIT_BYTES = 56 * 1024 * 1024
NORM_ROWS = 32


def _cparams(n_axes):
    return pltpu.CompilerParams(dimension_semantics=("arbitrary",) * n_axes,
                                vmem_limit_bytes=VMEM_LIMIT_BYTES)


def _silu(x):
    return x / (1.0 + jnp.exp(-x))


def _mod_row(i, tm):
    r0 = i * tm
    return jnp.where(r0 < T_CTX, CTX_MOD_ROW, (r0 - T_CTX) // LAT_L)


def _modulate(x, g_ref, mod_ref, shift_k, scale_k):
    ms = jnp.mean(x * x, axis=-1, keepdims=True)
    y = x * lax.rsqrt(ms + EPS) * g_ref[...]
    return y * (1.0 + mod_ref[0, scale_k:scale_k + 1, :]) + mod_ref[0, shift_k:shift_k + 1, :]


MOD_TM = 1024
N_CTX_BLK = T_CTX // MOD_TM


def _ctx_blk(i):
    return jnp.minimum(i, N_CTX_BLK - 1)


def _lat_blk(i):
    return jnp.maximum(i - N_CTX_BLK, 0)


def _first_mod_kernel(xc_ref, xl_ref, g_ref, mod_ref, h_ref):
    i = pl.program_id(0)

    def run(x_ref):
        for r in range(0, MOD_TM, NORM_ROWS):
            rows = slice(r, r + NORM_ROWS)
            h_ref[rows, :] = _modulate(x_ref[rows, :], g_ref, mod_ref, 0, 1).astype(BF16)

    @pl.when(i < N_CTX_BLK)
    def _():
        run(xc_ref)

    @pl.when(i >= N_CTX_BLK)
    def _():
        run(xl_ref)


def _first_mod(x_ctx, x_lat, gain, mods_l):
    return pl.pallas_call(
        _first_mod_kernel,
        grid=(T_ALL // MOD_TM,),
        in_specs=[
            pl.BlockSpec((MOD_TM, D), lambda i: (_ctx_blk(i), 0)),
            pl.BlockSpec((MOD_TM, D), lambda i: (_lat_blk(i), 0)),
            pl.BlockSpec((1, D), lambda i: (0, 0)),
            pl.BlockSpec((1, N_MOD, D), lambda i: (_mod_row(i, MOD_TM), 0, 0)),
        ],
        out_specs=pl.BlockSpec((MOD_TM, D), lambda i: (i, 0)),
        out_shape=jax.ShapeDtypeStruct((T_ALL, D), BF16),
        compiler_params=_cparams(1),
        name="first_mod",
    )(x_ctx, x_lat, gain, mods_l)


ADA_TN = 1024


def _adaln_kernel(cv_ref, w_ref, b_ref, o_ref):
    s = _silu(cv_ref[...]).astype(BF16)
    w = w_ref[0].astype(BF16)
    o_ref[0] = jnp.dot(s, w, preferred_element_type=F32) + b_ref[0]


def _adaln(cv, ada_w, ada_b):
    depth = ada_w.shape[0]
    n = N_MOD * D
    return pl.pallas_call(
        _adaln_kernel,
        grid=(depth, n // ADA_TN),
        in_specs=[
            pl.BlockSpec((MOD_ROWS, D), lambda l, j: (0, 0)),
            pl.BlockSpec((1, D, ADA_TN), lambda l, j: (l, 0, j)),
            pl.BlockSpec((1, 1, ADA_TN), lambda l, j: (l, 0, j)),
        ],
        out_specs=pl.BlockSpec((1, MOD_ROWS, ADA_TN), lambda l, j: (l, 0, j)),
        out_shape=jax.ShapeDtypeStruct((depth, MOD_ROWS, n), F32),
        compiler_params=_cparams(2),
        name="adaln",
    )(cv, ada_w, ada_b.reshape(depth, 1, n))


def _rope_tables(head_dim):
    half = head_dim // 2
    n_freq = half // 2
    inv = (np.float32(ROPE_BASE) ** (-np.arange(n_freq, dtype=np.float32) / np.float32(n_freq))).astype(np.float32)
    t = np.arange(LAT_L)
    rows = (t // GRID_W).astype(np.float32)[:, None] * inv[None, :]
    cols = (t % GRID_W).astype(np.float32)[:, None] * inv[None, :]
    zero = np.zeros_like(rows)
    cos = np.concatenate([np.cos(rows)] * 2 + [np.cos(cols)] * 2, axis=-1)
    sin_lo = np.concatenate([-np.sin(rows), zero, -np.sin(cols), zero], axis=-1)
    sin_hi = np.concatenate([zero, np.sin(rows), zero, np.sin(cols)], axis=-1)
    return tuple(np.tile(x.astype(np.float32), (1, max(1, 128 // head_dim))) for x in (cos, sin_lo, sin_hi))


def _rope_slab(x, cos, sin_lo, sin_hi, n_freq):
    up = pltpu.roll(x, 128 - n_freq, 1)
    dn = pltpu.roll(x, n_freq, 1)
    return x * cos + up * sin_lo + dn * sin_hi


PROJ_TM = 1024
PROJ_MC = 256
RET_NQK = 2 * RH * RDK
RET_N = RET_NQK + 3 * RH * RDV
RET_TN = RET_N // 2


def _ret_proj_kernel(h_ref, w_ref, cos_ref, slo_ref, shi_ref, o_ref):
    i = pl.program_id(0)
    j = pl.program_id(1)
    is_lat = i * PROJ_TM >= T_CTX

    def chunks(epilogue):
        for m in range(PROJ_TM // PROJ_MC):
            rows = slice(m * PROJ_MC, (m + 1) * PROJ_MC)
            acc = jnp.dot(h_ref[rows, :], w_ref[...], preferred_element_type=F32)
            epilogue(acc, rows)

    def plain(acc, rows):
        o_ref[rows, :] = acc.astype(BF16)

    def gate(acc, rows):
        o_ref[rows, :] = _silu(acc).astype(BF16)

    def rope_qk(acc, rows):
        for s in range(RET_NQK // 128):
            cols = slice(s * 128, (s + 1) * 128)
            tcols = slice((s % 2) * 128, (s % 2 + 1) * 128)
            y = _rope_slab(acc[:, cols], cos_ref[rows, tcols], slo_ref[rows, tcols], shi_ref[rows, tcols], RDK // 4)
            o_ref[rows, cols] = y.astype(BF16)
        o_ref[rows, RET_NQK:] = acc[:, RET_NQK:].astype(BF16)

    @pl.when(jnp.logical_and(j == 0, is_lat))
    def _():
        chunks(rope_qk)

    @pl.when(jnp.logical_and(j == 0, jnp.logical_not(is_lat)))
    def _():
        chunks(plain)

    @pl.when(j == 1)
    def _():
        chunks(gate)


def _ret_proj(h, w_all, tabs):
    cos, slo, shi = tabs
    n_pos = LAT_L // PROJ_TM
    tab_spec = pl.BlockSpec((PROJ_TM, RDK), lambda i, j: (i % n_pos, 0))
    return pl.pallas_call(
        _ret_proj_kernel,
        grid=(T_ALL // PROJ_TM, RET_N // RET_TN),
        in_specs=[
            pl.BlockSpec((PROJ_TM, D), lambda i, j: (i, 0)),
            pl.BlockSpec((D, RET_TN), lambda i, j: (0, j)),
            tab_spec, tab_spec, tab_spec,
        ],
        out_specs=pl.BlockSpec((PROJ_TM, RET_TN), lambda i, j: (i, j)),
        out_shape=jax.ShapeDtypeStruct((T_ALL, RET_N), BF16),
        compiler_params=_cparams(2),
        name="ret_proj",
    )(h, w_all, cos, slo, shi)


RET_BLOCK = 2048


def _ret_scan_kernel(n_seq, seq_len, has_init, emit_state, lg_ref, q_ref, k_ref, v_ref, gf_ref, gb_ref, *rest):
    rest = list(rest)
    s0f_ref = rest.pop(0) if has_init else None
    s0b_ref = rest.pop(0) if has_init else None
    y_ref = rest.pop(0)
    sf_ref = rest.pop(0) if emit_state else None
    sb_ref = rest.pop(0) if emit_state else None
    ybuf_ref, stf_ref, stb_ref, sbuf_ref, ktbuf_ref = rest

    hd = pl.program_id(1)
    n_chunks = seq_len // RC
    idx_col = lax.broadcasted_iota(jnp.int32, (RC, 1), 0).astype(F32)
    idx_row = lax.broadcasted_iota(jnp.int32, (1, RC), 1).astype(F32)
    diff = idx_col - idx_row

    lg_f = lg_ref[0, hd]
    lg_b = lg_ref[1, hd]
    dirs = {
        "f": dict(dmat=jnp.where(diff >= 0, jnp.exp(lg_f * jnp.maximum(diff, 0.0)), 0.0),
                  xi=jnp.exp(lg_f * (idx_col + 1.0)), zeta=jnp.exp(lg_f * (RC - 1.0 - idx_row)),
                  gch=jnp.exp(jnp.ones((1, RDV), F32) * (lg_f * RC)),
                  st_ref=stf_ref, s0_ref=s0f_ref, gate_ref=gf_ref, out_ref=sf_ref),
        "b": dict(dmat=jnp.where(diff <= 0, jnp.exp(lg_b * jnp.maximum(-diff, 0.0)), 0.0),
                  xi=jnp.exp(lg_b * (RC - idx_col)), zeta=jnp.exp(lg_b * idx_row),
                  gch=jnp.exp(jnp.ones((1, RDV), F32) * (lg_b * RC)),
                  st_ref=stb_ref, s0_ref=s0b_ref, gate_ref=gb_ref, out_ref=sb_ref),
    }

    def shared(rows, c, seen):
        q = q_ref[rows, :]
        if c in seen:
            return q, sbuf_ref[c], ktbuf_ref[c], v_ref[rows, :]
        k = k_ref[rows, :]
        s = lax.dot_general(q, k, (((1,), (1,)), ((), ())), preferred_element_type=F32)
        kt = k.astype(F32).T
        if n_chunks > 1:
            sbuf_ref[c] = s
            ktbuf_ref[c] = kt
            seen.add(c)
        return q, s, kt, v_ref[rows, :]

    def chunk_dir(d, sh, rows, first, last, sq):
        q, s, kt, v = sh
        sm = (s * d["dmat"]).astype(BF16)
        kzt = (kt * d["zeta"]).astype(BF16)
        both = jnp.dot(jnp.concatenate([sm, kzt], axis=0), v, preferred_element_type=F32)
        o, upd = both[:RC], both[RC:]
        st = None
        if has_init and first:
            st = d["s0_ref"][0, 0, 0]
        elif not first:
            st = d["st_ref"][...]
        if st is not None:
            o = o + jnp.dot(q, st.astype(BF16), preferred_element_type=F32) * d["xi"]
            upd = st * d["gch"] + upd
        if last and emit_state:
            d["out_ref"][sq, 0, 0] = upd
        elif not last:
            d["st_ref"][...] = upd
        o = o * lax.rsqrt(jnp.mean(o * o, axis=-1, keepdims=True) + EPS)
        return o * d["gate_ref"][rows, :].astype(F32)

    for sq in range(n_seq):
        half_done = set()
        seen = set()

        def emit(c, val):
            rows = slice(sq * seq_len + c * RC, sq * seq_len + (c + 1) * RC)
            if c in half_done:
                y_ref[rows, :] = (ybuf_ref[rows, :] + val).astype(BF16)
            else:
                ybuf_ref[rows, :] = val
                half_done.add(c)

        for t in range(n_chunks):
            cf, cb = t, n_chunks - 1 - t
            rows_f = slice(sq * seq_len + cf * RC, sq * seq_len + (cf + 1) * RC)
            rows_b = slice(sq * seq_len + cb * RC, sq * seq_len + (cb + 1) * RC)
            sh_f = shared(rows_f, cf, seen)
            sh_b = sh_f if cb == cf else shared(rows_b, cb, seen)
            first, last = t == 0, t == n_chunks - 1
            of = chunk_dir(dirs["f"], sh_f, rows_f, first, last, sq)
            ob = chunk_dir(dirs["b"], sh_b, rows_b, first, last, sq)
            if cf == cb:
                y_ref[rows_f, :] = (of + ob).astype(BF16)
            else:
                emit(cf, of)
                emit(cb, ob)


RET_CTX_BLK = T_CTX // RET_BLOCK
RET_CTX_SEQS = RET_BLOCK // CTX_L


def _ret_scan(qkvg, lg, s0f, s0b):
    n_qk = RH
    ctx_last = RET_CTX_BLK - 1

    def lat_b(b):
        return jnp.maximum(b - RET_CTX_BLK, 0)

    s0_spec = pl.BlockSpec((1, 1, 1, RDK, RDV), lambda b, h: (lat_b(b), 0, h, 0, 0))
    so_spec = pl.BlockSpec((RET_CTX_SEQS, 1, 1, RDK, RDV),
                           lambda b, h: (jnp.minimum(b, ctx_last), 0, jnp.where(b <= ctx_last, h, RH - 1), 0, 0))
    so_shape = jax.ShapeDtypeStruct((CTX_B, 1, RH, RDK, RDV), F32)

    def body(lg_ref, q_ref, k_ref, v_ref, gf_ref, gb_ref, s0f_ref, s0b_ref, y_ref, sf_ref, sb_ref, *scratch):
        is_ctx = pl.program_id(0) < RET_CTX_BLK

        @pl.when(is_ctx)
        def _():
            _ret_scan_kernel(RET_CTX_SEQS, CTX_L, False, True, lg_ref, q_ref, k_ref, v_ref, gf_ref, gb_ref,
                             y_ref, sf_ref, sb_ref, *scratch)

        @pl.when(jnp.logical_not(is_ctx))
        def _():
            _ret_scan_kernel(1, LAT_L, True, False, lg_ref, q_ref, k_ref, v_ref, gf_ref, gb_ref,
                             s0f_ref, s0b_ref, y_ref, *scratch)

    return pl.pallas_call(
        body,
        grid=(T_ALL // RET_BLOCK, RH),
        in_specs=[
            pl.BlockSpec(memory_space=pltpu.SMEM),
            pl.BlockSpec((RET_BLOCK, RDK), lambda b, h: (b, h)),
            pl.BlockSpec((RET_BLOCK, RDK), lambda b, h: (b, n_qk + h)),
            pl.BlockSpec((RET_BLOCK, RDV), lambda b, h: (b, n_qk + h)),
            pl.BlockSpec((RET_BLOCK, RDV), lambda b, h: (b, n_qk + RH + h)),
            pl.BlockSpec((RET_BLOCK, RDV), lambda b, h: (b, n_qk + 2 * RH + h)),
            s0_spec, s0_spec,
        ],
        out_specs=[pl.BlockSpec((RET_BLOCK, RDV), lambda b, h: (b, h)), so_spec, so_spec],
        out_shape=[jax.ShapeDtypeStruct((T_ALL, RH * RDV), BF16), so_shape, so_shape],
        scratch_shapes=[pltpu.VMEM((RET_BLOCK, RDV), F32), pltpu.VMEM((RDK, RDV), F32),
                        pltpu.VMEM((RDK, RDV), F32),
                        pltpu.VMEM((LAT_L // RC, RC, RC), F32), pltpu.VMEM((LAT_L // RC, RDK, RC), F32)],
        compiler_params=_cparams(2),
        name="ret_scan",
    )(lg, qkvg, qkvg, qkvg, qkvg, qkvg, s0f, s0b)


RES_TM = MOD_TM
RES_MC = 256


def _res_kernel(gate_k, split_x, next_mod, a_ref, w_ref, *rest):
    rest = list(rest)
    wb_ref = rest.pop()
    x_refs = [rest.pop(0), rest.pop(0)] if split_x else [rest.pop(0)]
    mod_ref = rest.pop(0)
    gain_ref = rest.pop(0)
    if next_mod is not None:
        nmod_ref = rest.pop(0)
        xo_ref, h_ref = rest
    else:
        yc_ref, yl_ref = rest
    i = pl.program_id(0)
    is_ctx = i < N_CTX_BLK
    gate = mod_ref[0, gate_k:gate_k + 1, :]

    @pl.when(i == 0)
    def _():
        for r in range(0, wb_ref.shape[0], RES_MC):
            wb_ref[r:r + RES_MC, :] = w_ref[0, r:r + RES_MC, :].astype(BF16)

    def run(x_ref, y_ref):
        for m in range(RES_TM // RES_MC):
            rows = slice(m * RES_MC, (m + 1) * RES_MC)
            acc = jnp.dot(a_ref[rows, :], wb_ref[...], preferred_element_type=F32)
            xn = x_ref[rows, :] + gate * acc
            if next_mod is not None:
                xo_ref[rows, :] = xn
                h_ref[rows, :] = _modulate(xn, gain_ref, nmod_ref, *next_mod).astype(BF16)
            else:
                ms = jnp.mean(xn * xn, axis=-1, keepdims=True)
                y_ref[rows, :] = xn * lax.rsqrt(ms + EPS) * gain_ref[...]

    if not split_x and next_mod is not None:
        run(x_refs[0], None)
        return

    @pl.when(is_ctx)
    def _():
        run(x_refs[0], None if next_mod is not None else yc_ref)

    @pl.when(jnp.logical_not(is_ctx))
    def _():
        run(x_refs[-1], None if next_mod is not None else yl_ref)


def _res_matmul(a, w, layer, x, mods_l, gate_k, gain, next_mods=None, next_mod=None):
    kdim = a.shape[1]
    split_x = isinstance(x, tuple)
    row_spec = pl.BlockSpec((RES_TM, D), lambda i: (i, 0))
    ctx_spec = pl.BlockSpec((RES_TM, D), lambda i: (_ctx_blk(i), 0))
    lat_spec = pl.BlockSpec((RES_TM, D), lambda i: (_lat_blk(i), 0))
    mod_spec = pl.BlockSpec((1, N_MOD, D), lambda i: (_mod_row(i, RES_TM), 0, 0))
    in_specs = [pl.BlockSpec((RES_TM, kdim), lambda i: (i, 0)),
                pl.BlockSpec((1, kdim, D), lambda i: (layer, 0, 0), pipeline_mode=pl.Buffered(1))]
    args = [a, w]
    if split_x:
        in_specs += [ctx_spec, lat_spec]
        args += list(x)
    else:
        in_specs.append(row_spec)
        args.append(x)
    in_specs += [mod_spec, pl.BlockSpec((1, D), lambda i: (0, 0))]
    args += [mods_l, gain]
    if next_mod is not None:
        in_specs.append(mod_spec)
        args.append(next_mods)
        out_specs = [row_spec, row_spec]
        out_shape = [jax.ShapeDtypeStruct((T_ALL, D), F32), jax.ShapeDtypeStruct((T_ALL, D), BF16)]
    else:
        out_specs = [ctx_spec, lat_spec]
        out_shape = [jax.ShapeDtypeStruct((T_CTX, D), F32), jax.ShapeDtypeStruct((T_LAT, D), F32)]
    return pl.pallas_call(
        functools.partial(_res_kernel, gate_k, split_x, next_mod),
        grid=(T_ALL // RES_TM,),
        in_specs=in_specs,
        out_specs=out_specs,
        out_shape=out_shape,
        scratch_shapes=[pltpu.VMEM((kdim, D), BF16)],
        compiler_params=_cparams(1),
        name="res_matmul_k%d%s" % (kdim, "_final" if next_mod is None else ""),
    )(*args)


FFN_TM = 2048
FFN_MC = 512
FFN_TN = 256
FFN_HALO = 16
FFN_TILES = D_FF // FFN_TN
FFN_EXT = FFN_MC + (FFN_MC // CTX_L + 1) * FFN_HALO


def _ffn_up_kernel(h_ref, wu_ref, wg_ref, cwu_ref, cwg_ref, cbu_ref, cbg_ref, o_ref, a_scr, t_scr):
    is_ctx = pl.program_id(0) * FFN_TM < T_CTX
    zeros = jnp.zeros((FFN_HALO, D), BF16)
    n_slab = FFN_TN // 128

    def chunk(c, pieces, segments):
        slot = c % 2
        out_r0 = segments[0][1]
        out_n = sum(n_tok for _, _, n_tok in segments)
        hm = jnp.concatenate(pieces, axis=0)
        n_ext = hm.shape[0]
        for k, w_ref in enumerate((wu_ref, wg_ref)):
            a = jnp.dot(hm, w_ref[0], preferred_element_type=F32)
            for s in range(n_slab):
                a_scr[slot, k * n_slab + s, 0:n_ext, :] = a[:, s * 128:(s + 1) * 128]

        def conv(slab, base, half, cw_ref, cb_ref, lanes):
            def rows(off):
                return a_scr[slot, slab, pl.ds(base + off, half, stride=2), :]
            w0, w1, w2 = cw_ref[0, 0:1, lanes], cw_ref[0, 1:2, lanes], cw_ref[0, 2:3, lanes]
            b = cb_ref[0, :, lanes]
            om1, ev, od, ep1 = rows(-1), rows(0), rows(1), rows(2)
            return om1 * w0 + ev * w1 + od * w2 + b, ev * w0 + od * w1 + ep1 * w2 + b

        for s in range(n_slab):
            lanes = slice(s * 128, (s + 1) * 128)
            for base, r0, n_tok in segments:
                half = n_tok // 2
                t0 = r0 - out_r0
                u_ev, u_od = conv(s, base, half, cwu_ref, cbu_ref, lanes)
                g_ev, g_od = conv(n_slab + s, base, half, cwg_ref, cbg_ref, lanes)
                t_scr[slot, s, pl.ds(t0, half, stride=2), :] = _silu(g_ev) * u_ev
                t_scr[slot, s, pl.ds(t0 + 1, half, stride=2), :] = _silu(g_od) * u_od
            o_ref[out_r0:out_r0 + out_n, lanes] = t_scr[slot, s, 0:out_n, :].astype(BF16)

    sizes = [FFN_MC] * (FFN_TM // FFN_MC)
    starts = [sum(sizes[:c]) for c in range(len(sizes))]

    @pl.when(is_ctx)
    def _():
        for c, (c0, mc) in enumerate(zip(starts, sizes)):
            pieces, segments = [zeros], []
            for q in range(mc // CTX_L):
                r0 = c0 + q * CTX_L
                segments.append((FFN_HALO + q * (CTX_L + FFN_HALO), r0, CTX_L))
                pieces += [h_ref[r0:r0 + CTX_L, :], zeros]
            chunk(c, pieces, segments)

    @pl.when(jnp.logical_not(is_ctx))
    def _():
        for c, (r0, mc) in enumerate(zip(starts, sizes)):
            lo = zeros if r0 % LAT_L == 0 else h_ref[r0 - FFN_HALO:r0, :]
            hi = zeros if (r0 + mc) % LAT_L == 0 else h_ref[r0 + mc:r0 + mc + FFN_HALO, :]
            chunk(c, [lo, h_ref[r0:r0 + mc, :], hi], [(FFN_HALO, r0, mc)])


def _ffn_up(h, w_in, conv_w, conv_b, layer):
    n_slab = FFN_TN // 128
    return pl.pallas_call(
        _ffn_up_kernel,
        grid=(T_ALL // FFN_TM, FFN_TILES),
        in_specs=[
            pl.BlockSpec((FFN_TM, D), lambda i, j: (i, 0)),
            pl.BlockSpec((1, D, FFN_TN), lambda i, j: (layer, 0, j)),
            pl.BlockSpec((1, D, FFN_TN), lambda i, j: (layer, 0, FFN_TILES + j)),
            pl.BlockSpec((1, 3, FFN_TN), lambda i, j: (layer, 0, j)),
            pl.BlockSpec((1, 3, FFN_TN), lambda i, j: (layer, 0, FFN_TILES + j)),
            pl.BlockSpec((1, 1, FFN_TN), lambda i, j: (layer, 0, j)),
            pl.BlockSpec((1, 1, FFN_TN), lambda i, j: (layer, 0, FFN_TILES + j)),
        ],
        out_specs=pl.BlockSpec((FFN_TM, FFN_TN), lambda i, j: (i, j)),
        out_shape=jax.ShapeDtypeStruct((T_ALL, D_FF), BF16),
        scratch_shapes=[
            pltpu.VMEM((2, 2 * n_slab, FFN_EXT, 128), F32),
            pltpu.VMEM((2, n_slab, FFN_MC, 128), F32),
        ],
        compiler_params=_cparams(2),
        name="ffn_up",
    )(h, w_in, w_in, conv_w, conv_w, conv_b, conv_b)


ATT_NQ = AH * AHD
ATT_NKV = AKV * AHD
ATT_DUP = 2 * ATT_NKV
ATT_N = ATT_NQ + 2 * ATT_DUP
ATT_TN = ATT_NQ
assert 2 * ATT_DUP == ATT_TN


def _dup_heads(w):
    lead = w.shape[:-1]
    w = w.reshape(lead + (AKV, AHD))
    return jnp.concatenate([w, w], axis=-1).reshape(lead + (ATT_DUP,))


def _attn_proj_kernel(h_ref, w_ref, cos_ref, slo_ref, shi_ref, q_ref, kd_ref, vd_ref, k32_ref, v32_ref):
    i = pl.program_id(0)
    j = pl.program_id(1)
    is_lat = i * PROJ_TM >= T_CTX

    def chunks(epilogue):
        for m in range(PROJ_TM // PROJ_MC):
            rows = slice(m * PROJ_MC, (m + 1) * PROJ_MC)
            epilogue(jnp.dot(h_ref[rows, :], w_ref[...], preferred_element_type=F32), rows)

    def rope_into(o_ref, acc, rows, c0, n_cols):
        for s in range(n_cols // 128):
            y = _rope_slab(acc[:, c0 + s * 128:c0 + (s + 1) * 128],
                           cos_ref[rows, :], slo_ref[rows, :], shi_ref[rows, :], AHD // 4)
            o_ref[rows, s * 128:(s + 1) * 128] = y.astype(BF16)

    def q_lat(acc, rows):
        rope_into(q_ref, acc, rows, 0, ATT_NQ)

    def q_ctx(acc, rows):
        q_ref[rows, :] = acc.astype(BF16)

    def kv_lat(acc, rows):
        rope_into(kd_ref, acc, rows, 0, ATT_DUP)
        vd_ref[rows, :] = acc[:, ATT_DUP:].astype(BF16)

    def undup_into(o_ref, acc, rows, c0):
        low = lax.broadcasted_iota(jnp.int32, (PROJ_MC, 128), 1) < AHD
        for s in range(ATT_NKV // 128):
            a = acc[:, c0 + 2 * s * 128:c0 + (2 * s + 1) * 128]
            b = acc[:, c0 + (2 * s + 1) * 128:c0 + (2 * s + 2) * 128]
            o_ref[rows, s * 128:(s + 1) * 128] = jnp.where(low, a, b)

    def kv_ctx(acc, rows):
        kd_ref[rows, :] = acc[:, :ATT_DUP].astype(BF16)
        vd_ref[rows, :] = acc[:, ATT_DUP:].astype(BF16)
        undup_into(k32_ref, acc, rows, 0)
        undup_into(v32_ref, acc, rows, ATT_DUP)

    for on_q, on_lat, epilogue in ((True, True, q_lat), (True, False, q_ctx),
                                   (False, True, kv_lat), (False, False, kv_ctx)):
        @pl.when(jnp.logical_and((j == 0) == on_q, is_lat == on_lat))
        def _():
            chunks(epilogue)


def _attn_proj(h, w_all, tabs):
    cos, slo, shi = tabs
    n_pos = LAT_L // PROJ_TM
    n_ctx = T_CTX // PROJ_TM
    tab_spec = pl.BlockSpec((PROJ_TM, 128), lambda i, j: (i % n_pos, 0))
    row_spec = pl.BlockSpec((PROJ_TM, ATT_DUP), lambda i, j: (i, 0))
    ctx_spec = pl.BlockSpec((PROJ_TM, ATT_NKV), lambda i, j: (jnp.minimum(i, n_ctx - 1), 0))
    return pl.pallas_call(
        _attn_proj_kernel,
        grid=(T_ALL // PROJ_TM, ATT_N // ATT_TN),
        in_specs=[
            pl.BlockSpec((PROJ_TM, D), lambda i, j: (i, 0)),
            pl.BlockSpec((D, ATT_TN), lambda i, j: (0, j)),
            tab_spec, tab_spec, tab_spec,
        ],
        out_specs=[
            pl.BlockSpec((PROJ_TM, ATT_NQ), lambda i, j: (i, 0)),
            row_spec, row_spec, ctx_spec, ctx_spec,
        ],
        out_shape=[
            jax.ShapeDtypeStruct((T_ALL, ATT_NQ), BF16),
            jax.ShapeDtypeStruct((T_ALL, ATT_DUP), BF16),
            jax.ShapeDtypeStruct((T_ALL, ATT_DUP), BF16),
            jax.ShapeDtypeStruct((T_CTX, ATT_NKV), F32),
            jax.ShapeDtypeStruct((T_CTX, ATT_NKV), F32),
        ],
        compiler_params=_cparams(2),
        name="attn_proj",
    )(h, w_all, cos, slo, shi)


def _head_scores(q_ref, kh, k_dup):
    nq = q_ref.shape[0]
    nk = k_dup.shape[0]
    low = lax.broadcasted_iota(jnp.int32, (nk, 128), 1) < AHD
    zero = jnp.zeros_like(k_dup)
    k2 = jnp.concatenate([jnp.where(low, k_dup, zero), jnp.where(low, zero, k_dup)], axis=0)
    base = kh * AG * AHD
    lhs = jnp.concatenate([q_ref[:, base:base + 128], q_ref[:, base + 128:base + 256]], axis=0)
    s = lax.dot_general(lhs, k2, (((1,), (1,)), ((), ())), preferred_element_type=F32)
    return [s[0:nq, 0:nk], s[0:nq, nk:], s[nq:, 0:nk], s[nq:, nk:]]


def _row_max(s):
    m = s[:, 0:128]
    for c in range(128, s.shape[1], 128):
        m = jnp.maximum(m, s[:, c:c + 128])
    return jnp.max(m, axis=-1, keepdims=True)


def _softmax_pv_store(scores, v_dup, sink_ref, kh, o_ref):
    nq = scores[0].shape[0]
    nk = v_dup.shape[0]
    v_aug = jnp.concatenate([v_dup, jnp.ones((nk, 128), BF16)], axis=1)
    ps, tails = [], []
    for g, s in enumerate(scores):
        sink = sink_ref[kh * AG + g]
        m = jnp.maximum(_row_max(s), sink)
        ps.append(jnp.exp(s - m).astype(BF16))
        tails.append(jnp.exp(sink - m))
    pv = jnp.dot(jnp.concatenate(ps, axis=0), v_aug, preferred_element_type=F32)
    outs = []
    for g in range(AG):
        rows = slice(g * nq, (g + 1) * nq)
        outs.append(pv[rows, 0:128] / (pv[rows, 128:256] + tails[g]))
    low = lax.broadcasted_iota(jnp.int32, (nq, 128), 1) < AHD
    base = kh * AG * AHD
    o_ref[:, base:base + 128] = jnp.where(low, outs[0], outs[1]).astype(BF16)
    o_ref[:, base + 128:base + 256] = jnp.where(low, outs[2], outs[3]).astype(BF16)


def _attend_all(q_ref, sink_ref, o_ref, kv_of, bias_of):
    def scores_of(kh):
        k_dup, _ = kv_of(kh)
        return bias_of(_head_scores(q_ref, kh, k_dup))

    nxt = scores_of(0)
    for kh in range(AKV):
        cur = nxt
        if kh + 1 < AKV:
            nxt = scores_of(kh + 1)
        _softmax_pv_store(cur, kv_of(kh)[1], sink_ref, kh, o_ref)


def _ctx_attn_kernel(sink_ref, q_ref, kd_ref, vd_ref, o_ref):
    def kv_of(kh):
        cols = slice(kh * 128, (kh + 1) * 128)
        return kd_ref[:, cols], vd_ref[:, cols]

    _attend_all(q_ref, sink_ref, o_ref, kv_of, lambda scores: scores)


BAND = ABLK + 2 * WINDOW
ATT_CTX_STEPS = T_CTX // ABLK
ATT_LAT_QBLK = LAT_L // ABLK


def _lat_attn_kernel(sink_ref, q_ref, kd_ref, vd_ref, ck_ref, cv_ref, o_ref):
    n = (pl.program_id(0) - ATT_CTX_STEPS) % ATT_LAT_QBLK
    start = pl.multiple_of(jnp.clip(n * ABLK - WINDOW, 0, LAT_L - BAND), WINDOW)
    qpos = n * ABLK + lax.broadcasted_iota(jnp.int32, (ABLK, BAND), 0)
    kpos = start + lax.broadcasted_iota(jnp.int32, (ABLK, BAND), 1)
    band_bias = jnp.where(jnp.abs(qpos - kpos) <= WINDOW, 0.0, NEG_INF)

    def kv_of(kh):
        cols = slice(kh * 128, (kh + 1) * 128)
        k = jnp.concatenate([kd_ref[pl.ds(start, BAND), cols], ck_ref[0, :, cols]], axis=0)
        v = jnp.concatenate([vd_ref[pl.ds(start, BAND), cols], cv_ref[0, :, cols]], axis=0)
        return k, v

    def bias_of(scores):
        return [jnp.concatenate([s[:, :BAND] + band_bias, s[:, BAND:]], axis=1) for s in scores]

    _attend_all(q_ref, sink_ref, o_ref, kv_of, bias_of)


def _attention(sink, q, kd, vd, cache_kd, cache_vd):
    def lat_seq(t):
        return jnp.maximum(t - ATT_CTX_STEPS, 0) // ATT_LAT_QBLK

    row_spec = pl.BlockSpec((ABLK, ATT_NQ), lambda t: (t, 0))
    ctx_spec = pl.BlockSpec((CTX_L, ATT_DUP), lambda t: (jnp.minimum(t // (CTX_L // ABLK), CTX_B - 1), 0))
    seq_spec = pl.BlockSpec((LAT_L, ATT_DUP), lambda t: (T_CTX // LAT_L + lat_seq(t), 0))
    cache_spec = pl.BlockSpec((1, CTX_L, ATT_DUP), lambda t: (lat_seq(t), 0, 0))

    def body(sink_ref, q_ref, kdc_ref, vdc_ref, kdl_ref, vdl_ref, ck_ref, cv_ref, o_ref):
        is_ctx = pl.program_id(0) < ATT_CTX_STEPS

        @pl.when(is_ctx)
        def _():
            _ctx_attn_kernel(sink_ref, q_ref, kdc_ref, vdc_ref, o_ref)

        @pl.when(jnp.logical_not(is_ctx))
        def _():
            _lat_attn_kernel(sink_ref, q_ref, kdl_ref, vdl_ref, ck_ref, cv_ref, o_ref)

    return pl.pallas_call(
        body,
        grid=(T_ALL // ABLK,),
        in_specs=[pl.BlockSpec(memory_space=pltpu.SMEM), row_spec,
                  ctx_spec, ctx_spec, seq_spec, seq_spec, cache_spec, cache_spec],
        out_specs=row_spec,
        out_shape=jax.ShapeDtypeStruct((T_ALL, ATT_NQ), BF16),
        compiler_params=_cparams(1),
        name="attention",
    )(sink, q, kd, vd, kd, vd, cache_kd, cache_vd)


def _log_sigmoid(x):
    return -(jnp.maximum(-x, 0.0) + jnp.log1p(jnp.exp(-jnp.abs(x))))


def kernel(x_prompt, x_sample, state_ret_fwd, state_ret_bwd, cache_attn_k, cache_attn_v, c, c_ctx, ada_w, ada_b, norm_mix, norm_ffn, ret_wq, ret_wk, ret_wv, ret_wg_fwd, ret_wg_bwd, ret_wo, ret_decay_fwd, ret_decay_bwd, attn_wq, attn_wk, attn_wv, attn_wo, attn_sink, ffn_w_in, ffn_conv_w, ffn_conv_b, ffn_w_out, norm_final):
    assert x_prompt.shape == (CTX_B, CTX_L, D) and x_sample.shape == (LAT_B, LAT_L, D)
    assert state_ret_fwd.shape == state_ret_bwd.shape == (LAT_B, 1, RH, RDK, RDV)
    assert cache_attn_k.shape == cache_attn_v.shape == (LAT_B, 1, CTX_L, AKV, AHD)
    assert ada_w.shape == (2, D, N_MOD * D) and ffn_w_in.shape == (2, D, 2 * D_FF)
    assert all(t.dtype == F32 for t in (x_prompt, x_sample, state_ret_fwd, cache_attn_k, ada_w, ffn_w_in))
    x_ctx = x_prompt.reshape(T_CTX, D)
    x_lat = x_sample.reshape(T_LAT, D)

    cv = jnp.concatenate([c, c_ctx[None, :], jnp.zeros((MOD_ROWS - LAT_B - 1, D), F32)], axis=0)
    mods = _adaln(cv, ada_w, ada_b).reshape(ada_w.shape[0], MOD_ROWS, N_MOD, D)

    h = _first_mod(x_ctx, x_lat, norm_mix[0:1], mods[0])
    w_ret = jnp.concatenate([ret_wq[0], ret_wk[0] * (RDK ** -0.5), ret_wv[0], ret_wg_fwd[0], ret_wg_bwd[0]],
                            axis=1).astype(BF16)
    qkvg = _ret_proj(h, w_ret, _rope_tables(RDK))
    lg = jnp.stack([_log_sigmoid(ret_decay_fwd[0].astype(F32)), _log_sigmoid(ret_decay_bwd[0].astype(F32))])
    y0, s_f, s_b = _ret_scan(qkvg, lg, state_ret_fwd, state_ret_bwd)
    w_in = ffn_w_in.astype(BF16)
    w_out = ffn_w_out
    conv_b = ffn_conv_b[:, None, :]
    x, h = _res_matmul(y0, ret_wo, 0, (x_ctx, x_lat), mods[0], 2,
                       norm_ffn[0:1], mods[0], (3, 4))
    act = _ffn_up(h, w_in, ffn_conv_w, conv_b, 0)
    x, h = _res_matmul(act, w_out, 0, x, mods[0], 5, norm_mix[1:2], mods[1], (0, 1))

    w_att = jnp.concatenate([attn_wq[0] * (AHD ** -0.5), _dup_heads(attn_wk[0]), _dup_heads(attn_wv[0])],
                            axis=1).astype(BF16)
    q, kd, vd, k32, v32 = _attn_proj(h, w_att, _rope_tables(AHD))
    sink = attn_sink[0].astype(F32)
    cache_kd = _dup_heads(cache_attn_k[:, 0].reshape(LAT_B, CTX_L, ATT_NKV)).astype(BF16)
    cache_vd = _dup_heads(cache_attn_v[:, 0].reshape(LAT_B, CTX_L, ATT_NKV)).astype(BF16)
    a = _attention(sink, q, kd, vd, cache_kd, cache_vd)
    x, h = _res_matmul(a, attn_wo, 0, x, mods[1], 2, norm_ffn[1:2], mods[1], (3, 4))
    act = _ffn_up(h, w_in, ffn_conv_w, conv_b, 1)
    y_ctx, y_lat = _res_matmul(act, w_out, 1, x, mods[1], 5, norm_final[None, :])

    y_prompt = y_ctx.reshape(CTX_B, CTX_L, D)
    y_sample = y_lat.reshape(LAT_B, LAT_L, D)
    new_k = k32.reshape(CTX_B, 1, CTX_L, AKV, AHD)
    new_v = v32.reshape(CTX_B, 1, CTX_L, AKV, AHD)
    return y_prompt, y_sample, s_f, s_b, new_k, new_v
```

```python
import functools

import jax
import jax.numpy as jnp
import numpy as np
from jax import lax
from jax.experimental import pallas as pl
from jax.experimental.pallas import tpu as pltpu

D = 1024
CTX_B, CTX_L = 32, 256
LAT_B, LAT_L = 8, 2048
T_CTX = CTX_B * CTX_L
T_LAT = LAT_B * LAT_L
T_ALL = T_CTX + T_LAT
GRID_W = 64
N_MOD = 6
MOD_ROWS = 16
CTX_MOD_ROW = LAT_B
RH, RDK, RDV = 4, 256, 512
RC = 256
AH, AKV, AG, AHD = 16, 4, 4, 64
WINDOW = 128
ABLK = 256
D_FF = 2816
ROPE_BASE = 10000.0
EPS = 1e-6
NEG_INF = -1e30

BF16 = jnp.bfloat16
F32 = jnp.float32

VMEM_LIMIT_BYTES = 56 * 1024 * 1024
NORM_ROWS = 32


def _cparams(n_axes):
    return pltpu.CompilerParams(dimension_semantics=("arbitrary",) * n_axes,
                                vmem_limit_bytes=VMEM_LIMIT_BYTES)


def _silu(x):
    return x / (1.0 + jnp.exp(-x))


def _mod_row(i, tm):
    r0 = i * tm
    return jnp.where(r0 < T_CTX, CTX_MOD_ROW, (r0 - T_CTX) // LAT_L)


def _modulate(x, g_ref, mod_ref, shift_k, scale_k):
    ms = jnp.mean(x * x, axis=-1, keepdims=True)
    y = x * lax.rsqrt(ms + EPS) * g_ref[...]
    return y * (1.0 + mod_ref[0, scale_k:scale_k + 1, :]) + mod_ref[0, shift_k:shift_k + 1, :]


MOD_TM = 1024
N_CTX_BLK = T_CTX // MOD_TM


def _ctx_blk(i):
    return jnp.minimum(i, N_CTX_BLK - 1)


def _lat_blk(i):
    return jnp.maximum(i - N_CTX_BLK, 0)


def _first_mod_kernel(xc_ref, xl_ref, g_ref, mod_ref, h_ref):
    i = pl.program_id(0)

    def run(x_ref):
        for r in range(0, MOD_TM, NORM_ROWS):
            rows = slice(r, r + NORM_ROWS)
            h_ref[rows, :] = _modulate(x_ref[rows, :], g_ref, mod_ref, 0, 1).astype(BF16)

    @pl.when(i < N_CTX_BLK)
    def _():
        run(xc_ref)

    @pl.when(i >= N_CTX_BLK)
    def _():
        run(xl_ref)


def _first_mod(x_ctx, x_lat, gain, mods_l):
    return pl.pallas_call(
        _first_mod_kernel,
        grid=(T_ALL // MOD_TM,),
        in_specs=[
            pl.BlockSpec((MOD_TM, D), lambda i: (_ctx_blk(i), 0)),
            pl.BlockSpec((MOD_TM, D), lambda i: (_lat_blk(i), 0)),
            pl.BlockSpec((1, D), lambda i: (0, 0)),
            pl.BlockSpec((1, N_MOD, D), lambda i: (_mod_row(i, MOD_TM), 0, 0)),
        ],
        out_specs=pl.BlockSpec((MOD_TM, D), lambda i: (i, 0)),
        out_shape=jax.ShapeDtypeStruct((T_ALL, D), BF16),
        compiler_params=_cparams(1),
        name="first_mod",
    )(x_ctx, x_lat, gain, mods_l)


ADA_TN = 1024


def _adaln_kernel(cv_ref, w_ref, b_ref, o_ref):
    s = _silu(cv_ref[...]).astype(BF16)
    w = w_ref[0].astype(BF16)
    o_ref[0] = jnp.dot(s, w, preferred_element_type=F32) + b_ref[0]


def _adaln(cv, ada_w, ada_b):
    depth = ada_w.shape[0]
    n = N_MOD * D
    return pl.pallas_call(
        _adaln_kernel,
        grid=(depth, n // ADA_TN),
        in_specs=[
            pl.BlockSpec((MOD_ROWS, D), lambda l, j: (0, 0)),
            pl.BlockSpec((1, D, ADA_TN), lambda l, j: (l, 0, j)),
            pl.BlockSpec((1, 1, ADA_TN), lambda l, j: (l, 0, j)),
        ],
        out_specs=pl.BlockSpec((1, MOD_ROWS, ADA_TN), lambda l, j: (l, 0, j)),
        out_shape=jax.ShapeDtypeStruct((depth, MOD_ROWS, n), F32),
        compiler_params=_cparams(2),
        name="adaln",
    )(cv, ada_w, ada_b.reshape(depth, 1, n))


def _rope_tables(head_dim):
    half = head_dim // 2
    n_freq = half // 2
    inv = (np.float32(ROPE_BASE) ** (-np.arange(n_freq, dtype=np.float32) / np.float32(n_freq))).astype(np.float32)
    t = np.arange(LAT_L)
    rows = (t // GRID_W).astype(np.float32)[:, None] * inv[None, :]
    cols = (t % GRID_W).astype(np.float32)[:, None] * inv[None, :]
    zero = np.zeros_like(rows)
    cos = np.concatenate([np.cos(rows)] * 2 + [np.cos(cols)] * 2, axis=-1)
    sin_lo = np.concatenate([-np.sin(rows), zero, -np.sin(cols), zero], axis=-1)
    sin_hi = np.concatenate([zero, np.sin(rows), zero, np.sin(cols)], axis=-1)
    return tuple(np.tile(x.astype(np.float32), (1, max(1, 128 // head_dim))) for x in (cos, sin_lo, sin_hi))


def _rope_slab(x, cos, sin_lo, sin_hi, n_freq):
    up = pltpu.roll(x, 128 - n_freq, 1)
    dn = pltpu.roll(x, n_freq, 1)
    return x * cos + up * sin_lo + dn * sin_hi


PROJ_TM = 1024
PROJ_MC = 256
RET_NQK = 2 * RH * RDK
RET_N = RET_NQK + 3 * RH * RDV
RET_TN = RET_N // 2


def _ret_proj_kernel(h_ref, w_ref, cos_ref, slo_ref, shi_ref, o_ref):
    i = pl.program_id(0)
    j = pl.program_id(1)
    is_lat = i * PROJ_TM >= T_CTX

    def chunks(epilogue):
        for m in range(PROJ_TM // PROJ_MC):
            rows = slice(m * PROJ_MC, (m + 1) * PROJ_MC)
            acc = jnp.dot(h_ref[rows, :], w_ref[...], preferred_element_type=F32)
            epilogue(acc, rows)

    def plain(acc, rows):
        o_ref[rows, :] = acc.astype(BF16)

    def gate(acc, rows):
        o_ref[rows, :] = _silu(acc).astype(BF16)

    def rope_qk(acc, rows):
        for s in range(RET_NQK // 128):
            cols = slice(s * 128, (s + 1) * 128)
            tcols = slice((s % 2) * 128, (s % 2 + 1) * 128)
            y = _rope_slab(acc[:, cols], cos_ref[rows, tcols], slo_ref[rows, tcols], shi_ref[rows, tcols], RDK // 4)
            o_ref[rows, cols] = y.astype(BF16)
        o_ref[rows, RET_NQK:] = acc[:, RET_NQK:].astype(BF16)

    @pl.when(jnp.logical_and(j == 0, is_lat))
    def _():
        chunks(rope_qk)

    @pl.when(jnp.logical_and(j == 0, jnp.logical_not(is_lat)))
    def _():
        chunks(plain)

    @pl.when(j == 1)
    def _():
        chunks(gate)


def _ret_proj(h, w_all, tabs):
    cos, slo, shi = tabs
    n_pos = LAT_L // PROJ_TM
    tab_spec = pl.BlockSpec((PROJ_TM, RDK), lambda i, j: (i % n_pos, 0))
    return pl.pallas_call(
        _ret_proj_kernel,
        grid=(T_ALL // PROJ_TM, RET_N // RET_TN),
        in_specs=[
            pl.BlockSpec((PROJ_TM, D), lambda i, j: (i, 0)),
            pl.BlockSpec((D, RET_TN), lambda i, j: (0, j)),
            tab_spec, tab_spec, tab_spec,
        ],
        out_specs=pl.BlockSpec((PROJ_TM, RET_TN), lambda i, j: (i, j)),
        out_shape=jax.ShapeDtypeStruct((T_ALL, RET_N), BF16),
        compiler_params=_cparams(2),
        name="ret_proj",
    )(h, w_all, cos, slo, shi)


RET_BLOCK = 2048


def _ret_scan_kernel(n_seq, seq_len, has_init, emit_state, lg_ref, q_ref, k_ref, v_ref, gf_ref, gb_ref, *rest):
    rest = list(rest)
    s0f_ref = rest.pop(0) if has_init else None
    s0b_ref = rest.pop(0) if has_init else None
    y_ref = rest.pop(0)
    sf_ref = rest.pop(0) if emit_state else None
    sb_ref = rest.pop(0) if emit_state else None
    ybuf_ref, stf_ref, stb_ref, sbuf_ref, ktbuf_ref = rest

    hd = pl.program_id(1)
    n_chunks = seq_len // RC
    idx_col = lax.broadcasted_iota(jnp.int32, (RC, 1), 0).astype(F32)
    idx_row = lax.broadcasted_iota(jnp.int32, (1, RC), 1).astype(F32)
    diff = idx_col - idx_row

    lg_f = lg_ref[0, hd]
    lg_b = lg_ref[1, hd]
    dirs = {
        "f": dict(dmat=jnp.where(diff >= 0, jnp.exp(lg_f * jnp.maximum(diff, 0.0)), 0.0),
                  xi=jnp.exp(lg_f * (idx_col + 1.0)), zeta=jnp.exp(lg_f * (RC - 1.0 - idx_row)),
                  gch=jnp.exp(jnp.ones((1, RDV), F32) * (lg_f * RC)),
                  st_ref=stf_ref, s0_ref=s0f_ref, gate_ref=gf_ref, out_ref=sf_ref),
        "b": dict(dmat=jnp.where(diff <= 0, jnp.exp(lg_b * jnp.maximum(-diff, 0.0)), 0.0),
                  xi=jnp.exp(lg_b * (RC - idx_col)), zeta=jnp.exp(lg_b * idx_row),
                  gch=jnp.exp(jnp.ones((1, RDV), F32) * (lg_b * RC)),
                  st_ref=stb_ref, s0_ref=s0b_ref, gate_ref=gb_ref, out_ref=sb_ref),
    }

    def shared(rows, c, seen):
        q = q_ref[rows, :]
        if c in seen:
            return q, sbuf_ref[c], ktbuf_ref[c], v_ref[rows, :]
        k = k_ref[rows, :]
        s = lax.dot_general(q, k, (((1,), (1,)), ((), ())), preferred_element_type=F32)
        kt = k.astype(F32).T
        if n_chunks > 1:
            sbuf_ref[c] = s
            ktbuf_ref[c] = kt
            seen.add(c)
        return q, s, kt, v_ref[rows, :]

    def chunk_dir(d, sh, rows, first, last, sq):
        q, s, kt, v = sh
        sm = (s * d["dmat"]).astype(BF16)
        kzt = (kt * d["zeta"]).astype(BF16)
        both = jnp.dot(jnp.concatenate([sm, kzt], axis=0), v, preferred_element_type=F32)
        o, upd = both[:RC], both[RC:]
        st = None
        if has_init and first:
            st = d["s0_ref"][0, 0, 0]
        elif not first:
            st = d["st_ref"][...]
        if st is not None:
            o = o + jnp.dot(q, st.astype(BF16), preferred_element_type=F32) * d["xi"]
            upd = st * d["gch"] + upd
        if last and emit_state:
            d["out_ref"][sq, 0, 0] = upd
        elif not last:
            d["st_ref"][...] = upd
        o = o * lax.rsqrt(jnp.mean(o * o, axis=-1, keepdims=True) + EPS)
        return o * d["gate_ref"][rows, :].astype(F32)

    for sq in range(n_seq):
        half_done = set()
        seen = set()

        def emit(c, val):
            rows = slice(sq * seq_len + c * RC, sq * seq_len + (c + 1) * RC)
            if c in half_done:
                y_ref[rows, :] = (ybuf_ref[rows, :] + val).astype(BF16)
            else:
                ybuf_ref[rows, :] = val
                half_done.add(c)

        for t in range(n_chunks):
            cf, cb = t, n_chunks - 1 - t
            rows_f = slice(sq * seq_len + cf * RC, sq * seq_len + (cf + 1) * RC)
            rows_b = slice(sq * seq_len + cb * RC, sq * seq_len + (cb + 1) * RC)
            sh_f = shared(rows_f, cf, seen)
            sh_b = sh_f if cb == cf else shared(rows_b, cb, seen)
            first, last = t == 0, t == n_chunks - 1
            of = chunk_dir(dirs["f"], sh_f, rows_f, first, last, sq)
            ob = chunk_dir(dirs["b"], sh_b, rows_b, first, last, sq)
            if cf == cb:
                y_ref[rows_f, :] = (of + ob).astype(BF16)
            else:
                emit(cf, of)
                emit(cb, ob)


RET_CTX_BLK = T_CTX // RET_BLOCK
RET_CTX_SEQS = RET_BLOCK // CTX_L


def _ret_scan(qkvg, lg, s0f, s0b):
    n_qk = RH
    ctx_last = RET_CTX_BLK - 1

    def lat_b(b):
        return jnp.maximum(b - RET_CTX_BLK, 0)

    s0_spec = pl.BlockSpec((1, 1, 1, RDK, RDV), lambda b, h: (lat_b(b), 0, h, 0, 0))
    so_spec = pl.BlockSpec((RET_CTX_SEQS, 1, 1, RDK, RDV),
                           lambda b, h: (jnp.minimum(b, ctx_last), 0, jnp.where(b <= ctx_last, h, RH - 1), 0, 0))
    so_shape = jax.ShapeDtypeStruct((CTX_B, 1, RH, RDK, RDV), F32)

    def body(lg_ref, q_ref, k_ref, v_ref, gf_ref, gb_ref, s0f_ref, s0b_ref, y_ref, sf_ref, sb_ref, *scratch):
        is_ctx = pl.program_id(0) < RET_CTX_BLK

        @pl.when(is_ctx)
        def _():
            _ret_scan_kernel(RET_CTX_SEQS, CTX_L, False, True, lg_ref, q_ref, k_ref, v_ref, gf_ref, gb_ref,
                             y_ref, sf_ref, sb_ref, *scratch)

        @pl.when(jnp.logical_not(is_ctx))
        def _():
            _ret_scan_kernel(1, LAT_L, True, False, lg_ref, q_ref, k_ref, v_ref, gf_ref, gb_ref,
                             s0f_ref, s0b_ref, y_ref, *scratch)

    return pl.pallas_call(
        body,
        grid=(T_ALL // RET_BLOCK, RH),
        in_specs=[
            pl.BlockSpec(memory_space=pltpu.SMEM),
            pl.BlockSpec((RET_BLOCK, RDK), lambda b, h: (b, h)),
            pl.BlockSpec((RET_BLOCK, RDK), lambda b, h: (b, n_qk + h)),
            pl.BlockSpec((RET_BLOCK, RDV), lambda b, h: (b, n_qk + h)),
            pl.BlockSpec((RET_BLOCK, RDV), lambda b, h: (b, n_qk + RH + h)),
            pl.BlockSpec((RET_BLOCK, RDV), lambda b, h: (b, n_qk + 2 * RH + h)),
            s0_spec, s0_spec,
        ],
        out_specs=[pl.BlockSpec((RET_BLOCK, RDV), lambda b, h: (b, h)), so_spec, so_spec],
        out_shape=[jax.ShapeDtypeStruct((T_ALL, RH * RDV), BF16), so_shape, so_shape],
        scratch_shapes=[pltpu.VMEM((RET_BLOCK, RDV), F32), pltpu.VMEM((RDK, RDV), F32),
                        pltpu.VMEM((RDK, RDV), F32),
                        pltpu.VMEM((LAT_L // RC, RC, RC), F32), pltpu.VMEM((LAT_L // RC, RDK, RC), F32)],
        compiler_params=_cparams(2),
        name="ret_scan",
    )(lg, qkvg, qkvg, qkvg, qkvg, qkvg, s0f, s0b)


RES_TM = MOD_TM
RES_MC = 256


def _res_kernel(gate_k, split_x, next_mod, a_ref, w_ref, *rest):
    rest = list(rest)
    wb_ref = rest.pop()
    x_refs = [rest.pop(0), rest.pop(0)] if split_x else [rest.pop(0)]
    mod_ref = rest.pop(0)
    gain_ref = rest.pop(0)
    if next_mod is not None:
        nmod_ref = rest.pop(0)
        xo_ref, h_ref = rest
    else:
        yc_ref, yl_ref = rest
    i = pl.program_id(0)
    is_ctx = i < N_CTX_BLK
    gate = mod_ref[0, gate_k:gate_k + 1, :]

    @pl.when(i == 0)
    def _():
        for r in range(0, wb_ref.shape[0], RES_MC):
            wb_ref[r:r + RES_MC, :] = w_ref[0, r:r + RES_MC, :].astype(BF16)

    def run(x_ref, y_ref):
        for m in range(RES_TM // RES_MC):
            rows = slice(m * RES_MC, (m + 1) * RES_MC)
            acc = jnp.dot(a_ref[rows, :], wb_ref[...], preferred_element_type=F32)
            xn = x_ref[rows, :] + gate * acc
            if next_mod is not None:
                xo_ref[rows, :] = xn
                h_ref[rows, :] = _modulate(xn, gain_ref, nmod_ref, *next_mod).astype(BF16)
            else:
                ms = jnp.mean(xn * xn, axis=-1, keepdims=True)
                y_ref[rows, :] = xn * lax.rsqrt(ms + EPS) * gain_ref[...]

    if not split_x and next_mod is not None:
        run(x_refs[0], None)
        return

    @pl.when(is_ctx)
    def _():
        run(x_refs[0], None if next_mod is not None else yc_ref)

    @pl.when(jnp.logical_not(is_ctx))
    def _():
        run(x_refs[-1], None if next_mod is not None else yl_ref)


def _res_matmul(a, w, layer, x, mods_l, gate_k, gain, next_mods=None, next_mod=None):
    kdim = a.shape[1]
    split_x = isinstance(x, tuple)
    row_spec = pl.BlockSpec((RES_TM, D), lambda i: (i, 0))
    ctx_spec = pl.BlockSpec((RES_TM, D), lambda i: (_ctx_blk(i), 0))
    lat_spec = pl.BlockSpec((RES_TM, D), lambda i: (_lat_blk(i), 0))
    mod_spec = pl.BlockSpec((1, N_MOD, D), lambda i: (_mod_row(i, RES_TM), 0, 0))
    in_specs = [pl.BlockSpec((RES_TM, kdim), lambda i: (i, 0)),
                pl.BlockSpec((1, kdim, D), lambda i: (layer, 0, 0), pipeline_mode=pl.Buffered(1))]
    args = [a, w]
    if split_x:
        in_specs += [ctx_spec, lat_spec]
        args += list(x)
    else:
        in_specs.append(row_spec)
        args.append(x)
    in_specs += [mod_spec, pl.BlockSpec((1, D), lambda i: (0, 0))]
    args += [mods_l, gain]
    if next_mod is not None:
        in_specs.append(mod_spec)
        args.append(next_mods)
        out_specs = [row_spec, row_spec]
        out_shape = [jax.ShapeDtypeStruct((T_ALL, D), F32), jax.ShapeDtypeStruct((T_ALL, D), BF16)]
    else:
        out_specs = [ctx_spec, lat_spec]
        out_shape = [jax.ShapeDtypeStruct((T_CTX, D), F32), jax.ShapeDtypeStruct((T_LAT, D), F32)]
    return pl.pallas_call(
        functools.partial(_res_kernel, gate_k, split_x, next_mod),
        grid=(T_ALL // RES_TM,),
        in_specs=in_specs,
        out_specs=out_specs,
        out_shape=out_shape,
        scratch_shapes=[pltpu.VMEM((kdim, D), BF16)],
        compiler_params=_cparams(1),
        name="res_matmul_k%d%s" % (kdim, "_final" if next_mod is None else ""),
    )(*args)


FFN_TM = 2048
FFN_MC = 512
FFN_TN = 256
FFN_HALO = 16
FFN_TILES = D_FF // FFN_TN
FFN_EXT = FFN_MC + (FFN_MC // CTX_L + 1) * FFN_HALO


def _ffn_up_kernel(h_ref, wu_ref, wg_ref, cwu_ref, cwg_ref, cbu_ref, cbg_ref, o_ref, a_scr, t_scr):
    is_ctx = pl.program_id(0) * FFN_TM < T_CTX
    zeros = jnp.zeros((FFN_HALO, D), BF16)
    n_slab = FFN_TN // 128

    def weights():
        return [wu_ref[0].astype(BF16), wg_ref[0].astype(BF16)]

    def chunk(c, wts, pieces, segments):
        slot = c % 2
        out_r0 = segments[0][1]
        out_n = sum(n_tok for _, _, n_tok in segments)
        hm = jnp.concatenate(pieces, axis=0)
        n_ext = hm.shape[0]
        for k, w in enumerate(wts):
            a = jnp.dot(hm, w, preferred_element_type=F32)
            for s in range(n_slab):
                a_scr[slot, k * n_slab + s, 0:n_ext, :] = a[:, s * 128:(s + 1) * 128]

        def conv(slab, base, half, cw_ref, cb_ref, lanes):
            def rows(off):
                return a_scr[slot, slab, pl.ds(base + off, half, stride=2), :]
            w0, w1, w2 = cw_ref[0, 0:1, lanes], cw_ref[0, 1:2, lanes], cw_ref[0, 2:3, lanes]
            b = cb_ref[0, :, lanes]
            om1, ev, od, ep1 = rows(-1), rows(0), rows(1), rows(2)
            return om1 * w0 + ev * w1 + od * w2 + b, ev * w0 + od * w1 + ep1 * w2 + b

        for s in range(n_slab):
            lanes = slice(s * 128, (s + 1) * 128)
            for base, r0, n_tok in segments:
                half = n_tok // 2
                t0 = r0 - out_r0
                u_ev, u_od = conv(s, base, half, cwu_ref, cbu_ref, lanes)
                g_ev, g_od = conv(n_slab + s, base, half, cwg_ref, cbg_ref, lanes)
                t_scr[slot, s, pl.ds(t0, half, stride=2), :] = _silu(g_ev) * u_ev
                t_scr[slot, s, pl.ds(t0 + 1, half, stride=2), :] = _silu(g_od) * u_od
            o_ref[out_r0:out_r0 + out_n, lanes] = t_scr[slot, s, 0:out_n, :].astype(BF16)

    sizes = [FFN_MC] * (FFN_TM // FFN_MC)
    starts = [sum(sizes[:c]) for c in range(len(sizes))]

    @pl.when(is_ctx)
    def _():
        wts = weights()
        for c, (c0, mc) in enumerate(zip(starts, sizes)):
            pieces, segments = [zeros], []
            for q in range(mc // CTX_L):
                r0 = c0 + q * CTX_L
                segments.append((FFN_HALO + q * (CTX_L + FFN_HALO), r0, CTX_L))
                pieces += [h_ref[r0:r0 + CTX_L, :], zeros]
            chunk(c, wts, pieces, segments)

    @pl.when(jnp.logical_not(is_ctx))
    def _():
        wts = weights()
        for c, (r0, mc) in enumerate(zip(starts, sizes)):
            lo = zeros if r0 % LAT_L == 0 else h_ref[r0 - FFN_HALO:r0, :]
            hi = zeros if (r0 + mc) % LAT_L == 0 else h_ref[r0 + mc:r0 + mc + FFN_HALO, :]
            chunk(c, wts, [lo, h_ref[r0:r0 + mc, :], hi], [(FFN_HALO, r0, mc)])


def _ffn_up(h, w_in, conv_w, conv_b, layer):
    n_slab = FFN_TN // 128
    return pl.pallas_call(
        _ffn_up_kernel,
        grid=(T_ALL // FFN_TM, FFN_TILES),
        in_specs=[
            pl.BlockSpec((FFN_TM, D), lambda i, j: (i, 0)),
            pl.BlockSpec((1, D, FFN_TN), lambda i, j: (layer, 0, j)),
            pl.BlockSpec((1, D, FFN_TN), lambda i, j: (layer, 0, FFN_TILES + j)),
            pl.BlockSpec((1, 3, FFN_TN), lambda i, j: (layer, 0, j)),
            pl.BlockSpec((1, 3, FFN_TN), lambda i, j: (layer, 0, FFN_TILES + j)),
            pl.BlockSpec((1, 1, FFN_TN), lambda i, j: (layer, 0, j)),
            pl.BlockSpec((1, 1, FFN_TN), lambda i, j: (layer, 0, FFN_TILES + j)),
        ],
        out_specs=pl.BlockSpec((FFN_TM, FFN_TN), lambda i, j: (i, j)),
        out_shape=jax.ShapeDtypeStruct((T_ALL, D_FF), BF16),
        scratch_shapes=[
            pltpu.VMEM((2, 2 * n_slab, FFN_EXT, 128), F32),
            pltpu.VMEM((2, n_slab, FFN_MC, 128), F32),
        ],
        compiler_params=_cparams(2),
        name="ffn_up",
    )(h, w_in, w_in, conv_w, conv_w, conv_b, conv_b)


ATT_NQ = AH * AHD
ATT_NKV = AKV * AHD
ATT_DUP = 2 * ATT_NKV
ATT_N = ATT_NQ + 2 * ATT_DUP
ATT_TN = ATT_NQ
assert 2 * ATT_DUP == ATT_TN


def _dup_heads(w):
    lead = w.shape[:-1]
    w = w.reshape(lead + (AKV, AHD))
    return jnp.concatenate([w, w], axis=-1).reshape(lead + (ATT_DUP,))


def _attn_proj_kernel(h_ref, w_ref, cos_ref, slo_ref, shi_ref, q_ref, kd_ref, vd_ref, k32_ref, v32_ref):
    i = pl.program_id(0)
    j = pl.program_id(1)
    is_lat = i * PROJ_TM >= T_CTX

    def chunks(epilogue):
        for m in range(PROJ_TM // PROJ_MC):
            rows = slice(m * PROJ_MC, (m + 1) * PROJ_MC)
            epilogue(jnp.dot(h_ref[rows, :], w_ref[...], preferred_element_type=F32), rows)

    def rope_into(o_ref, acc, rows, c0, n_cols):
        for s in range(n_cols // 128):
            y = _rope_slab(acc[:, c0 + s * 128:c0 + (s + 1) * 128],
                           cos_ref[rows, :], slo_ref[rows, :], shi_ref[rows, :], AHD // 4)
            o_ref[rows, s * 128:(s + 1) * 128] = y.astype(BF16)

    def q_lat(acc, rows):
        rope_into(q_ref, acc, rows, 0, ATT_NQ)

    def q_ctx(acc, rows):
        q_ref[rows, :] = acc.astype(BF16)

    def kv_lat(acc, rows):
        rope_into(kd_ref, acc, rows, 0, ATT_DUP)
        vd_ref[rows, :] = acc[:, ATT_DUP:].astype(BF16)

    def undup_into(o_ref, acc, rows, c0):
        low = lax.broadcasted_iota(jnp.int32, (PROJ_MC, 128), 1) < AHD
        for s in range(ATT_NKV // 128):
            a = acc[:, c0 + 2 * s * 128:c0 + (2 * s + 1) * 128]
            b = acc[:, c0 + (2 * s + 1) * 128:c0 + (2 * s + 2) * 128]
            o_ref[rows, s * 128:(s + 1) * 128] = jnp.where(low, a, b)

    def kv_ctx(acc, rows):
        kd_ref[rows, :] = acc[:, :ATT_DUP].astype(BF16)
        vd_ref[rows, :] = acc[:, ATT_DUP:].astype(BF16)
        undup_into(k32_ref, acc, rows, 0)
        undup_into(v32_ref, acc, rows, ATT_DUP)

    for on_q, on_lat, epilogue in ((True, True, q_lat), (True, False, q_ctx),
                                   (False, True, kv_lat), (False, False, kv_ctx)):
        @pl.when(jnp.logical_and((j == 0) == on_q, is_lat == on_lat))
        def _():
            chunks(epilogue)


def _attn_proj(h, w_all, tabs):
    cos, slo, shi = tabs
    n_pos = LAT_L // PROJ_TM
    n_ctx = T_CTX // PROJ_TM
    tab_spec = pl.BlockSpec((PROJ_TM, 128), lambda i, j: (i % n_pos, 0))
    row_spec = pl.BlockSpec((PROJ_TM, ATT_DUP), lambda i, j: (i, 0))
    ctx_spec = pl.BlockSpec((PROJ_TM, ATT_NKV), lambda i, j: (jnp.minimum(i, n_ctx - 1), 0))
    return pl.pallas_call(
        _attn_proj_kernel,
        grid=(T_ALL // PROJ_TM, ATT_N // ATT_TN),
        in_specs=[
            pl.BlockSpec((PROJ_TM, D), lambda i, j: (i, 0)),
            pl.BlockSpec((D, ATT_TN), lambda i, j: (0, j)),
            tab_spec, tab_spec, tab_spec,
        ],
        out_specs=[
            pl.BlockSpec((PROJ_TM, ATT_NQ), lambda i, j: (i, 0)),
            row_spec, row_spec, ctx_spec, ctx_spec,
        ],
        out_shape=[
            jax.ShapeDtypeStruct((T_ALL, ATT_NQ), BF16),
            jax.ShapeDtypeStruct((T_ALL, ATT_DUP), BF16),
            jax.ShapeDtypeStruct((T_ALL, ATT_DUP), BF16),
            jax.ShapeDtypeStruct((T_CTX, ATT_NKV), F32),
            jax.ShapeDtypeStruct((T_CTX, ATT_NKV), F32),
        ],
        compiler_params=_cparams(2),
        name="attn_proj",
    )(h, w_all, cos, slo, shi)


def _head_scores(q_ref, kh, k_dup):
    nq = q_ref.shape[0]
    nk = k_dup.shape[0]
    low = lax.broadcasted_iota(jnp.int32, (nk, 128), 1) < AHD
    zero = jnp.zeros_like(k_dup)
    k2 = jnp.concatenate([jnp.where(low, k_dup, zero), jnp.where(low, zero, k_dup)], axis=0)
    base = kh * AG * AHD
    lhs = jnp.concatenate([q_ref[:, base:base + 128], q_ref[:, base + 128:base + 256]], axis=0)
    s = lax.dot_general(lhs, k2, (((1,), (1,)), ((), ())), preferred_element_type=F32)
    return [s[0:nq, 0:nk], s[0:nq, nk:], s[nq:, 0:nk], s[nq:, nk:]]


def _row_max(s):
    m = s[:, 0:128]
    for c in range(128, s.shape[1], 128):
        m = jnp.maximum(m, s[:, c:c + 128])
    return jnp.max(m, axis=-1, keepdims=True)


def _softmax_pv_store(scores, v_dup, sink_ref, kh, o_ref):
    nq = scores[0].shape[0]
    nk = v_dup.shape[0]
    v_aug = jnp.concatenate([v_dup, jnp.ones((nk, 128), BF16)], axis=1)
    ps, tails = [], []
    for g, s in enumerate(scores):
        sink = sink_ref[kh * AG + g]
        m = jnp.maximum(_row_max(s), sink)
        ps.append(jnp.exp(s - m).astype(BF16))
        tails.append(jnp.exp(sink - m))
    pv = jnp.dot(jnp.concatenate(ps, axis=0), v_aug, preferred_element_type=F32)
    outs = []
    for g in range(AG):
        rows = slice(g * nq, (g + 1) * nq)
        outs.append(pv[rows, 0:128] / (pv[rows, 128:256] + tails[g]))
    low = lax.broadcasted_iota(jnp.int32, (nq, 128), 1) < AHD
    base = kh * AG * AHD
    o_ref[:, base:base + 128] = jnp.where(low, outs[0], outs[1]).astype(BF16)
    o_ref[:, base + 128:base + 256] = jnp.where(low, outs[2], outs[3]).astype(BF16)


def _attend_all(q_ref, sink_ref, o_ref, kv_of, bias_of):
    def scores_of(kh):
        k_dup, _ = kv_of(kh)
        return bias_of(_head_scores(q_ref, kh, k_dup))

    nxt = scores_of(0)
    for kh in range(AKV):
        cur = nxt
        if kh + 1 < AKV:
            nxt = scores_of(kh + 1)
        _softmax_pv_store(cur, kv_of(kh)[1], sink_ref, kh, o_ref)


def _ctx_attn_kernel(sink_ref, q_ref, kd_ref, vd_ref, o_ref):
    def kv_of(kh):
        cols = slice(kh * 128, (kh + 1) * 128)
        return kd_ref[:, cols], vd_ref[:, cols]

    _attend_all(q_ref, sink_ref, o_ref, kv_of, lambda scores: scores)


BAND = ABLK + 2 * WINDOW
ATT_CTX_STEPS = T_CTX // ABLK
ATT_LAT_QBLK = LAT_L // ABLK


def _lat_attn_kernel(sink_ref, q_ref, kd_ref, vd_ref, ck_ref, cv_ref, o_ref):
    n = (pl.program_id(0) - ATT_CTX_STEPS) % ATT_LAT_QBLK
    start = pl.multiple_of(jnp.clip(n * ABLK - WINDOW, 0, LAT_L - BAND), WINDOW)
    qpos = n * ABLK + lax.broadcasted_iota(jnp.int32, (ABLK, BAND), 0)
    kpos = start + lax.broadcasted_iota(jnp.int32, (ABLK, BAND), 1)
    band_bias = jnp.where(jnp.abs(qpos - kpos) <= WINDOW, 0.0, NEG_INF)

    def kv_of(kh):
        cols = slice(kh * 128, (kh + 1) * 128)
        k = jnp.concatenate([kd_ref[pl.ds(start, BAND), cols], ck_ref[0, :, cols]], axis=0)
        v = jnp.concatenate([vd_ref[pl.ds(start, BAND), cols], cv_ref[0, :, cols]], axis=0)
        return k, v

    def bias_of(scores):
        return [jnp.concatenate([s[:, :BAND] + band_bias, s[:, BAND:]], axis=1) for s in scores]

    _attend_all(q_ref, sink_ref, o_ref, kv_of, bias_of)


def _attention(sink, q, kd, vd, cache_kd, cache_vd):
    def lat_seq(t):
        return jnp.maximum(t - ATT_CTX_STEPS, 0) // ATT_LAT_QBLK

    row_spec = pl.BlockSpec((ABLK, ATT_NQ), lambda t: (t, 0))
    ctx_spec = pl.BlockSpec((CTX_L, ATT_DUP), lambda t: (jnp.minimum(t // (CTX_L // ABLK), CTX_B - 1), 0))
    seq_spec = pl.BlockSpec((LAT_L, ATT_DUP), lambda t: (T_CTX // LAT_L + lat_seq(t), 0))
    cache_spec = pl.BlockSpec((1, CTX_L, ATT_DUP), lambda t: (lat_seq(t), 0, 0))

    def body(sink_ref, q_ref, kdc_ref, vdc_ref, kdl_ref, vdl_ref, ck_ref, cv_ref, o_ref):
        is_ctx = pl.program_id(0) < ATT_CTX_STEPS

        @pl.when(is_ctx)
        def _():
            _ctx_attn_kernel(sink_ref, q_ref, kdc_ref, vdc_ref, o_ref)

        @pl.when(jnp.logical_not(is_ctx))
        def _():
            _lat_attn_kernel(sink_ref, q_ref, kdl_ref, vdl_ref, ck_ref, cv_ref, o_ref)

    return pl.pallas_call(
        body,
        grid=(T_ALL // ABLK,),
        in_specs=[pl.BlockSpec(memory_space=pltpu.SMEM), row_spec,
                  ctx_spec, ctx_spec, seq_spec, seq_spec, cache_spec, cache_spec],
        out_specs=row_spec,
        out_shape=jax.ShapeDtypeStruct((T_ALL, ATT_NQ), BF16),
        compiler_params=_cparams(1),
        name="attention",
    )(sink, q, kd, vd, kd, vd, cache_kd, cache_vd)


def _log_sigmoid(x):
    return -(jnp.maximum(-x, 0.0) + jnp.log1p(jnp.exp(-jnp.abs(x))))


def kernel(x_prompt, x_sample, state_ret_fwd, state_ret_bwd, cache_attn_k, cache_attn_v, c, c_ctx, ada_w, ada_b, norm_mix, norm_ffn, ret_wq, ret_wk, ret_wv, ret_wg_fwd, ret_wg_bwd, ret_wo, ret_decay_fwd, ret_decay_bwd, attn_wq, attn_wk, attn_wv, attn_wo, attn_sink, ffn_w_in, ffn_conv_w, ffn_conv_b, ffn_w_out, norm_final):
    assert x_prompt.shape == (CTX_B, CTX_L, D) and x_sample.shape == (LAT_B, LAT_L, D)
    assert state_ret_fwd.shape == state_ret_bwd.shape == (LAT_B, 1, RH, RDK, RDV)
    assert cache_attn_k.shape == cache_attn_v.shape == (LAT_B, 1, CTX_L, AKV, AHD)
    assert ada_w.shape == (2, D, N_MOD * D) and ffn_w_in.shape == (2, D, 2 * D_FF)
    assert all(t.dtype == F32 for t in (x_prompt, x_sample, state_ret_fwd, cache_attn_k, ada_w, ffn_w_in))
    x_ctx = x_prompt.reshape(T_CTX, D)
    x_lat = x_sample.reshape(T_LAT, D)

    cv = jnp.concatenate([c, c_ctx[None, :], jnp.zeros((MOD_ROWS - LAT_B - 1, D), F32)], axis=0)
    mods = _adaln(cv, ada_w, ada_b).reshape(ada_w.shape[0], MOD_ROWS, N_MOD, D)

    h = _first_mod(x_ctx, x_lat, norm_mix[0:1], mods[0])
    w_ret = jnp.concatenate([ret_wq[0], ret_wk[0] * (RDK ** -0.5), ret_wv[0], ret_wg_fwd[0], ret_wg_bwd[0]],
                            axis=1).astype(BF16)
    qkvg = _ret_proj(h, w_ret, _rope_tables(RDK))
    lg = jnp.stack([_log_sigmoid(ret_decay_fwd[0].astype(F32)), _log_sigmoid(ret_decay_bwd[0].astype(F32))])
    y0, s_f, s_b = _ret_scan(qkvg, lg, state_ret_fwd, state_ret_bwd)
    w_in = ffn_w_in
    w_out = ffn_w_out
    conv_b = ffn_conv_b[:, None, :]
    x, h = _res_matmul(y0, ret_wo, 0, (x_ctx, x_lat), mods[0], 2,
                       norm_ffn[0:1], mods[0], (3, 4))
    act = _ffn_up(h, w_in, ffn_conv_w, conv_b, 0)
    x, h = _res_matmul(act, w_out, 0, x, mods[0], 5, norm_mix[1:2], mods[1], (0, 1))

    w_att = jnp.concatenate([attn_wq[0] * (AHD ** -0.5), _dup_heads(attn_wk[0]), _dup_heads(attn_wv[0])],
                            axis=1).astype(BF16)
    q, kd, vd, k32, v32 = _attn_proj(h, w_att, _rope_tables(AHD))
    sink = attn_sink[0].astype(F32)
    cache_kd = _dup_heads(cache_attn_k[:, 0].reshape(LAT_B, CTX_L, ATT_NKV)).astype(BF16)
    cache_vd = _dup_heads(cache_attn_v[:, 0].reshape(LAT_B, CTX_L, ATT_NKV)).astype(BF16)
    a = _attention(sink, q, kd, vd, cache_kd, cache_vd)
    x, h = _res_matmul(a, attn_wo, 0, x, mods[1], 2, norm_ffn[1:2], mods[1], (3, 4))
    act = _ffn_up(h, w_in, ffn_conv_w, conv_b, 1)
    y_ctx, y_lat = _res_matmul(act, w_out, 1, x, mods[1], 5, norm_final[None, :])

    y_prompt = y_ctx.reshape(CTX_B, CTX_L, D)
    y_sample = y_lat.reshape(LAT_B, LAT_L, D)
    new_k = k32.reshape(CTX_B, 1, CTX_L, AKV, AHD)
    new_v = v32.reshape(CTX_B, 1, CTX_L, AKV, AHD)
    return y_prompt, y_sample, s_f, s_b, new_k, new_v
```
